```python
import math
import jax, jax.numpy as jnp
from jax import lax
import numpy as np

D_MODEL = 2048
BATCH = 8
SEQ = 8192
DEPTH = 2

N_HEADS = 8
HEAD_DIM = 128
ATTN_W = N_HEADS * HEAD_DIM
CONV_CH = D_MODEL // 2
CONV_K = 31
FFN_HIDDEN = ((8 * D_MODEL // 3 + 255) // 256) * 256
PLE_DIM = 256
Q_BLOCK = 128
EPS = 1e-6
IN_COLS = 2 * CONV_CH + 3 * ATTN_W + N_HEADS + 2 * D_MODEL
NEG_INF = -1e30

kernel_name = "hybrid_conformer_fox_gated_trunk"


def rmsnorm(x, g):
    xf = x.astype(jnp.float32)
    y = xf * lax.rsqrt(jnp.mean(xf * xf, axis=-1, keepdims=True) + EPS)
    return (y * g.astype(jnp.float32)).astype(x.dtype)


def layernorm(x, g, b):
    xf = x.astype(jnp.float32)
    mu = jnp.mean(xf, axis=-1, keepdims=True)
    var = jnp.mean(jnp.square(xf - mu), axis=-1, keepdims=True)
    y = (xf - mu) * lax.rsqrt(var + EPS)
    return (y * g.astype(jnp.float32) + b.astype(jnp.float32)).astype(x.dtype)


def conformer_conv(glu_in, conv_w, conv_b, ln_g, ln_b, w_conv_out):
    a, gate = jnp.split(glu_in, 2, axis=-1)
    u = a * jax.nn.sigmoid(gate)
    u = lax.conv_general_dilated(
        u, conv_w[:, None, :].astype(u.dtype),
        window_strides=(1,), padding=[(CONV_K - 1, 0)],
        dimension_numbers=("NWC", "WIO", "NWC"),
        feature_group_count=CONV_CH) + conv_b
    u = jax.nn.silu(layernorm(u, ln_g, ln_b))
    return u @ w_conv_out


def forgetting_attention(q, k, v, log_f):
    b, s, h, d = q.shape
    nb = s // Q_BLOCK
    scale = 1.0 / math.sqrt(d)
    c = jnp.cumsum(log_f, axis=1).transpose(0, 2, 1)
    q_blocks = q.reshape(b, nb, Q_BLOCK, h, d).transpose(1, 0, 2, 3, 4)
    c_blocks = c.reshape(b, h, nb, Q_BLOCK).transpose(2, 0, 1, 3)
    starts = jnp.arange(nb, dtype=jnp.int32) * Q_BLOCK
    k_pos = jnp.arange(s, dtype=jnp.int32)

    def one_block(args):
        qb, cb, start = args
        sc = jnp.einsum("bqhd,bkhd->bhqk", qb, k,
                        preferred_element_type=jnp.float32) * scale
        sc = sc + cb[..., :, None] - c[:, :, None, :]
        q_pos = start + jnp.arange(Q_BLOCK, dtype=jnp.int32)
        mask = k_pos[None, :] <= q_pos[:, None]
        sc = jnp.where(mask, sc, NEG_INF)
        pr = jax.nn.softmax(sc, axis=-1)
        return jnp.einsum("bhqk,bkhd->bqhd", pr.astype(v.dtype), v)

    out = lax.map(one_block, (q_blocks, c_blocks, starts))
    return out.transpose(1, 0, 2, 3, 4).reshape(b, s, h * d)


def _fwd_setup_inputs(seed: int = 0) -> dict:
    key = jax.random.key(seed)
    ks = jax.random.split(key, 20)
    f32 = jnp.float32

    def w(k, shape, fan_in):
        return jax.random.normal(k, shape, f32) * (fan_in ** -0.5)

    def gain(k, shape):
        return 1.0 + 0.02 * jax.random.normal(k, shape, f32)

    return {
        "x": jax.random.normal(ks[0], (BATCH, SEQ, D_MODEL), f32),
        "p": jax.random.normal(ks[1], (DEPTH, BATCH, SEQ, PLE_DIM), f32),
        "norm_mix_g": gain(ks[2], (DEPTH, D_MODEL)),
        "w_in": w(ks[3], (DEPTH, D_MODEL, IN_COLS), D_MODEL),
        "b_forget": 2.0 + 0.1 * jax.random.normal(ks[4], (DEPTH, N_HEADS), f32),
        "conv_w": w(ks[5], (DEPTH, CONV_K, CONV_CH), CONV_K),
        "conv_b": 0.02 * jax.random.normal(ks[6], (DEPTH, CONV_CH), f32),
        "conv_ln_g": gain(ks[7], (DEPTH, CONV_CH)),
        "conv_ln_b": 0.02 * jax.random.normal(ks[8], (DEPTH, CONV_CH), f32),
        "w_conv_out": w(ks[9], (DEPTH, CONV_CH, D_MODEL), CONV_CH),
        "w_attn_out": w(ks[10], (DEPTH, ATTN_W, D_MODEL), ATTN_W),
        "w_out": w(ks[11], (DEPTH, D_MODEL, D_MODEL), D_MODEL),
        "norm_ffn_g": gain(ks[12], (DEPTH, D_MODEL)),
        "w_gate_up": w(ks[13], (DEPTH, D_MODEL, 2 * FFN_HIDDEN), D_MODEL),
        "w_down": w(ks[14], (DEPTH, FFN_HIDDEN, D_MODEL), FFN_HIDDEN),
        "norm_ple_g": gain(ks[15], (DEPTH, D_MODEL)),
        "w_ple_gate": w(ks[16], (DEPTH, D_MODEL, D_MODEL), D_MODEL),
        "w_ple_proj": w(ks[17], (DEPTH, PLE_DIM, D_MODEL), PLE_DIM),
        "final_g": gain(ks[18], (D_MODEL,)),
    }


def _fwd_reference(x, p, norm_mix_g, w_in, b_forget, conv_w, conv_b, conv_ln_g, conv_ln_b,
              w_conv_out, w_attn_out, w_out, norm_ffn_g, w_gate_up, w_down,
              norm_ple_g, w_ple_gate, w_ple_proj, final_g):
    b, s, _ = x.shape
    split_pts = np.cumsum([2 * CONV_CH, ATTN_W, ATTN_W, ATTN_W, N_HEADS, D_MODEL]).tolist()
    for i in range(DEPTH):
        h = rmsnorm(x, norm_mix_g[i])
        proj = h @ w_in[i]
        glu_in, q, k, v, f_logit, g_conv, g_attn = jnp.split(proj, split_pts, axis=-1)

        y_conv = conformer_conv(glu_in, conv_w[i], conv_b[i], conv_ln_g[i],
                                conv_ln_b[i], w_conv_out[i])

        q = q.reshape(b, s, N_HEADS, HEAD_DIM)
        k = k.reshape(b, s, N_HEADS, HEAD_DIM)
        v = v.reshape(b, s, N_HEADS, HEAD_DIM)
        log_f = jax.nn.log_sigmoid((f_logit + b_forget[i]).astype(jnp.float32))
        y_attn = forgetting_attention(q, k, v, log_f) @ w_attn_out[i]

        merged = jax.nn.sigmoid(g_conv) * y_conv + jax.nn.sigmoid(g_attn) * y_attn
        x = x + merged @ w_out[i]

        hf = rmsnorm(x, norm_ffn_g[i])
        gate, up = jnp.split(hf @ w_gate_up[i], 2, axis=-1)
        x = x + (jax.nn.silu(gate) * up) @ w_down[i]

        hp = rmsnorm(x, norm_ple_g[i])
        x = x + jax.nn.sigmoid(hp @ w_ple_gate[i]) * (p[i] @ w_ple_proj[i])
    return rmsnorm(x, final_g)


import jax as _jax
import jax.numpy as _jnp

TWIN_FORMAT = 'train_step'
FWD_PARAMS = ['x', 'p', 'norm_mix_g', 'w_in', 'b_forget', 'conv_w', 'conv_b', 'conv_ln_g', 'conv_ln_b', 'w_conv_out', 'w_attn_out', 'w_out', 'norm_ffn_g', 'w_gate_up', 'w_down', 'norm_ple_g', 'w_ple_gate', 'w_ple_proj', 'final_g']
TWIN_WEIGHTS = ['norm_mix_g', 'w_in', 'b_forget', 'conv_w', 'conv_b', 'conv_ln_g', 'conv_ln_b', 'w_conv_out', 'w_attn_out', 'w_out', 'norm_ffn_g', 'w_gate_up', 'w_down', 'norm_ple_g', 'w_ple_gate', 'w_ple_proj', 'final_g']
TWIN_DIFF_INPUT = 'x'
TWIN_INPUTS = ['x', 'p', 'norm_mix_g', 'w_in', 'b_forget', 'conv_w', 'conv_b', 'conv_ln_g', 'conv_ln_b', 'w_conv_out', 'w_attn_out', 'w_out', 'norm_ffn_g', 'w_gate_up', 'w_down', 'norm_ple_g', 'w_ple_gate', 'w_ple_proj', 'final_g', 'loss_target', 'm_norm_mix_g', 'm_w_in', 'm_b_forget', 'm_conv_w', 'm_conv_b', 'm_conv_ln_g', 'm_conv_ln_b', 'm_w_conv_out', 'm_w_attn_out', 'm_w_out', 'm_norm_ffn_g', 'm_w_gate_up', 'm_w_down', 'm_norm_ple_g', 'm_w_ple_gate', 'm_w_ple_proj', 'm_final_g', 'v_norm_mix_g', 'v_w_in', 'v_b_forget', 'v_conv_w', 'v_conv_b', 'v_conv_ln_g', 'v_conv_ln_b', 'v_w_conv_out', 'v_w_attn_out', 'v_w_out', 'v_norm_ffn_g', 'v_w_gate_up', 'v_w_down', 'v_norm_ple_g', 'v_w_ple_gate', 'v_w_ple_proj', 'v_final_g']
TWIN_OUTPUTS = ['loss', 'grad_x', 'grad_norm_mix_g', 'grad_w_in', 'grad_b_forget', 'grad_conv_w', 'grad_conv_b', 'grad_conv_ln_g', 'grad_conv_ln_b', 'grad_w_conv_out', 'grad_w_attn_out', 'grad_w_out', 'grad_norm_ffn_g', 'grad_w_gate_up', 'grad_w_down', 'grad_norm_ple_g', 'grad_w_ple_gate', 'grad_w_ple_proj', 'grad_final_g', 'delta_norm_mix_g', 'delta_w_in', 'delta_b_forget', 'delta_conv_w', 'delta_conv_b', 'delta_conv_ln_g', 'delta_conv_ln_b', 'delta_w_conv_out', 'delta_w_attn_out', 'delta_w_out', 'delta_norm_ffn_g', 'delta_w_gate_up', 'delta_w_down', 'delta_norm_ple_g', 'delta_w_ple_gate', 'delta_w_ple_proj', 'delta_final_g', 'new_m_norm_mix_g', 'new_m_w_in', 'new_m_b_forget', 'new_m_conv_w', 'new_m_conv_b', 'new_m_conv_ln_g', 'new_m_conv_ln_b', 'new_m_w_conv_out', 'new_m_w_attn_out', 'new_m_w_out', 'new_m_norm_ffn_g', 'new_m_w_gate_up', 'new_m_w_down', 'new_m_norm_ple_g', 'new_m_w_ple_gate', 'new_m_w_ple_proj', 'new_m_final_g', 'new_v_norm_mix_g', 'new_v_w_in', 'new_v_b_forget', 'new_v_conv_w', 'new_v_conv_b', 'new_v_conv_ln_g', 'new_v_conv_ln_b', 'new_v_w_conv_out', 'new_v_w_attn_out', 'new_v_w_out', 'new_v_norm_ffn_g', 'new_v_w_gate_up', 'new_v_w_down', 'new_v_norm_ple_g', 'new_v_w_ple_gate', 'new_v_w_ple_proj', 'new_v_final_g']
TWIN_LEAF_KINDS = {'loss': 'loss', 'grad_x': 'grad_x', 'grad_norm_mix_g': 'grad_w', 'grad_w_in': 'grad_w', 'grad_b_forget': 'grad_w', 'grad_conv_w': 'grad_w', 'grad_conv_b': 'grad_w', 'grad_conv_ln_g': 'grad_w', 'grad_conv_ln_b': 'grad_w', 'grad_w_conv_out': 'grad_w', 'grad_w_attn_out': 'grad_w', 'grad_w_out': 'grad_w', 'grad_norm_ffn_g': 'grad_w', 'grad_w_gate_up': 'grad_w', 'grad_w_down': 'grad_w', 'grad_norm_ple_g': 'grad_w', 'grad_w_ple_gate': 'grad_w', 'grad_w_ple_proj': 'grad_w', 'grad_final_g': 'grad_w', 'delta_norm_mix_g': 'delta_w', 'delta_w_in': 'delta_w', 'delta_b_forget': 'delta_w', 'delta_conv_w': 'delta_w', 'delta_conv_b': 'delta_w', 'delta_conv_ln_g': 'delta_w', 'delta_conv_ln_b': 'delta_w', 'delta_w_conv_out': 'delta_w', 'delta_w_attn_out': 'delta_w', 'delta_w_out': 'delta_w', 'delta_norm_ffn_g': 'delta_w', 'delta_w_gate_up': 'delta_w', 'delta_w_down': 'delta_w', 'delta_norm_ple_g': 'delta_w', 'delta_w_ple_gate': 'delta_w', 'delta_w_ple_proj': 'delta_w', 'delta_final_g': 'delta_w', 'new_m_norm_mix_g': 'new_m', 'new_m_w_in': 'new_m', 'new_m_b_forget': 'new_m', 'new_m_conv_w': 'new_m', 'new_m_conv_b': 'new_m', 'new_m_conv_ln_g': 'new_m', 'new_m_conv_ln_b': 'new_m', 'new_m_w_conv_out': 'new_m', 'new_m_w_attn_out': 'new_m', 'new_m_w_out': 'new_m', 'new_m_norm_ffn_g': 'new_m', 'new_m_w_gate_up': 'new_m', 'new_m_w_down': 'new_m', 'new_m_norm_ple_g': 'new_m', 'new_m_w_ple_gate': 'new_m', 'new_m_w_ple_proj': 'new_m', 'new_m_final_g': 'new_m', 'new_v_norm_mix_g': 'new_v', 'new_v_w_in': 'new_v', 'new_v_b_forget': 'new_v', 'new_v_conv_w': 'new_v', 'new_v_conv_b': 'new_v', 'new_v_conv_ln_g': 'new_v', 'new_v_conv_ln_b': 'new_v', 'new_v_w_conv_out': 'new_v', 'new_v_w_attn_out': 'new_v', 'new_v_w_out': 'new_v', 'new_v_norm_ffn_g': 'new_v', 'new_v_w_gate_up': 'new_v', 'new_v_w_down': 'new_v', 'new_v_norm_ple_g': 'new_v', 'new_v_w_ple_gate': 'new_v', 'new_v_w_ple_proj': 'new_v', 'new_v_final_g': 'new_v'}


def _forward(args):
    return _fwd_reference(*[args[k] for k in FWD_PARAMS])


def _output_shape():
    def fwd():
        inp = _fwd_setup_inputs(0)
        return _fwd_reference(*[inp[k] for k in FWD_PARAMS])
    out = _jax.eval_shape(fwd)
    return out.shape, out.dtype

N_MICROBATCH = 1
ADAM_LR = 0.001
ADAM_B1 = 0.9
ADAM_B2 = 0.999
ADAM_EPS = 1e-08
ADAM_WD = 0.01
ADAM_STEP = 10
PER_EXAMPLE_BATCH_AXIS = {'x': 0, 'p': 1, 'loss_target': 0}
SHARED_INPUTS = []
_WEIGHT_DTYPES = {'norm_mix_g': _jnp.float32, 'w_in': _jnp.float32, 'b_forget': _jnp.float32, 'conv_w': _jnp.float32, 'conv_b': _jnp.float32, 'conv_ln_g': _jnp.float32, 'conv_ln_b': _jnp.float32, 'w_conv_out': _jnp.float32, 'w_attn_out': _jnp.float32, 'w_out': _jnp.float32, 'norm_ffn_g': _jnp.float32, 'w_gate_up': _jnp.float32, 'w_down': _jnp.float32, 'norm_ple_g': _jnp.float32, 'w_ple_gate': _jnp.float32, 'w_ple_proj': _jnp.float32, 'final_g': _jnp.float32}
MOMENT_SCALE = {'norm_mix_g': 5.643883e-02, 'w_in': 2.716499e-02, 'b_forget': 2.104614e-01, 'conv_w': 5.288945e-02, 'conv_b': 1.135819e-01, 'conv_ln_g': 6.386986e-02, 'conv_ln_b': 5.631681e-02, 'w_conv_out': 3.672223e-02, 'w_attn_out': 2.575217e-02, 'w_out': 4.442565e-02, 'norm_ffn_g': 7.734848e-02, 'w_gate_up': 3.279609e-02, 'w_down': 5.358494e-02, 'norm_ple_g': 1.899183e-02, 'w_ple_gate': 1.888168e-02, 'w_ple_proj': 4.811107e-02, 'final_g': 3.199570e+01}


def _to_microbatches(a, axis):
    t = _jnp.moveaxis(a, axis, 0)
    t = t.reshape((N_MICROBATCH, t.shape[0] // N_MICROBATCH) + t.shape[1:])
    return _jnp.moveaxis(t, 1, axis + 1)


def setup_inputs(seed: int = 0) -> dict:
    inp = _fwd_setup_inputs(seed)
    key = _jax.random.fold_in(_jax.random.key(seed), 7919)
    shape, _ = _output_shape()
    out = dict(inp)
    out["loss_target"] = _jax.random.normal(_jax.random.fold_in(key, 0), shape, _jnp.float32)
    for i, name in enumerate(TWIN_WEIGHTS):
        w = inp[name].astype(_jnp.float32)
        if MOMENT_SCALE is None:
            s = _jnp.sqrt(_jnp.mean(_jnp.square(w)) + 1e-30)
        else:
            s = MOMENT_SCALE[name]
        km, kv = _jax.random.split(_jax.random.fold_in(key, i + 1))
        out[name] = w
        out["m_" + name] = s * _jax.random.normal(km, w.shape, _jnp.float32)
        out["v_" + name] = (s * s) * _jax.random.uniform(kv, w.shape, _jnp.float32, 0.5, 1.5)
    if N_MICROBATCH > 1:
        for name, axis in PER_EXAMPLE_BATCH_AXIS.items():
            out[name] = _to_microbatches(out[name], axis)
    return {'x': out['x'], 'p': out['p'], 'norm_mix_g': out['norm_mix_g'], 'w_in': out['w_in'], 'b_forget': out['b_forget'], 'conv_w': out['conv_w'], 'conv_b': out['conv_b'], 'conv_ln_g': out['conv_ln_g'], 'conv_ln_b': out['conv_ln_b'], 'w_conv_out': out['w_conv_out'], 'w_attn_out': out['w_attn_out'], 'w_out': out['w_out'], 'norm_ffn_g': out['norm_ffn_g'], 'w_gate_up': out['w_gate_up'], 'w_down': out['w_down'], 'norm_ple_g': out['norm_ple_g'], 'w_ple_gate': out['w_ple_gate'], 'w_ple_proj': out['w_ple_proj'], 'final_g': out['final_g'], 'loss_target': out['loss_target'], 'm_norm_mix_g': out['m_norm_mix_g'], 'm_w_in': out['m_w_in'], 'm_b_forget': out['m_b_forget'], 'm_conv_w': out['m_conv_w'], 'm_conv_b': out['m_conv_b'], 'm_conv_ln_g': out['m_conv_ln_g'], 'm_conv_ln_b': out['m_conv_ln_b'], 'm_w_conv_out': out['m_w_conv_out'], 'm_w_attn_out': out['m_w_attn_out'], 'm_w_out': out['m_w_out'], 'm_norm_ffn_g': out['m_norm_ffn_g'], 'm_w_gate_up': out['m_w_gate_up'], 'm_w_down': out['m_w_down'], 'm_norm_ple_g': out['m_norm_ple_g'], 'm_w_ple_gate': out['m_w_ple_gate'], 'm_w_ple_proj': out['m_w_ple_proj'], 'm_final_g': out['m_final_g'], 'v_norm_mix_g': out['v_norm_mix_g'], 'v_w_in': out['v_w_in'], 'v_b_forget': out['v_b_forget'], 'v_conv_w': out['v_conv_w'], 'v_conv_b': out['v_conv_b'], 'v_conv_ln_g': out['v_conv_ln_g'], 'v_conv_ln_b': out['v_conv_ln_b'], 'v_w_conv_out': out['v_w_conv_out'], 'v_w_attn_out': out['v_w_attn_out'], 'v_w_out': out['v_w_out'], 'v_norm_ffn_g': out['v_norm_ffn_g'], 'v_w_gate_up': out['v_w_gate_up'], 'v_w_down': out['v_w_down'], 'v_norm_ple_g': out['v_norm_ple_g'], 'v_w_ple_gate': out['v_w_ple_gate'], 'v_w_ple_proj': out['v_w_ple_proj'], 'v_final_g': out['v_final_g']}


def _loss(weights, diff, rest, loss_target):
    with _jax.named_scope("forward"):
        args = {**rest, TWIN_DIFF_INPUT: diff, **{k: w.astype(_WEIGHT_DTYPES[k]) for k, w in weights.items()}}
        y = _forward(args)
    with _jax.named_scope("loss_head"):
        err = _jnp.square(y.astype(_jnp.float32) - loss_target)
        return 0.5 * _jnp.sum(_jnp.mean(err, axis=-1)) if err.ndim else 0.5 * err


def _adamw(w, g, m, v):
    m = ADAM_B1 * m + (1.0 - ADAM_B1) * g
    v = ADAM_B2 * v + (1.0 - ADAM_B2) * _jnp.square(g)
    m_hat = m / (1.0 - ADAM_B1 ** ADAM_STEP)
    v_hat = v / (1.0 - ADAM_B2 ** ADAM_STEP)
    delta = -ADAM_LR * (m_hat / (_jnp.sqrt(v_hat) + ADAM_EPS) + ADAM_WD * w)
    return delta, m, v


def reference(x, p, norm_mix_g, w_in, b_forget, conv_w, conv_b, conv_ln_g, conv_ln_b, w_conv_out, w_attn_out, w_out, norm_ffn_g, w_gate_up, w_down, norm_ple_g, w_ple_gate, w_ple_proj, final_g, loss_target, m_norm_mix_g, m_w_in, m_b_forget, m_conv_w, m_conv_b, m_conv_ln_g, m_conv_ln_b, m_w_conv_out, m_w_attn_out, m_w_out, m_norm_ffn_g, m_w_gate_up, m_w_down, m_norm_ple_g, m_w_ple_gate, m_w_ple_proj, m_final_g, v_norm_mix_g, v_w_in, v_b_forget, v_conv_w, v_conv_b, v_conv_ln_g, v_conv_ln_b, v_w_conv_out, v_w_attn_out, v_w_out, v_norm_ffn_g, v_w_gate_up, v_w_down, v_norm_ple_g, v_w_ple_gate, v_w_ple_proj, v_final_g):
    given = dict(x=x, p=p, norm_mix_g=norm_mix_g, w_in=w_in, b_forget=b_forget, conv_w=conv_w, conv_b=conv_b, conv_ln_g=conv_ln_g, conv_ln_b=conv_ln_b, w_conv_out=w_conv_out, w_attn_out=w_attn_out, w_out=w_out, norm_ffn_g=norm_ffn_g, w_gate_up=w_gate_up, w_down=w_down, norm_ple_g=norm_ple_g, w_ple_gate=w_ple_gate, w_ple_proj=w_ple_proj, final_g=final_g, loss_target=loss_target, m_norm_mix_g=m_norm_mix_g, m_w_in=m_w_in, m_b_forget=m_b_forget, m_conv_w=m_conv_w, m_conv_b=m_conv_b, m_conv_ln_g=m_conv_ln_g, m_conv_ln_b=m_conv_ln_b, m_w_conv_out=m_w_conv_out, m_w_attn_out=m_w_attn_out, m_w_out=m_w_out, m_norm_ffn_g=m_norm_ffn_g, m_w_gate_up=m_w_gate_up, m_w_down=m_w_down, m_norm_ple_g=m_norm_ple_g, m_w_ple_gate=m_w_ple_gate, m_w_ple_proj=m_w_ple_proj, m_final_g=m_final_g, v_norm_mix_g=v_norm_mix_g, v_w_in=v_w_in, v_b_forget=v_b_forget, v_conv_w=v_conv_w, v_conv_b=v_conv_b, v_conv_ln_g=v_conv_ln_g, v_conv_ln_b=v_conv_ln_b, v_w_conv_out=v_w_conv_out, v_w_attn_out=v_w_attn_out, v_w_out=v_w_out, v_norm_ffn_g=v_norm_ffn_g, v_w_gate_up=v_w_gate_up, v_w_down=v_w_down, v_norm_ple_g=v_norm_ple_g, v_w_ple_gate=v_w_ple_gate, v_w_ple_proj=v_w_ple_proj, v_final_g=v_final_g)
    weights = {n: given[n] for n in TWIN_WEIGHTS}
    shared = {n: given[n] for n in SHARED_INPUTS}
    per_example = {n: given[n] for n in ['x', 'p']}
    grad_fn = _jax.value_and_grad(_loss, argnums=(0, 1))

    def one_microbatch(ex, loss_target):
        ex = dict(ex)
        diff = ex.pop(TWIN_DIFF_INPUT)
        return grad_fn(weights, diff, {**shared, **ex}, loss_target)

    if N_MICROBATCH == 1:
        loss, (grad_w, grad_x) = one_microbatch(per_example, given["loss_target"])
    else:
        def body(carry, xs):
            loss_sum, grad_sum = carry
            l_k, (gw_k, gx_k) = one_microbatch(xs[0], xs[1])
            with _jax.named_scope("update"):
                return (loss_sum + l_k, _jax.tree.map(_jnp.add, grad_sum, gw_k)), gx_k

        init = (_jnp.zeros((), _jnp.float32), _jax.tree.map(_jnp.zeros_like, weights))
        (loss, grad_w), grad_x = _jax.lax.scan(body, init, (per_example, given["loss_target"]))
    with _jax.named_scope("update"):
        delta_w, new_m, new_v = {}, {}, {}
        for n in TWIN_WEIGHTS:
            delta_w[n], new_m[n], new_v[n] = _adamw(weights[n], grad_w[n], given["m_" + n], given["v_" + n])
    return (loss, grad_x, *[grad_w[n] for n in TWIN_WEIGHTS], *[delta_w[n] for n in TWIN_WEIGHTS],
            *[new_m[n] for n in TWIN_WEIGHTS], *[new_v[n] for n in TWIN_WEIGHTS])
```

```python
import functools
import math

import jax
import jax.numpy as jnp
from jax import lax
from jax.experimental import pallas as pl
from jax.experimental.pallas import tpu as pltpu

F32 = jnp.float32
BF16 = jnp.bfloat16

EPS = 1e-6
CONV_K = 31
NEG_INF = -1e30
ADAM_LR = 0.001
ADAM_B1 = 0.9
ADAM_B2 = 0.999
ADAM_EPS = 1e-08
ADAM_WD = 0.01
ADAM_STEP = 10

LANES = 128
VMEM_LIMIT = 60 * 1024 * 1024
PACK_W = 1024
N_CHIPS = 4
GATHER_CHUNKS = 4
PACK_ROW_ALIGN = 2 * GATHER_CHUNKS * 16
CONV_HALO = 32
CUM_BLOCK = 256

MM_TM = 1024
MM_TN = 1024
MM_TK = 512
ROW_TILE = 256
CONV_TILE = 256
ATTN_TILE = 512

MESH = pl.DeviceIdType.MESH
ANY = pl.BlockSpec(memory_space=pl.ANY)


def _params(*sem):
    return pltpu.CompilerParams(dimension_semantics=sem, vmem_limit_bytes=VMEM_LIMIT)


def _tile(dim, pref, align=LANES):
    if dim <= pref:
        return dim
    t = (pref // align) * align
    while t >= align:
        if dim % t == 0:
            return t
        t -= align
    return dim


def _sig(x):
    return 1.0 / (1.0 + jnp.exp(-x))


def _op(a):
    return a if isinstance(a, tuple) else (a, 0)


_DIMS = {
    "nn": (((1,), (0,)), ((), ())),
    "nt": (((1,), (1,)), ((), ())),
    "tn": (((0,), (0,)), ((), ())),
}


def _matmul(name, mode, a_ops, b_ops, terms, n_acc, m, n, k, out_dtypes, epi=None, extras=(),
            tm=None, tn=None, tk=None):
    tm = tm or _tile(m, MM_TM)
    tn = tn or _tile(n, MM_TN)
    tk = tk or _tile(k, MM_TK)
    nk = k // tk
    a_ops = [_op(a) for a in a_ops]
    b_ops = [_op(b) for b in b_ops]
    extras = [(kind, _op(e)) for kind, e in extras]
    na, nb, ne, no = len(a_ops), len(b_ops), len(extras), len(out_dtypes)

    def a_spec(off):
        if mode == "tn":
            assert off % tm == 0
            return pl.BlockSpec((tk, tm), lambda i, j, kk: (kk, i + off // tm))
        assert off % tk == 0
        return pl.BlockSpec((tm, tk), lambda i, j, kk: (i, kk + off // tk))

    def b_spec(off):
        if mode == "nt":
            assert off % tk == 0
            return pl.BlockSpec((tn, tk), lambda i, j, kk: (j, kk + off // tk))
        assert off % tn == 0
        return pl.BlockSpec((tk, tn), lambda i, j, kk: (kk, j + off // tn))

    def e_spec(kind, off):
        assert off % tn == 0
        if kind == "n":
            return pl.BlockSpec((1, tn), lambda i, j, kk: (0, j + off // tn))
        return pl.BlockSpec((tm, tn), lambda i, j, kk: (i, j + off // tn))

    def body(*refs):
        a_refs = refs[:na]
        b_refs = refs[na:na + nb]
        e_refs = refs[na + nb:na + nb + ne]
        o_refs = refs[na + nb + ne:na + nb + ne + no]
        acc_refs = refs[na + nb + ne + no:]
        kk = pl.program_id(2)

        @pl.when(kk == 0)
        def _():
            for acc in acc_refs:
                acc[...] = jnp.zeros(acc.shape, F32)

        av = [r[...].astype(BF16) for r in a_refs]
        bv = [r[...].astype(BF16) for r in b_refs]
        for ai, bi, ci in terms:
            acc_refs[ci][...] += lax.dot_general(av[ai], bv[bi], _DIMS[mode], preferred_element_type=F32)

        @pl.when(kk == nk - 1)
        def _():
            accs = [acc[...] for acc in acc_refs]
            outs = epi(accs, [e[...] for e in e_refs]) if epi is not None else accs
            for o, val in zip(o_refs, outs):
                o[...] = val.astype(o.dtype)

    outs = pl.pallas_call(
        body,
        name=name,
        grid=(m // tm, n // tn, nk),
        in_specs=[a_spec(off) for _, off in a_ops] + [b_spec(off) for _, off in b_ops]
        + [e_spec(kind, off) for kind, (_, off) in extras],
        out_specs=[pl.BlockSpec((tm, tn), lambda i, j, kk: (i, j)) for _ in out_dtypes],
        out_shape=[jax.ShapeDtypeStruct((m, n), dt) for dt in out_dtypes],
        scratch_shapes=[pltpu.VMEM((tm, tn), F32) for _ in range(n_acc)],
        compiler_params=_params("parallel", "parallel", "arbitrary"),
    )(*[a for a, _ in a_ops], *[b for b, _ in b_ops], *[e for _, (e, _) in extras])
    return outs


def _mm(name, mode, a, b, m, n, k, out_dtype, **kw):
    return _matmul(name, mode, [a], [b], [(0, 0, 0)], 1, m, n, k, [out_dtype], **kw)[0]


def _ew(name, fn, ins, m, n, out_dtypes, tm, tn=None):
    tn = tn or n
    ins = [_op(a) for a in ins]
    ni = len(ins)

    def spec(off):
        assert off % tn == 0
        return pl.BlockSpec((tm, tn), lambda i, j: (i, j + off // tn))

    def body(*refs):
        outs = fn(*[r[...] for r in refs[:ni]])
        for o, val in zip(refs[ni:], outs):
            o[...] = val.astype(o.dtype)

    return pl.pallas_call(
        body,
        name=name,
        grid=(m // tm, n // tn),
        in_specs=[spec(off) for _, off in ins],
        out_specs=[pl.BlockSpec((tm, tn), lambda i, j: (i, j)) for _ in out_dtypes],
        out_shape=[jax.ShapeDtypeStruct((m, n), dt) for dt in out_dtypes],
        compiler_params=_params("parallel", "parallel"),
    )(*[a for a, _ in ins])


def _rms_fwd(name, x, g, t, d):
    tr = _tile(t, ROW_TILE, 8)

    def body(x_ref, g_ref, h_ref):
        xv = x_ref[...]
        r = lax.rsqrt(jnp.mean(xv * xv, axis=1, keepdims=True) + EPS)
        h_ref[...] = (xv * r * g_ref[...]).astype(BF16)

    return pl.pallas_call(
        body,
        name=name,
        grid=(t // tr,),
        in_specs=[pl.BlockSpec((tr, d), lambda i: (i, 0)), pl.BlockSpec((1, d), lambda i: (0, 0))],
        out_specs=pl.BlockSpec((tr, d), lambda i: (i, 0)),
        out_shape=jax.ShapeDtypeStruct((t, d), BF16),
        compiler_params=_params("parallel"),
    )(x, g)


def _rms_bwd_rows(dh, xv, g):
    r = lax.rsqrt(jnp.mean(xv * xv, axis=1, keepdims=True) + EPS)
    xhat = xv * r
    dxh = dh * g
    dx = r * (dxh - xhat * jnp.mean(dxh * xhat, axis=1, keepdims=True))
    return dx, dh * xhat


def _rms_bwd(name, dh, x, g, dres, t, d):
    tr = _tile(t, ROW_TILE, 8)

    def body(dh_ref, x_ref, g_ref, dres_ref, dx_ref, dxb_ref, dg_ref):
        @pl.when(pl.program_id(0) == 0)
        def _():
            dg_ref[...] = jnp.zeros(dg_ref.shape, F32)

        dx, dg_rows = _rms_bwd_rows(dh_ref[...].astype(F32), x_ref[...], g_ref[...])
        dx = dx + dres_ref[...]
        dx_ref[...] = dx
        dxb_ref[...] = dx.astype(BF16)
        dg_ref[...] += jnp.sum(dg_rows, axis=0, keepdims=True)

    row = pl.BlockSpec((tr, d), lambda i: (i, 0))
    vec = pl.BlockSpec((1, d), lambda i: (0, 0))
    return pl.pallas_call(
        body,
        name=name,
        grid=(t // tr,),
        in_specs=[row, row, vec, row],
        out_specs=[row, row, vec],
        out_shape=[jax.ShapeDtypeStruct((t, d), F32), jax.ShapeDtypeStruct((t, d), BF16),
                   jax.ShapeDtypeStruct((1, d), F32)],
        compiler_params=_params("arbitrary"),
    )(dh, x, g, dres)


def _loss_head(x, g, target, t, d):
    tr = _tile(t, ROW_TILE, 8)

    def body(x_ref, g_ref, tgt_ref, loss_ref, dx_ref, dxb_ref, dg_ref):
        @pl.when(pl.program_id(0) == 0)
        def _():
            dg_ref[...] = jnp.zeros(dg_ref.shape, F32)
            loss_ref[...] = jnp.zeros(loss_ref.shape, F32)

        xv = x_ref[...]
        gv = g_ref[...]
        r = lax.rsqrt(jnp.mean(xv * xv, axis=1, keepdims=True) + EPS)
        err = xv * r * gv - tgt_ref[...]
        loss_ref[...] += (0.5 / d) * jnp.sum(err * err)
        dx, dg_rows = _rms_bwd_rows(err * (1.0 / d), xv, gv)
        dx_ref[...] = dx
        dxb_ref[...] = dx.astype(BF16)
        dg_ref[...] += jnp.sum(dg_rows, axis=0, keepdims=True)

    row = pl.BlockSpec((tr, d), lambda i: (i, 0))
    vec = pl.BlockSpec((1, d), lambda i: (0, 0))
    one = pl.BlockSpec((1, LANES), lambda i: (0, 0))
    return pl.pallas_call(
        body,
        name="loss_head",
        grid=(t // tr,),
        in_specs=[row, vec, row],
        out_specs=[one, row, row, vec],
        out_shape=[jax.ShapeDtypeStruct((1, LANES), F32), jax.ShapeDtypeStruct((t, d), F32),
                   jax.ShapeDtypeStruct((t, d), BF16), jax.ShapeDtypeStruct((1, d), F32)],
        compiler_params=_params("arbitrary"),
    )(x, g, target)


def _layernorm_rows(cv, g, b):
    mu = jnp.mean(cv, axis=1, keepdims=True)
    xc = cv - mu
    rstd = lax.rsqrt(jnp.mean(xc * xc, axis=1, keepdims=True) + EPS)
    xhat = xc * rstd
    return xhat, rstd, xhat * g + b


def _conv_fwd(proj, conv_w, conv_b, ln_g, ln_b, t, c, a_off, g_off):
    tm = _tile(t, CONV_TILE, CONV_HALO)
    per = tm // CONV_HALO
    ab, gb = a_off // c, g_off // c

    def body(a_ref, g_ref, ap_ref, gp_ref, w_ref, cb_ref, lg_ref, lb_ref, u2_ref, c_ref, upad):
        i = pl.program_id(0)
        u_prev = ap_ref[...].astype(F32) * _sig(gp_ref[...].astype(F32))
        upad[pl.ds(0, CONV_HALO), :] = jnp.where(i > 0, u_prev, 0.0)
        upad[pl.ds(CONV_HALO, tm), :] = a_ref[...].astype(F32) * _sig(g_ref[...].astype(F32))
        acc = jnp.zeros((tm, c), F32) + cb_ref[...]
        for k in range(CONV_K):
            acc = acc + w_ref[pl.ds(k, 1), :] * upad[pl.ds(CONV_HALO - (CONV_K - 1) + k, tm), :]
        c_ref[...] = acc
        _, _, z = _layernorm_rows(acc, lg_ref[...], lb_ref[...])
        u2_ref[...] = (z * _sig(z)).astype(BF16)

    vec = pl.BlockSpec((1, c), lambda i: (0, 0))
    return pl.pallas_call(
        body,
        name="conv_fwd",
        grid=(t // tm,),
        in_specs=[
            pl.BlockSpec((tm, c), lambda i: (i, ab)),
            pl.BlockSpec((tm, c), lambda i: (i, gb)),
            pl.BlockSpec((CONV_HALO, c), lambda i: (jnp.maximum(i * per - 1, 0), ab)),
            pl.BlockSpec((CONV_HALO, c), lambda i: (jnp.maximum(i * per - 1, 0), gb)),
            pl.BlockSpec((CONV_HALO, c), lambda i: (0, 0)), vec, vec, vec,
        ],
        out_specs=[pl.BlockSpec((tm, c), lambda i: (i, 0)), pl.BlockSpec((tm, c), lambda i: (i, 0))],
        out_shape=[jax.ShapeDtypeStruct((t, c), BF16), jax.ShapeDtypeStruct((t, c), F32)],
        scratch_shapes=[pltpu.VMEM((CONV_HALO + tm, c), F32)],
        compiler_params=_params("parallel"),
    )(proj, proj, proj, proj, conv_w, conv_b, ln_g, ln_b)


def _conv_bwd(proj, cpre, du2, conv_w, ln_g, ln_b, t, c, a_off, g_off):
    tm = _tile(t, CONV_TILE, CONV_HALO)
    per = tm // CONV_HALO
    nt = t // tm
    last_halo = t // CONV_HALO - 1
    ab, gb = a_off // c, g_off // c

    def body(a_ref, g_ref, ap_ref, gp_ref, c_ref, cn_ref, du_ref, dun_ref, w_ref, lg_ref, lb_ref,
             da_ref, dg_ref, gw_ref, gcb_ref, glg_ref, glb_ref, upad, dpad):
        i = pl.program_id(0)

        @pl.when(i == 0)
        def _():
            gw_ref[...] = jnp.zeros(gw_ref.shape, F32)
            gcb_ref[...] = jnp.zeros(gcb_ref.shape, F32)
            glg_ref[...] = jnp.zeros(glg_ref.shape, F32)
            glb_ref[...] = jnp.zeros(glb_ref.shape, F32)

        lg = lg_ref[...]
        lb = lb_ref[...]

        def ln_bwd(cv, duv):
            xhat, rstd, z = _layernorm_rows(cv, lg, lb)
            sz = _sig(z)
            dz = duv * (sz * (1.0 + z * (1.0 - sz)))
            dxh = dz * lg
            dc = rstd * (dxh - jnp.mean(dxh, axis=1, keepdims=True)
                         - xhat * jnp.mean(dxh * xhat, axis=1, keepdims=True))
            return dc, dz, xhat

        dc, dz, xhat = ln_bwd(c_ref[...], du_ref[...].astype(F32))
        dc_next, _, _ = ln_bwd(cn_ref[...], dun_ref[...].astype(F32))
        glg_ref[...] += jnp.sum(dz * xhat, axis=0, keepdims=True)
        glb_ref[...] += jnp.sum(dz, axis=0, keepdims=True)
        gcb_ref[...] += jnp.sum(dc, axis=0, keepdims=True)
        dpad[pl.ds(0, tm), :] = dc
        dpad[pl.ds(tm, CONV_HALO), :] = jnp.where(i < nt - 1, dc_next, 0.0)

        av = a_ref[...].astype(F32)
        sg = _sig(g_ref[...].astype(F32))
        u_prev = ap_ref[...].astype(F32) * _sig(gp_ref[...].astype(F32))
        upad[pl.ds(0, CONV_HALO), :] = jnp.where(i > 0, u_prev, 0.0)
        upad[pl.ds(CONV_HALO, tm), :] = av * sg

        du = jnp.zeros((tm, c), F32)
        for k in range(CONV_K):
            du = du + w_ref[pl.ds(k, 1), :] * dpad[pl.ds(CONV_K - 1 - k, tm), :]
            gw_ref[pl.ds(k, 1), :] += jnp.sum(
                dc * upad[pl.ds(CONV_HALO - (CONV_K - 1) + k, tm), :], axis=0, keepdims=True)
        da_ref[...] = (du * sg).astype(BF16)
        dg_ref[...] = (du * av * sg * (1.0 - sg)).astype(BF16)

    vec = pl.BlockSpec((1, c), lambda i: (0, 0))
    cur = pl.BlockSpec((tm, c), lambda i: (i, 0))
    nxt = pl.BlockSpec((CONV_HALO, c), lambda i: (jnp.minimum((i + 1) * per, last_halo), 0))
    wsp = pl.BlockSpec((CONV_HALO, c), lambda i: (0, 0))
    return pl.pallas_call(
        body,
        name="conv_bwd",
        grid=(nt,),
        in_specs=[
            pl.BlockSpec((tm, c), lambda i: (i, ab)),
            pl.BlockSpec((tm, c), lambda i: (i, gb)),
            pl.BlockSpec((CONV_HALO, c), lambda i: (jnp.maximum(i * per - 1, 0), ab)),
            pl.BlockSpec((CONV_HALO, c), lambda i: (jnp.maximum(i * per - 1, 0), gb)),
            cur, nxt, cur, nxt, wsp, vec, vec,
        ],
        out_specs=[cur, cur, wsp, vec, vec, vec],
        out_shape=[jax.ShapeDtypeStruct((t, c), BF16), jax.ShapeDtypeStruct((t, c), BF16),
                   jax.ShapeDtypeStruct((CONV_HALO, c), F32), jax.ShapeDtypeStruct((1, c), F32),
                   jax.ShapeDtypeStruct((1, c), F32), jax.ShapeDtypeStruct((1, c), F32)],
        scratch_shapes=[pltpu.VMEM((CONV_HALO + tm, c), F32), pltpu.VMEM((tm + CONV_HALO, c), F32)],
        compiler_params=_params("arbitrary"),
    )(proj, proj, proj, proj, cpre, cpre, du2, du2, conv_w, ln_g, ln_b)


def _split3_dot(x, tri):
    hi = x.astype(BF16)
    r1 = x - hi.astype(F32)
    mid = r1.astype(BF16)
    lo = (r1 - mid.astype(F32)).astype(BF16)
    dot = functools.partial(jnp.dot, preferred_element_type=F32)
    return dot(hi, tri) + dot(mid, tri) + dot(lo, tri)


def _to_blocks(a, nb, blk):
    return a.reshape(a.shape[0], nb, blk).transpose(1, 0, 2)


def _from_blocks(a):
    return a.transpose(1, 0, 2).reshape(a.shape[1], -1)


def _forget_fwd(f_t, b_col, nh, t):
    blk = _tile(t, CUM_BLOCK)
    nb = t // blk

    def body(f_ref, b_ref, c_ref):
        ri = lax.broadcasted_iota(jnp.int32, (blk, blk), 0)
        ci = lax.broadcasted_iota(jnp.int32, (blk, blk), 1)
        tri = (ri <= ci).astype(BF16)

        def step(bi, carry):
            xv = f_ref[bi] + b_ref[:, :1]
            lf = jnp.minimum(xv, 0.0) - jnp.log(1.0 + jnp.exp(-jnp.abs(xv)))
            cs = _split3_dot(lf, tri) + carry
            c_ref[bi] = cs
            return cs[:, blk - 1:blk]

        lax.fori_loop(0, nb, step, jnp.zeros((nh, 1), F32))

    out = pl.pallas_call(
        body,
        name="forget_fwd",
        out_shape=jax.ShapeDtypeStruct((nb, nh, blk), F32),
        compiler_params=pltpu.CompilerParams(vmem_limit_bytes=VMEM_LIMIT),
    )(_to_blocks(f_t, nb, blk), b_col)
    return _from_blocks(out)


def _forget_bwd(dc, f_t, b_col, nh, t):
    blk = _tile(t, CUM_BLOCK)
    nb = t // blk

    def body(dc_ref, f_ref, b_ref, df_ref, db_ref):
        ri = lax.broadcasted_iota(jnp.int32, (blk, blk), 0)
        ci = lax.broadcasted_iota(jnp.int32, (blk, blk), 1)
        tri = (ri >= ci).astype(BF16)

        def step(n, carry):
            tail, db = carry
            bi = nb - 1 - n
            rc = _split3_dot(dc_ref[bi], tri) + tail
            df = rc * _sig(-(f_ref[bi] + b_ref[:, :1]))
            df_ref[bi] = df
            return rc[:, 0:1], db + jnp.sum(df, axis=1, keepdims=True)

        _, db = lax.fori_loop(0, nb, step, (jnp.zeros((nh, 1), F32), jnp.zeros((nh, 1), F32)))
        db_ref[...] = jnp.broadcast_to(db, db_ref.shape)

    df, db = pl.pallas_call(
        body,
        name="forget_bwd",
        out_shape=[jax.ShapeDtypeStruct((nb, nh, blk), F32), jax.ShapeDtypeStruct((nh, LANES), F32)],
        compiler_params=pltpu.CompilerParams(vmem_limit_bytes=VMEM_LIMIT),
    )(_to_blocks(dc, nb, blk), _to_blocks(f_t, nb, blk), b_col)
    return _from_blocks(df), db


def _scores(q, k, cq_ref, ck_ref, scale, masked, tq, tk):
    s = lax.dot_general(q, k, _DIMS["nt"], preferred_element_type=F32) * scale
    s = s + cq_ref[0][:, :1] - ck_ref[0]
    if masked:
        row = lax.broadcasted_iota(jnp.int32, (tq, tk), 0)
        col = lax.broadcasted_iota(jnp.int32, (tq, tk), 1)
        s = jnp.where(col <= row, s, NEG_INF)
    return s


def _attn_fwd(proj, cq, ck, t, nh, hd, q_off, k_off, v_off):
    tq = _tile(t, ATTN_TILE)
    nq = t // tq
    scale = 1.0 / math.sqrt(hd)
    qb, kb, vb = q_off // hd, k_off // hd, v_off // hd

    def body(q_ref, k_ref, v_ref, cq_ref, ck_ref, o_ref, lse_ref, m_ref, l_ref, acc_ref):
        i, j = pl.program_id(1), pl.program_id(2)

        @pl.when(j == 0)
        def _():
            m_ref[...] = jnp.full(m_ref.shape, NEG_INF, F32)
            l_ref[...] = jnp.zeros(l_ref.shape, F32)
            acc_ref[...] = jnp.zeros(acc_ref.shape, F32)

        def step(masked):
            s = _scores(q_ref[...], k_ref[...], cq_ref, ck_ref, scale, masked, tq, tq)
            m_prev = m_ref[:, :1]
            m_new = jnp.maximum(m_prev, jnp.max(s, axis=1, keepdims=True))
            p = jnp.exp(s - m_new)
            alpha = jnp.exp(m_prev - m_new)
            l_ref[...] = alpha * l_ref[...] + jnp.sum(p, axis=1, keepdims=True)
            acc_ref[...] = alpha * acc_ref[...] + jnp.dot(p.astype(BF16), v_ref[...],
                                                          preferred_element_type=F32)
            m_ref[...] = jnp.broadcast_to(m_new, m_ref.shape)

        @pl.when(j < i)
        def _():
            step(False)

        @pl.when(j == i)
        def _():
            step(True)
            o_ref[...] = (acc_ref[...] / l_ref[:, :1]).astype(BF16)
            lse_ref[0] = m_ref[...] + jnp.log(l_ref[...])

    return pl.pallas_call(
        body,
        name="attn_fwd",
        grid=(nh, nq, nq),
        in_specs=[
            pl.BlockSpec((tq, hd), lambda h, i, j: (i, qb + h)),
            pl.BlockSpec((tq, hd), lambda h, i, j: (jnp.minimum(j, i), kb + h)),
            pl.BlockSpec((tq, hd), lambda h, i, j: (jnp.minimum(j, i), vb + h)),
            pl.BlockSpec((1, tq, LANES), lambda h, i, j: (h, i, 0)),
            pl.BlockSpec((1, 1, tq), lambda h, i, j: (h, 0, jnp.minimum(j, i))),
        ],
        out_specs=[pl.BlockSpec((tq, hd), lambda h, i, j: (i, h)),
                   pl.BlockSpec((1, tq, LANES), lambda h, i, j: (h, i, 0))],
        out_shape=[jax.ShapeDtypeStruct((t, nh * hd), BF16), jax.ShapeDtypeStruct((nh, t, LANES), F32)],
        scratch_shapes=[pltpu.VMEM((tq, LANES), F32), pltpu.VMEM((tq, LANES), F32), pltpu.VMEM((tq, hd), F32)],
        compiler_params=_params("parallel", "parallel", "arbitrary"),
    )(proj, proj, proj, cq, ck)


def _attn_bwd(proj, o, do, lse, cq, ck, t, nh, hd, q_off, k_off, v_off):
    tq = _tile(t, ATTN_TILE)
    nq = t // tq
    scale = 1.0 / math.sqrt(hd)
    qb, kb, vb = q_off // hd, k_off // hd, v_off // hd

    def body(q_ref, k_ref, v_ref, o_ref, do_ref, lse_ref, cq_ref, ck_ref,
             dq_ref, dk_ref, dv_ref, dck_ref, dcq_ref, dq_all, dcq_all, dk_acc, dv_acc, dck_acc):
        j, i = pl.program_id(1), pl.program_id(2)

        @pl.when((j == 0) & (i == 0))
        def _():
            dq_all[...] = jnp.zeros(dq_all.shape, F32)
            dcq_all[...] = jnp.zeros(dcq_all.shape, F32)

        @pl.when(i == 0)
        def _():
            dk_acc[...] = jnp.zeros(dk_acc.shape, F32)
            dv_acc[...] = jnp.zeros(dv_acc.shape, F32)
            dck_acc[...] = jnp.zeros(dck_acc.shape, F32)

        def step(masked):
            q, k, v, dov = q_ref[...], k_ref[...], v_ref[...], do_ref[...]
            s = _scores(q, k, cq_ref, ck_ref, scale, masked, tq, tq)
            p = jnp.exp(s - lse_ref[0][:, :1])
            delta = jnp.sum(dov.astype(F32) * o_ref[...].astype(F32), axis=1, keepdims=True)
            dp = lax.dot_general(dov, v, _DIMS["nt"], preferred_element_type=F32)
            ds = p * (dp - delta)
            dsb = ds.astype(BF16)
            dv_acc[...] += lax.dot_general(p.astype(BF16), dov, _DIMS["tn"], preferred_element_type=F32)
            dk_acc[...] += lax.dot_general(dsb, q, _DIMS["tn"], preferred_element_type=F32)
            dck_acc[...] -= jnp.sum(ds, axis=0, keepdims=True)
            rows = pl.ds(pl.multiple_of(i * tq, tq), tq)
            dq_new = dq_all[rows, :] + jnp.dot(dsb, k, preferred_element_type=F32)
            dq_all[rows, :] = dq_new
            dcq_new = dcq_all[rows, :] + jnp.sum(ds, axis=1, keepdims=True)
            dcq_all[rows, :] = dcq_new
            return dq_new, dcq_new

        @pl.when(i > j)
        def _():
            step(False)

        @pl.when(i == j)
        def _():
            dq_new, dcq_new = step(True)
            dq_ref[...] = (dq_new * scale).astype(BF16)
            dcq_ref[0] = dcq_new

        @pl.when(i == nq - 1)
        def _():
            dk_ref[...] = (dk_acc[...] * scale).astype(BF16)
            dv_ref[...] = dv_acc[...].astype(BF16)
            dck_ref[0] = dck_acc[...]

    qrow = lambda h, j, i: jnp.maximum(i, j)
    return pl.pallas_call(
        body,
        name="attn_bwd",
        grid=(nh, nq, nq),
        in_specs=[
            pl.BlockSpec((tq, hd), lambda h, j, i: (qrow(h, j, i), qb + h)),
            pl.BlockSpec((tq, hd), lambda h, j, i: (j, kb + h)),
            pl.BlockSpec((tq, hd), lambda h, j, i: (j, vb + h)),
            pl.BlockSpec((tq, hd), lambda h, j, i: (qrow(h, j, i), h)),
            pl.BlockSpec((tq, hd), lambda h, j, i: (qrow(h, j, i), h)),
            pl.BlockSpec((1, tq, LANES), lambda h, j, i: (h, qrow(h, j, i), 0)),
            pl.BlockSpec((1, tq, LANES), lambda h, j, i: (h, qrow(h, j, i), 0)),
            pl.BlockSpec((1, 1, tq), lambda h, j, i: (h, 0, j)),
        ],
        out_specs=[
            pl.BlockSpec((tq, hd), lambda h, j, i: (j, h)),
            pl.BlockSpec((tq, hd), lambda h, j, i: (j, h)),
            pl.BlockSpec((tq, hd), lambda h, j, i: (j, h)),
            pl.BlockSpec((1, 1, tq), lambda h, j, i: (h, 0, j)),
            pl.BlockSpec((1, tq, LANES), lambda h, j, i: (h, j, 0)),
        ],
        out_shape=[jax.ShapeDtypeStruct((t, nh * hd), BF16), jax.ShapeDtypeStruct((t, nh * hd), BF16),
                   jax.ShapeDtypeStruct((t, nh * hd), BF16), jax.ShapeDtypeStruct((nh, 1, t), F32),
                   jax.ShapeDtypeStruct((nh, t, LANES), F32)],
        scratch_shapes=[pltpu.VMEM((t, hd), F32), pltpu.VMEM((t, LANES), F32), pltpu.VMEM((tq, hd), F32),
                        pltpu.VMEM((tq, hd), F32), pltpu.VMEM((1, tq), F32)],
        compiler_params=_params("arbitrary", "arbitrary", "arbitrary"),
    )(proj, proj, proj, o, do, lse, cq, ck)


class _Dims:
    def __init__(self, t, d, c, nh, aw, f, pd):
        self.t, self.d, self.c, self.nh, self.aw, self.f, self.pd = t, d, c, nh, aw, f, pd
        self.hd = aw // nh
        self.a_off, self.g_off = 0, c
        self.q_off, self.k_off, self.v_off = 2 * c, 2 * c + aw, 2 * c + 2 * aw
        self.gc_off = 2 * c + 3 * aw
        self.ga_off = self.gc_off + d
        self.n_main = self.ga_off + d


def _ffn_tn(f):
    return _tile(f, 1536)


def _layer_fwd(dm, x, p, w):
    t, d, c, f = dm.t, dm.d, dm.c, dm.f
    s = {"x": x}
    h = _rms_fwd("rms_mix", x, w["n_mix"], t, d)
    proj = _mm("mm_in", "nn", h, w["main"], t, dm.n_main, d, BF16)
    fl = _mm("mm_forget", "nn", h, w["f"], t, LANES, d, F32)
    u2, cpre = _conv_fwd(proj, w["conv_w"], w["conv_b"], w["ln_g"], w["ln_b"], t, c, dm.a_off, dm.g_off)
    f_t = fl[:, :dm.nh].T
    cum = _forget_fwd(f_t, w["b_f"], dm.nh, t)
    cq = jnp.broadcast_to(cum[:, :, None], (dm.nh, t, LANES))
    ck = cum.reshape(dm.nh, 1, t)
    o, lse = _attn_fwd(proj, cq, ck, t, dm.nh, dm.hd, dm.q_off, dm.k_off, dm.v_off)

    def epi_conv(accs, ex):
        return accs[0], _sig(ex[0].astype(F32)) * accs[0]

    yc, m1 = _matmul("mm_conv_out", "nn", [u2], [w["co"]], [(0, 0, 0)], 1, t, d, c, [BF16, BF16],
                     epi=epi_conv, extras=[("mn", (proj, dm.gc_off))], tm=512)

    def epi_attn(accs, ex):
        return accs[0], ex[1].astype(F32) + _sig(ex[0].astype(F32)) * accs[0]

    ya, merged = _matmul("mm_attn_out", "nn", [o], [w["ao"]], [(0, 0, 0)], 1, t, d, dm.aw, [BF16, BF16],
                         epi=epi_attn, extras=[("mn", (proj, dm.ga_off)), ("mn", m1)], tm=512)
    x1 = _matmul("mm_out", "nn", [merged], [w["o"]], [(0, 0, 0)], 1, t, d, d, [F32],
                 epi=lambda accs, ex: [ex[0] + accs[0]], extras=[("mn", x)], tm=512)[0]

    hf = _rms_fwd("rms_ffn", x1, w["n_ffn"], t, d)

    def epi_glu(accs, ex):
        gate, up = accs
        return gate, up, gate * _sig(gate) * up

    gate, up, act = _matmul("mm_gate_up", "nn", [hf], [w["g"], w["u"]], [(0, 0, 0), (0, 1, 1)], 2, t, f, d,
                            [BF16, BF16, BF16], epi=epi_glu, tm=512, tn=_ffn_tn(f))
    x2 = _matmul("mm_down", "nn", [act], [w["d"]], [(0, 0, 0)], 1, t, d, f, [F32],
                 epi=lambda accs, ex: [ex[0] + accs[0]], extras=[("mn", x1)], tm=512)[0]

    hp = _rms_fwd("rms_ple", x2, w["n_ple"], t, d)
    pp = _mm("mm_ple_proj", "nn", p, w["pp"], t, d, dm.pd, BF16, tm=512)

    def epi_ple(accs, ex):
        sg = _sig(accs[0])
        return sg, ex[1] + sg * ex[0].astype(F32)

    sg, x3 = _matmul("mm_ple_gate", "nn", [hp], [w["pg"]], [(0, 0, 0)], 1, t, d, d, [BF16, F32],
                     epi=epi_ple, extras=[("mn", pp), ("mn", x2)], tm=512)
    s.update(h=h, proj=proj, f_t=f_t, cq=cq, ck=ck, u2=u2, cpre=cpre, o=o, lse=lse, yc=yc, ya=ya,
             merged=merged, x1=x1, hf=hf, gate=gate, up=up, act=act, x2=x2, hp=hp, pp=pp, sg=sg, p=p)
    return x3, s


def _layer_bwd(dm, dx3, dx3b, s, w):
    t, d, c, f = dm.t, dm.d, dm.c, dm.f
    g = {}

    def ple_ew(dxv, sgv, ppv):
        sgf, ppf = sgv.astype(F32), ppv.astype(F32)
        return dxv * sgf, dxv * ppf * sgf * (1.0 - sgf)

    d_pp, d_z = _ew("ple_bwd", ple_ew, [dx3, s["sg"], s["pp"]], t, d, [BF16, BF16], _tile(t, ROW_TILE, 8))
    g["pp"] = _mm("gw_ple_proj", "tn", s["p"], d_pp, dm.pd, d, t, F32)
    g["pg"] = _mm("gw_ple_gate", "tn", s["hp"], d_z, d, d, t, F32)
    d_hp = _mm("dx_ple_gate", "nt", d_z, w["pg"], t, d, d, BF16)
    dx2, dx2b, g["n_ple"] = _rms_bwd("rms_ple_bwd", d_hp, s["x2"], w["n_ple"], dx3, t, d)

    def epi_dglu(accs, ex):
        gate, up = ex[0].astype(F32), ex[1].astype(F32)
        sg = _sig(gate)
        return accs[0] * up * (sg * (1.0 + gate * (1.0 - sg))), accs[0] * gate * sg

    d_gate, d_up = _matmul("dx_down", "nt", [dx2b], [w["d"]], [(0, 0, 0)], 1, t, f, d, [BF16, BF16],
                           epi=epi_dglu, extras=[("mn", s["gate"]), ("mn", s["up"])], tm=512, tn=_ffn_tn(f))
    g["d"] = _mm("gw_down", "tn", s["act"], dx2b, f, d, t, F32, tm=_ffn_tn(f))
    g["g"] = _mm("gw_gate", "tn", s["hf"], d_gate, d, f, t, F32, tn=_ffn_tn(f))
    g["u"] = _mm("gw_up", "tn", s["hf"], d_up, d, f, t, F32, tn=_ffn_tn(f))
    d_hf = _matmul("dx_gate_up", "nt", [d_gate, d_up], [w["g"], w["u"]], [(0, 0, 0), (1, 1, 0)], 1, t, d, f,
                   [BF16])[0]
    dx1, dx1b, g["n_ffn"] = _rms_bwd("rms_ffn_bwd", d_hf, s["x1"], w["n_ffn"], dx2, t, d)

    g["o"] = _mm("gw_out", "tn", s["merged"], dx1b, d, d, t, F32)

    def epi_dmerge(accs, ex):
        dmv = accs[0]
        sgc, sga = _sig(ex[0].astype(F32)), _sig(ex[1].astype(F32))
        ycv, yav = ex[2].astype(F32), ex[3].astype(F32)
        return dmv * sgc, dmv * sga, dmv * ycv * sgc * (1.0 - sgc), dmv * yav * sga * (1.0 - sga)

    d_yc, d_ya, d_gc, d_ga = _matmul(
        "dx_out", "nt", [dx1b], [w["o"]], [(0, 0, 0)], 1, t, d, d, [BF16] * 4, epi=epi_dmerge,
        extras=[("mn", (s["proj"], dm.gc_off)), ("mn", (s["proj"], dm.ga_off)), ("mn", s["yc"]), ("mn", s["ya"])],
        tm=512, tn=_tile(d, 512))
    g["co"] = _mm("gw_conv_out", "tn", s["u2"], d_yc, c, d, t, F32)
    d_u2 = _mm("dx_conv_out", "nt", d_yc, w["co"], t, c, d, BF16)
    g["ao"] = _mm("gw_attn_out", "tn", s["o"], d_ya, dm.aw, d, t, F32)
    d_o = _mm("dx_attn_out", "nt", d_ya, w["ao"], t, dm.aw, d, BF16)

    dq, dk, dv, dck, dcq = _attn_bwd(s["proj"], s["o"], d_o, s["lse"], s["cq"], s["ck"], t, dm.nh, dm.hd,
                                dm.q_off, dm.k_off, dm.v_off)
    d_ft, g_bf = _forget_bwd(dck.reshape(dm.nh, t) + dcq[:, :, 0], s["f_t"], w["b_f"], dm.nh, t)
    g["b_f"] = g_bf[:, 0]
    d_f = jnp.pad(d_ft.T, ((0, 0), (0, LANES - dm.nh))).astype(BF16)

    d_a, d_gg, g["conv_w"], g["conv_b"], g["ln_g"], g["ln_b"] = _conv_bwd(
        s["proj"], s["cpre"], d_u2, w["conv_w"], w["ln_g"], w["ln_b"], t, c, dm.a_off, dm.g_off)

    d_proj = jnp.concatenate([d_a, d_gg, dq, dk, dv, d_gc, d_ga], axis=1)
    g["main"] = _mm("gw_in", "tn", s["h"], d_proj, d, dm.n_main, t, F32)
    g["f"] = _mm("gw_forget", "tn", s["h"], d_f, d, LANES, t, F32)
    d_h_f = _mm("dx_forget", "nt", d_f, w["f"], t, d, LANES, BF16)
    d_h = _matmul("dx_in", "nt", [d_proj], [w["main"]], [(0, 0, 0)], 1, t, d, dm.n_main, [BF16],
                  epi=lambda accs, ex: [accs[0] + ex[0].astype(F32)], extras=[("mn", d_h_f)])[0]
    dx, dxb, g["n_mix"] = _rms_bwd("rms_mix_bwd", d_h, s["x"], w["n_mix"], dx1, t, d)
    return dx, dxb, g


def _adamw_tiles(wv, gv, mv, vv):
    m_new = ADAM_B1 * mv + (1.0 - ADAM_B1) * gv
    v_new = ADAM_B2 * vv + (1.0 - ADAM_B2) * (gv * gv)
    m_hat = m_new / (1.0 - ADAM_B1 ** ADAM_STEP)
    v_hat = v_new / (1.0 - ADAM_B2 ** ADAM_STEP)
    delta = -ADAM_LR * (m_hat / (jnp.sqrt(v_hat) + ADAM_EPS) + ADAM_WD * wv)
    return delta, m_new, v_new


def _adamw(name, wv, gv, mv, vv):
    shape = wv.shape
    cols = shape[-1]
    rows = wv.size // cols
    tm = _tile(rows, max(8, (1 << 18) // cols), 8)
    flat = [a.reshape(rows, cols) for a in (wv, gv, mv, vv)]
    outs = _ew(name, _adamw_tiles, flat, rows, cols, [F32, F32, F32], tm)
    return [o.reshape(shape) for o in outs]


def _place():
    return lax.axis_index("x"), lax.axis_index("y"), lax.axis_index("c")


def _other_chips(x, y):
    return [(1 - x, y), (x, 1 - y), (1 - x, 1 - y)]


def _gather_weights(pack):
    rows, width = pack.shape
    half = rows // 2
    piece = half // GATHER_CHUNKS
    n_copy = 3 * GATHER_CHUNKS

    def body(pack_ref, out_ref, ici_send, ici_recv, d2d_send, d2d_recv, local_sem):
        x, y, c = _place()
        me = 2 * x + y
        sibling = (x, y, 1 - c)
        chips = _other_chips(x, y)

        def rows_of(chip, core_half, q):
            return out_ref.at[chip, pl.ds(core_half * half + q * piece, piece), :]

        mine = pltpu.make_async_copy(pack_ref, out_ref.at[me], local_sem)
        mine.start()
        sends = []
        for r, (px, py) in enumerate(chips):
            for q in range(GATHER_CHUNKS):
                n = r * GATHER_CHUNKS + q
                cp = pltpu.make_async_remote_copy(
                    src_ref=pack_ref.at[pl.ds(c * half + q * piece, piece), :], dst_ref=rows_of(me, c, q),
                    send_sem=ici_send.at[n], recv_sem=ici_recv.at[n], device_id=(px, py, c), device_id_type=MESH)
                cp.start()
                sends.append(cp)
        for r, (px, py) in enumerate(chips):
            chip = 2 * px + py
            for q in range(GATHER_CHUNKS):
                n = r * GATHER_CHUNKS + q
                landed = rows_of(chip, c, q)
                pltpu.make_async_remote_copy(
                    src_ref=landed, dst_ref=landed, send_sem=ici_send.at[n], recv_sem=ici_recv.at[n],
                    device_id=(px, py, c), device_id_type=MESH).wait_recv()
                fwd = pltpu.make_async_remote_copy(
                    src_ref=landed, dst_ref=landed, send_sem=d2d_send.at[n], recv_sem=d2d_recv.at[n],
                    device_id=sibling, device_id_type=MESH)
                fwd.start()
                sends.append(fwd)
        for r, (px, py) in enumerate(chips):
            chip = 2 * px + py
            for q in range(GATHER_CHUNKS):
                n = r * GATHER_CHUNKS + q
                got = rows_of(chip, 1 - c, q)
                pltpu.make_async_remote_copy(
                    src_ref=got, dst_ref=got, send_sem=d2d_send.at[n], recv_sem=d2d_recv.at[n],
                    device_id=sibling, device_id_type=MESH).wait_recv()
        for cp in sends:
            cp.wait_send()
        mine.wait()

    return pl.pallas_call(
        body,
        name="gather_weights",
        in_specs=[ANY],
        out_specs=ANY,
        out_shape=jax.ShapeDtypeStruct((N_CHIPS, rows, width), pack.dtype),
        scratch_shapes=[pltpu.SemaphoreType.DMA((n_copy,)), pltpu.SemaphoreType.DMA((n_copy,)),
                        pltpu.SemaphoreType.DMA((n_copy,)), pltpu.SemaphoreType.DMA((n_copy,)),
                        pltpu.SemaphoreType.DMA],
    )(pack)


def _swap_with_sibling(name, send):
    def body(send_ref, got_ref, send_sem, recv_sem):
        x, y, c = _place()
        cp = pltpu.make_async_remote_copy(src_ref=send_ref, dst_ref=got_ref, send_sem=send_sem, recv_sem=recv_sem,
                                          device_id=(x, y, 1 - c), device_id_type=MESH)
        cp.start()
        cp.wait()

    return pl.pallas_call(
        body,
        name=name,
        in_specs=[ANY],
        out_specs=ANY,
        out_shape=jax.ShapeDtypeStruct(send.shape, send.dtype),
        scratch_shapes=[pltpu.SemaphoreType.DMA, pltpu.SemaphoreType.DMA],
    )(send)


def _exchange_chips(parts):
    def body(parts_ref, got_ref, send_sems, recv_sems, local_sem):
        x, y, c = _place()
        me = 2 * x + y
        mine = pltpu.make_async_copy(parts_ref.at[me], got_ref.at[me], local_sem)
        mine.start()
        sends = []
        for r, (px, py) in enumerate(_other_chips(x, y)):
            cp = pltpu.make_async_remote_copy(
                src_ref=parts_ref.at[2 * px + py], dst_ref=got_ref.at[me], send_sem=send_sems.at[r],
                recv_sem=recv_sems.at[r], device_id=(px, py, c), device_id_type=MESH)
            cp.start()
            sends.append(cp)
        for r, (px, py) in enumerate(_other_chips(x, y)):
            got = got_ref.at[2 * px + py]
            pltpu.make_async_remote_copy(
                src_ref=got, dst_ref=got, send_sem=send_sems.at[r], recv_sem=recv_sems.at[r],
                device_id=(px, py, c), device_id_type=MESH).wait_recv()
        for cp in sends:
            cp.wait_send()
        mine.wait()

    return pl.pallas_call(
        body,
        name="exchange_chips",
        in_specs=[ANY],
        out_specs=ANY,
        out_shape=jax.ShapeDtypeStruct(parts.shape, parts.dtype),
        scratch_shapes=[pltpu.SemaphoreType.DMA((3,)), pltpu.SemaphoreType.DMA((3,)), pltpu.SemaphoreType.DMA],
    )(parts)


def _allreduce_small(v):
    rows, width = v.shape

    def body(v_ref, out_ref, slots, send_sems, recv_sems):
        x, y, c = _place()
        me = 4 * x + 2 * y + c
        slots[me] = v_ref[...]
        peers = [(x, y, 1 - c)]
        for px, py in _other_chips(x, y):
            peers += [(px, py, c), (px, py, 1 - c)]
        sends = []
        for r, peer in enumerate(peers):
            cp = pltpu.make_async_remote_copy(
                src_ref=v_ref, dst_ref=slots.at[me], send_sem=send_sems.at[r], recv_sem=recv_sems.at[r],
                device_id=peer, device_id_type=MESH)
            cp.start()
            sends.append(cp)
        for r, (px, py, pc) in enumerate(peers):
            got = slots.at[4 * px + 2 * py + pc]
            pltpu.make_async_remote_copy(
                src_ref=got, dst_ref=got, send_sem=send_sems.at[r], recv_sem=recv_sems.at[r],
                device_id=(px, py, pc), device_id_type=MESH).wait_recv()
        for cp in sends:
            cp.wait_send()
        total = slots[0]
        for n in range(1, 8):
            total = total + slots[n]
        out_ref[...] = total

    vm = pl.BlockSpec(memory_space=pltpu.VMEM)
    return pl.pallas_call(
        body,
        name="allreduce_small",
        in_specs=[vm],
        out_specs=vm,
        out_shape=jax.ShapeDtypeStruct((rows, width), F32),
        scratch_shapes=[pltpu.VMEM((8, rows, width), F32), pltpu.SemaphoreType.DMA((7,)),
                        pltpu.SemaphoreType.DMA((7,))],
    )(v)


BIG = (("w_in", 1), ("w_conv_out", 1), ("w_attn_out", 1), ("w_out", 0), ("w_gate_up", 1), ("w_down", 0),
       ("w_ple_gate", 0), ("w_ple_proj", 1))


def _pad_rows(flat, row_align):
    n = flat.shape[0]
    rows = -(-n // PACK_W)
    rows = -(-rows // row_align) * row_align
    return jnp.pad(flat, (0, rows * PACK_W - n)).reshape(rows, PACK_W)


def _unpack(buf, shapes):
    flat = buf.reshape(-1)
    out, off = [], 0
    for shp in shapes:
        n = math.prod(shp)
        out.append(flat[off:off + n].reshape(shp))
        off += n
    return out


def _add_halves(mine, got, rows, out_dtype):
    return _ew("add_core_partials", lambda a, b: [a + b], [mine, got], rows, PACK_W, [out_dtype],
               _tile(rows, 256, 16))[0]


def _sum_chips(got, rows):
    def fn(*parts):
        total = parts[0].astype(F32)
        for part in parts[1:]:
            total = total + part.astype(F32)
        return [total]

    return _ew("sum_chip_partials", fn, [got[k] for k in range(N_CHIPS)], rows, PACK_W, [F32],
               _tile(rows, 256, 16))[0]


def kernel(x, p, norm_mix_g, w_in, b_forget, conv_w, conv_b, conv_ln_g, conv_ln_b, w_conv_out, w_attn_out, w_out, norm_ffn_g, w_gate_up, w_down, norm_ple_g, w_ple_gate, w_ple_proj, final_g, loss_target, m_norm_mix_g, m_w_in, m_b_forget, m_conv_w, m_conv_b, m_conv_ln_g, m_conv_ln_b, m_w_conv_out, m_w_attn_out, m_w_out, m_norm_ffn_g, m_w_gate_up, m_w_down, m_norm_ple_g, m_w_ple_gate, m_w_ple_proj, m_final_g, v_norm_mix_g, v_w_in, v_b_forget, v_conv_w, v_conv_b, v_conv_ln_g, v_conv_ln_b, v_w_conv_out, v_w_attn_out, v_w_out, v_norm_ffn_g, v_w_gate_up, v_w_down, v_norm_ple_g, v_w_ple_gate, v_w_ple_proj, v_final_g):
    wts = dict(norm_mix_g=norm_mix_g, w_in=w_in, b_forget=b_forget, conv_w=conv_w, conv_b=conv_b,
               conv_ln_g=conv_ln_g, conv_ln_b=conv_ln_b, w_conv_out=w_conv_out, w_attn_out=w_attn_out,
               w_out=w_out, norm_ffn_g=norm_ffn_g, w_gate_up=w_gate_up, w_down=w_down, norm_ple_g=norm_ple_g,
               w_ple_gate=w_ple_gate, w_ple_proj=w_ple_proj, final_g=final_g)
    mom1 = dict(norm_mix_g=m_norm_mix_g, w_in=m_w_in, b_forget=m_b_forget, conv_w=m_conv_w, conv_b=m_conv_b,
                conv_ln_g=m_conv_ln_g, conv_ln_b=m_conv_ln_b, w_conv_out=m_w_conv_out, w_attn_out=m_w_attn_out,
                w_out=m_w_out, norm_ffn_g=m_norm_ffn_g, w_gate_up=m_w_gate_up, w_down=m_w_down,
                norm_ple_g=m_norm_ple_g, w_ple_gate=m_w_ple_gate, w_ple_proj=m_w_ple_proj, final_g=m_final_g)
    mom2 = dict(norm_mix_g=v_norm_mix_g, w_in=v_w_in, b_forget=v_b_forget, conv_w=v_conv_w, conv_b=v_conv_b,
                conv_ln_g=v_conv_ln_g, conv_ln_b=v_conv_ln_b, w_conv_out=v_w_conv_out, w_attn_out=v_w_attn_out,
                w_out=v_w_out, norm_ffn_g=v_norm_ffn_g, w_gate_up=v_w_gate_up, w_down=v_w_down,
                norm_ple_g=v_norm_ple_g, w_ple_gate=v_w_ple_gate, w_ple_proj=v_w_ple_proj, final_g=v_final_g)
    order = list(wts)
    depth = w_in.shape[0]
    t, d = x.shape[1], x.shape[2]
    c = conv_ln_g.shape[1]
    nh = b_forget.shape[1]
    aw = w_attn_out.shape[1]
    f = N_CHIPS * w_down.shape[1]
    pd = w_ple_proj.shape[1]
    dm = _Dims(t, d, c, nh, aw, f, pd)
    n_split = 2 * c + 3 * aw
    cw = conv_w.shape[2]
    chip = 2 * lax.axis_index("x") + lax.axis_index("y")
    core = lax.axis_index("c")

    shard_shapes = [wts[n].shape for n, _ in BIG] + [conv_w.shape + (2,)]
    flat = [wts[n].astype(BF16).reshape(-1) for n, _ in BIG]
    flat.append(lax.bitcast_convert_type(conv_w, BF16).reshape(-1))
    pack = _pad_rows(jnp.concatenate(flat), PACK_ROW_ALIGN)
    gathered = _gather_weights(pack)
    per_chip = [_unpack(gathered[k], shard_shapes) for k in range(N_CHIPS)]
    full = {}
    for n, (name, axis) in enumerate(BIG):
        full[name] = jnp.concatenate([per_chip[k][n] for k in range(N_CHIPS)], axis=axis + 1)
    conv_w_full = jnp.concatenate(
        [lax.bitcast_convert_type(per_chip[k][len(BIG)], F32) for k in range(N_CHIPS)], axis=2)

    def layer_weights(l):
        wi = full["w_in"][l]
        return {
            "main": jnp.concatenate([wi[:, :n_split], wi[:, n_split + nh:]], axis=1),
            "f": jnp.pad(wi[:, n_split:n_split + nh], ((0, 0), (0, LANES - nh))),
            "conv_w": jnp.pad(conv_w_full[l], ((0, CONV_HALO - CONV_K), (0, 0))),
            "conv_b": conv_b[l][None], "ln_g": conv_ln_g[l][None], "ln_b": conv_ln_b[l][None],
            "b_f": jnp.broadcast_to(b_forget[l][:, None], (nh, LANES)),
            "co": full["w_conv_out"][l], "ao": full["w_attn_out"][l], "o": full["w_out"][l],
            "g": full["w_gate_up"][l][:, :f], "u": full["w_gate_up"][l][:, f:], "d": full["w_down"][l],
            "pg": full["w_ple_gate"][l], "pp": full["w_ple_proj"][l],
            "n_mix": norm_mix_g[l][None], "n_ffn": norm_ffn_g[l][None], "n_ple": norm_ple_g[l][None],
        }

    lw = [layer_weights(l) for l in range(depth)]

    xl = x[0]
    saved = []
    for l in range(depth):
        xl, s = _layer_fwd(dm, xl, p[l, 0], lw[l])
        saved.append(s)
    loss_row, dx, dxb, g_final = _loss_head(xl, final_g[None], loss_target[0], t, d)
    loss = lax.psum(loss_row[0, 0], ("x", "y", "c"))
    lg = [None] * depth
    for l in reversed(range(depth)):
        dx, dxb, lg[l] = _layer_bwd(dm, dx, dxb, saved[l], lw[l])

    def stacked(key):
        return jnp.stack([lg[l][key] for l in range(depth)])

    g_main, g_f = stacked("main"), stacked("f")
    grads_full = {
        "w_in": jnp.concatenate([g_main[:, :, :n_split], g_f[:, :, :nh], g_main[:, :, n_split:]], axis=2),
        "w_conv_out": stacked("co"), "w_attn_out": stacked("ao"), "w_out": stacked("o"),
        "w_gate_up": jnp.concatenate([stacked("g"), stacked("u")], axis=2), "w_down": stacked("d"),
        "w_ple_gate": stacked("pg"), "w_ple_proj": stacked("pp"),
    }

    def for_chip(k):
        parts = []
        for name, axis in BIG:
            gfull = grads_full[name]
            size = gfull.shape[axis + 1] // N_CHIPS
            parts.append(lax.slice_in_dim(gfull, k * size, (k + 1) * size, axis=axis + 1).reshape(-1))
        return _pad_rows(jnp.concatenate(parts), PACK_ROW_ALIGN)

    gpack = jnp.stack([for_chip(k) for k in range(N_CHIPS)])
    rows = gpack.shape[1]
    half = rows // 2
    keep = lax.dynamic_slice_in_dim(gpack, core * half, half, axis=1).reshape(N_CHIPS * half, PACK_W)
    give = lax.dynamic_slice_in_dim(gpack, (1 - core) * half, half, axis=1).reshape(N_CHIPS * half, PACK_W)
    got = _swap_with_sibling("swap_core_partials", give)
    chip_part = _add_halves(keep, got, N_CHIPS * half, BF16).reshape(N_CHIPS, half, PACK_W)
    from_chips = _exchange_chips(chip_part)
    my_half = _sum_chips(from_chips, half)
    other_half = _swap_with_sibling("swap_reduced_halves", my_half)
    lower = jnp.where(core == 0, my_half, other_half)
    upper = jnp.where(core == 0, other_half, my_half)
    gshard = _unpack(jnp.concatenate([lower, upper]), [wts[n].shape for n, _ in BIG])
    grads = {name: gshard[n] for n, (name, _) in enumerate(BIG)}

    small = ["norm_mix_g", "b_forget", "conv_b", "conv_ln_g", "conv_ln_b", "norm_ffn_g", "norm_ple_g", "final_g"]
    small_g = {
        "norm_mix_g": jnp.concatenate([lg[l]["n_mix"] for l in range(depth)]),
        "b_forget": stacked("b_f"),
        "conv_b": jnp.concatenate([lg[l]["conv_b"] for l in range(depth)]),
        "conv_ln_g": jnp.concatenate([lg[l]["ln_g"] for l in range(depth)]),
        "conv_ln_b": jnp.concatenate([lg[l]["ln_b"] for l in range(depth)]),
        "norm_ffn_g": jnp.concatenate([lg[l]["n_ffn"] for l in range(depth)]),
        "norm_ple_g": jnp.concatenate([lg[l]["n_ple"] for l in range(depth)]),
        "final_g": g_final[0],
    }
    conv_w_g = jnp.stack([lg[l]["conv_w"][:CONV_K] for l in range(depth)])
    small_shapes = [wts[n].shape for n in small] + [conv_w_g.shape]
    small_pack = _pad_rows(jnp.concatenate([small_g[n].reshape(-1) for n in small] + [conv_w_g.reshape(-1)]), 8)
    small_sum = _unpack(_allreduce_small(small_pack), small_shapes)
    for n, name in enumerate(small):
        grads[name] = small_sum[n]
    grads["conv_w"] = lax.dynamic_slice_in_dim(small_sum[len(small)], chip * cw, cw, axis=2)

    def pack_small(src):
        return _pad_rows(jnp.concatenate([src[n].reshape(-1) for n in small]), 8)

    small_upd = _adamw("adamw_small", pack_small(wts), pack_small(grads), pack_small(mom1), pack_small(mom2))
    small_upd = [_unpack(u, [wts[n].shape for n in small]) for u in small_upd]
    delta, new_m, new_v = {}, {}, {}
    for n, name in enumerate(small):
        delta[name], new_m[name], new_v[name] = small_upd[0][n], small_upd[1][n], small_upd[2][n]
    for name in [n for n, _ in BIG] + ["conv_w"]:
        delta[name], new_m[name], new_v[name] = _adamw("adamw_" + name, wts[name], grads[name], mom1[name],
                                                       mom2[name])

    return (loss, dx[None], *[grads[n] for n in order], *[delta[n] for n in order],
            *[new_m[n] for n in order], *[new_v[n] for n in order])
```

```python
import functools
import math

import jax
import jax.numpy as jnp
from jax import lax
from jax.experimental import pallas as pl
from jax.experimental.pallas import tpu as pltpu

F32 = jnp.float32
BF16 = jnp.bfloat16

EPS = 1e-6
CONV_K = 31
NEG_INF = -1e30
ADAM_LR = 0.001
ADAM_B1 = 0.9
ADAM_B2 = 0.999
ADAM_EPS = 1e-08
ADAM_WD = 0.01
ADAM_STEP = 10

LANES = 128
VMEM_LIMIT = 60 * 1024 * 1024
PACK_W = 1024
N_CHIPS = 4
CONV_HALO = 32
CUM_BLOCK = 256

MM_TM = 1024
MM_TN = 1024
MM_TK = 512
ROW_TILE = 256
CONV_TILE = 256
ATTN_TILE = 512
ATTN_SUB = 256

MESH = pl.DeviceIdType.MESH
ANY = pl.BlockSpec(memory_space=pl.ANY)


def _params(*sem):
    return pltpu.CompilerParams(dimension_semantics=sem, vmem_limit_bytes=VMEM_LIMIT)


def _tile(dim, pref, align=LANES):
    if dim <= pref:
        return dim
    t = (pref // align) * align
    while t >= align:
        if dim % t == 0:
            return t
        t -= align
    return dim


def _sig(x):
    return 1.0 / (1.0 + jnp.exp(-x))


def _op(a):
    if not isinstance(a, tuple):
        return a, 0, ()
    return a if len(a) == 3 else (a[0], a[1], ())


def _spec(block, index, lead):
    if not lead:
        return pl.BlockSpec(block, index)
    return pl.BlockSpec((None,) * len(lead) + block, lambda *g: tuple(lead) + index(*g))


_DIMS = {
    "nn": (((1,), (0,)), ((), ())),
    "nt": (((1,), (1,)), ((), ())),
    "tn": (((0,), (0,)), ((), ())),
}


def _matmul(name, mode, a_ops, b_ops, terms, n_acc, m, n, k, out_dtypes, epi=None, extras=(),
            tm=None, tn=None, tk=None):
    tm = tm or _tile(m, MM_TM)
    tn = tn or _tile(n, MM_TN)
    tk = tk or _tile(k, MM_TK)
    nk = k // tk
    a_ops = [_op(a) for a in a_ops]
    b_ops = [_op(b) for b in b_ops]
    extras = [(kind, _op(e)) for kind, e in extras]
    na, nb, ne, no = len(a_ops), len(b_ops), len(extras), len(out_dtypes)

    def a_spec(off, lead):
        if mode == "tn":
            assert off % tm == 0
            return _spec((tk, tm), lambda i, j, kk: (kk, i + off // tm), lead)
        assert off % tk == 0
        return _spec((tm, tk), lambda i, j, kk: (i, kk + off // tk), lead)

    def b_spec(off, lead):
        if mode == "nt":
            assert off % tk == 0
            return _spec((tn, tk), lambda i, j, kk: (j, kk + off // tk), lead)
        assert off % tn == 0
        return _spec((tk, tn), lambda i, j, kk: (kk, j + off // tn), lead)

    def e_spec(kind, off, lead):
        assert off % tn == 0
        if kind == "n":
            return _spec((1, tn), lambda i, j, kk: (0, j + off // tn), lead)
        return _spec((tm, tn), lambda i, j, kk: (i, j + off // tn), lead)

    def body(*refs):
        a_refs = refs[:na]
        b_refs = refs[na:na + nb]
        e_refs = refs[na + nb:na + nb + ne]
        o_refs = refs[na + nb + ne:na + nb + ne + no]
        acc_refs = refs[na + nb + ne + no:]
        kk = pl.program_id(2)

        @pl.when(kk == 0)
        def _():
            for acc in acc_refs:
                acc[...] = jnp.zeros(acc.shape, F32)

        av = [r[...].astype(BF16) for r in a_refs]
        bv = [r[...].astype(BF16) for r in b_refs]
        for ai, bi, ci in terms:
            acc_refs[ci][...] += lax.dot_general(av[ai], bv[bi], _DIMS[mode], preferred_element_type=F32)

        @pl.when(kk == nk - 1)
        def _():
            accs = [acc[...] for acc in acc_refs]
            outs = epi(accs, [e[...] for e in e_refs]) if epi is not None else accs
            for o, val in zip(o_refs, outs):
                o[...] = val.astype(o.dtype)

    outs = pl.pallas_call(
        body,
        name=name,
        grid=(m // tm, n // tn, nk),
        in_specs=[a_spec(off, lead) for _, off, lead in a_ops] + [b_spec(off, lead) for _, off, lead in b_ops]
        + [e_spec(kind, off, lead) for kind, (_, off, lead) in extras],
        out_specs=[pl.BlockSpec((tm, tn), lambda i, j, kk: (i, j)) for _ in out_dtypes],
        out_shape=[jax.ShapeDtypeStruct((m, n), dt) for dt in out_dtypes],
        scratch_shapes=[pltpu.VMEM((tm, tn), F32) for _ in range(n_acc)],
        compiler_params=_params("parallel", "parallel", "arbitrary"),
    )(*[a[0] for a in a_ops], *[b[0] for b in b_ops], *[e[0] for _, e in extras])
    return outs


def _mm(name, mode, a, b, m, n, k, out_dtype, **kw):
    return _matmul(name, mode, [a], [b], [(0, 0, 0)], 1, m, n, k, [out_dtype], **kw)[0]


def _ew(name, fn, ins, m, n, out_dtypes, tm, tn=None):
    tn = tn or n
    ins = [_op(a) for a in ins]
    ni = len(ins)

    def spec(off, lead):
        assert off % tn == 0
        return _spec((tm, tn), lambda i, j: (i, j + off // tn), lead)

    def body(*refs):
        outs = fn(*[r[...] for r in refs[:ni]])
        for o, val in zip(refs[ni:], outs):
            o[...] = val.astype(o.dtype)

    return pl.pallas_call(
        body,
        name=name,
        grid=(m // tm, n // tn),
        in_specs=[spec(off, lead) for _, off, lead in ins],
        out_specs=[pl.BlockSpec((tm, tn), lambda i, j: (i, j)) for _ in out_dtypes],
        out_shape=[jax.ShapeDtypeStruct((m, n), dt) for dt in out_dtypes],
        compiler_params=_params("parallel", "parallel"),
    )(*[a[0] for a in ins])


def _rms_fwd(name, x, g, t, d):
    tr = _tile(t, ROW_TILE, 8)

    def body(x_ref, g_ref, h_ref):
        xv = x_ref[...]
        r = lax.rsqrt(jnp.mean(xv * xv, axis=1, keepdims=True) + EPS)
        h_ref[...] = (xv * r * g_ref[...]).astype(BF16)

    return pl.pallas_call(
        body,
        name=name,
        grid=(t // tr,),
        in_specs=[pl.BlockSpec((tr, d), lambda i: (i, 0)), pl.BlockSpec((1, d), lambda i: (0, 0))],
        out_specs=pl.BlockSpec((tr, d), lambda i: (i, 0)),
        out_shape=jax.ShapeDtypeStruct((t, d), BF16),
        compiler_params=_params("parallel"),
    )(x, g)


def _rms_bwd_rows(dh, xv, g):
    r = lax.rsqrt(jnp.mean(xv * xv, axis=1, keepdims=True) + EPS)
    xhat = xv * r
    dxh = dh * g
    dx = r * (dxh - xhat * jnp.mean(dxh * xhat, axis=1, keepdims=True))
    return dx, dh * xhat


def _rms_bwd(name, dh, x, g, dres, t, d):
    tr = _tile(t, ROW_TILE, 8)

    def body(dh_ref, x_ref, g_ref, dres_ref, dx_ref, dxb_ref, dg_ref):
        @pl.when(pl.program_id(0) == 0)
        def _():
            dg_ref[...] = jnp.zeros(dg_ref.shape, F32)

        dx, dg_rows = _rms_bwd_rows(dh_ref[...].astype(F32), x_ref[...], g_ref[...])
        dx = dx + dres_ref[...]
        dx_ref[...] = dx
        dxb_ref[...] = dx.astype(BF16)
        dg_ref[...] += jnp.sum(dg_rows, axis=0, keepdims=True)

    row = pl.BlockSpec((tr, d), lambda i: (i, 0))
    vec = pl.BlockSpec((1, d), lambda i: (0, 0))
    return pl.pallas_call(
        body,
        name=name,
        grid=(t // tr,),
        in_specs=[row, row, vec, row],
        out_specs=[row, row, vec],
        out_shape=[jax.ShapeDtypeStruct((t, d), F32), jax.ShapeDtypeStruct((t, d), BF16),
                   jax.ShapeDtypeStruct((1, d), F32)],
        compiler_params=_params("arbitrary"),
    )(dh, x, g, dres)


def _loss_head(x, g, target, t, d):
    tr = _tile(t, ROW_TILE, 8)

    def body(x_ref, g_ref, tgt_ref, loss_ref, dx_ref, dxb_ref, dg_ref):
        @pl.when(pl.program_id(0) == 0)
        def _():
            dg_ref[...] = jnp.zeros(dg_ref.shape, F32)
            loss_ref[...] = jnp.zeros(loss_ref.shape, F32)

        xv = x_ref[...]
        gv = g_ref[...]
        r = lax.rsqrt(jnp.mean(xv * xv, axis=1, keepdims=True) + EPS)
        err = xv * r * gv - tgt_ref[...]
        loss_ref[...] += (0.5 / d) * jnp.sum(err * err)
        dx, dg_rows = _rms_bwd_rows(err * (1.0 / d), xv, gv)
        dx_ref[...] = dx
        dxb_ref[...] = dx.astype(BF16)
        dg_ref[...] += jnp.sum(dg_rows, axis=0, keepdims=True)

    row = pl.BlockSpec((tr, d), lambda i: (i, 0))
    vec = pl.BlockSpec((1, d), lambda i: (0, 0))
    one = pl.BlockSpec((1, LANES), lambda i: (0, 0))
    return pl.pallas_call(
        body,
        name="loss_head",
        grid=(t // tr,),
        in_specs=[row, vec, row],
        out_specs=[one, row, row, vec],
        out_shape=[jax.ShapeDtypeStruct((1, LANES), F32), jax.ShapeDtypeStruct((t, d), F32),
                   jax.ShapeDtypeStruct((t, d), BF16), jax.ShapeDtypeStruct((1, d), F32)],
        compiler_params=_params("arbitrary"),
    )(x, g, target)


def _layernorm_rows(cv, g, b):
    mu = jnp.mean(cv, axis=1, keepdims=True)
    xc = cv - mu
    rstd = lax.rsqrt(jnp.mean(xc * xc, axis=1, keepdims=True) + EPS)
    xhat = xc * rstd
    return xhat, rstd, xhat * g + b


def _conv_fwd(proj, conv_w, conv_b, ln_g, ln_b, t, c, a_off, g_off):
    tm = _tile(t, CONV_TILE, CONV_HALO)
    per = tm // CONV_HALO
    ab, gb = a_off // c, g_off // c

    def body(a_ref, g_ref, ap_ref, gp_ref, w_ref, cb_ref, lg_ref, lb_ref, u2_ref, c_ref, upad):
        i = pl.program_id(0)
        u_prev = ap_ref[...].astype(F32) * _sig(gp_ref[...].astype(F32))
        upad[pl.ds(0, CONV_HALO), :] = jnp.where(i > 0, u_prev, 0.0)
        upad[pl.ds(CONV_HALO, tm), :] = a_ref[...].astype(F32) * _sig(g_ref[...].astype(F32))
        acc = jnp.zeros((tm, c), F32) + cb_ref[...]
        for k in range(CONV_K):
            acc = acc + w_ref[pl.ds(k, 1), :] * upad[pl.ds(CONV_HALO - (CONV_K - 1) + k, tm), :]
        c_ref[...] = acc
        _, _, z = _layernorm_rows(acc, lg_ref[...], lb_ref[...])
        u2_ref[...] = (z * _sig(z)).astype(BF16)

    vec = pl.BlockSpec((1, c), lambda i: (0, 0))
    return pl.pallas_call(
        body,
        name="conv_fwd",
        grid=(t // tm,),
        in_specs=[
            pl.BlockSpec((tm, c), lambda i: (i, ab)),
            pl.BlockSpec((tm, c), lambda i: (i, gb)),
            pl.BlockSpec((CONV_HALO, c), lambda i: (jnp.maximum(i * per - 1, 0), ab)),
            pl.BlockSpec((CONV_HALO, c), lambda i: (jnp.maximum(i * per - 1, 0), gb)),
            pl.BlockSpec((CONV_HALO, c), lambda i: (0, 0)), vec, vec, vec,
        ],
        out_specs=[pl.BlockSpec((tm, c), lambda i: (i, 0)), pl.BlockSpec((tm, c), lambda i: (i, 0))],
        out_shape=[jax.ShapeDtypeStruct((t, c), BF16), jax.ShapeDtypeStruct((t, c), F32)],
        scratch_shapes=[pltpu.VMEM((CONV_HALO + tm, c), F32)],
        compiler_params=_params("parallel"),
    )(proj, proj, proj, proj, conv_w, conv_b, ln_g, ln_b)


def _conv_bwd(proj, cpre, du2, conv_w, ln_g, ln_b, t, c, a_off, g_off):
    tm = _tile(t, CONV_TILE, CONV_HALO)
    per = tm // CONV_HALO
    nt = t // tm
    last_halo = t // CONV_HALO - 1
    ab, gb = a_off // c, g_off // c

    def body(a_ref, g_ref, ap_ref, gp_ref, c_ref, cn_ref, du_ref, dun_ref, w_ref, lg_ref, lb_ref,
             da_ref, dg_ref, gw_ref, gcb_ref, glg_ref, glb_ref, upad, dpad):
        i = pl.program_id(0)

        @pl.when(i == 0)
        def _():
            gw_ref[...] = jnp.zeros(gw_ref.shape, F32)
            gcb_ref[...] = jnp.zeros(gcb_ref.shape, F32)
            glg_ref[...] = jnp.zeros(glg_ref.shape, F32)
            glb_ref[...] = jnp.zeros(glb_ref.shape, F32)

        lg = lg_ref[...]
        lb = lb_ref[...]

        def ln_bwd(cv, duv):
            xhat, rstd, z = _layernorm_rows(cv, lg, lb)
            sz = _sig(z)
            dz = duv * (sz * (1.0 + z * (1.0 - sz)))
            dxh = dz * lg
            dc = rstd * (dxh - jnp.mean(dxh, axis=1, keepdims=True)
                         - xhat * jnp.mean(dxh * xhat, axis=1, keepdims=True))
            return dc, dz, xhat

        dc, dz, xhat = ln_bwd(c_ref[...], du_ref[...].astype(F32))
        dc_next, _, _ = ln_bwd(cn_ref[...], dun_ref[...].astype(F32))
        glg_ref[...] += jnp.sum(dz * xhat, axis=0, keepdims=True)
        glb_ref[...] += jnp.sum(dz, axis=0, keepdims=True)
        gcb_ref[...] += jnp.sum(dc, axis=0, keepdims=True)
        dpad[pl.ds(0, tm), :] = dc
        dpad[pl.ds(tm, CONV_HALO), :] = jnp.where(i < nt - 1, dc_next, 0.0)

        av = a_ref[...].astype(F32)
        sg = _sig(g_ref[...].astype(F32))
        u_prev = ap_ref[...].astype(F32) * _sig(gp_ref[...].astype(F32))
        upad[pl.ds(0, CONV_HALO), :] = jnp.where(i > 0, u_prev, 0.0)
        upad[pl.ds(CONV_HALO, tm), :] = av * sg

        du = jnp.zeros((tm, c), F32)
        for k in range(CONV_K):
            du = du + w_ref[pl.ds(k, 1), :] * dpad[pl.ds(CONV_K - 1 - k, tm), :]
            gw_ref[pl.ds(k, 1), :] += jnp.sum(
                dc * upad[pl.ds(CONV_HALO - (CONV_K - 1) + k, tm), :], axis=0, keepdims=True)
        da_ref[...] = (du * sg).astype(BF16)
        dg_ref[...] = (du * av * sg * (1.0 - sg)).astype(BF16)

    vec = pl.BlockSpec((1, c), lambda i: (0, 0))
    cur = pl.BlockSpec((tm, c), lambda i: (i, 0))
    nxt = pl.BlockSpec((CONV_HALO, c), lambda i: (jnp.minimum((i + 1) * per, last_halo), 0))
    wsp = pl.BlockSpec((CONV_HALO, c), lambda i: (0, 0))
    return pl.pallas_call(
        body,
        name="conv_bwd",
        grid=(nt,),
        in_specs=[
            pl.BlockSpec((tm, c), lambda i: (i, ab)),
            pl.BlockSpec((tm, c), lambda i: (i, gb)),
            pl.BlockSpec((CONV_HALO, c), lambda i: (jnp.maximum(i * per - 1, 0), ab)),
            pl.BlockSpec((CONV_HALO, c), lambda i: (jnp.maximum(i * per - 1, 0), gb)),
            cur, nxt, cur, nxt, wsp, vec, vec,
        ],
        out_specs=[cur, cur, wsp, vec, vec, vec],
        out_shape=[jax.ShapeDtypeStruct((t, c), BF16), jax.ShapeDtypeStruct((t, c), BF16),
                   jax.ShapeDtypeStruct((CONV_HALO, c), F32), jax.ShapeDtypeStruct((1, c), F32),
                   jax.ShapeDtypeStruct((1, c), F32), jax.ShapeDtypeStruct((1, c), F32)],
        scratch_shapes=[pltpu.VMEM((CONV_HALO + tm, c), F32), pltpu.VMEM((tm + CONV_HALO, c), F32)],
        compiler_params=_params("arbitrary"),
    )(proj, proj, proj, proj, cpre, cpre, du2, du2, conv_w, ln_g, ln_b)


def _split3(x):
    hi = x.astype(BF16).astype(F32)
    r1 = x - hi
    mid = r1.astype(BF16).astype(F32)
    lo = (r1 - mid).astype(BF16).astype(F32)
    return hi, mid, lo


def _split3_dot(x, tri):
    dot = functools.partial(jnp.dot, preferred_element_type=F32)
    hi, mid, lo = _split3(x)
    return dot(hi.astype(BF16), tri) + dot(mid.astype(BF16), tri) + dot(lo.astype(BF16), tri)


def _to_blocks(a, nb, blk):
    return a.reshape(a.shape[0], nb, blk).transpose(1, 0, 2)


def _from_blocks(a):
    return a.transpose(1, 0, 2).reshape(a.shape[1], -1)


def _forget_fwd(f_t, b_col, nh, t):
    blk = _tile(t, CUM_BLOCK)
    nb = t // blk

    def body(f_ref, b_ref, c_ref):
        ri = lax.broadcasted_iota(jnp.int32, (blk, blk), 0)
        ci = lax.broadcasted_iota(jnp.int32, (blk, blk), 1)
        tri = (ri <= ci).astype(BF16)

        def step(bi, carry):
            xv = f_ref[bi] + b_ref[:, :1]
            lf = jnp.minimum(xv, 0.0) - jnp.log(1.0 + jnp.exp(-jnp.abs(xv)))
            cs = _split3_dot(lf, tri) + carry
            c_ref[bi] = cs
            return cs[:, blk - 1:blk]

        lax.fori_loop(0, nb, step, jnp.zeros((nh, 1), F32))

    out = pl.pallas_call(
        body,
        name="forget_fwd",
        out_shape=jax.ShapeDtypeStruct((nb, nh, blk), F32),
        compiler_params=pltpu.CompilerParams(vmem_limit_bytes=VMEM_LIMIT),
    )(_to_blocks(f_t, nb, blk), b_col)
    return _from_blocks(out)


def _forget_bwd(dc, f_t, b_col, nh, t):
    blk = _tile(t, CUM_BLOCK)
    nb = t // blk

    def body(dc_ref, f_ref, b_ref, df_ref, db_ref):
        ri = lax.broadcasted_iota(jnp.int32, (blk, blk), 0)
        ci = lax.broadcasted_iota(jnp.int32, (blk, blk), 1)
        tri = (ri >= ci).astype(BF16)

        def step(n, carry):
            tail, db = carry
            bi = nb - 1 - n
            rc = _split3_dot(dc_ref[bi], tri) + tail
            df = rc * _sig(-(f_ref[bi] + b_ref[:, :1]))
            df_ref[bi] = df
            return rc[:, 0:1], db + jnp.sum(df, axis=1, keepdims=True)

        _, db = lax.fori_loop(0, nb, step, (jnp.zeros((nh, 1), F32), jnp.zeros((nh, 1), F32)))
        db_ref[...] = jnp.broadcast_to(db, db_ref.shape)

    df, db = pl.pallas_call(
        body,
        name="forget_bwd",
        out_shape=[jax.ShapeDtypeStruct((nb, nh, blk), F32), jax.ShapeDtypeStruct((nh, LANES), F32)],
        compiler_params=pltpu.CompilerParams(vmem_limit_bytes=VMEM_LIMIT),
    )(_to_blocks(dc, nb, blk), _to_blocks(f_t, nb, blk), b_col)
    return _from_blocks(df), db


def _lanes(parts, rows):
    lane = lax.broadcasted_iota(jnp.int32, (rows, LANES), 1)
    out = jnp.zeros((rows, LANES), F32)
    for n, part in enumerate(parts):
        out = jnp.where(lane == n, part, out)
    return out


def _attn_prep_fwd(proj, cs, t, nh, hd, q_off, k_off, v_off):
    tr = _tile(t, ATTN_TILE, 16)
    qb, kb, vb = q_off // hd, k_off // hd, v_off // hd

    def body(q_ref, k_ref, v_ref, cs_ref, qa_ref, ka_ref, va_ref):
        hi, mid, lo = _split3(cs_ref[0][:, :1])
        qa_ref[0, :, :hd] = q_ref[...]
        qa_ref[0, :, hd:] = _lanes([1.0, 1.0, 1.0, hi, mid, lo], tr).astype(BF16)
        ka_ref[0, :, :hd] = k_ref[...]
        ka_ref[0, :, hd:] = _lanes([-hi, -mid, -lo] + [1.0] * 6, tr).astype(BF16)
        va_ref[0, :, :hd] = v_ref[...]
        va_ref[0, :, hd:] = _lanes([-1.0, -1.0, -1.0], tr).astype(BF16)

    wide = pl.BlockSpec((1, tr, 2 * hd), lambda h, i: (h, i, 0))
    return pl.pallas_call(
        body,
        name="attn_prep_fwd",
        grid=(nh, t // tr),
        in_specs=[pl.BlockSpec((tr, hd), lambda h, i: (i, qb + h)), pl.BlockSpec((tr, hd), lambda h, i: (i, kb + h)),
                  pl.BlockSpec((tr, hd), lambda h, i: (i, vb + h)), pl.BlockSpec((1, tr, LANES), lambda h, i: (h, i, 0))],
        out_specs=[wide, wide, wide],
        out_shape=[jax.ShapeDtypeStruct((nh, t, 2 * hd), BF16)] * 3,
        compiler_params=_params("parallel", "parallel"),
    )(proj, proj, proj, cs)


def _attn_prep_bwd(qa, lse, o, do, t, nh, hd):
    tr = _tile(t, ATTN_TILE, 16)
    inv_scale = math.sqrt(hd)

    def body(qa_ref, lse_ref, o_ref, do_ref, qb_ref, da_ref):
        l_hi, l_mid, l_lo = _split3(lse_ref[0][:, :1] * (-inv_scale))
        lane = lax.broadcasted_iota(jnp.int32, (tr, LANES), 1)
        extra = qa_ref[0, :, hd:].astype(F32)
        extra = jnp.where(lane == 6, l_hi, jnp.where(lane == 7, l_mid, jnp.where(lane == 8, l_lo, extra)))
        qb_ref[0, :, :hd] = qa_ref[0, :, :hd]
        qb_ref[0, :, hd:] = extra.astype(BF16)
        dov = do_ref[...]
        delta = jnp.sum(dov.astype(F32) * o_ref[...].astype(F32), axis=1, keepdims=True)
        da_ref[0, :, :hd] = dov
        da_ref[0, :, hd:] = _lanes(list(_split3(delta)), tr).astype(BF16)

    wide = pl.BlockSpec((1, tr, 2 * hd), lambda h, i: (h, i, 0))
    head = pl.BlockSpec((tr, hd), lambda h, i: (i, h))
    return pl.pallas_call(
        body,
        name="attn_prep_bwd",
        grid=(nh, t // tr),
        in_specs=[wide, pl.BlockSpec((1, tr, LANES), lambda h, i: (h, i, 0)), head, head],
        out_specs=[wide, wide],
        out_shape=[jax.ShapeDtypeStruct((nh, t, 2 * hd), BF16)] * 2,
        compiler_params=_params("parallel", "parallel"),
    )(qa, lse, o, do)


def _causal(s, row0, rows, cols):
    row = lax.broadcasted_iota(jnp.int32, (rows, cols), 0) + row0
    col = lax.broadcasted_iota(jnp.int32, (rows, cols), 1)
    return jnp.where(col <= row, s, NEG_INF)


def _attn_fwd(qa, ka, proj, t, nh, hd, v_off):
    tq = _tile(t, ATTN_TILE)
    sub = _tile(tq, ATTN_SUB)
    ns = tq // sub
    scale = 1.0 / math.sqrt(hd)
    vb = v_off // hd

    def body(qa_ref, ka_ref, v_ref, o_ref, lse_ref):
        i = pl.program_id(1)

        def tile(j, carry, masked):
            rows = pl.ds(pl.multiple_of(j * tq, tq), tq)
            kj = ka_ref[0, rows, :]
            vj = v_ref[rows, :]
            new = []
            for r in range(ns):
                m, l, acc = carry[r]
                s = lax.dot_general(qa_ref[0, pl.ds(r * sub, sub), :], kj, _DIMS["nt"], preferred_element_type=F32)
                if masked:
                    s = _causal(s, r * sub, sub, tq)
                m_new = jnp.maximum(m, jnp.max(s, axis=1, keepdims=True))
                p = jnp.exp((s - m_new) * scale)
                alpha = jnp.exp((m - m_new) * scale)
                l = alpha * l + jnp.sum(p, axis=1, keepdims=True)
                acc = alpha * acc + jnp.dot(p.astype(BF16), vj, preferred_element_type=F32)
                new.append((m_new, l, acc))
            return tuple(new)

        init = tuple((jnp.full((sub, 1), NEG_INF, F32), jnp.zeros((sub, 1), F32), jnp.zeros((sub, hd), F32))
                     for _ in range(ns))
        carry = lax.fori_loop(0, i, lambda j, cr: tile(j, cr, False), init)
        carry = tile(i, carry, True)
        for r in range(ns):
            m, l, acc = carry[r]
            o_ref[pl.ds(r * sub, sub), :] = (acc / l).astype(BF16)
            lse_ref[0, pl.ds(r * sub, sub), :] = jnp.broadcast_to(m * scale + jnp.log(l), (sub, LANES))

    return pl.pallas_call(
        body,
        name="attn_fwd",
        grid=(nh, t // tq),
        in_specs=[pl.BlockSpec((1, tq, 2 * hd), lambda h, i: (h, i, 0)),
                  pl.BlockSpec((1, t, 2 * hd), lambda h, i: (h, 0, 0)),
                  pl.BlockSpec((t, hd), lambda h, i: (0, vb + h))],
        out_specs=[pl.BlockSpec((tq, hd), lambda h, i: (i, h)),
                   pl.BlockSpec((1, tq, LANES), lambda h, i: (h, i, 0))],
        out_shape=[jax.ShapeDtypeStruct((t, nh * hd), BF16), jax.ShapeDtypeStruct((nh, t, LANES), F32)],
        compiler_params=_params("parallel", "parallel"),
    )(qa, ka, proj)


def _attn_bwd(qb, ka, va, da, t, nh, hd):
    tq = _tile(t, ATTN_TILE)
    nq = t // tq
    sub = _tile(tq, ATTN_SUB)
    ns = tq // sub
    scale = 1.0 / math.sqrt(hd)

    def body(qb_ref, ka_ref, va_ref, da_ref, dq_ref, dk_ref, dv_ref, dck_ref, dcq_ref, dq_all):
        j = pl.program_id(1)

        @pl.when(j == 0)
        def _():
            dq_all[...] = jnp.zeros(dq_all.shape, F32)
            dcq_ref[...] = jnp.zeros(dcq_ref.shape, F32)

        kaj = ka_ref[0]
        vaj = va_ref[0]
        kj = kaj[:, :hd]

        def tile(i, carry, masked):
            dk, dv, dck = carry
            for r in range(ns):
                rows = pl.ds(pl.multiple_of(i * tq + r * sub, sub), sub)
                qr = qb_ref[0, rows, :]
                dr = da_ref[0, rows, :]
                s = lax.dot_general(qr, kaj, _DIMS["nt"], preferred_element_type=F32)
                if masked:
                    s = _causal(s, r * sub, sub, tq)
                p = jnp.exp(s * scale)
                ds = p * lax.dot_general(dr, vaj, _DIMS["nt"], preferred_element_type=F32)
                dsb = ds.astype(BF16)
                dv = dv + lax.dot_general(p.astype(BF16), dr[:, :hd], _DIMS["tn"], preferred_element_type=F32)
                dk = dk + lax.dot_general(dsb, qr[:, :hd], _DIMS["tn"], preferred_element_type=F32)
                dck = dck - jnp.sum(ds, axis=0, keepdims=True)
                dq_all[rows, :] += jnp.dot(dsb, kj, preferred_element_type=F32)
                dcq_ref[0, rows, :] += jnp.sum(ds, axis=1, keepdims=True)
            return dk, dv, dck

        carry = (jnp.zeros((tq, hd), F32), jnp.zeros((tq, hd), F32), jnp.zeros((1, tq), F32))
        carry = tile(j, carry, True)
        dk, dv, dck = lax.fori_loop(j + 1, nq, lambda i, cr: tile(i, cr, False), carry)
        dk_ref[...] = (dk * scale).astype(BF16)
        dv_ref[...] = dv.astype(BF16)
        dck_ref[0] = dck

        @pl.when(j == nq - 1)
        def _():
            dq_ref[...] = (dq_all[...] * scale).astype(BF16)

    whole = pl.BlockSpec((1, t, 2 * hd), lambda h, j: (h, 0, 0))
    block = pl.BlockSpec((1, tq, 2 * hd), lambda h, j: (h, j, 0))
    return pl.pallas_call(
        body,
        name="attn_bwd",
        grid=(nh, nq),
        in_specs=[whole, block, block, whole],
        out_specs=[
            pl.BlockSpec((t, hd), lambda h, j: (0, h)),
            pl.BlockSpec((tq, hd), lambda h, j: (j, h)),
            pl.BlockSpec((tq, hd), lambda h, j: (j, h)),
            pl.BlockSpec((1, 1, tq), lambda h, j: (h, 0, j)),
            pl.BlockSpec((1, t, LANES), lambda h, j: (h, 0, 0)),
        ],
        out_shape=[jax.ShapeDtypeStruct((t, nh * hd), BF16), jax.ShapeDtypeStruct((t, nh * hd), BF16),
                   jax.ShapeDtypeStruct((t, nh * hd), BF16), jax.ShapeDtypeStruct((nh, 1, t), F32),
                   jax.ShapeDtypeStruct((nh, t, LANES), F32)],
        scratch_shapes=[pltpu.VMEM((t, hd), F32)],
        compiler_params=_params("arbitrary", "arbitrary"),
    )(qb, ka, va, da)


class _Dims:
    def __init__(self, t, d, c, nh, aw, f, pd):
        self.t, self.d, self.c, self.nh, self.aw, self.f, self.pd = t, d, c, nh, aw, f, pd
        self.hd = aw // nh
        self.a_off, self.g_off = 0, c
        self.q_off, self.k_off, self.v_off = 2 * c, 2 * c + aw, 2 * c + 2 * aw
        self.gc_off = 2 * c + 3 * aw
        self.ga_off = self.gc_off + d
        self.n_main = self.ga_off + d


def _ffn_tn(f):
    return _tile(f, 1536)


def _layer_fwd(dm, x, p, w):
    t, d, c, f = dm.t, dm.d, dm.c, dm.f
    s = {"x": x}
    h = _rms_fwd("rms_mix", x, w["n_mix"], t, d)
    proj = _mm("mm_in", "nn", h, w["main"], t, dm.n_main, d, BF16)
    fl = _mm("mm_forget", "nn", h, w["f"], t, LANES, d, F32)
    u2, cpre = _conv_fwd(proj, w["conv_w"], w["conv_b"], w["ln_g"], w["ln_b"], t, c, dm.a_off, dm.g_off)
    f_t = fl[:, :dm.nh].T
    cum = _forget_fwd(f_t, w["b_f"], dm.nh, t)
    cs = jnp.broadcast_to((cum * math.sqrt(dm.hd))[:, :, None], (dm.nh, t, LANES))
    qa, ka, va = _attn_prep_fwd(proj, cs, t, dm.nh, dm.hd, dm.q_off, dm.k_off, dm.v_off)
    o, lse = _attn_fwd(qa, ka, proj, t, dm.nh, dm.hd, dm.v_off)

    def epi_conv(accs, ex):
        return accs[0], _sig(ex[0].astype(F32)) * accs[0]

    yc, m1 = _matmul("mm_conv_out", "nn", [u2], [w["co"]], [(0, 0, 0)], 1, t, d, c, [BF16, BF16],
                     epi=epi_conv, extras=[("mn", (proj, dm.gc_off))], tm=512)

    def epi_attn(accs, ex):
        return accs[0], ex[1].astype(F32) + _sig(ex[0].astype(F32)) * accs[0]

    ya, merged = _matmul("mm_attn_out", "nn", [o], [w["ao"]], [(0, 0, 0)], 1, t, d, dm.aw, [BF16, BF16],
                         epi=epi_attn, extras=[("mn", (proj, dm.ga_off)), ("mn", m1)], tm=512)
    x1 = _matmul("mm_out", "nn", [merged], [w["o"]], [(0, 0, 0)], 1, t, d, d, [F32],
                 epi=lambda accs, ex: [ex[0] + accs[0]], extras=[("mn", x)], tm=512)[0]

    hf = _rms_fwd("rms_ffn", x1, w["n_ffn"], t, d)

    def epi_glu(accs, ex):
        gate, up = accs
        return gate, up, gate * _sig(gate) * up

    gate, up, act = _matmul("mm_gate_up", "nn", [hf], [w["g"], w["u"]], [(0, 0, 0), (0, 1, 1)], 2, t, f, d,
                            [BF16, BF16, BF16], epi=epi_glu, tm=512, tn=_ffn_tn(f))
    x2 = _matmul("mm_down", "nn", [act], [w["d"]], [(0, 0, 0)], 1, t, d, f, [F32],
                 epi=lambda accs, ex: [ex[0] + accs[0]], extras=[("mn", x1)], tm=512)[0]

    hp = _rms_fwd("rms_ple", x2, w["n_ple"], t, d)
    pp = _mm("mm_ple_proj", "nn", p, w["pp"], t, d, dm.pd, BF16, tm=512)

    def epi_ple(accs, ex):
        sg = _sig(accs[0])
        return sg, ex[1] + sg * ex[0].astype(F32)

    sg, x3 = _matmul("mm_ple_gate", "nn", [hp], [w["pg"]], [(0, 0, 0)], 1, t, d, d, [BF16, F32],
                     epi=epi_ple, extras=[("mn", pp), ("mn", x2)], tm=512)
    s.update(h=h, proj=proj, f_t=f_t, qa=qa, ka=ka, va=va, u2=u2, cpre=cpre, o=o, lse=lse, yc=yc, ya=ya,
             merged=merged, x1=x1, hf=hf, gate=gate, up=up, act=act, x2=x2, hp=hp, pp=pp, sg=sg, p=p)
    return x3, s


def _layer_bwd(dm, dx3, dx3b, s, w):
    t, d, c, f = dm.t, dm.d, dm.c, dm.f
    g = {}

    def ple_ew(dxv, sgv, ppv):
        sgf, ppf = sgv.astype(F32), ppv.astype(F32)
        return dxv * sgf, dxv * ppf * sgf * (1.0 - sgf)

    d_pp, d_z = _ew("ple_bwd", ple_ew, [dx3, s["sg"], s["pp"]], t, d, [BF16, BF16], _tile(t, ROW_TILE, 8))
    g["pp"] = _mm("gw_ple_proj", "tn", s["p"], d_pp, dm.pd, d, t, F32)
    g["pg"] = _mm("gw_ple_gate", "tn", s["hp"], d_z, d, d, t, F32)
    d_hp = _mm("dx_ple_gate", "nt", d_z, w["pg"], t, d, d, BF16)
    dx2, dx2b, g["n_ple"] = _rms_bwd("rms_ple_bwd", d_hp, s["x2"], w["n_ple"], dx3, t, d)

    def epi_dglu(accs, ex):
        gate, up = ex[0].astype(F32), ex[1].astype(F32)
        sg = _sig(gate)
        return accs[0] * up * (sg * (1.0 + gate * (1.0 - sg))), accs[0] * gate * sg

    d_gate, d_up = _matmul("dx_down", "nt", [dx2b], [w["d"]], [(0, 0, 0)], 1, t, f, d, [BF16, BF16],
                           epi=epi_dglu, extras=[("mn", s["gate"]), ("mn", s["up"])], tm=512, tn=_ffn_tn(f))
    g["d"] = _mm("gw_down", "tn", s["act"], dx2b, f, d, t, F32, tm=_ffn_tn(f))
    g["g"] = _mm("gw_gate", "tn", s["hf"], d_gate, d, f, t, F32, tn=_ffn_tn(f))
    g["u"] = _mm("gw_up", "tn", s["hf"], d_up, d, f, t, F32, tn=_ffn_tn(f))
    d_hf = _matmul("dx_gate_up", "nt", [d_gate, d_up], [w["g"], w["u"]], [(0, 0, 0), (1, 1, 0)], 1, t, d, f,
                   [BF16])[0]
    dx1, dx1b, g["n_ffn"] = _rms_bwd("rms_ffn_bwd", d_hf, s["x1"], w["n_ffn"], dx2, t, d)

    g["o"] = _mm("gw_out", "tn", s["merged"], dx1b, d, d, t, F32)

    def epi_dmerge(accs, ex):
        dmv = accs[0]
        sgc, sga = _sig(ex[0].astype(F32)), _sig(ex[1].astype(F32))
        ycv, yav = ex[2].astype(F32), ex[3].astype(F32)
        return dmv * sgc, dmv * sga, dmv * ycv * sgc * (1.0 - sgc), dmv * yav * sga * (1.0 - sga)

    d_yc, d_ya, d_gc, d_ga = _matmul(
        "dx_out", "nt", [dx1b], [w["o"]], [(0, 0, 0)], 1, t, d, d, [BF16] * 4, epi=epi_dmerge,
        extras=[("mn", (s["proj"], dm.gc_off)), ("mn", (s["proj"], dm.ga_off)), ("mn", s["yc"]), ("mn", s["ya"])],
        tm=512, tn=_tile(d, 512))
    g["co"] = _mm("gw_conv_out", "tn", s["u2"], d_yc, c, d, t, F32)
    d_u2 = _mm("dx_conv_out", "nt", d_yc, w["co"], t, c, d, BF16)
    g["ao"] = _mm("gw_attn_out", "tn", s["o"], d_ya, dm.aw, d, t, F32)
    d_o = _mm("dx_attn_out", "nt", d_ya, w["ao"], t, dm.aw, d, BF16)

    qb, da = _attn_prep_bwd(s["qa"], s["lse"], s["o"], d_o, t, dm.nh, dm.hd)
    dq, dk, dv, dck, dcq = _attn_bwd(qb, s["ka"], s["va"], da, t, dm.nh, dm.hd)
    d_ft, g_bf = _forget_bwd(dck.reshape(dm.nh, t) + dcq[:, :, 0], s["f_t"], w["b_f"], dm.nh, t)
    g["b_f"] = g_bf[:, 0]
    d_f = jnp.pad(d_ft.T, ((0, 0), (0, LANES - dm.nh))).astype(BF16)

    d_a, d_gg, g["conv_w"], g["conv_b"], g["ln_g"], g["ln_b"] = _conv_bwd(
        s["proj"], s["cpre"], d_u2, w["conv_w"], w["ln_g"], w["ln_b"], t, c, dm.a_off, dm.g_off)

    d_proj = jnp.concatenate([d_a, d_gg, dq, dk, dv, d_gc, d_ga], axis=1)
    g["main"] = _mm("gw_in", "tn", s["h"], d_proj, d, dm.n_main, t, F32)
    g["f"] = _mm("gw_forget", "tn", s["h"], d_f, d, LANES, t, F32)
    d_h_f = _mm("dx_forget", "nt", d_f, w["f"], t, d, LANES, BF16)
    d_h = _matmul("dx_in", "nt", [d_proj], [w["main"]], [(0, 0, 0)], 1, t, d, dm.n_main, [BF16],
                  epi=lambda accs, ex: [accs[0] + ex[0].astype(F32)], extras=[("mn", d_h_f)])[0]
    dx, dxb, g["n_mix"] = _rms_bwd("rms_mix_bwd", d_h, s["x"], w["n_mix"], dx1, t, d)
    return dx, dxb, g


def _adamw_tiles(wv, gv, mv, vv):
    m_new = ADAM_B1 * mv + (1.0 - ADAM_B1) * gv
    v_new = ADAM_B2 * vv + (1.0 - ADAM_B2) * (gv * gv)
    m_hat = m_new / (1.0 - ADAM_B1 ** ADAM_STEP)
    v_hat = v_new / (1.0 - ADAM_B2 ** ADAM_STEP)
    delta = -ADAM_LR * (m_hat / (jnp.sqrt(v_hat) + ADAM_EPS) + ADAM_WD * wv)
    return delta, m_new, v_new


def _adamw(name, wv, gv, mv, vv):
    shape = wv.shape
    cols = shape[-1]
    rows = wv.size // cols
    tm = _tile(rows, max(8, (1 << 18) // cols), 8)
    flat = [a.reshape(rows, cols) for a in (wv, gv, mv, vv)]
    outs = _ew(name, _adamw_tiles, flat, rows, cols, [F32, F32, F32], tm)
    return [o.reshape(shape) for o in outs]


def _place():
    return lax.axis_index("x"), lax.axis_index("y"), lax.axis_index("c")


def _other_chips(x, y):
    return [(1 - x, y), (x, 1 - y), (1 - x, 1 - y)]


def _window(ref, kind, chip, size):
    if kind == "chip":
        return ref.at[chip]
    if kind == "row":
        return ref.at[pl.ds(chip * size, size), :]
    return ref.at[:, pl.ds(pl.multiple_of(chip * size, LANES), size)]


def _full_shape(shard, kind):
    depth, a, b = shard.shape
    if kind == "chip":
        return (depth, N_CHIPS, a, b)
    return (depth, N_CHIPS * a, b) if kind == "row" else (depth, a, N_CHIPS * b)


def _gather_weights(shards, kinds):
    n = len(shards)
    sizes = [s.shape[1] if k == "row" else s.shape[2] for s, k in zip(shards, kinds)]

    def body(*refs):
        ins, outs = refs[:n], refs[n:2 * n]
        ici_send, ici_recv, d2d_send, d2d_recv, local_sems = refs[2 * n:]
        x, y, c = _place()
        me = 2 * x + y
        sibling = (x, y, 1 - c)
        chips = _other_chips(x, y)

        def slab(w, chip, layer):
            return _window(outs[w].at[layer], kinds[w], chip, sizes[w])

        local = []
        for w in range(n):
            for layer in range(2):
                cp = pltpu.make_async_copy(ins[w].at[layer], slab(w, me, layer), local_sems.at[2 * w + layer])
                cp.start()
                local.append(cp)
        sends = []
        for w in range(n):
            for r, (px, py) in enumerate(chips):
                cp = pltpu.make_async_remote_copy(
                    src_ref=ins[w].at[c], dst_ref=slab(w, me, c), send_sem=ici_send.at[3 * w + r],
                    recv_sem=ici_recv.at[3 * w + r], device_id=(px, py, c), device_id_type=MESH)
                cp.start()
                sends.append(cp)
        for w in range(n):
            for r, (px, py) in enumerate(chips):
                landed = slab(w, 2 * px + py, c)
                pltpu.make_async_remote_copy(
                    src_ref=landed, dst_ref=landed, send_sem=ici_send.at[3 * w + r], recv_sem=ici_recv.at[3 * w + r],
                    device_id=(px, py, c), device_id_type=MESH).wait_recv()
                fwd = pltpu.make_async_remote_copy(
                    src_ref=landed, dst_ref=landed, send_sem=d2d_send.at[3 * w + r], recv_sem=d2d_recv.at[3 * w + r],
                    device_id=sibling, device_id_type=MESH)
                fwd.start()
                sends.append(fwd)
        for w in range(n):
            for r, (px, py) in enumerate(chips):
                got = slab(w, 2 * px + py, 1 - c)
                pltpu.make_async_remote_copy(
                    src_ref=got, dst_ref=got, send_sem=d2d_send.at[3 * w + r], recv_sem=d2d_recv.at[3 * w + r],
                    device_id=sibling, device_id_type=MESH).wait_recv()
        for cp in sends:
            cp.wait_send()
        for cp in local:
            cp.wait()

    return pl.pallas_call(
        body,
        name="gather_weights",
        in_specs=[ANY] * n,
        out_specs=[ANY] * n,
        out_shape=[jax.ShapeDtypeStruct(_full_shape(s, k), s.dtype) for s, k in zip(shards, kinds)],
        scratch_shapes=[pltpu.SemaphoreType.DMA((3 * n,))] * 4 + [pltpu.SemaphoreType.DMA((2 * n,))],
    )(*shards)


def _swap_layer_grads(g0, g1):
    n = len(g0)

    def body(*refs):
        r0, r1, got = refs[:n], refs[n:2 * n], refs[2 * n:3 * n]
        send_sems, recv_sems = refs[3 * n:]
        x, y, c = _place()

        def copy(w, src):
            return pltpu.make_async_remote_copy(src_ref=src, dst_ref=got[w], send_sem=send_sems.at[w],
                                                recv_sem=recv_sems.at[w], device_id=(x, y, 1 - c), device_id_type=MESH)

        @pl.when(c == 0)
        def _():
            for w in range(n):
                copy(w, r1[w]).start()

        @pl.when(c == 1)
        def _():
            for w in range(n):
                copy(w, r0[w]).start()

        for w in range(n):
            copy(w, r0[w]).wait()

    return pl.pallas_call(
        body,
        name="swap_layer_grads",
        in_specs=[ANY] * (2 * n),
        out_specs=[ANY] * n,
        out_shape=[jax.ShapeDtypeStruct(a.shape, a.dtype) for a in g0],
        scratch_shapes=[pltpu.SemaphoreType.DMA((n,)), pltpu.SemaphoreType.DMA((n,))],
    )(*g0, *g1)


def _add_core_partials(name, g0, g1, got, core):
    rows, cols = got.shape
    tm, tn = _tile(rows, 256, 8), _tile(cols, 2048)

    def body(core_ref, g0_ref, g1_ref, got_ref, out_ref):
        mine = jnp.where(core_ref[0] == 0, g0_ref[...], g1_ref[...])
        out_ref[...] = (mine + got_ref[...]).astype(BF16)

    blk = pl.BlockSpec((tm, tn), lambda i, j, core_ref: (i, j))
    return pl.pallas_call(
        body,
        name=name,
        grid_spec=pltpu.PrefetchScalarGridSpec(num_scalar_prefetch=1, grid=(rows // tm, cols // tn),
                                               in_specs=[blk, blk, blk], out_specs=blk),
        out_shape=jax.ShapeDtypeStruct((rows, cols), BF16),
        compiler_params=_params("parallel", "parallel"),
    )(core.reshape(1), g0, g1, got)


def _exchange_chip_grads(parts, kinds, sizes):
    n = len(parts)

    def shard_shape(p, kind, size):
        if kind == "chip":
            return p.shape[1:]
        return (size, p.shape[1]) if kind == "row" else (p.shape[0], size)

    def body(*refs):
        ins, outs = refs[:n], refs[n:2 * n]
        send_sems, recv_sems, local_sems = refs[2 * n:]
        x, y, c = _place()
        me = 2 * x + y
        local, sends = [], []
        for w in range(n):
            cp = pltpu.make_async_copy(_window(ins[w], kinds[w], me, sizes[w]), outs[w].at[me], local_sems.at[w])
            cp.start()
            local.append(cp)
        for w in range(n):
            for r, (px, py) in enumerate(_other_chips(x, y)):
                cp = pltpu.make_async_remote_copy(
                    src_ref=_window(ins[w], kinds[w], 2 * px + py, sizes[w]), dst_ref=outs[w].at[me],
                    send_sem=send_sems.at[3 * w + r], recv_sem=recv_sems.at[3 * w + r],
                    device_id=(px, py, c), device_id_type=MESH)
                cp.start()
                sends.append(cp)
        for w in range(n):
            for r, (px, py) in enumerate(_other_chips(x, y)):
                got = outs[w].at[2 * px + py]
                pltpu.make_async_remote_copy(
                    src_ref=got, dst_ref=got, send_sem=send_sems.at[3 * w + r], recv_sem=recv_sems.at[3 * w + r],
                    device_id=(px, py, c), device_id_type=MESH).wait_recv()
        for cp in sends:
            cp.wait_send()
        for cp in local:
            cp.wait()

    return pl.pallas_call(
        body,
        name="exchange_chip_grads",
        in_specs=[ANY] * n,
        out_specs=[ANY] * n,
        out_shape=[jax.ShapeDtypeStruct((N_CHIPS,) + tuple(shard_shape(p, k, s)), p.dtype)
                   for p, k, s in zip(parts, kinds, sizes)],
        scratch_shapes=[pltpu.SemaphoreType.DMA((3 * n,)), pltpu.SemaphoreType.DMA((3 * n,)),
                        pltpu.SemaphoreType.DMA((n,))],
    )(*parts)


def _sum_chip_partials(name, got):
    _, rows, cols = got.shape
    tm = _tile(rows, max(8, (1 << 19) // cols), 16)

    def fn(*parts):
        total = parts[0].astype(F32)
        for part in parts[1:]:
            total = total + part.astype(F32)
        return [total]

    return _ew(name, fn, [(got, 0, (k,)) for k in range(N_CHIPS)], rows, cols, [F32], tm)[0]


def _share_reduced(mine):
    n = len(mine)

    def body(*refs):
        ins, outs = refs[:n], refs[n:2 * n]
        send_sems, recv_sems, local_sems = refs[2 * n:]
        x, y, c = _place()
        local, sends = [], []
        for w in range(n):
            cp = pltpu.make_async_copy(ins[w], outs[w].at[c], local_sems.at[w])
            cp.start()
            local.append(cp)
            cp = pltpu.make_async_remote_copy(src_ref=ins[w], dst_ref=outs[w].at[c], send_sem=send_sems.at[w],
                                              recv_sem=recv_sems.at[w], device_id=(x, y, 1 - c), device_id_type=MESH)
            cp.start()
            sends.append(cp)
        for w in range(n):
            got = outs[w].at[1 - c]
            pltpu.make_async_remote_copy(src_ref=got, dst_ref=got, send_sem=send_sems.at[w], recv_sem=recv_sems.at[w],
                                         device_id=(x, y, 1 - c), device_id_type=MESH).wait_recv()
        for cp in sends:
            cp.wait_send()
        for cp in local:
            cp.wait()

    return pl.pallas_call(
        body,
        name="share_reduced",
        in_specs=[ANY] * n,
        out_specs=[ANY] * n,
        out_shape=[jax.ShapeDtypeStruct((2,) + a.shape, a.dtype) for a in mine],
        scratch_shapes=[pltpu.SemaphoreType.DMA((n,)), pltpu.SemaphoreType.DMA((n,)), pltpu.SemaphoreType.DMA((n,))],
    )(*mine)


def _allreduce_small(v):
    rows, width = v.shape

    def body(v_ref, out_ref, slots, send_sems, recv_sems):
        x, y, c = _place()
        me = 4 * x + 2 * y + c
        slots[me] = v_ref[...]
        peers = [(x, y, 1 - c)]
        for px, py in _other_chips(x, y):
            peers += [(px, py, c), (px, py, 1 - c)]
        sends = []
        for r, peer in enumerate(peers):
            cp = pltpu.make_async_remote_copy(
                src_ref=v_ref, dst_ref=slots.at[me], send_sem=send_sems.at[r], recv_sem=recv_sems.at[r],
                device_id=peer, device_id_type=MESH)
            cp.start()
            sends.append(cp)
        for r, (px, py, pc) in enumerate(peers):
            got = slots.at[4 * px + 2 * py + pc]
            pltpu.make_async_remote_copy(
                src_ref=got, dst_ref=got, send_sem=send_sems.at[r], recv_sem=recv_sems.at[r],
                device_id=(px, py, pc), device_id_type=MESH).wait_recv()
        for cp in sends:
            cp.wait_send()
        total = slots[0]
        for n in range(1, 8):
            total = total + slots[n]
        out_ref[...] = total

    vm = pl.BlockSpec(memory_space=pltpu.VMEM)
    return pl.pallas_call(
        body,
        name="allreduce_small",
        in_specs=[vm],
        out_specs=vm,
        out_shape=jax.ShapeDtypeStruct((rows, width), F32),
        scratch_shapes=[pltpu.VMEM((8, rows, width), F32), pltpu.SemaphoreType.DMA((7,)),
                        pltpu.SemaphoreType.DMA((7,))],
    )(v)


def _pad_rows(flat, row_align):
    n = flat.shape[0]
    rows = -(-n // PACK_W)
    rows = -(-rows // row_align) * row_align
    return jnp.pad(flat, (0, rows * PACK_W - n)).reshape(rows, PACK_W)


def _unpack(buf, shapes):
    flat = buf.reshape(-1)
    out, off = [], 0
    for shp in shapes:
        n = math.prod(shp)
        out.append(flat[off:off + n].reshape(shp))
        off += n
    return out


BIG = (("w_in", "chip"), ("w_conv_out", "col"), ("w_attn_out", "col"), ("w_out", "row"), ("w_gate_up", "col"),
       ("w_down", "row"), ("w_ple_gate", "row"), ("w_ple_proj", "col"))


def kernel(x, p, norm_mix_g, w_in, b_forget, conv_w, conv_b, conv_ln_g, conv_ln_b, w_conv_out, w_attn_out, w_out, norm_ffn_g, w_gate_up, w_down, norm_ple_g, w_ple_gate, w_ple_proj, final_g, loss_target, m_norm_mix_g, m_w_in, m_b_forget, m_conv_w, m_conv_b, m_conv_ln_g, m_conv_ln_b, m_w_conv_out, m_w_attn_out, m_w_out, m_norm_ffn_g, m_w_gate_up, m_w_down, m_norm_ple_g, m_w_ple_gate, m_w_ple_proj, m_final_g, v_norm_mix_g, v_w_in, v_b_forget, v_conv_w, v_conv_b, v_conv_ln_g, v_conv_ln_b, v_w_conv_out, v_w_attn_out, v_w_out, v_norm_ffn_g, v_w_gate_up, v_w_down, v_norm_ple_g, v_w_ple_gate, v_w_ple_proj, v_final_g):
    wts = dict(norm_mix_g=norm_mix_g, w_in=w_in, b_forget=b_forget, conv_w=conv_w, conv_b=conv_b,
               conv_ln_g=conv_ln_g, conv_ln_b=conv_ln_b, w_conv_out=w_conv_out, w_attn_out=w_attn_out,
               w_out=w_out, norm_ffn_g=norm_ffn_g, w_gate_up=w_gate_up, w_down=w_down, norm_ple_g=norm_ple_g,
               w_ple_gate=w_ple_gate, w_ple_proj=w_ple_proj, final_g=final_g)
    mom1 = dict(norm_mix_g=m_norm_mix_g, w_in=m_w_in, b_forget=m_b_forget, conv_w=m_conv_w, conv_b=m_conv_b,
                conv_ln_g=m_conv_ln_g, conv_ln_b=m_conv_ln_b, w_conv_out=m_w_conv_out, w_attn_out=m_w_attn_out,
                w_out=m_w_out, norm_ffn_g=m_norm_ffn_g, w_gate_up=m_w_gate_up, w_down=m_w_down,
                norm_ple_g=m_norm_ple_g, w_ple_gate=m_w_ple_gate, w_ple_proj=m_w_ple_proj, final_g=m_final_g)
    mom2 = dict(norm_mix_g=v_norm_mix_g, w_in=v_w_in, b_forget=v_b_forget, conv_w=v_conv_w, conv_b=v_conv_b,
                conv_ln_g=v_conv_ln_g, conv_ln_b=v_conv_ln_b, w_conv_out=v_w_conv_out, w_attn_out=v_w_attn_out,
                w_out=v_w_out, norm_ffn_g=v_norm_ffn_g, w_gate_up=v_w_gate_up, w_down=v_w_down,
                norm_ple_g=v_norm_ple_g, w_ple_gate=v_w_ple_gate, w_ple_proj=v_w_ple_proj, final_g=v_final_g)
    order = list(wts)
    depth = w_in.shape[0]
    assert depth == 2, "the exchanges give one layer to each of a chip's two cores"
    t, d = x.shape[1], x.shape[2]
    c = conv_ln_g.shape[1]
    nh = b_forget.shape[1]
    aw = w_attn_out.shape[1]
    f = N_CHIPS * w_down.shape[1]
    pd = w_ple_proj.shape[1]
    dm = _Dims(t, d, c, nh, aw, f, pd)
    n_split = 2 * c + 3 * aw
    cw = conv_w.shape[2]
    chip = 2 * lax.axis_index("x") + lax.axis_index("y")
    core = lax.axis_index("c")
    big_names = [name for name, _ in BIG]
    big_kinds = [kind for _, kind in BIG]

    gathered = _gather_weights([wts[name].astype(BF16) for name in big_names] + [conv_w], big_kinds + ["chip"])
    full = dict(zip(big_names + ["conv_w"], gathered))

    def layer_weights(l):
        wi = jnp.concatenate([full["w_in"][l, k] for k in range(N_CHIPS)], axis=1)
        cwl = jnp.concatenate([full["conv_w"][l, k] for k in range(N_CHIPS)], axis=1)
        return {
            "main": jnp.concatenate([wi[:, :n_split], wi[:, n_split + nh:]], axis=1),
            "f": jnp.pad(wi[:, n_split:n_split + nh], ((0, 0), (0, LANES - nh))),
            "conv_w": jnp.pad(cwl, ((0, CONV_HALO - CONV_K), (0, 0))),
            "conv_b": conv_b[l][None], "ln_g": conv_ln_g[l][None], "ln_b": conv_ln_b[l][None],
            "b_f": jnp.broadcast_to(b_forget[l][:, None], (nh, LANES)),
            "co": (full["w_conv_out"], 0, (l,)), "ao": (full["w_attn_out"], 0, (l,)), "o": (full["w_out"], 0, (l,)),
            "g": (full["w_gate_up"], 0, (l,)), "u": (full["w_gate_up"], f, (l,)), "d": (full["w_down"], 0, (l,)),
            "pg": (full["w_ple_gate"], 0, (l,)), "pp": (full["w_ple_proj"], 0, (l,)),
            "n_mix": norm_mix_g[l][None], "n_ffn": norm_ffn_g[l][None], "n_ple": norm_ple_g[l][None],
        }

    lw = [layer_weights(l) for l in range(depth)]

    xl = x[0]
    saved = []
    for l in range(depth):
        xl, s = _layer_fwd(dm, xl, p[l, 0], lw[l])
        saved.append(s)
    loss_row, dx, dxb, g_final = _loss_head(xl, final_g[None], loss_target[0], t, d)
    loss = lax.psum(loss_row[0, 0], ("x", "y", "c"))
    lg = [None] * depth
    for l in reversed(range(depth)):
        dx, dxb, lg[l] = _layer_bwd(dm, dx, dxb, saved[l], lw[l])

    def layer_grads(l):
        g = lg[l]
        w_in_g = jnp.concatenate([g["main"][:, :n_split], g["f"][:, :nh], g["main"][:, n_split:]], axis=1)
        return [w_in_g.reshape(d, N_CHIPS, -1).transpose(1, 0, 2), g["co"], g["ao"], g["o"],
                jnp.concatenate([g["g"], g["u"]], axis=1), g["d"], g["pg"], g["pp"]]

    g0, g1 = layer_grads(0), layer_grads(1)
    sizes = [wts[name].shape[1] if kind == "row" else wts[name].shape[2] for name, kind in BIG]
    from_sibling = _swap_layer_grads(g0, g1)

    def flat2(a):
        return a.reshape(-1, a.shape[-1])

    chip_parts = [_add_core_partials("add_core_partials_" + name, flat2(a0), flat2(a1), flat2(got), core)
                  .reshape(got.shape) for name, a0, a1, got in zip(big_names, g0, g1, from_sibling)]
    from_chips = _exchange_chip_grads(chip_parts, big_kinds, sizes)
    reduced = [_sum_chip_partials("sum_chip_partials_" + name, got) for name, got in zip(big_names, from_chips)]
    grads = dict(zip(big_names, _share_reduced(reduced)))

    def stacked(key):
        return jnp.stack([lg[l][key] for l in range(depth)])

    small = ["norm_mix_g", "b_forget", "conv_b", "conv_ln_g", "conv_ln_b", "norm_ffn_g", "norm_ple_g", "final_g"]
    small_g = {
        "norm_mix_g": jnp.concatenate([lg[l]["n_mix"] for l in range(depth)]),
        "b_forget": stacked("b_f"),
        "conv_b": jnp.concatenate([lg[l]["conv_b"] for l in range(depth)]),
        "conv_ln_g": jnp.concatenate([lg[l]["ln_g"] for l in range(depth)]),
        "conv_ln_b": jnp.concatenate([lg[l]["ln_b"] for l in range(depth)]),
        "norm_ffn_g": jnp.concatenate([lg[l]["n_ffn"] for l in range(depth)]),
        "norm_ple_g": jnp.concatenate([lg[l]["n_ple"] for l in range(depth)]),
        "final_g": g_final[0],
    }
    conv_w_g = jnp.stack([lg[l]["conv_w"][:CONV_K] for l in range(depth)])
    small_shapes = [wts[n].shape for n in small] + [conv_w_g.shape]
    small_pack = _pad_rows(jnp.concatenate([small_g[n].reshape(-1) for n in small] + [conv_w_g.reshape(-1)]), 8)
    small_sum = _unpack(_allreduce_small(small_pack), small_shapes)
    for n, name in enumerate(small):
        grads[name] = small_sum[n]
    grads["conv_w"] = lax.dynamic_slice_in_dim(small_sum[len(small)], chip * cw, cw, axis=2)

    def pack_small(src):
        return _pad_rows(jnp.concatenate([src[n].reshape(-1) for n in small]), 8)

    small_upd = _adamw("adamw_small", pack_small(wts), pack_small(grads), pack_small(mom1), pack_small(mom2))
    small_upd = [_unpack(u, [wts[n].shape for n in small]) for u in small_upd]
    delta, new_m, new_v = {}, {}, {}
    for n, name in enumerate(small):
        delta[name], new_m[name], new_v[name] = small_upd[0][n], small_upd[1][n], small_upd[2][n]
    for name in big_names + ["conv_w"]:
        delta[name], new_m[name], new_v[name] = _adamw("adamw_" + name, wts[name], grads[name], mom1[name],
                                                       mom2[name])

    return (loss, dx[None], *[grads[n] for n in order], *[delta[n] for n in order],
            *[new_m[n] for n in order], *[new_v[n] for n in order])
```

```python
import functools
import math

import jax
import jax.numpy as jnp
from jax import lax
from jax.experimental import pallas as pl
from jax.experimental.pallas import tpu as pltpu

F32 = jnp.float32
BF16 = jnp.bfloat16

EPS = 1e-6
CONV_K = 31
NEG_INF = -1e30
LOG2E = 1.4426950408889634
ADAM_LR = 0.001
ADAM_B1 = 0.9
ADAM_B2 = 0.999
ADAM_EPS = 1e-08
ADAM_WD = 0.01
ADAM_STEP = 10

LANES = 128
VMEM_LIMIT = 60 * 1024 * 1024
PACK_W = 1024
N_CHIPS = 4
CONV_HALO = 32
CUM_BLOCK = 256

MM_TM = 1024
MM_TN = 1024
MM_TK = 2048
ROW_TILE = 256
CONV_TILE = 256
ATTN_TILE = 512
ATTN_SUB = 256

MESH = pl.DeviceIdType.MESH
ANY = pl.BlockSpec(memory_space=pl.ANY)


def _params(*sem):
    return pltpu.CompilerParams(dimension_semantics=sem, vmem_limit_bytes=VMEM_LIMIT)


def _tile(dim, pref, align=LANES):
    if dim <= pref:
        return dim
    t = (pref // align) * align
    while t >= align:
        if dim % t == 0:
            return t
        t -= align
    return dim


def _sig(x):
    return 1.0 / (1.0 + jnp.exp(-x))


def _op(a):
    if not isinstance(a, tuple):
        return a, 0, ()
    return a if len(a) == 3 else (a[0], a[1], ())


def _spec(block, index, lead):
    if not lead:
        return pl.BlockSpec(block, index)
    return pl.BlockSpec((None,) * len(lead) + block, lambda *g: tuple(lead) + index(*g))


_DIMS = {
    "nn": (((1,), (0,)), ((), ())),
    "nt": (((1,), (1,)), ((), ())),
    "tn": (((0,), (0,)), ((), ())),
}


def _matmul(name, mode, a_ops, b_ops, terms, n_acc, m, n, k, out_dtypes, epi=None, extras=(),
            tm=None, tn=None, tk=None):
    tm = tm or _tile(m, MM_TM)
    tn = tn or _tile(n, MM_TN)
    tk = tk or _tile(k, MM_TK)
    nk = k // tk
    a_ops = [_op(a) for a in a_ops]
    b_ops = [_op(b) for b in b_ops]
    extras = [(kind, _op(e)) for kind, e in extras]
    na, nb, ne, no = len(a_ops), len(b_ops), len(extras), len(out_dtypes)

    def a_spec(off, lead):
        if mode == "tn":
            assert off % tm == 0
            return _spec((tk, tm), lambda i, j, kk: (kk, i + off // tm), lead)
        assert off % tk == 0
        return _spec((tm, tk), lambda i, j, kk: (i, kk + off // tk), lead)

    def b_spec(off, lead):
        if mode == "nt":
            assert off % tk == 0
            return _spec((tn, tk), lambda i, j, kk: (j, kk + off // tk), lead)
        assert off % tn == 0
        return _spec((tk, tn), lambda i, j, kk: (kk, j + off // tn), lead)

    def e_spec(kind, off, lead):
        assert off % tn == 0
        if kind == "n":
            return _spec((1, tn), lambda i, j, kk: (0, j + off // tn), lead)
        return _spec((tm, tn), lambda i, j, kk: (i, j + off // tn), lead)

    def body(*refs):
        a_refs = refs[:na]
        b_refs = refs[na:na + nb]
        e_refs = refs[na + nb:na + nb + ne]
        o_refs = refs[na + nb + ne:na + nb + ne + no]
        acc_refs = refs[na + nb + ne + no:]
        kk = pl.program_id(2)
        av = [r[...].astype(BF16) for r in a_refs]
        bv = [r[...].astype(BF16) for r in b_refs]
        sums = [None] * n_acc
        for ai, bi, ci in terms:
            part = lax.dot_general(av[ai], bv[bi], _DIMS[mode], preferred_element_type=F32)
            sums[ci] = part if sums[ci] is None else sums[ci] + part

        def finish(accs):
            outs = epi(accs, [e[...] for e in e_refs]) if epi is not None else accs
            for o, val in zip(o_refs, outs):
                o[...] = val.astype(o.dtype)

        if nk == 1:
            finish(sums)
            return

        @pl.when(kk == 0)
        def _():
            for acc, part in zip(acc_refs, sums):
                acc[...] = part

        @pl.when(kk > 0)
        def _():
            for acc, part in zip(acc_refs, sums):
                acc[...] += part

        @pl.when(kk == nk - 1)
        def _():
            finish([acc[...] for acc in acc_refs])

    outs = pl.pallas_call(
        body,
        name=name,
        grid=(m // tm, n // tn, nk),
        in_specs=[a_spec(off, lead) for _, off, lead in a_ops] + [b_spec(off, lead) for _, off, lead in b_ops]
        + [e_spec(kind, off, lead) for kind, (_, off, lead) in extras],
        out_specs=[pl.BlockSpec((tm, tn), lambda i, j, kk: (i, j)) for _ in out_dtypes],
        out_shape=[jax.ShapeDtypeStruct((m, n), dt) for dt in out_dtypes],
        scratch_shapes=[pltpu.VMEM((tm, tn), F32) for _ in range(n_acc if nk > 1 else 0)],
        compiler_params=_params("parallel", "parallel", "arbitrary"),
    )(*[a[0] for a in a_ops], *[b[0] for b in b_ops], *[e[0] for _, e in extras])
    return outs


def _mm(name, mode, a, b, m, n, k, out_dtype, **kw):
    return _matmul(name, mode, [a], [b], [(0, 0, 0)], 1, m, n, k, [out_dtype], **kw)[0]


def _ew(name, fn, ins, m, n, out_dtypes, tm, tn=None):
    tn = tn or n
    ins = [_op(a) for a in ins]
    ni = len(ins)

    def spec(off, lead):
        assert off % tn == 0
        return _spec((tm, tn), lambda i, j: (i, j + off // tn), lead)

    def body(*refs):
        outs = fn(*[r[...] for r in refs[:ni]])
        for o, val in zip(refs[ni:], outs):
            o[...] = val.astype(o.dtype)

    return pl.pallas_call(
        body,
        name=name,
        grid=(m // tm, n // tn),
        in_specs=[spec(off, lead) for _, off, lead in ins],
        out_specs=[pl.BlockSpec((tm, tn), lambda i, j: (i, j)) for _ in out_dtypes],
        out_shape=[jax.ShapeDtypeStruct((m, n), dt) for dt in out_dtypes],
        compiler_params=_params("parallel", "parallel"),
    )(*[a[0] for a in ins])


def _rms_fwd(name, x, g, t, d):
    tr = _tile(t, ROW_TILE, 8)

    def body(x_ref, g_ref, h_ref):
        xv = x_ref[...]
        r = lax.rsqrt(jnp.mean(xv * xv, axis=1, keepdims=True) + EPS)
        h_ref[...] = (xv * r * g_ref[...]).astype(BF16)

    return pl.pallas_call(
        body,
        name=name,
        grid=(t // tr,),
        in_specs=[pl.BlockSpec((tr, d), lambda i: (i, 0)), pl.BlockSpec((1, d), lambda i: (0, 0))],
        out_specs=pl.BlockSpec((tr, d), lambda i: (i, 0)),
        out_shape=jax.ShapeDtypeStruct((t, d), BF16),
        compiler_params=_params("parallel"),
    )(x, g)


def _rms_bwd_rows(dh, xv, g):
    r = lax.rsqrt(jnp.mean(xv * xv, axis=1, keepdims=True) + EPS)
    xhat = xv * r
    dxh = dh * g
    dx = r * (dxh - xhat * jnp.mean(dxh * xhat, axis=1, keepdims=True))
    return dx, dh * xhat


def _rms_bwd(name, dh, x, g, dres, t, d):
    tr = _tile(t, ROW_TILE, 8)

    def body(dh_ref, x_ref, g_ref, dres_ref, dx_ref, dxb_ref, dg_ref):
        @pl.when(pl.program_id(0) == 0)
        def _():
            dg_ref[...] = jnp.zeros(dg_ref.shape, F32)

        dx, dg_rows = _rms_bwd_rows(dh_ref[...].astype(F32), x_ref[...], g_ref[...])
        dx = dx + dres_ref[...]
        dx_ref[...] = dx
        dxb_ref[...] = dx.astype(BF16)
        dg_ref[...] += jnp.sum(dg_rows, axis=0, keepdims=True)

    row = pl.BlockSpec((tr, d), lambda i: (i, 0))
    vec = pl.BlockSpec((1, d), lambda i: (0, 0))
    return pl.pallas_call(
        body,
        name=name,
        grid=(t // tr,),
        in_specs=[row, row, vec, row],
        out_specs=[row, row, vec],
        out_shape=[jax.ShapeDtypeStruct((t, d), F32), jax.ShapeDtypeStruct((t, d), BF16),
                   jax.ShapeDtypeStruct((1, d), F32)],
        compiler_params=_params("arbitrary"),
    )(dh, x, g, dres)


def _loss_head(x, g, target, t, d):
    tr = _tile(t, ROW_TILE, 8)

    def body(x_ref, g_ref, tgt_ref, loss_ref, dx_ref, dxb_ref, dg_ref):
        @pl.when(pl.program_id(0) == 0)
        def _():
            dg_ref[...] = jnp.zeros(dg_ref.shape, F32)
            loss_ref[...] = jnp.zeros(loss_ref.shape, F32)

        xv = x_ref[...]
        gv = g_ref[...]
        r = lax.rsqrt(jnp.mean(xv * xv, axis=1, keepdims=True) + EPS)
        err = xv * r * gv - tgt_ref[...]
        loss_ref[...] += (0.5 / d) * jnp.sum(err * err)
        dx, dg_rows = _rms_bwd_rows(err * (1.0 / d), xv, gv)
        dx_ref[...] = dx
        dxb_ref[...] = dx.astype(BF16)
        dg_ref[...] += jnp.sum(dg_rows, axis=0, keepdims=True)

    row = pl.BlockSpec((tr, d), lambda i: (i, 0))
    vec = pl.BlockSpec((1, d), lambda i: (0, 0))
    one = pl.BlockSpec((1, LANES), lambda i: (0, 0))
    return pl.pallas_call(
        body,
        name="loss_head",
        grid=(t // tr,),
        in_specs=[row, vec, row],
        out_specs=[one, row, row, vec],
        out_shape=[jax.ShapeDtypeStruct((1, LANES), F32), jax.ShapeDtypeStruct((t, d), F32),
                   jax.ShapeDtypeStruct((t, d), BF16), jax.ShapeDtypeStruct((1, d), F32)],
        compiler_params=_params("arbitrary"),
    )(x, g, target)


def _layernorm_rows(cv, g, b):
    mu = jnp.mean(cv, axis=1, keepdims=True)
    xc = cv - mu
    rstd = lax.rsqrt(jnp.mean(xc * xc, axis=1, keepdims=True) + EPS)
    xhat = xc * rstd
    return xhat, rstd, xhat * g + b


def _conv_fwd(proj, conv_w, conv_b, ln_g, ln_b, t, c, a_off, g_off):
    tm = _tile(t, CONV_TILE, CONV_HALO)
    per = tm // CONV_HALO
    ab, gb = a_off // c, g_off // c

    def body(a_ref, g_ref, ap_ref, gp_ref, w_ref, cb_ref, lg_ref, lb_ref, u2_ref, c_ref, upad):
        i = pl.program_id(0)
        u_prev = ap_ref[...].astype(F32) * _sig(gp_ref[...].astype(F32))
        upad[pl.ds(0, CONV_HALO), :] = jnp.where(i > 0, u_prev, 0.0)
        upad[pl.ds(CONV_HALO, tm), :] = a_ref[...].astype(F32) * _sig(g_ref[...].astype(F32))
        acc = jnp.zeros((tm, c), F32) + cb_ref[...]
        for k in range(CONV_K):
            acc = acc + w_ref[pl.ds(k, 1), :] * upad[pl.ds(CONV_HALO - (CONV_K - 1) + k, tm), :]
        c_ref[...] = acc
        _, _, z = _layernorm_rows(acc, lg_ref[...], lb_ref[...])
        u2_ref[...] = (z * _sig(z)).astype(BF16)

    vec = pl.BlockSpec((1, c), lambda i: (0, 0))
    return pl.pallas_call(
        body,
        name="conv_fwd",
        grid=(t // tm,),
        in_specs=[
            pl.BlockSpec((tm, c), lambda i: (i, ab)),
            pl.BlockSpec((tm, c), lambda i: (i, gb)),
            pl.BlockSpec((CONV_HALO, c), lambda i: (jnp.maximum(i * per - 1, 0), ab)),
            pl.BlockSpec((CONV_HALO, c), lambda i: (jnp.maximum(i * per - 1, 0), gb)),
            pl.BlockSpec((CONV_HALO, c), lambda i: (0, 0)), vec, vec, vec,
        ],
        out_specs=[pl.BlockSpec((tm, c), lambda i: (i, 0)), pl.BlockSpec((tm, c), lambda i: (i, 0))],
        out_shape=[jax.ShapeDtypeStruct((t, c), BF16), jax.ShapeDtypeStruct((t, c), F32)],
        scratch_shapes=[pltpu.VMEM((CONV_HALO + tm, c), F32)],
        compiler_params=_params("parallel"),
    )(proj, proj, proj, proj, conv_w, conv_b, ln_g, ln_b)


def _conv_bwd(proj, cpre, du2, conv_w, ln_g, ln_b, t, c, a_off, g_off):
    tm = _tile(t, CONV_TILE, CONV_HALO)
    per = tm // CONV_HALO
    nt = t // tm
    last_halo = t // CONV_HALO - 1
    ab, gb = a_off // c, g_off // c

    def body(a_ref, g_ref, ap_ref, gp_ref, c_ref, cn_ref, du_ref, dun_ref, w_ref, lg_ref, lb_ref,
             da_ref, dg_ref, gw_ref, gcb_ref, glg_ref, glb_ref, upad, dpad):
        i = pl.program_id(0)

        @pl.when(i == 0)
        def _():
            gw_ref[...] = jnp.zeros(gw_ref.shape, F32)
            gcb_ref[...] = jnp.zeros(gcb_ref.shape, F32)
            glg_ref[...] = jnp.zeros(glg_ref.shape, F32)
            glb_ref[...] = jnp.zeros(glb_ref.shape, F32)

        lg = lg_ref[...]
        lb = lb_ref[...]

        def ln_bwd(cv, duv):
            xhat, rstd, z = _layernorm_rows(cv, lg, lb)
            sz = _sig(z)
            dz = duv * (sz * (1.0 + z * (1.0 - sz)))
            dxh = dz * lg
            dc = rstd * (dxh - jnp.mean(dxh, axis=1, keepdims=True)
                         - xhat * jnp.mean(dxh * xhat, axis=1, keepdims=True))
            return dc, dz, xhat

        dc, dz, xhat = ln_bwd(c_ref[...], du_ref[...].astype(F32))
        dc_next, _, _ = ln_bwd(cn_ref[...], dun_ref[...].astype(F32))
        glg_ref[...] += jnp.sum(dz * xhat, axis=0, keepdims=True)
        glb_ref[...] += jnp.sum(dz, axis=0, keepdims=True)
        gcb_ref[...] += jnp.sum(dc, axis=0, keepdims=True)
        dpad[pl.ds(0, tm), :] = dc
        dpad[pl.ds(tm, CONV_HALO), :] = jnp.where(i < nt - 1, dc_next, 0.0)

        av = a_ref[...].astype(F32)
        sg = _sig(g_ref[...].astype(F32))
        u_prev = ap_ref[...].astype(F32) * _sig(gp_ref[...].astype(F32))
        upad[pl.ds(0, CONV_HALO), :] = jnp.where(i > 0, u_prev, 0.0)
        upad[pl.ds(CONV_HALO, tm), :] = av * sg

        du = jnp.zeros((tm, c), F32)
        for k in range(CONV_K):
            du = du + w_ref[pl.ds(k, 1), :] * dpad[pl.ds(CONV_K - 1 - k, tm), :]
            gw_ref[pl.ds(k, 1), :] += jnp.sum(
                dc * upad[pl.ds(CONV_HALO - (CONV_K - 1) + k, tm), :], axis=0, keepdims=True)
        da_ref[...] = (du * sg).astype(BF16)
        dg_ref[...] = (du * av * sg * (1.0 - sg)).astype(BF16)

    vec = pl.BlockSpec((1, c), lambda i: (0, 0))
    cur = pl.BlockSpec((tm, c), lambda i: (i, 0))
    nxt = pl.BlockSpec((CONV_HALO, c), lambda i: (jnp.minimum((i + 1) * per, last_halo), 0))
    wsp = pl.BlockSpec((CONV_HALO, c), lambda i: (0, 0))
    return pl.pallas_call(
        body,
        name="conv_bwd",
        grid=(nt,),
        in_specs=[
            pl.BlockSpec((tm, c), lambda i: (i, ab)),
            pl.BlockSpec((tm, c), lambda i: (i, gb)),
            pl.BlockSpec((CONV_HALO, c), lambda i: (jnp.maximum(i * per - 1, 0), ab)),
            pl.BlockSpec((CONV_HALO, c), lambda i: (jnp.maximum(i * per - 1, 0), gb)),
            cur, nxt, cur, nxt, wsp, vec, vec,
        ],
        out_specs=[cur, cur, wsp, vec, vec, vec],
        out_shape=[jax.ShapeDtypeStruct((t, c), BF16), jax.ShapeDtypeStruct((t, c), BF16),
                   jax.ShapeDtypeStruct((CONV_HALO, c), F32), jax.ShapeDtypeStruct((1, c), F32),
                   jax.ShapeDtypeStruct((1, c), F32), jax.ShapeDtypeStruct((1, c), F32)],
        scratch_shapes=[pltpu.VMEM((CONV_HALO + tm, c), F32), pltpu.VMEM((tm + CONV_HALO, c), F32)],
        compiler_params=_params("arbitrary"),
    )(proj, proj, proj, proj, cpre, cpre, du2, du2, conv_w, ln_g, ln_b)


def _split3(x):
    hi = x.astype(BF16).astype(F32)
    r1 = x - hi
    mid = r1.astype(BF16).astype(F32)
    lo = (r1 - mid).astype(BF16).astype(F32)
    return hi, mid, lo


def _split3_dot(x, tri):
    dot = functools.partial(jnp.dot, preferred_element_type=F32)
    hi, mid, lo = _split3(x)
    return dot(hi.astype(BF16), tri) + dot(mid.astype(BF16), tri) + dot(lo.astype(BF16), tri)


def _to_blocks(a, nb, blk):
    return a.reshape(a.shape[0], nb, blk).transpose(1, 0, 2)


def _from_blocks(a):
    return a.transpose(1, 0, 2).reshape(a.shape[1], -1)


def _forget_fwd(f_t, b_col, nh, t):
    blk = _tile(t, CUM_BLOCK)
    nb = t // blk

    def body(f_ref, b_ref, c_ref):
        ri = lax.broadcasted_iota(jnp.int32, (blk, blk), 0)
        ci = lax.broadcasted_iota(jnp.int32, (blk, blk), 1)
        tri = (ri <= ci).astype(BF16)

        def step(bi, carry):
            xv = f_ref[bi] + b_ref[:, :1]
            lf = jnp.minimum(xv, 0.0) - jnp.log(1.0 + jnp.exp(-jnp.abs(xv)))
            cs = _split3_dot(lf, tri) + carry
            c_ref[bi] = cs
            return cs[:, blk - 1:blk]

        lax.fori_loop(0, nb, step, jnp.zeros((nh, 1), F32))

    out = pl.pallas_call(
        body,
        name="forget_fwd",
        out_shape=jax.ShapeDtypeStruct((nb, nh, blk), F32),
        compiler_params=pltpu.CompilerParams(vmem_limit_bytes=VMEM_LIMIT),
    )(_to_blocks(f_t, nb, blk), b_col)
    return _from_blocks(out)


def _forget_bwd(dc, f_t, b_col, nh, t):
    blk = _tile(t, CUM_BLOCK)
    nb = t // blk

    def body(dc_ref, f_ref, b_ref, df_ref, db_ref):
        ri = lax.broadcasted_iota(jnp.int32, (blk, blk), 0)
        ci = lax.broadcasted_iota(jnp.int32, (blk, blk), 1)
        tri = (ri >= ci).astype(BF16)

        def step(n, carry):
            tail, db = carry
            bi = nb - 1 - n
            rc = _split3_dot(dc_ref[bi], tri) + tail
            df = rc * _sig(-(f_ref[bi] + b_ref[:, :1]))
            df_ref[bi] = df
            return rc[:, 0:1], db + jnp.sum(df, axis=1, keepdims=True)

        _, db = lax.fori_loop(0, nb, step, (jnp.zeros((nh, 1), F32), jnp.zeros((nh, 1), F32)))
        db_ref[...] = jnp.broadcast_to(db, db_ref.shape)

    df, db = pl.pallas_call(
        body,
        name="forget_bwd",
        out_shape=[jax.ShapeDtypeStruct((nb, nh, blk), F32), jax.ShapeDtypeStruct((nh, LANES), F32)],
        compiler_params=pltpu.CompilerParams(vmem_limit_bytes=VMEM_LIMIT),
    )(_to_blocks(dc, nb, blk), _to_blocks(f_t, nb, blk), b_col)
    return _from_blocks(df), db


def _lanes(parts, rows):
    lane = lax.broadcasted_iota(jnp.int32, (rows, LANES), 1)
    out = jnp.zeros((rows, LANES), F32)
    for n, part in enumerate(parts):
        out = jnp.where(lane == n, part, out)
    return out


def _attn_prep_fwd(proj, cs, t, nh, hd, q_off, k_off, v_off):
    tr = _tile(t, ATTN_TILE, 16)
    qb, kb, vb = q_off // hd, k_off // hd, v_off // hd

    def body(q_ref, k_ref, v_ref, cs_ref, qa_ref, ka_ref, va_ref):
        hi, mid, lo = _split3(cs_ref[0][:, :1])
        qa_ref[0, :, :hd] = q_ref[...]
        qa_ref[0, :, hd:] = _lanes([1.0, 1.0, 1.0, hi, mid, lo], tr).astype(BF16)
        ka_ref[0, :, :hd] = k_ref[...]
        ka_ref[0, :, hd:] = _lanes([-hi, -mid, -lo] + [1.0] * 6, tr).astype(BF16)
        va_ref[0, :, :hd] = v_ref[...]
        va_ref[0, :, hd:] = _lanes([-1.0, -1.0, -1.0], tr).astype(BF16)

    wide = pl.BlockSpec((1, tr, 2 * hd), lambda h, i: (h, i, 0))
    return pl.pallas_call(
        body,
        name="attn_prep_fwd",
        grid=(nh, t // tr),
        in_specs=[pl.BlockSpec((tr, hd), lambda h, i: (i, qb + h)), pl.BlockSpec((tr, hd), lambda h, i: (i, kb + h)),
                  pl.BlockSpec((tr, hd), lambda h, i: (i, vb + h)), pl.BlockSpec((1, tr, LANES), lambda h, i: (h, i, 0))],
        out_specs=[wide, wide, wide],
        out_shape=[jax.ShapeDtypeStruct((nh, t, 2 * hd), BF16)] * 3,
        compiler_params=_params("parallel", "parallel"),
    )(proj, proj, proj, cs)


def _attn_prep_bwd(qa, lse, o, do, t, nh, hd):
    tr = _tile(t, ATTN_TILE, 16)
    inv_scale = math.sqrt(hd)

    def body(qa_ref, lse_ref, o_ref, do_ref, qb_ref, da_ref):
        l_hi, l_mid, l_lo = _split3(lse_ref[0][:, :1] * (-inv_scale))
        lane = lax.broadcasted_iota(jnp.int32, (tr, LANES), 1)
        extra = qa_ref[0, :, hd:].astype(F32)
        extra = jnp.where(lane == 6, l_hi, jnp.where(lane == 7, l_mid, jnp.where(lane == 8, l_lo, extra)))
        qb_ref[0, :, :hd] = qa_ref[0, :, :hd]
        qb_ref[0, :, hd:] = extra.astype(BF16)
        dov = do_ref[...]
        delta = jnp.sum(dov.astype(F32) * o_ref[...].astype(F32), axis=1, keepdims=True)
        da_ref[0, :, :hd] = dov
        da_ref[0, :, hd:] = _lanes(list(_split3(delta)), tr).astype(BF16)

    wide = pl.BlockSpec((1, tr, 2 * hd), lambda h, i: (h, i, 0))
    head = pl.BlockSpec((tr, hd), lambda h, i: (i, h))
    return pl.pallas_call(
        body,
        name="attn_prep_bwd",
        grid=(nh, t // tr),
        in_specs=[wide, pl.BlockSpec((1, tr, LANES), lambda h, i: (h, i, 0)), head, head],
        out_specs=[wide, wide],
        out_shape=[jax.ShapeDtypeStruct((nh, t, 2 * hd), BF16)] * 2,
        compiler_params=_params("parallel", "parallel"),
    )(qa, lse, o, do)


def _causal(s, row0, rows, cols):
    row = lax.broadcasted_iota(jnp.int32, (rows, cols), 0) + row0
    col = lax.broadcasted_iota(jnp.int32, (rows, cols), 1)
    return jnp.where(col <= row, s, NEG_INF)


def _attn_fwd(qa, ka, proj, t, nh, hd, v_off):
    tq = _tile(t, ATTN_TILE)
    sub = _tile(tq, ATTN_SUB)
    ns = tq // sub
    scale = 1.0 / math.sqrt(hd)
    vb = v_off // hd

    def body(qa_ref, ka_ref, v_ref, o_ref, lse_ref):
        i = pl.program_id(1)

        def tile(j, carry, masked):
            rows = pl.ds(pl.multiple_of(j * tq, tq), tq)
            kj = ka_ref[0, rows, :]
            vj = v_ref[rows, :]
            new = []
            for r in range(ns):
                m, l, acc = carry[r]
                s = lax.dot_general(qa_ref[0, pl.ds(r * sub, sub), :], kj, _DIMS["nt"], preferred_element_type=F32)
                if masked:
                    s = _causal(s, r * sub, sub, tq)
                m_new = jnp.maximum(m, jnp.max(s, axis=1, keepdims=True))
                p = jnp.exp2((s - m_new) * (scale * LOG2E))
                alpha = jnp.exp2((m - m_new) * (scale * LOG2E))
                l = alpha * l + jnp.sum(p, axis=1, keepdims=True)
                acc = alpha * acc + jnp.dot(p.astype(BF16), vj, preferred_element_type=F32)
                new.append((m_new, l, acc))
            return tuple(new)

        init = tuple((jnp.full((sub, 1), NEG_INF, F32), jnp.zeros((sub, 1), F32), jnp.zeros((sub, hd), F32))
                     for _ in range(ns))
        carry = lax.fori_loop(0, i, lambda j, cr: tile(j, cr, False), init)
        carry = tile(i, carry, True)
        for r in range(ns):
            m, l, acc = carry[r]
            o_ref[pl.ds(r * sub, sub), :] = (acc / l).astype(BF16)
            lse_ref[0, pl.ds(r * sub, sub), :] = jnp.broadcast_to(m * scale + jnp.log(l), (sub, LANES))

    return pl.pallas_call(
        body,
        name="attn_fwd",
        grid=(nh, t // tq),
        in_specs=[pl.BlockSpec((1, tq, 2 * hd), lambda h, i: (h, i, 0)),
                  pl.BlockSpec((1, t, 2 * hd), lambda h, i: (h, 0, 0)),
                  pl.BlockSpec((t, hd), lambda h, i: (0, vb + h))],
        out_specs=[pl.BlockSpec((tq, hd), lambda h, i: (i, h)),
                   pl.BlockSpec((1, tq, LANES), lambda h, i: (h, i, 0))],
        out_shape=[jax.ShapeDtypeStruct((t, nh * hd), BF16), jax.ShapeDtypeStruct((nh, t, LANES), F32)],
        compiler_params=_params("parallel", "parallel"),
    )(qa, ka, proj)


def _attn_bwd(qb, ka, va, da, t, nh, hd):
    tq = _tile(t, ATTN_TILE)
    nq = t // tq
    sub = _tile(tq, ATTN_SUB)
    ns = tq // sub
    scale = 1.0 / math.sqrt(hd)

    def body(qb_ref, ka_ref, va_ref, da_ref, dq_ref, dk_ref, dv_ref, dck_ref, dcq_ref, dq_all):
        j = pl.program_id(1)

        @pl.when(j == 0)
        def _():
            dq_all[...] = jnp.zeros(dq_all.shape, F32)
            dcq_ref[...] = jnp.zeros(dcq_ref.shape, F32)

        kaj = ka_ref[0]
        vaj = va_ref[0]
        kj = kaj[:, :hd]

        def tile(i, carry, masked):
            dk, dv, dck = carry
            for r in range(ns):
                rows = pl.ds(pl.multiple_of(i * tq + r * sub, sub), sub)
                qr = qb_ref[0, rows, :]
                dr = da_ref[0, rows, :]
                s = lax.dot_general(qr, kaj, _DIMS["nt"], preferred_element_type=F32)
                if masked:
                    s = _causal(s, r * sub, sub, tq)
                p = jnp.exp2(s * (scale * LOG2E))
                ds = p * lax.dot_general(dr, vaj, _DIMS["nt"], preferred_element_type=F32)
                dsb = ds.astype(BF16)
                dv = dv + lax.dot_general(p.astype(BF16), dr[:, :hd], _DIMS["tn"], preferred_element_type=F32)
                dk = dk + lax.dot_general(dsb, qr[:, :hd], _DIMS["tn"], preferred_element_type=F32)
                dck = dck - jnp.sum(ds, axis=0, keepdims=True)
                dq_all[rows, :] += jnp.dot(dsb, kj, preferred_element_type=F32)
                dcq_ref[0, rows, :] += jnp.sum(ds, axis=1, keepdims=True)
            return dk, dv, dck

        carry = (jnp.zeros((tq, hd), F32), jnp.zeros((tq, hd), F32), jnp.zeros((1, tq), F32))
        carry = tile(j, carry, True)
        dk, dv, dck = lax.fori_loop(j + 1, nq, lambda i, cr: tile(i, cr, False), carry)
        dk_ref[...] = (dk * scale).astype(BF16)
        dv_ref[...] = dv.astype(BF16)
        dck_ref[0] = dck

        @pl.when(j == nq - 1)
        def _():
            dq_ref[...] = (dq_all[...] * scale).astype(BF16)

    whole = pl.BlockSpec((1, t, 2 * hd), lambda h, j: (h, 0, 0))
    block = pl.BlockSpec((1, tq, 2 * hd), lambda h, j: (h, j, 0))
    return pl.pallas_call(
        body,
        name="attn_bwd",
        grid=(nh, nq),
        in_specs=[whole, block, block, whole],
        out_specs=[
            pl.BlockSpec((t, hd), lambda h, j: (0, h)),
            pl.BlockSpec((tq, hd), lambda h, j: (j, h)),
            pl.BlockSpec((tq, hd), lambda h, j: (j, h)),
            pl.BlockSpec((1, 1, tq), lambda h, j: (h, 0, j)),
            pl.BlockSpec((1, t, LANES), lambda h, j: (h, 0, 0)),
        ],
        out_shape=[jax.ShapeDtypeStruct((t, nh * hd), BF16), jax.ShapeDtypeStruct((t, nh * hd), BF16),
                   jax.ShapeDtypeStruct((t, nh * hd), BF16), jax.ShapeDtypeStruct((nh, 1, t), F32),
                   jax.ShapeDtypeStruct((nh, t, LANES), F32)],
        scratch_shapes=[pltpu.VMEM((t, hd), F32)],
        compiler_params=_params("arbitrary", "arbitrary"),
    )(qb, ka, va, da)


class _Dims:
    def __init__(self, t, d, c, nh, aw, f, pd):
        self.t, self.d, self.c, self.nh, self.aw, self.f, self.pd = t, d, c, nh, aw, f, pd
        self.hd = aw // nh
        self.a_off, self.g_off = 0, c
        self.q_off, self.k_off, self.v_off = 2 * c, 2 * c + aw, 2 * c + 2 * aw
        self.gc_off = 2 * c + 3 * aw
        self.ga_off = self.gc_off + d
        self.n_main = self.ga_off + d


def _ffn_tn(f):
    return _tile(f, 1536)


def _layer_fwd(dm, x, p, w):
    t, d, c, f = dm.t, dm.d, dm.c, dm.f
    s = {"x": x}
    h = _rms_fwd("rms_mix", x, w["n_mix"], t, d)
    proj = _mm("mm_in", "nn", h, w["main"], t, dm.n_main, d, BF16)
    fl = _mm("mm_forget", "nn", h, w["f"], t, LANES, d, F32)
    u2, cpre = _conv_fwd(proj, w["conv_w"], w["conv_b"], w["ln_g"], w["ln_b"], t, c, dm.a_off, dm.g_off)
    f_t = fl[:, :dm.nh].T
    cum = _forget_fwd(f_t, w["b_f"], dm.nh, t)
    cs = jnp.broadcast_to((cum * math.sqrt(dm.hd))[:, :, None], (dm.nh, t, LANES))
    qa, ka, va = _attn_prep_fwd(proj, cs, t, dm.nh, dm.hd, dm.q_off, dm.k_off, dm.v_off)
    o, lse = _attn_fwd(qa, ka, proj, t, dm.nh, dm.hd, dm.v_off)

    def epi_conv(accs, ex):
        return accs[0], _sig(ex[0].astype(F32)) * accs[0]

    yc, m1 = _matmul("mm_conv_out", "nn", [u2], [w["co"]], [(0, 0, 0)], 1, t, d, c, [BF16, BF16],
                     epi=epi_conv, extras=[("mn", (proj, dm.gc_off))], tm=512)

    def epi_attn(accs, ex):
        return accs[0], ex[1].astype(F32) + _sig(ex[0].astype(F32)) * accs[0]

    ya, merged = _matmul("mm_attn_out", "nn", [o], [w["ao"]], [(0, 0, 0)], 1, t, d, dm.aw, [BF16, BF16],
                         epi=epi_attn, extras=[("mn", (proj, dm.ga_off)), ("mn", m1)], tm=512)
    x1 = _matmul("mm_out", "nn", [merged], [w["o"]], [(0, 0, 0)], 1, t, d, d, [F32],
                 epi=lambda accs, ex: [ex[0] + accs[0]], extras=[("mn", x)], tm=512)[0]

    hf = _rms_fwd("rms_ffn", x1, w["n_ffn"], t, d)

    def epi_glu(accs, ex):
        gate, up = accs
        return gate, up, gate * _sig(gate) * up

    gate, up, act = _matmul("mm_gate_up", "nn", [hf], [w["g"], w["u"]], [(0, 0, 0), (0, 1, 1)], 2, t, f, d,
                            [BF16, BF16, BF16], epi=epi_glu, tm=512, tn=_ffn_tn(f), tk=_tile(d, 1024))
    x2 = _matmul("mm_down", "nn", [act], [w["d"]], [(0, 0, 0)], 1, t, d, f, [F32],
                 epi=lambda accs, ex: [ex[0] + accs[0]], extras=[("mn", x1)], tm=512)[0]

    hp = _rms_fwd("rms_ple", x2, w["n_ple"], t, d)
    pp = _mm("mm_ple_proj", "nn", p, w["pp"], t, d, dm.pd, BF16, tm=512)

    def epi_ple(accs, ex):
        sg = _sig(accs[0])
        return sg, ex[1] + sg * ex[0].astype(F32)

    sg, x3 = _matmul("mm_ple_gate", "nn", [hp], [w["pg"]], [(0, 0, 0)], 1, t, d, d, [BF16, F32],
                     epi=epi_ple, extras=[("mn", pp), ("mn", x2)], tm=512)
    s.update(h=h, proj=proj, f_t=f_t, qa=qa, ka=ka, va=va, u2=u2, cpre=cpre, o=o, lse=lse, yc=yc, ya=ya,
             merged=merged, x1=x1, hf=hf, gate=gate, up=up, act=act, x2=x2, hp=hp, pp=pp, sg=sg, p=p)
    return x3, s


def _layer_bwd(dm, dx3, dx3b, s, w):
    t, d, c, f = dm.t, dm.d, dm.c, dm.f
    g = {}

    def ple_ew(dxv, sgv, ppv):
        sgf, ppf = sgv.astype(F32), ppv.astype(F32)
        return dxv * sgf, dxv * ppf * sgf * (1.0 - sgf)

    d_pp, d_z = _ew("ple_bwd", ple_ew, [dx3, s["sg"], s["pp"]], t, d, [BF16, BF16], _tile(t, ROW_TILE, 8))
    g["pp"] = _mm("gw_ple_proj", "tn", s["p"], d_pp, dm.pd, d, t, F32)
    g["pg"] = _mm("gw_ple_gate", "tn", s["hp"], d_z, d, d, t, F32)
    d_hp = _mm("dx_ple_gate", "nt", d_z, w["pg"], t, d, d, BF16)
    dx2, dx2b, g["n_ple"] = _rms_bwd("rms_ple_bwd", d_hp, s["x2"], w["n_ple"], dx3, t, d)

    def epi_dglu(accs, ex):
        gate, up = ex[0].astype(F32), ex[1].astype(F32)
        sg = _sig(gate)
        return accs[0] * up * (sg * (1.0 + gate * (1.0 - sg))), accs[0] * gate * sg

    d_gate, d_up = _matmul("dx_down", "nt", [dx2b], [w["d"]], [(0, 0, 0)], 1, t, f, d, [BF16, BF16],
                           epi=epi_dglu, extras=[("mn", s["gate"]), ("mn", s["up"])], tm=512, tn=_ffn_tn(f),
                           tk=_tile(d, 1024))
    g["d"] = _mm("gw_down", "tn", s["act"], dx2b, f, d, t, F32, tm=_ffn_tn(f))
    g["g"] = _mm("gw_gate", "tn", s["hf"], d_gate, d, f, t, F32, tn=_ffn_tn(f))
    g["u"] = _mm("gw_up", "tn", s["hf"], d_up, d, f, t, F32, tn=_ffn_tn(f))
    d_hf = _matmul("dx_gate_up", "nt", [d_gate, d_up], [w["g"], w["u"]], [(0, 0, 0), (1, 1, 0)], 1, t, d, f,
                   [BF16])[0]
    dx1, dx1b, g["n_ffn"] = _rms_bwd("rms_ffn_bwd", d_hf, s["x1"], w["n_ffn"], dx2, t, d)

    g["o"] = _mm("gw_out", "tn", s["merged"], dx1b, d, d, t, F32)

    def epi_dmerge(accs, ex):
        dmv = accs[0]
        sgc, sga = _sig(ex[0].astype(F32)), _sig(ex[1].astype(F32))
        ycv, yav = ex[2].astype(F32), ex[3].astype(F32)
        return dmv * sgc, dmv * sga, dmv * ycv * sgc * (1.0 - sgc), dmv * yav * sga * (1.0 - sga)

    d_yc, d_ya, d_gc, d_ga = _matmul(
        "dx_out", "nt", [dx1b], [w["o"]], [(0, 0, 0)], 1, t, d, d, [BF16] * 4, epi=epi_dmerge,
        extras=[("mn", (s["proj"], dm.gc_off)), ("mn", (s["proj"], dm.ga_off)), ("mn", s["yc"]), ("mn", s["ya"])],
        tm=512, tn=_tile(d, 512))
    g["co"] = _mm("gw_conv_out", "tn", s["u2"], d_yc, c, d, t, F32)
    d_u2 = _mm("dx_conv_out", "nt", d_yc, w["co"], t, c, d, BF16)
    g["ao"] = _mm("gw_attn_out", "tn", s["o"], d_ya, dm.aw, d, t, F32)
    d_o = _mm("dx_attn_out", "nt", d_ya, w["ao"], t, dm.aw, d, BF16)

    qb, da = _attn_prep_bwd(s["qa"], s["lse"], s["o"], d_o, t, dm.nh, dm.hd)
    dq, dk, dv, dck, dcq = _attn_bwd(qb, s["ka"], s["va"], da, t, dm.nh, dm.hd)
    d_ft, g_bf = _forget_bwd(dck.reshape(dm.nh, t) + dcq[:, :, 0], s["f_t"], w["b_f"], dm.nh, t)
    g["b_f"] = g_bf[:, 0]
    d_f = jnp.pad(d_ft.T, ((0, 0), (0, LANES - dm.nh))).astype(BF16)

    d_a, d_gg, g["conv_w"], g["conv_b"], g["ln_g"], g["ln_b"] = _conv_bwd(
        s["proj"], s["cpre"], d_u2, w["conv_w"], w["ln_g"], w["ln_b"], t, c, dm.a_off, dm.g_off)

    d_proj = jnp.concatenate([d_a, d_gg, dq, dk, dv, d_gc, d_ga], axis=1)
    g["main"] = _mm("gw_in", "tn", s["h"], d_proj, d, dm.n_main, t, F32)
    g["f"] = _mm("gw_forget", "tn", s["h"], d_f, d, LANES, t, F32)
    d_h_f = _mm("dx_forget", "nt", d_f, w["f"], t, d, LANES, BF16)
    d_h = _matmul("dx_in", "nt", [d_proj], [w["main"]], [(0, 0, 0)], 1, t, d, dm.n_main, [BF16],
                  epi=lambda accs, ex: [accs[0] + ex[0].astype(F32)], extras=[("mn", d_h_f)])[0]
    dx, dxb, g["n_mix"] = _rms_bwd("rms_mix_bwd", d_h, s["x"], w["n_mix"], dx1, t, d)
    return dx, dxb, g


def _adamw_tiles(wv, gv, mv, vv):
    m_new = ADAM_B1 * mv + (1.0 - ADAM_B1) * gv
    v_new = ADAM_B2 * vv + (1.0 - ADAM_B2) * (gv * gv)
    m_hat = m_new / (1.0 - ADAM_B1 ** ADAM_STEP)
    v_hat = v_new / (1.0 - ADAM_B2 ** ADAM_STEP)
    delta = -ADAM_LR * (m_hat / (jnp.sqrt(v_hat) + ADAM_EPS) + ADAM_WD * wv)
    return delta, m_new, v_new


def _adamw(name, wv, gv, mv, vv):
    shape = wv.shape
    cols = shape[-1]
    rows = wv.size // cols
    tm = _tile(rows, max(8, (1 << 18) // cols), 8)
    flat = [a.reshape(rows, cols) for a in (wv, gv, mv, vv)]
    outs = _ew(name, _adamw_tiles, flat, rows, cols, [F32, F32, F32], tm)
    return [o.reshape(shape) for o in outs]


def _place():
    return lax.axis_index("x"), lax.axis_index("y"), lax.axis_index("c")


def _other_chips(x, y):
    return [(1 - x, y), (x, 1 - y), (1 - x, 1 - y)]


def _window(ref, kind, chip, size):
    if kind == "chip":
        return ref.at[chip]
    if kind == "row":
        return ref.at[pl.ds(chip * size, size), :]
    return ref.at[:, pl.ds(pl.multiple_of(chip * size, LANES), size)]


def _full_shape(shard, kind):
    depth, a, b = shard.shape
    if kind == "chip":
        return (depth, N_CHIPS, a, b)
    return (depth, N_CHIPS * a, b) if kind == "row" else (depth, a, N_CHIPS * b)


def _gather_weights(shards, kinds):
    n = len(shards)
    sizes = [s.shape[1] if k == "row" else s.shape[2] for s, k in zip(shards, kinds)]

    def body(*refs):
        ins, outs = refs[:n], refs[n:2 * n]
        ici_send, ici_recv, d2d_send, d2d_recv, own_send, own_recv = refs[2 * n:]
        x, y, c = _place()
        me = 2 * x + y
        sibling = (x, y, 1 - c)
        chips = _other_chips(x, y)

        def slab(w, chip, layer):
            return _window(outs[w].at[layer], kinds[w], chip, sizes[w])

        own = []
        for w in range(n):
            for layer in range(2):
                cp = pltpu.make_async_remote_copy(
                    src_ref=ins[w].at[layer], dst_ref=slab(w, me, layer), send_sem=own_send.at[2 * w + layer],
                    recv_sem=own_recv.at[2 * w + layer], device_id=sibling, device_id_type=MESH)
                cp.start()
                own.append(cp)
        sends = []
        for w in range(n):
            for r, (px, py) in enumerate(chips):
                cp = pltpu.make_async_remote_copy(
                    src_ref=ins[w].at[c], dst_ref=slab(w, me, c), send_sem=ici_send.at[3 * w + r],
                    recv_sem=ici_recv.at[3 * w + r], device_id=(px, py, c), device_id_type=MESH)
                cp.start()
                sends.append(cp)
        for w in range(n):
            for r, (px, py) in enumerate(chips):
                landed = slab(w, 2 * px + py, c)
                pltpu.make_async_remote_copy(
                    src_ref=landed, dst_ref=landed, send_sem=ici_send.at[3 * w + r], recv_sem=ici_recv.at[3 * w + r],
                    device_id=(px, py, c), device_id_type=MESH).wait_recv()
                fwd = pltpu.make_async_remote_copy(
                    src_ref=landed, dst_ref=landed, send_sem=d2d_send.at[3 * w + r], recv_sem=d2d_recv.at[3 * w + r],
                    device_id=sibling, device_id_type=MESH)
                fwd.start()
                sends.append(fwd)
        for w in range(n):
            for r, (px, py) in enumerate(chips):
                got = slab(w, 2 * px + py, 1 - c)
                pltpu.make_async_remote_copy(
                    src_ref=got, dst_ref=got, send_sem=d2d_send.at[3 * w + r], recv_sem=d2d_recv.at[3 * w + r],
                    device_id=sibling, device_id_type=MESH).wait_recv()
        for cp in sends:
            cp.wait_send()
        for cp in own:
            cp.wait()

    return pl.pallas_call(
        body,
        name="gather_weights",
        in_specs=[ANY] * n,
        out_specs=[ANY] * n,
        out_shape=[jax.ShapeDtypeStruct(_full_shape(s, k), s.dtype) for s, k in zip(shards, kinds)],
        scratch_shapes=[pltpu.SemaphoreType.DMA((3 * n,))] * 4 + [pltpu.SemaphoreType.DMA((2 * n,))] * 2,
    )(*shards)


def _swap_layer_grads(g0, g1):
    n = len(g0)

    def body(*refs):
        r0, r1, got = refs[:n], refs[n:2 * n], refs[2 * n:3 * n]
        send_sems, recv_sems = refs[3 * n:]
        x, y, c = _place()

        def copy(w, src):
            return pltpu.make_async_remote_copy(src_ref=src, dst_ref=got[w], send_sem=send_sems.at[w],
                                                recv_sem=recv_sems.at[w], device_id=(x, y, 1 - c), device_id_type=MESH)

        @pl.when(c == 0)
        def _():
            for w in range(n):
                copy(w, r1[w]).start()

        @pl.when(c == 1)
        def _():
            for w in range(n):
                copy(w, r0[w]).start()

        for w in range(n):
            copy(w, r0[w]).wait()

    return pl.pallas_call(
        body,
        name="swap_layer_grads",
        in_specs=[ANY] * (2 * n),
        out_specs=[ANY] * n,
        out_shape=[jax.ShapeDtypeStruct(a.shape, a.dtype) for a in g0],
        scratch_shapes=[pltpu.SemaphoreType.DMA((n,)), pltpu.SemaphoreType.DMA((n,))],
    )(*g0, *g1)


def _add_core_partials(name, g0, g1, got, core):
    rows, cols = got.shape
    tm, tn = _tile(rows, 256, 8), _tile(cols, 2048)

    def body(core_ref, g0_ref, g1_ref, got_ref, out_ref):
        mine = jnp.where(core_ref[0] == 0, g0_ref[...], g1_ref[...])
        out_ref[...] = (mine + got_ref[...]).astype(BF16)

    blk = pl.BlockSpec((tm, tn), lambda i, j, core_ref: (i, j))
    return pl.pallas_call(
        body,
        name=name,
        grid_spec=pltpu.PrefetchScalarGridSpec(num_scalar_prefetch=1, grid=(rows // tm, cols // tn),
                                               in_specs=[blk, blk, blk], out_specs=blk),
        out_shape=jax.ShapeDtypeStruct((rows, cols), BF16),
        compiler_params=_params("parallel", "parallel"),
    )(core.reshape(1), g0, g1, got)


def _exchange_chip_grads(parts, kinds, sizes):
    n = len(parts)

    def body(*refs):
        ins, outs = refs[:n], refs[n:2 * n]
        send_sems, recv_sems = refs[2 * n:]
        x, y, c = _place()
        sends = []
        for w in range(n):
            for r, (px, py) in enumerate(_other_chips(x, y)):
                cp = pltpu.make_async_remote_copy(
                    src_ref=_window(ins[w], kinds[w], 2 * px + py, sizes[w]), dst_ref=outs[w].at[r],
                    send_sem=send_sems.at[3 * w + r], recv_sem=recv_sems.at[3 * w + r],
                    device_id=(px, py, c), device_id_type=MESH)
                cp.start()
                sends.append(cp)
        for w in range(n):
            for r, (px, py) in enumerate(_other_chips(x, y)):
                got = outs[w].at[r]
                pltpu.make_async_remote_copy(
                    src_ref=got, dst_ref=got, send_sem=send_sems.at[3 * w + r], recv_sem=recv_sems.at[3 * w + r],
                    device_id=(px, py, c), device_id_type=MESH).wait_recv()
        for cp in sends:
            cp.wait_send()

    return pl.pallas_call(
        body,
        name="exchange_chip_grads",
        in_specs=[ANY] * n,
        out_specs=[ANY] * n,
        out_shape=[jax.ShapeDtypeStruct((3,) + tuple(_shard_shape(p, k, s)), p.dtype)
                   for p, k, s in zip(parts, kinds, sizes)],
        scratch_shapes=[pltpu.SemaphoreType.DMA((3 * n,)), pltpu.SemaphoreType.DMA((3 * n,))],
    )(*parts)


def _shard_shape(whole, kind, size):
    if kind == "chip":
        return whole.shape[1:]
    return (size, whole.shape[1]) if kind == "row" else (whole.shape[0], size)


def _sum_chip_partials(name, part, got, kind, size, chip, core):
    rows, cols = _shard_shape(part, kind, size)
    tm = _tile(rows, max(8, (1 << 19) // cols), 16)

    def body(chip_ref, core_ref, part_ref, g0_ref, g1_ref, g2_ref, out_ref):
        total = part_ref[...].astype(F32)
        for ref in (g0_ref, g1_ref, g2_ref):
            total = total + ref[...].astype(F32)
        out_ref[...] = total

    if kind == "chip":
        mine = pl.BlockSpec((None, tm, cols), lambda i, chip_ref, core_ref: (chip_ref[0], i, 0))
    elif kind == "row":
        mine = pl.BlockSpec((tm, cols), lambda i, chip_ref, core_ref: (chip_ref[0] * (rows // tm) + i, 0))
    else:
        mine = pl.BlockSpec((tm, cols), lambda i, chip_ref, core_ref: (i, chip_ref[0]))
    theirs = [pl.BlockSpec((None, tm, cols), functools.partial(lambda r, i, chip_ref, core_ref: (r, i, 0), r))
              for r in range(3)]
    return pl.pallas_call(
        body,
        name=name,
        grid_spec=pltpu.PrefetchScalarGridSpec(
            num_scalar_prefetch=2, grid=(rows // tm,), in_specs=[mine] + theirs,
            out_specs=pl.BlockSpec((None, tm, cols), lambda i, chip_ref, core_ref: (core_ref[0], i, 0))),
        out_shape=jax.ShapeDtypeStruct((2, rows, cols), F32),
        compiler_params=_params("parallel"),
    )(chip.reshape(1), core.reshape(1), part, got, got, got)


def _share_reduced(both):
    n = len(both)

    def body(*refs):
        ins, outs = refs[:n], refs[n:2 * n]
        send_sems, recv_sems = refs[2 * n:]
        x, y, c = _place()
        sends = []
        for w in range(n):
            cp = pltpu.make_async_remote_copy(src_ref=ins[w].at[c], dst_ref=outs[w].at[c], send_sem=send_sems.at[w],
                                              recv_sem=recv_sems.at[w], device_id=(x, y, 1 - c), device_id_type=MESH)
            cp.start()
            sends.append(cp)
        for w in range(n):
            got = outs[w].at[1 - c]
            pltpu.make_async_remote_copy(src_ref=got, dst_ref=got, send_sem=send_sems.at[w], recv_sem=recv_sems.at[w],
                                         device_id=(x, y, 1 - c), device_id_type=MESH).wait_recv()
        for cp in sends:
            cp.wait_send()

    return pl.pallas_call(
        body,
        name="share_reduced",
        in_specs=[ANY] * n,
        out_specs=[ANY] * n,
        out_shape=[jax.ShapeDtypeStruct(a.shape, a.dtype) for a in both],
        input_output_aliases={w: w for w in range(n)},
        scratch_shapes=[pltpu.SemaphoreType.DMA((n,)), pltpu.SemaphoreType.DMA((n,))],
    )(*both)


def _allreduce_small(v):
    rows, width = v.shape

    def body(v_ref, out_ref, slots, send_sems, recv_sems):
        x, y, c = _place()
        me = 4 * x + 2 * y + c
        slots[me] = v_ref[...]
        peers = [(x, y, 1 - c)]
        for px, py in _other_chips(x, y):
            peers += [(px, py, c), (px, py, 1 - c)]
        sends = []
        for r, peer in enumerate(peers):
            cp = pltpu.make_async_remote_copy(
                src_ref=v_ref, dst_ref=slots.at[me], send_sem=send_sems.at[r], recv_sem=recv_sems.at[r],
                device_id=peer, device_id_type=MESH)
            cp.start()
            sends.append(cp)
        for r, (px, py, pc) in enumerate(peers):
            got = slots.at[4 * px + 2 * py + pc]
            pltpu.make_async_remote_copy(
                src_ref=got, dst_ref=got, send_sem=send_sems.at[r], recv_sem=recv_sems.at[r],
                device_id=(px, py, pc), device_id_type=MESH).wait_recv()
        for cp in sends:
            cp.wait_send()
        total = slots[0]
        for n in range(1, 8):
            total = total + slots[n]
        out_ref[...] = total

    vm = pl.BlockSpec(memory_space=pltpu.VMEM)
    return pl.pallas_call(
        body,
        name="allreduce_small",
        in_specs=[vm],
        out_specs=vm,
        out_shape=jax.ShapeDtypeStruct((rows, width), F32),
        scratch_shapes=[pltpu.VMEM((8, rows, width), F32), pltpu.SemaphoreType.DMA((7,)),
                        pltpu.SemaphoreType.DMA((7,))],
    )(v)


def _pad_rows(flat, row_align):
    n = flat.shape[0]
    rows = -(-n // PACK_W)
    rows = -(-rows // row_align) * row_align
    return jnp.pad(flat, (0, rows * PACK_W - n)).reshape(rows, PACK_W)


def _unpack(buf, shapes):
    flat = buf.reshape(-1)
    out, off = [], 0
    for shp in shapes:
        n = math.prod(shp)
        out.append(flat[off:off + n].reshape(shp))
        off += n
    return out


BIG = (("w_in", "chip"), ("w_conv_out", "col"), ("w_attn_out", "col"), ("w_out", "row"), ("w_gate_up", "col"),
       ("w_down", "row"), ("w_ple_gate", "row"), ("w_ple_proj", "col"))


def kernel(x, p, norm_mix_g, w_in, b_forget, conv_w, conv_b, conv_ln_g, conv_ln_b, w_conv_out, w_attn_out, w_out, norm_ffn_g, w_gate_up, w_down, norm_ple_g, w_ple_gate, w_ple_proj, final_g, loss_target, m_norm_mix_g, m_w_in, m_b_forget, m_conv_w, m_conv_b, m_conv_ln_g, m_conv_ln_b, m_w_conv_out, m_w_attn_out, m_w_out, m_norm_ffn_g, m_w_gate_up, m_w_down, m_norm_ple_g, m_w_ple_gate, m_w_ple_proj, m_final_g, v_norm_mix_g, v_w_in, v_b_forget, v_conv_w, v_conv_b, v_conv_ln_g, v_conv_ln_b, v_w_conv_out, v_w_attn_out, v_w_out, v_norm_ffn_g, v_w_gate_up, v_w_down, v_norm_ple_g, v_w_ple_gate, v_w_ple_proj, v_final_g):
    wts = dict(norm_mix_g=norm_mix_g, w_in=w_in, b_forget=b_forget, conv_w=conv_w, conv_b=conv_b,
               conv_ln_g=conv_ln_g, conv_ln_b=conv_ln_b, w_conv_out=w_conv_out, w_attn_out=w_attn_out,
               w_out=w_out, norm_ffn_g=norm_ffn_g, w_gate_up=w_gate_up, w_down=w_down, norm_ple_g=norm_ple_g,
               w_ple_gate=w_ple_gate, w_ple_proj=w_ple_proj, final_g=final_g)
    mom1 = dict(norm_mix_g=m_norm_mix_g, w_in=m_w_in, b_forget=m_b_forget, conv_w=m_conv_w, conv_b=m_conv_b,
                conv_ln_g=m_conv_ln_g, conv_ln_b=m_conv_ln_b, w_conv_out=m_w_conv_out, w_attn_out=m_w_attn_out,
                w_out=m_w_out, norm_ffn_g=m_norm_ffn_g, w_gate_up=m_w_gate_up, w_down=m_w_down,
                norm_ple_g=m_norm_ple_g, w_ple_gate=m_w_ple_gate, w_ple_proj=m_w_ple_proj, final_g=m_final_g)
    mom2 = dict(norm_mix_g=v_norm_mix_g, w_in=v_w_in, b_forget=v_b_forget, conv_w=v_conv_w, conv_b=v_conv_b,
                conv_ln_g=v_conv_ln_g, conv_ln_b=v_conv_ln_b, w_conv_out=v_w_conv_out, w_attn_out=v_w_attn_out,
                w_out=v_w_out, norm_ffn_g=v_norm_ffn_g, w_gate_up=v_w_gate_up, w_down=v_w_down,
                norm_ple_g=v_norm_ple_g, w_ple_gate=v_w_ple_gate, w_ple_proj=v_w_ple_proj, final_g=v_final_g)
    order = list(wts)
    depth = w_in.shape[0]
    assert depth == 2, "the exchanges give one layer to each of a chip's two cores"
    t, d = x.shape[1], x.shape[2]
    c = conv_ln_g.shape[1]
    nh = b_forget.shape[1]
    aw = w_attn_out.shape[1]
    f = N_CHIPS * w_down.shape[1]
    pd = w_ple_proj.shape[1]
    dm = _Dims(t, d, c, nh, aw, f, pd)
    n_split = 2 * c + 3 * aw
    cw = conv_w.shape[2]
    chip = 2 * lax.axis_index("x") + lax.axis_index("y")
    core = lax.axis_index("c")
    big_names = [name for name, _ in BIG]
    big_kinds = [kind for _, kind in BIG]

    gathered = _gather_weights([wts[name].astype(BF16) for name in big_names] + [conv_w], big_kinds + ["chip"])
    full = dict(zip(big_names + ["conv_w"], gathered))

    def layer_weights(l):
        wi = jnp.concatenate([full["w_in"][l, k] for k in range(N_CHIPS)], axis=1)
        cwl = jnp.concatenate([full["conv_w"][l, k] for k in range(N_CHIPS)], axis=1)
        return {
            "main": jnp.concatenate([wi[:, :n_split], wi[:, n_split + nh:]], axis=1),
            "f": jnp.pad(wi[:, n_split:n_split + nh], ((0, 0), (0, LANES - nh))),
            "conv_w": jnp.pad(cwl, ((0, CONV_HALO - CONV_K), (0, 0))),
            "conv_b": conv_b[l][None], "ln_g": conv_ln_g[l][None], "ln_b": conv_ln_b[l][None],
            "b_f": jnp.broadcast_to(b_forget[l][:, None], (nh, LANES)),
            "co": (full["w_conv_out"], 0, (l,)), "ao": (full["w_attn_out"], 0, (l,)), "o": (full["w_out"], 0, (l,)),
            "g": (full["w_gate_up"], 0, (l,)), "u": (full["w_gate_up"], f, (l,)), "d": (full["w_down"], 0, (l,)),
            "pg": (full["w_ple_gate"], 0, (l,)), "pp": (full["w_ple_proj"], 0, (l,)),
            "n_mix": norm_mix_g[l][None], "n_ffn": norm_ffn_g[l][None], "n_ple": norm_ple_g[l][None],
        }

    lw = [layer_weights(l) for l in range(depth)]

    xl = x[0]
    saved = []
    for l in range(depth):
        xl, s = _layer_fwd(dm, xl, p[l, 0], lw[l])
        saved.append(s)
    loss_row, dx, dxb, g_final = _loss_head(xl, final_g[None], loss_target[0], t, d)
    loss = lax.psum(loss_row[0, 0], ("x", "y", "c"))
    lg = [None] * depth
    for l in reversed(range(depth)):
        dx, dxb, lg[l] = _layer_bwd(dm, dx, dxb, saved[l], lw[l])

    def layer_grads(l):
        g = lg[l]
        w_in_g = jnp.concatenate([g["main"][:, :n_split], g["f"][:, :nh], g["main"][:, n_split:]], axis=1)
        return [w_in_g.reshape(d, N_CHIPS, -1).transpose(1, 0, 2), g["co"], g["ao"], g["o"],
                jnp.concatenate([g["g"], g["u"]], axis=1), g["d"], g["pg"], g["pp"]]

    g0, g1 = layer_grads(0), layer_grads(1)
    sizes = [wts[name].shape[1] if kind == "row" else wts[name].shape[2] for name, kind in BIG]
    from_sibling = _swap_layer_grads(g0, g1)

    def flat2(a):
        return a.reshape(-1, a.shape[-1])

    chip_parts = [_add_core_partials("add_core_partials_" + name, flat2(a0), flat2(a1), flat2(got), core)
                  .reshape(got.shape) for name, a0, a1, got in zip(big_names, g0, g1, from_sibling)]
    from_chips = _exchange_chip_grads(chip_parts, big_kinds, sizes)
    reduced = [_sum_chip_partials("sum_chip_partials_" + name, part, got, kind, size, chip, core)
               for name, part, got, kind, size in zip(big_names, chip_parts, from_chips, big_kinds, sizes)]
    grads = dict(zip(big_names, _share_reduced(reduced)))

    def stacked(key):
        return jnp.stack([lg[l][key] for l in range(depth)])

    small = ["norm_mix_g", "b_forget", "conv_b", "conv_ln_g", "conv_ln_b", "norm_ffn_g", "norm_ple_g", "final_g"]
    small_g = {
        "norm_mix_g": jnp.concatenate([lg[l]["n_mix"] for l in range(depth)]),
        "b_forget": stacked("b_f"),
        "conv_b": jnp.concatenate([lg[l]["conv_b"] for l in range(depth)]),
        "conv_ln_g": jnp.concatenate([lg[l]["ln_g"] for l in range(depth)]),
        "conv_ln_b": jnp.concatenate([lg[l]["ln_b"] for l in range(depth)]),
        "norm_ffn_g": jnp.concatenate([lg[l]["n_ffn"] for l in range(depth)]),
        "norm_ple_g": jnp.concatenate([lg[l]["n_ple"] for l in range(depth)]),
        "final_g": g_final[0],
    }
    conv_w_g = jnp.stack([lg[l]["conv_w"][:CONV_K] for l in range(depth)])
    small_shapes = [wts[n].shape for n in small] + [conv_w_g.shape]
    small_pack = _pad_rows(jnp.concatenate([small_g[n].reshape(-1) for n in small] + [conv_w_g.reshape(-1)]), 8)
    small_sum = _unpack(_allreduce_small(small_pack), small_shapes)
    for n, name in enumerate(small):
        grads[name] = small_sum[n]
    grads["conv_w"] = lax.dynamic_slice_in_dim(small_sum[len(small)], chip * cw, cw, axis=2)

    def pack_small(src):
        return _pad_rows(jnp.concatenate([src[n].reshape(-1) for n in small]), 8)

    small_upd = _adamw("adamw_small", pack_small(wts), pack_small(grads), pack_small(mom1), pack_small(mom2))
    small_upd = [_unpack(u, [wts[n].shape for n in small]) for u in small_upd]
    delta, new_m, new_v = {}, {}, {}
    for n, name in enumerate(small):
        delta[name], new_m[name], new_v[name] = small_upd[0][n], small_upd[1][n], small_upd[2][n]
    for name in big_names + ["conv_w"]:
        delta[name], new_m[name], new_v[name] = _adamw("adamw_" + name, wts[name], grads[name], mom1[name],
                                                       mom2[name])

    return (loss, dx[None], *[grads[n] for n in order], *[delta[n] for n in order],
            *[new_m[n] for n in order], *[new_v[n] for n in order])
```

```python
import functools
import math

import jax
import jax.numpy as jnp
from jax import lax
from jax.experimental import pallas as pl
from jax.experimental.pallas import tpu as pltpu

F32 = jnp.float32
BF16 = jnp.bfloat16

EPS = 1e-6
CONV_K = 31
NEG_INF = -1e30
LOG2E = 1.4426950408889634
ADAM_LR = 0.001
ADAM_B1 = 0.9
ADAM_B2 = 0.999
ADAM_EPS = 1e-08
ADAM_WD = 0.01
ADAM_STEP = 10

LANES = 128
VMEM_LIMIT = 60 * 1024 * 1024
PACK_W = 1024
N_CHIPS = 4
CONV_HALO = 32
CUM_BLOCK = 256

MM_TM = 1024
MM_TN = 1024
MM_TK = 2048
ROW_TILE = 256
CONV_TILE = 256
ATTN_TILE = 1024
ATTN_SUB = 1024

MESH = pl.DeviceIdType.MESH
ANY = pl.BlockSpec(memory_space=pl.ANY)


def _params(*sem):
    return pltpu.CompilerParams(dimension_semantics=sem, vmem_limit_bytes=VMEM_LIMIT)


def _tile(dim, pref, align=LANES):
    if dim <= pref:
        return dim
    t = (pref // align) * align
    while t >= align:
        if dim % t == 0:
            return t
        t -= align
    return dim


def _sig(x):
    return 1.0 / (1.0 + jnp.exp(-x))


def _op(a):
    if not isinstance(a, tuple):
        return a, 0, ()
    return a if len(a) == 3 else (a[0], a[1], ())


def _spec(block, index, lead):
    if not lead:
        return pl.BlockSpec(block, index)
    return pl.BlockSpec((None,) * len(lead) + block, lambda *g: tuple(lead) + index(*g))


_DIMS = {
    "nn": (((1,), (0,)), ((), ())),
    "nt": (((1,), (1,)), ((), ())),
    "tn": (((0,), (0,)), ((), ())),
}


def _matmul(name, mode, a_ops, b_ops, terms, n_acc, m, n, k, out_dtypes, epi=None, extras=(),
            tm=None, tn=None, tk=None):
    tm = tm or _tile(m, MM_TM)
    tn = tn or _tile(n, MM_TN)
    tk = tk or _tile(k, MM_TK)
    nk = k // tk
    a_ops = [_op(a) for a in a_ops]
    b_ops = [_op(b) for b in b_ops]
    extras = [(kind, _op(e)) for kind, e in extras]
    na, nb, ne, no = len(a_ops), len(b_ops), len(extras), len(out_dtypes)

    def a_spec(off, lead):
        if mode == "tn":
            assert off % tm == 0
            return _spec((tk, tm), lambda i, j, kk: (kk, i + off // tm), lead)
        assert off % tk == 0
        return _spec((tm, tk), lambda i, j, kk: (i, kk + off // tk), lead)

    def b_spec(off, lead):
        if mode == "nt":
            assert off % tk == 0
            return _spec((tn, tk), lambda i, j, kk: (j, kk + off // tk), lead)
        assert off % tn == 0
        return _spec((tk, tn), lambda i, j, kk: (kk, j + off // tn), lead)

    def e_spec(kind, off, lead):
        assert off % tn == 0
        if kind == "n":
            return _spec((1, tn), lambda i, j, kk: (0, j + off // tn), lead)
        return _spec((tm, tn), lambda i, j, kk: (i, j + off // tn), lead)

    def body(*refs):
        a_refs = refs[:na]
        b_refs = refs[na:na + nb]
        e_refs = refs[na + nb:na + nb + ne]
        o_refs = refs[na + nb + ne:na + nb + ne + no]
        acc_refs = refs[na + nb + ne + no:]
        kk = pl.program_id(2)
        av = [r[...].astype(BF16) for r in a_refs]
        bv = [r[...].astype(BF16) for r in b_refs]
        sums = [None] * n_acc
        for ai, bi, ci in terms:
            part = lax.dot_general(av[ai], bv[bi], _DIMS[mode], preferred_element_type=F32)
            sums[ci] = part if sums[ci] is None else sums[ci] + part

        def finish(accs):
            outs = epi(accs, [e[...] for e in e_refs]) if epi is not None else accs
            for o, val in zip(o_refs, outs):
                o[...] = val.astype(o.dtype)

        if nk == 1:
            finish(sums)
            return

        @pl.when(kk == 0)
        def _():
            for acc, part in zip(acc_refs, sums):
                acc[...] = part

        @pl.when(kk > 0)
        def _():
            for acc, part in zip(acc_refs, sums):
                acc[...] += part

        @pl.when(kk == nk - 1)
        def _():
            finish([acc[...] for acc in acc_refs])

    outs = pl.pallas_call(
        body,
        name=name,
        grid=(m // tm, n // tn, nk),
        in_specs=[a_spec(off, lead) for _, off, lead in a_ops] + [b_spec(off, lead) for _, off, lead in b_ops]
        + [e_spec(kind, off, lead) for kind, (_, off, lead) in extras],
        out_specs=[pl.BlockSpec((tm, tn), lambda i, j, kk: (i, j)) for _ in out_dtypes],
        out_shape=[jax.ShapeDtypeStruct((m, n), dt) for dt in out_dtypes],
        scratch_shapes=[pltpu.VMEM((tm, tn), F32) for _ in range(n_acc if nk > 1 else 0)],
        compiler_params=_params("parallel", "parallel", "arbitrary"),
    )(*[a[0] for a in a_ops], *[b[0] for b in b_ops], *[e[0] for _, e in extras])
    return outs


def _mm(name, mode, a, b, m, n, k, out_dtype, **kw):
    return _matmul(name, mode, [a], [b], [(0, 0, 0)], 1, m, n, k, [out_dtype], **kw)[0]


def _ew(name, fn, ins, m, n, out_dtypes, tm, tn=None):
    tn = tn or n
    ins = [_op(a) for a in ins]
    ni = len(ins)

    def spec(off, lead):
        assert off % tn == 0
        return _spec((tm, tn), lambda i, j: (i, j + off // tn), lead)

    def body(*refs):
        outs = fn(*[r[...] for r in refs[:ni]])
        for o, val in zip(refs[ni:], outs):
            o[...] = val.astype(o.dtype)

    return pl.pallas_call(
        body,
        name=name,
        grid=(m // tm, n // tn),
        in_specs=[spec(off, lead) for _, off, lead in ins],
        out_specs=[pl.BlockSpec((tm, tn), lambda i, j: (i, j)) for _ in out_dtypes],
        out_shape=[jax.ShapeDtypeStruct((m, n), dt) for dt in out_dtypes],
        compiler_params=_params("parallel", "parallel"),
    )(*[a[0] for a in ins])


def _rms_fwd(name, x, g, t, d):
    tr = _tile(t, ROW_TILE, 8)

    def body(x_ref, g_ref, h_ref):
        xv = x_ref[...]
        r = lax.rsqrt(jnp.mean(xv * xv, axis=1, keepdims=True) + EPS)
        h_ref[...] = (xv * r * g_ref[...]).astype(BF16)

    return pl.pallas_call(
        body,
        name=name,
        grid=(t // tr,),
        in_specs=[pl.BlockSpec((tr, d), lambda i: (i, 0)), pl.BlockSpec((1, d), lambda i: (0, 0))],
        out_specs=pl.BlockSpec((tr, d), lambda i: (i, 0)),
        out_shape=jax.ShapeDtypeStruct((t, d), BF16),
        compiler_params=_params("parallel"),
    )(x, g)


def _rms_bwd_rows(dh, xv, g):
    r = lax.rsqrt(jnp.mean(xv * xv, axis=1, keepdims=True) + EPS)
    xhat = xv * r
    dxh = dh * g
    dx = r * (dxh - xhat * jnp.mean(dxh * xhat, axis=1, keepdims=True))
    return dx, dh * xhat


def _rms_bwd(name, dh, x, g, dres, t, d):
    tr = _tile(t, ROW_TILE, 8)

    def body(dh_ref, x_ref, g_ref, dres_ref, dx_ref, dxb_ref, dg_ref):
        @pl.when(pl.program_id(0) == 0)
        def _():
            dg_ref[...] = jnp.zeros(dg_ref.shape, F32)

        dx, dg_rows = _rms_bwd_rows(dh_ref[...].astype(F32), x_ref[...], g_ref[...])
        dx = dx + dres_ref[...]
        dx_ref[...] = dx
        dxb_ref[...] = dx.astype(BF16)
        dg_ref[...] += jnp.sum(dg_rows, axis=0, keepdims=True)

    row = pl.BlockSpec((tr, d), lambda i: (i, 0))
    vec = pl.BlockSpec((1, d), lambda i: (0, 0))
    return pl.pallas_call(
        body,
        name=name,
        grid=(t // tr,),
        in_specs=[row, row, vec, row],
        out_specs=[row, row, vec],
        out_shape=[jax.ShapeDtypeStruct((t, d), F32), jax.ShapeDtypeStruct((t, d), BF16),
                   jax.ShapeDtypeStruct((1, d), F32)],
        compiler_params=_params("arbitrary"),
    )(dh, x, g, dres)


def _loss_head(x, g, target, t, d):
    tr = _tile(t, ROW_TILE, 8)

    def body(x_ref, g_ref, tgt_ref, loss_ref, dx_ref, dxb_ref, dg_ref):
        @pl.when(pl.program_id(0) == 0)
        def _():
            dg_ref[...] = jnp.zeros(dg_ref.shape, F32)
            loss_ref[...] = jnp.zeros(loss_ref.shape, F32)

        xv = x_ref[...]
        gv = g_ref[...]
        r = lax.rsqrt(jnp.mean(xv * xv, axis=1, keepdims=True) + EPS)
        err = xv * r * gv - tgt_ref[...]
        loss_ref[...] += (0.5 / d) * jnp.sum(err * err)
        dx, dg_rows = _rms_bwd_rows(err * (1.0 / d), xv, gv)
        dx_ref[...] = dx
        dxb_ref[...] = dx.astype(BF16)
        dg_ref[...] += jnp.sum(dg_rows, axis=0, keepdims=True)

    row = pl.BlockSpec((tr, d), lambda i: (i, 0))
    vec = pl.BlockSpec((1, d), lambda i: (0, 0))
    one = pl.BlockSpec((1, LANES), lambda i: (0, 0))
    return pl.pallas_call(
        body,
        name="loss_head",
        grid=(t // tr,),
        in_specs=[row, vec, row],
        out_specs=[one, row, row, vec],
        out_shape=[jax.ShapeDtypeStruct((1, LANES), F32), jax.ShapeDtypeStruct((t, d), F32),
                   jax.ShapeDtypeStruct((t, d), BF16), jax.ShapeDtypeStruct((1, d), F32)],
        compiler_params=_params("arbitrary"),
    )(x, g, target)


def _layernorm_rows(cv, g, b):
    mu = jnp.mean(cv, axis=1, keepdims=True)
    xc = cv - mu
    rstd = lax.rsqrt(jnp.mean(xc * xc, axis=1, keepdims=True) + EPS)
    xhat = xc * rstd
    return xhat, rstd, xhat * g + b


def _conv_fwd(proj, conv_w, conv_b, ln_g, ln_b, t, c, a_off, g_off):
    tm = _tile(t, CONV_TILE, CONV_HALO)
    per = tm // CONV_HALO
    ab, gb = a_off // c, g_off // c

    def body(a_ref, g_ref, ap_ref, gp_ref, w_ref, cb_ref, lg_ref, lb_ref, u2_ref, c_ref, upad):
        i = pl.program_id(0)
        u_prev = ap_ref[...].astype(F32) * _sig(gp_ref[...].astype(F32))
        upad[pl.ds(0, CONV_HALO), :] = jnp.where(i > 0, u_prev, 0.0)
        upad[pl.ds(CONV_HALO, tm), :] = a_ref[...].astype(F32) * _sig(g_ref[...].astype(F32))
        acc = jnp.zeros((tm, c), F32) + cb_ref[...]
        for k in range(CONV_K):
            acc = acc + w_ref[pl.ds(k, 1), :] * upad[pl.ds(CONV_HALO - (CONV_K - 1) + k, tm), :]
        c_ref[...] = acc
        _, _, z = _layernorm_rows(acc, lg_ref[...], lb_ref[...])
        u2_ref[...] = (z * _sig(z)).astype(BF16)

    vec = pl.BlockSpec((1, c), lambda i: (0, 0))
    return pl.pallas_call(
        body,
        name="conv_fwd",
        grid=(t // tm,),
        in_specs=[
            pl.BlockSpec((tm, c), lambda i: (i, ab)),
            pl.BlockSpec((tm, c), lambda i: (i, gb)),
            pl.BlockSpec((CONV_HALO, c), lambda i: (jnp.maximum(i * per - 1, 0), ab)),
            pl.BlockSpec((CONV_HALO, c), lambda i: (jnp.maximum(i * per - 1, 0), gb)),
            pl.BlockSpec((CONV_HALO, c), lambda i: (0, 0)), vec, vec, vec,
        ],
        out_specs=[pl.BlockSpec((tm, c), lambda i: (i, 0)), pl.BlockSpec((tm, c), lambda i: (i, 0))],
        out_shape=[jax.ShapeDtypeStruct((t, c), BF16), jax.ShapeDtypeStruct((t, c), F32)],
        scratch_shapes=[pltpu.VMEM((CONV_HALO + tm, c), F32)],
        compiler_params=_params("parallel"),
    )(proj, proj, proj, proj, conv_w, conv_b, ln_g, ln_b)


def _conv_bwd(proj, cpre, du2, conv_w, ln_g, ln_b, t, c, a_off, g_off):
    tm = _tile(t, CONV_TILE, CONV_HALO)
    per = tm // CONV_HALO
    nt = t // tm
    last_halo = t // CONV_HALO - 1
    ab, gb = a_off // c, g_off // c

    def body(a_ref, g_ref, ap_ref, gp_ref, c_ref, cn_ref, du_ref, dun_ref, w_ref, lg_ref, lb_ref,
             da_ref, dg_ref, gw_ref, gcb_ref, glg_ref, glb_ref, upad, dpad):
        i = pl.program_id(0)

        @pl.when(i == 0)
        def _():
            gw_ref[...] = jnp.zeros(gw_ref.shape, F32)
            gcb_ref[...] = jnp.zeros(gcb_ref.shape, F32)
            glg_ref[...] = jnp.zeros(glg_ref.shape, F32)
            glb_ref[...] = jnp.zeros(glb_ref.shape, F32)

        lg = lg_ref[...]
        lb = lb_ref[...]

        def ln_bwd(cv, duv):
            xhat, rstd, z = _layernorm_rows(cv, lg, lb)
            sz = _sig(z)
            dz = duv * (sz * (1.0 + z * (1.0 - sz)))
            dxh = dz * lg
            dc = rstd * (dxh - jnp.mean(dxh, axis=1, keepdims=True)
                         - xhat * jnp.mean(dxh * xhat, axis=1, keepdims=True))
            return dc, dz, xhat

        dc, dz, xhat = ln_bwd(c_ref[...], du_ref[...].astype(F32))
        dc_next, _, _ = ln_bwd(cn_ref[...], dun_ref[...].astype(F32))
        glg_ref[...] += jnp.sum(dz * xhat, axis=0, keepdims=True)
        glb_ref[...] += jnp.sum(dz, axis=0, keepdims=True)
        gcb_ref[...] += jnp.sum(dc, axis=0, keepdims=True)
        dpad[pl.ds(0, tm), :] = dc
        dpad[pl.ds(tm, CONV_HALO), :] = jnp.where(i < nt - 1, dc_next, 0.0)

        av = a_ref[...].astype(F32)
        sg = _sig(g_ref[...].astype(F32))
        u_prev = ap_ref[...].astype(F32) * _sig(gp_ref[...].astype(F32))
        upad[pl.ds(0, CONV_HALO), :] = jnp.where(i > 0, u_prev, 0.0)
        upad[pl.ds(CONV_HALO, tm), :] = av * sg

        du = jnp.zeros((tm, c), F32)
        for k in range(CONV_K):
            du = du + w_ref[pl.ds(k, 1), :] * dpad[pl.ds(CONV_K - 1 - k, tm), :]
            gw_ref[pl.ds(k, 1), :] += jnp.sum(
                dc * upad[pl.ds(CONV_HALO - (CONV_K - 1) + k, tm), :], axis=0, keepdims=True)
        da_ref[...] = (du * sg).astype(BF16)
        dg_ref[...] = (du * av * sg * (1.0 - sg)).astype(BF16)

    vec = pl.BlockSpec((1, c), lambda i: (0, 0))
    cur = pl.BlockSpec((tm, c), lambda i: (i, 0))
    nxt = pl.BlockSpec((CONV_HALO, c), lambda i: (jnp.minimum((i + 1) * per, last_halo), 0))
    wsp = pl.BlockSpec((CONV_HALO, c), lambda i: (0, 0))
    return pl.pallas_call(
        body,
        name="conv_bwd",
        grid=(nt,),
        in_specs=[
            pl.BlockSpec((tm, c), lambda i: (i, ab)),
            pl.BlockSpec((tm, c), lambda i: (i, gb)),
            pl.BlockSpec((CONV_HALO, c), lambda i: (jnp.maximum(i * per - 1, 0), ab)),
            pl.BlockSpec((CONV_HALO, c), lambda i: (jnp.maximum(i * per - 1, 0), gb)),
            cur, nxt, cur, nxt, wsp, vec, vec,
        ],
        out_specs=[cur, cur, wsp, vec, vec, vec],
        out_shape=[jax.ShapeDtypeStruct((t, c), BF16), jax.ShapeDtypeStruct((t, c), BF16),
                   jax.ShapeDtypeStruct((CONV_HALO, c), F32), jax.ShapeDtypeStruct((1, c), F32),
                   jax.ShapeDtypeStruct((1, c), F32), jax.ShapeDtypeStruct((1, c), F32)],
        scratch_shapes=[pltpu.VMEM((CONV_HALO + tm, c), F32), pltpu.VMEM((tm + CONV_HALO, c), F32)],
        compiler_params=_params("arbitrary"),
    )(proj, proj, proj, proj, cpre, cpre, du2, du2, conv_w, ln_g, ln_b)


def _split3(x):
    hi = x.astype(BF16).astype(F32)
    r1 = x - hi
    mid = r1.astype(BF16).astype(F32)
    lo = (r1 - mid).astype(BF16).astype(F32)
    return hi, mid, lo


def _split3_dot(x, tri):
    dot = functools.partial(jnp.dot, preferred_element_type=F32)
    hi, mid, lo = _split3(x)
    return dot(hi.astype(BF16), tri) + dot(mid.astype(BF16), tri) + dot(lo.astype(BF16), tri)


def _to_blocks(a, nb, blk):
    return a.reshape(a.shape[0], nb, blk).transpose(1, 0, 2)


def _from_blocks(a):
    return a.transpose(1, 0, 2).reshape(a.shape[1], -1)


def _forget_fwd(f_t, b_col, nh, t):
    blk = _tile(t, CUM_BLOCK)
    nb = t // blk

    def body(f_ref, b_ref, c_ref):
        ri = lax.broadcasted_iota(jnp.int32, (blk, blk), 0)
        ci = lax.broadcasted_iota(jnp.int32, (blk, blk), 1)
        tri = (ri <= ci).astype(BF16)

        def step(bi, carry):
            xv = f_ref[bi] + b_ref[:, :1]
            lf = jnp.minimum(xv, 0.0) - jnp.log(1.0 + jnp.exp(-jnp.abs(xv)))
            cs = _split3_dot(lf, tri) + carry
            c_ref[bi] = cs
            return cs[:, blk - 1:blk]

        lax.fori_loop(0, nb, step, jnp.zeros((nh, 1), F32))

    out = pl.pallas_call(
        body,
        name="forget_fwd",
        out_shape=jax.ShapeDtypeStruct((nb, nh, blk), F32),
        compiler_params=pltpu.CompilerParams(vmem_limit_bytes=VMEM_LIMIT),
    )(_to_blocks(f_t, nb, blk), b_col)
    return _from_blocks(out)


def _forget_bwd(dc, f_t, b_col, nh, t):
    blk = _tile(t, CUM_BLOCK)
    nb = t // blk

    def body(dc_ref, f_ref, b_ref, df_ref, db_ref):
        ri = lax.broadcasted_iota(jnp.int32, (blk, blk), 0)
        ci = lax.broadcasted_iota(jnp.int32, (blk, blk), 1)
        tri = (ri >= ci).astype(BF16)

        def step(n, carry):
            tail, db = carry
            bi = nb - 1 - n
            rc = _split3_dot(dc_ref[bi], tri) + tail
            df = rc * _sig(-(f_ref[bi] + b_ref[:, :1]))
            df_ref[bi] = df
            return rc[:, 0:1], db + jnp.sum(df, axis=1, keepdims=True)

        _, db = lax.fori_loop(0, nb, step, (jnp.zeros((nh, 1), F32), jnp.zeros((nh, 1), F32)))
        db_ref[...] = jnp.broadcast_to(db, db_ref.shape)

    df, db = pl.pallas_call(
        body,
        name="forget_bwd",
        out_shape=[jax.ShapeDtypeStruct((nb, nh, blk), F32), jax.ShapeDtypeStruct((nh, LANES), F32)],
        compiler_params=pltpu.CompilerParams(vmem_limit_bytes=VMEM_LIMIT),
    )(_to_blocks(dc, nb, blk), _to_blocks(f_t, nb, blk), b_col)
    return _from_blocks(df), db


def _lanes(parts, rows):
    lane = lax.broadcasted_iota(jnp.int32, (rows, LANES), 1)
    out = jnp.zeros((rows, LANES), F32)
    for n, part in enumerate(parts):
        out = jnp.where(lane == n, part, out)
    return out


def _attn_prep_fwd(proj, cs, t, nh, hd, q_off, k_off, v_off):
    tr = _tile(t, ATTN_TILE, 16)
    qb, kb, vb = q_off // hd, k_off // hd, v_off // hd

    def body(q_ref, k_ref, v_ref, cs_ref, qa_ref, ka_ref, va_ref):
        hi, mid, lo = _split3(cs_ref[0][:, :1])
        qa_ref[0, :, :hd] = q_ref[...]
        qa_ref[0, :, hd:] = _lanes([1.0, 1.0, 1.0, hi, mid, lo], tr).astype(BF16)
        ka_ref[0, :, :hd] = k_ref[...]
        ka_ref[0, :, hd:] = _lanes([-hi, -mid, -lo] + [1.0] * 6, tr).astype(BF16)
        va_ref[0, :, :hd] = v_ref[...]
        va_ref[0, :, hd:] = _lanes([-1.0, -1.0, -1.0], tr).astype(BF16)

    wide = pl.BlockSpec((1, tr, 2 * hd), lambda h, i: (h, i, 0))
    return pl.pallas_call(
        body,
        name="attn_prep_fwd",
        grid=(nh, t // tr),
        in_specs=[pl.BlockSpec((tr, hd), lambda h, i: (i, qb + h)), pl.BlockSpec((tr, hd), lambda h, i: (i, kb + h)),
                  pl.BlockSpec((tr, hd), lambda h, i: (i, vb + h)), pl.BlockSpec((1, tr, LANES), lambda h, i: (h, i, 0))],
        out_specs=[wide, wide, wide],
        out_shape=[jax.ShapeDtypeStruct((nh, t, 2 * hd), BF16)] * 3,
        compiler_params=_params("parallel", "parallel"),
    )(proj, proj, proj, cs)


def _attn_prep_bwd(qa, lse, o, do, t, nh, hd):
    tr = _tile(t, ATTN_TILE, 16)
    inv_scale = math.sqrt(hd)

    def body(qa_ref, lse_ref, o_ref, do_ref, qb_ref, da_ref):
        l_hi, l_mid, l_lo = _split3(lse_ref[0][:, :1] * (-inv_scale))
        lane = lax.broadcasted_iota(jnp.int32, (tr, LANES), 1)
        extra = qa_ref[0, :, hd:].astype(F32)
        extra = jnp.where(lane == 6, l_hi, jnp.where(lane == 7, l_mid, jnp.where(lane == 8, l_lo, extra)))
        qb_ref[0, :, :hd] = qa_ref[0, :, :hd]
        qb_ref[0, :, hd:] = extra.astype(BF16)
        dov = do_ref[...]
        delta = jnp.sum(dov.astype(F32) * o_ref[...].astype(F32), axis=1, keepdims=True)
        da_ref[0, :, :hd] = dov
        da_ref[0, :, hd:] = _lanes(list(_split3(delta)), tr).astype(BF16)

    wide = pl.BlockSpec((1, tr, 2 * hd), lambda h, i: (h, i, 0))
    head = pl.BlockSpec((tr, hd), lambda h, i: (i, h))
    return pl.pallas_call(
        body,
        name="attn_prep_bwd",
        grid=(nh, t // tr),
        in_specs=[wide, pl.BlockSpec((1, tr, LANES), lambda h, i: (h, i, 0)), head, head],
        out_specs=[wide, wide],
        out_shape=[jax.ShapeDtypeStruct((nh, t, 2 * hd), BF16)] * 2,
        compiler_params=_params("parallel", "parallel"),
    )(qa, lse, o, do)


def _causal(s, row0, rows, cols):
    row = lax.broadcasted_iota(jnp.int32, (rows, cols), 0) + row0
    col = lax.broadcasted_iota(jnp.int32, (rows, cols), 1)
    return jnp.where(col <= row, s, NEG_INF)


def _attn_fwd(qa, ka, proj, t, nh, hd, v_off):
    tq = _tile(t, ATTN_TILE)
    sub = _tile(tq, ATTN_SUB)
    ns = tq // sub
    scale = 1.0 / math.sqrt(hd)
    vb = v_off // hd

    def body(qa_ref, ka_ref, v_ref, o_ref, lse_ref):
        i = pl.program_id(1)

        def tile(j, carry, masked):
            rows = pl.ds(pl.multiple_of(j * tq, tq), tq)
            kj = ka_ref[0, rows, :]
            vj = v_ref[rows, :]
            new = []
            for r in range(ns):
                m, l, acc = carry[r]
                s = lax.dot_general(qa_ref[0, pl.ds(r * sub, sub), :], kj, _DIMS["nt"], preferred_element_type=F32)
                if masked:
                    s = _causal(s, r * sub, sub, tq)
                m_new = jnp.maximum(m, jnp.max(s, axis=1, keepdims=True))
                p = jnp.exp2((s - m_new) * (scale * LOG2E))
                alpha = jnp.exp2((m - m_new) * (scale * LOG2E))
                l = alpha * l + jnp.sum(p, axis=1, keepdims=True)
                acc = alpha * acc + jnp.dot(p.astype(BF16), vj, preferred_element_type=F32)
                new.append((m_new, l, acc))
            return tuple(new)

        init = tuple((jnp.full((sub, 1), NEG_INF, F32), jnp.zeros((sub, 1), F32), jnp.zeros((sub, hd), F32))
                     for _ in range(ns))
        carry = lax.fori_loop(0, i, lambda j, cr: tile(j, cr, False), init)
        carry = tile(i, carry, True)
        for r in range(ns):
            m, l, acc = carry[r]
            o_ref[pl.ds(r * sub, sub), :] = (acc / l).astype(BF16)
            lse_ref[0, pl.ds(r * sub, sub), :] = jnp.broadcast_to(m * scale + jnp.log(l), (sub, LANES))

    return pl.pallas_call(
        body,
        name="attn_fwd",
        grid=(nh, t // tq),
        in_specs=[pl.BlockSpec((1, tq, 2 * hd), lambda h, i: (h, i, 0)),
                  pl.BlockSpec((1, t, 2 * hd), lambda h, i: (h, 0, 0)),
                  pl.BlockSpec((t, hd), lambda h, i: (0, vb + h))],
        out_specs=[pl.BlockSpec((tq, hd), lambda h, i: (i, h)),
                   pl.BlockSpec((1, tq, LANES), lambda h, i: (h, i, 0))],
        out_shape=[jax.ShapeDtypeStruct((t, nh * hd), BF16), jax.ShapeDtypeStruct((nh, t, LANES), F32)],
        compiler_params=_params("parallel", "parallel"),
    )(qa, ka, proj)


def _attn_bwd(qb, ka, va, da, t, nh, hd):
    tq = _tile(t, ATTN_TILE)
    nq = t // tq
    sub = _tile(tq, ATTN_SUB)
    ns = tq // sub
    scale = 1.0 / math.sqrt(hd)

    def body(qb_ref, ka_ref, va_ref, da_ref, dq_ref, dk_ref, dv_ref, dck_ref, dcq_ref, dq_all):
        j = pl.program_id(1)

        @pl.when(j == 0)
        def _():
            dq_all[...] = jnp.zeros(dq_all.shape, F32)
            dcq_ref[...] = jnp.zeros(dcq_ref.shape, F32)

        kaj = ka_ref[0]
        vaj = va_ref[0]
        kj = kaj[:, :hd]

        def tile(i, carry, masked):
            dk, dv, dck = carry
            for r in range(ns):
                rows = pl.ds(pl.multiple_of(i * tq + r * sub, sub), sub)
                qr = qb_ref[0, rows, :]
                dr = da_ref[0, rows, :]
                s = lax.dot_general(qr, kaj, _DIMS["nt"], preferred_element_type=F32)
                if masked:
                    s = _causal(s, r * sub, sub, tq)
                p = jnp.exp2(s * (scale * LOG2E))
                ds = p * lax.dot_general(dr, vaj, _DIMS["nt"], preferred_element_type=F32)
                dsb = ds.astype(BF16)
                dv = dv + lax.dot_general(p.astype(BF16), dr[:, :hd], _DIMS["tn"], preferred_element_type=F32)
                dk = dk + lax.dot_general(dsb, qr[:, :hd], _DIMS["tn"], preferred_element_type=F32)
                dck = dck - jnp.sum(ds, axis=0, keepdims=True)
                dq_all[rows, :] += jnp.dot(dsb, kj, preferred_element_type=F32)
                dcq_ref[0, rows, :] += jnp.sum(ds, axis=1, keepdims=True)
            return dk, dv, dck

        carry = (jnp.zeros((tq, hd), F32), jnp.zeros((tq, hd), F32), jnp.zeros((1, tq), F32))
        carry = tile(j, carry, True)
        dk, dv, dck = lax.fori_loop(j + 1, nq, lambda i, cr: tile(i, cr, False), carry)
        dk_ref[...] = (dk * scale).astype(BF16)
        dv_ref[...] = dv.astype(BF16)
        dck_ref[0] = dck

        @pl.when(j == nq - 1)
        def _():
            dq_ref[...] = (dq_all[...] * scale).astype(BF16)

    whole = pl.BlockSpec((1, t, 2 * hd), lambda h, j: (h, 0, 0))
    block = pl.BlockSpec((1, tq, 2 * hd), lambda h, j: (h, j, 0))
    return pl.pallas_call(
        body,
        name="attn_bwd",
        grid=(nh, nq),
        in_specs=[whole, block, block, whole],
        out_specs=[
            pl.BlockSpec((t, hd), lambda h, j: (0, h)),
            pl.BlockSpec((tq, hd), lambda h, j: (j, h)),
            pl.BlockSpec((tq, hd), lambda h, j: (j, h)),
            pl.BlockSpec((1, 1, tq), lambda h, j: (h, 0, j)),
            pl.BlockSpec((1, t, LANES), lambda h, j: (h, 0, 0)),
        ],
        out_shape=[jax.ShapeDtypeStruct((t, nh * hd), BF16), jax.ShapeDtypeStruct((t, nh * hd), BF16),
                   jax.ShapeDtypeStruct((t, nh * hd), BF16), jax.ShapeDtypeStruct((nh, 1, t), F32),
                   jax.ShapeDtypeStruct((nh, t, LANES), F32)],
        scratch_shapes=[pltpu.VMEM((t, hd), F32)],
        compiler_params=_params("arbitrary", "arbitrary"),
    )(qb, ka, va, da)


class _Dims:
    def __init__(self, t, d, c, nh, aw, f, pd):
        self.t, self.d, self.c, self.nh, self.aw, self.f, self.pd = t, d, c, nh, aw, f, pd
        self.hd = aw // nh
        self.a_off, self.g_off = 0, c
        self.q_off, self.k_off, self.v_off = 2 * c, 2 * c + aw, 2 * c + 2 * aw
        self.gc_off = 2 * c + 3 * aw
        self.ga_off = self.gc_off + d
        self.n_main = self.ga_off + d


def _ffn_tn(f):
    return _tile(f, 1536)


def _layer_fwd(dm, x, p, w):
    t, d, c, f = dm.t, dm.d, dm.c, dm.f
    s = {"x": x}
    h = _rms_fwd("rms_mix", x, w["n_mix"], t, d)
    proj = _mm("mm_in", "nn", h, w["main"], t, dm.n_main, d, BF16)
    fl = _mm("mm_forget", "nn", h, w["f"], t, LANES, d, F32)
    u2, cpre = _conv_fwd(proj, w["conv_w"], w["conv_b"], w["ln_g"], w["ln_b"], t, c, dm.a_off, dm.g_off)
    f_t = fl[:, :dm.nh].T
    cum = _forget_fwd(f_t, w["b_f"], dm.nh, t)
    cs = jnp.broadcast_to((cum * math.sqrt(dm.hd))[:, :, None], (dm.nh, t, LANES))
    qa, ka, va = _attn_prep_fwd(proj, cs, t, dm.nh, dm.hd, dm.q_off, dm.k_off, dm.v_off)
    o, lse = _attn_fwd(qa, ka, proj, t, dm.nh, dm.hd, dm.v_off)

    def epi_conv(accs, ex):
        return accs[0], _sig(ex[0].astype(F32)) * accs[0]

    yc, m1 = _matmul("mm_conv_out", "nn", [u2], [w["co"]], [(0, 0, 0)], 1, t, d, c, [BF16, BF16],
                     epi=epi_conv, extras=[("mn", (proj, dm.gc_off))], tm=512)

    def epi_attn(accs, ex):
        return accs[0], ex[1].astype(F32) + _sig(ex[0].astype(F32)) * accs[0]

    ya, merged = _matmul("mm_attn_out", "nn", [o], [w["ao"]], [(0, 0, 0)], 1, t, d, dm.aw, [BF16, BF16],
                         epi=epi_attn, extras=[("mn", (proj, dm.ga_off)), ("mn", m1)], tm=512)
    x1 = _matmul("mm_out", "nn", [merged], [w["o"]], [(0, 0, 0)], 1, t, d, d, [F32],
                 epi=lambda accs, ex: [ex[0] + accs[0]], extras=[("mn", x)], tm=512)[0]

    hf = _rms_fwd("rms_ffn", x1, w["n_ffn"], t, d)

    def epi_glu(accs, ex):
        gate, up = accs
        return gate, up, gate * _sig(gate) * up

    gate, up, act = _matmul("mm_gate_up", "nn", [hf], [w["g"], w["u"]], [(0, 0, 0), (0, 1, 1)], 2, t, f, d,
                            [BF16, BF16, BF16], epi=epi_glu, tm=512, tn=_ffn_tn(f), tk=_tile(d, 1024))
    x2 = _matmul("mm_down", "nn", [act], [w["d"]], [(0, 0, 0)], 1, t, d, f, [F32],
                 epi=lambda accs, ex: [ex[0] + accs[0]], extras=[("mn", x1)], tm=512)[0]

    hp = _rms_fwd("rms_ple", x2, w["n_ple"], t, d)
    pp = _mm("mm_ple_proj", "nn", p, w["pp"], t, d, dm.pd, BF16, tm=512)

    def epi_ple(accs, ex):
        sg = _sig(accs[0])
        return sg, ex[1] + sg * ex[0].astype(F32)

    sg, x3 = _matmul("mm_ple_gate", "nn", [hp], [w["pg"]], [(0, 0, 0)], 1, t, d, d, [BF16, F32],
                     epi=epi_ple, extras=[("mn", pp), ("mn", x2)], tm=512)
    s.update(h=h, proj=proj, f_t=f_t, qa=qa, ka=ka, va=va, u2=u2, cpre=cpre, o=o, lse=lse, yc=yc, ya=ya,
             merged=merged, x1=x1, hf=hf, gate=gate, up=up, act=act, x2=x2, hp=hp, pp=pp, sg=sg, p=p)
    return x3, s


def _layer_bwd(dm, dx3, dx3b, s, w):
    t, d, c, f = dm.t, dm.d, dm.c, dm.f
    g = {}

    def ple_ew(dxv, sgv, ppv):
        sgf, ppf = sgv.astype(F32), ppv.astype(F32)
        return dxv * sgf, dxv * ppf * sgf * (1.0 - sgf)

    d_pp, d_z = _ew("ple_bwd", ple_ew, [dx3, s["sg"], s["pp"]], t, d, [BF16, BF16], _tile(t, ROW_TILE, 8))
    g["pp"] = _mm("gw_ple_proj", "tn", s["p"], d_pp, dm.pd, d, t, F32)
    g["pg"] = _mm("gw_ple_gate", "tn", s["hp"], d_z, d, d, t, F32)
    d_hp = _mm("dx_ple_gate", "nt", d_z, w["pg"], t, d, d, BF16)
    dx2, dx2b, g["n_ple"] = _rms_bwd("rms_ple_bwd", d_hp, s["x2"], w["n_ple"], dx3, t, d)

    def epi_dglu(accs, ex):
        gate, up = ex[0].astype(F32), ex[1].astype(F32)
        sg = _sig(gate)
        return accs[0] * up * (sg * (1.0 + gate * (1.0 - sg))), accs[0] * gate * sg

    d_gate, d_up = _matmul("dx_down", "nt", [dx2b], [w["d"]], [(0, 0, 0)], 1, t, f, d, [BF16, BF16],
                           epi=epi_dglu, extras=[("mn", s["gate"]), ("mn", s["up"])], tm=512, tn=_ffn_tn(f),
                           tk=_tile(d, 1024))
    g["d"] = _mm("gw_down", "tn", s["act"], dx2b, f, d, t, F32, tm=_ffn_tn(f))
    g["g"] = _mm("gw_gate", "tn", s["hf"], d_gate, d, f, t, F32, tn=_ffn_tn(f))
    g["u"] = _mm("gw_up", "tn", s["hf"], d_up, d, f, t, F32, tn=_ffn_tn(f))
    d_hf = _matmul("dx_gate_up", "nt", [d_gate, d_up], [w["g"], w["u"]], [(0, 0, 0), (1, 1, 0)], 1, t, d, f,
                   [BF16])[0]
    dx1, dx1b, g["n_ffn"] = _rms_bwd("rms_ffn_bwd", d_hf, s["x1"], w["n_ffn"], dx2, t, d)

    g["o"] = _mm("gw_out", "tn", s["merged"], dx1b, d, d, t, F32)

    def epi_dmerge(accs, ex):
        dmv = accs[0]
        sgc, sga = _sig(ex[0].astype(F32)), _sig(ex[1].astype(F32))
        ycv, yav = ex[2].astype(F32), ex[3].astype(F32)
        return dmv * sgc, dmv * sga, dmv * ycv * sgc * (1.0 - sgc), dmv * yav * sga * (1.0 - sga)

    d_yc, d_ya, d_gc, d_ga = _matmul(
        "dx_out", "nt", [dx1b], [w["o"]], [(0, 0, 0)], 1, t, d, d, [BF16] * 4, epi=epi_dmerge,
        extras=[("mn", (s["proj"], dm.gc_off)), ("mn", (s["proj"], dm.ga_off)), ("mn", s["yc"]), ("mn", s["ya"])],
        tm=512, tn=_tile(d, 512))
    g["co"] = _mm("gw_conv_out", "tn", s["u2"], d_yc, c, d, t, F32)
    d_u2 = _mm("dx_conv_out", "nt", d_yc, w["co"], t, c, d, BF16)
    g["ao"] = _mm("gw_attn_out", "tn", s["o"], d_ya, dm.aw, d, t, F32)
    d_o = _mm("dx_attn_out", "nt", d_ya, w["ao"], t, dm.aw, d, BF16)

    qb, da = _attn_prep_bwd(s["qa"], s["lse"], s["o"], d_o, t, dm.nh, dm.hd)
    dq, dk, dv, dck, dcq = _attn_bwd(qb, s["ka"], s["va"], da, t, dm.nh, dm.hd)
    d_ft, g_bf = _forget_bwd(dck.reshape(dm.nh, t) + dcq[:, :, 0], s["f_t"], w["b_f"], dm.nh, t)
    g["b_f"] = g_bf[:, 0]
    d_f = jnp.pad(d_ft.T, ((0, 0), (0, LANES - dm.nh))).astype(BF16)

    d_a, d_gg, g["conv_w"], g["conv_b"], g["ln_g"], g["ln_b"] = _conv_bwd(
        s["proj"], s["cpre"], d_u2, w["conv_w"], w["ln_g"], w["ln_b"], t, c, dm.a_off, dm.g_off)

    d_proj = jnp.concatenate([d_a, d_gg, dq, dk, dv, d_gc, d_ga], axis=1)
    g["main"] = _mm("gw_in", "tn", s["h"], d_proj, d, dm.n_main, t, F32)
    g["f"] = _mm("gw_forget", "tn", s["h"], d_f, d, LANES, t, F32)
    d_h_f = _mm("dx_forget", "nt", d_f, w["f"], t, d, LANES, BF16)
    d_h = _matmul("dx_in", "nt", [d_proj], [w["main"]], [(0, 0, 0)], 1, t, d, dm.n_main, [BF16],
                  epi=lambda accs, ex: [accs[0] + ex[0].astype(F32)], extras=[("mn", d_h_f)])[0]
    dx, dxb, g["n_mix"] = _rms_bwd("rms_mix_bwd", d_h, s["x"], w["n_mix"], dx1, t, d)
    return dx, dxb, g


def _adamw_tiles(wv, gv, mv, vv):
    m_new = ADAM_B1 * mv + (1.0 - ADAM_B1) * gv
    v_new = ADAM_B2 * vv + (1.0 - ADAM_B2) * (gv * gv)
    m_hat = m_new / (1.0 - ADAM_B1 ** ADAM_STEP)
    v_hat = v_new / (1.0 - ADAM_B2 ** ADAM_STEP)
    delta = -ADAM_LR * (m_hat / (jnp.sqrt(v_hat) + ADAM_EPS) + ADAM_WD * wv)
    return delta, m_new, v_new


def _adamw(name, wv, gv, mv, vv):
    shape = wv.shape
    cols = shape[-1]
    rows = wv.size // cols
    tm = _tile(rows, max(8, (1 << 18) // cols), 8)
    flat = [a.reshape(rows, cols) for a in (wv, gv, mv, vv)]
    outs = _ew(name, _adamw_tiles, flat, rows, cols, [F32, F32, F32], tm)
    return [o.reshape(shape) for o in outs]


def _place():
    return lax.axis_index("x"), lax.axis_index("y"), lax.axis_index("c")


def _other_chips(x, y):
    return [(1 - x, y), (x, 1 - y), (1 - x, 1 - y)]


def _window(ref, kind, chip, size):
    if kind == "chip":
        return ref.at[chip]
    if kind == "row":
        return ref.at[pl.ds(chip * size, size), :]
    return ref.at[:, pl.ds(pl.multiple_of(chip * size, LANES), size)]


def _full_shape(shard, kind):
    depth, a, b = shard.shape
    if kind == "chip":
        return (depth, N_CHIPS, a, b)
    return (depth, N_CHIPS * a, b) if kind == "row" else (depth, a, N_CHIPS * b)


def _gather_weights(shards, kinds):
    n = len(shards)
    sizes = [s.shape[1] if k == "row" else s.shape[2] for s, k in zip(shards, kinds)]

    def body(*refs):
        ins, outs = refs[:n], refs[n:2 * n]
        ici_send, ici_recv, d2d_send, d2d_recv, own_send, own_recv = refs[2 * n:]
        x, y, c = _place()
        me = 2 * x + y
        sibling = (x, y, 1 - c)
        chips = _other_chips(x, y)

        def slab(w, chip, layer):
            return _window(outs[w].at[layer], kinds[w], chip, sizes[w])

        own = []
        for w in range(n):
            for layer in range(2):
                cp = pltpu.make_async_remote_copy(
                    src_ref=ins[w].at[layer], dst_ref=slab(w, me, layer), send_sem=own_send.at[2 * w + layer],
                    recv_sem=own_recv.at[2 * w + layer], device_id=sibling, device_id_type=MESH)
                cp.start()
                own.append(cp)
        sends = []
        for w in range(n):
            for r, (px, py) in enumerate(chips):
                cp = pltpu.make_async_remote_copy(
                    src_ref=ins[w].at[c], dst_ref=slab(w, me, c), send_sem=ici_send.at[3 * w + r],
                    recv_sem=ici_recv.at[3 * w + r], device_id=(px, py, c), device_id_type=MESH)
                cp.start()
                sends.append(cp)
        for w in range(n):
            for r, (px, py) in enumerate(chips):
                landed = slab(w, 2 * px + py, c)
                pltpu.make_async_remote_copy(
                    src_ref=landed, dst_ref=landed, send_sem=ici_send.at[3 * w + r], recv_sem=ici_recv.at[3 * w + r],
                    device_id=(px, py, c), device_id_type=MESH).wait_recv()
                fwd = pltpu.make_async_remote_copy(
                    src_ref=landed, dst_ref=landed, send_sem=d2d_send.at[3 * w + r], recv_sem=d2d_recv.at[3 * w + r],
                    device_id=sibling, device_id_type=MESH)
                fwd.start()
                sends.append(fwd)
        for w in range(n):
            for r, (px, py) in enumerate(chips):
                got = slab(w, 2 * px + py, 1 - c)
                pltpu.make_async_remote_copy(
                    src_ref=got, dst_ref=got, send_sem=d2d_send.at[3 * w + r], recv_sem=d2d_recv.at[3 * w + r],
                    device_id=sibling, device_id_type=MESH).wait_recv()
        for cp in sends:
            cp.wait_send()
        for cp in own:
            cp.wait()

    return pl.pallas_call(
        body,
        name="gather_weights",
        in_specs=[ANY] * n,
        out_specs=[ANY] * n,
        out_shape=[jax.ShapeDtypeStruct(_full_shape(s, k), s.dtype) for s, k in zip(shards, kinds)],
        scratch_shapes=[pltpu.SemaphoreType.DMA((3 * n,))] * 4 + [pltpu.SemaphoreType.DMA((2 * n,))] * 2,
    )(*shards)


def _swap_layer_grads(g0, g1):
    n = len(g0)

    def body(*refs):
        r0, r1, got = refs[:n], refs[n:2 * n], refs[2 * n:3 * n]
        send_sems, recv_sems = refs[3 * n:]
        x, y, c = _place()

        def copy(w, src):
            return pltpu.make_async_remote_copy(src_ref=src, dst_ref=got[w], send_sem=send_sems.at[w],
                                                recv_sem=recv_sems.at[w], device_id=(x, y, 1 - c), device_id_type=MESH)

        @pl.when(c == 0)
        def _():
            for w in range(n):
                copy(w, r1[w]).start()

        @pl.when(c == 1)
        def _():
            for w in range(n):
                copy(w, r0[w]).start()

        for w in range(n):
            copy(w, r0[w]).wait()

    return pl.pallas_call(
        body,
        name="swap_layer_grads",
        in_specs=[ANY] * (2 * n),
        out_specs=[ANY] * n,
        out_shape=[jax.ShapeDtypeStruct(a.shape, a.dtype) for a in g0],
        scratch_shapes=[pltpu.SemaphoreType.DMA((n,)), pltpu.SemaphoreType.DMA((n,))],
    )(*g0, *g1)


def _add_core_partials(name, g0, g1, got, core):
    rows, cols = got.shape
    tm, tn = _tile(rows, 256, 8), _tile(cols, 2048)

    def body(core_ref, g0_ref, g1_ref, got_ref, out_ref):
        mine = jnp.where(core_ref[0] == 0, g0_ref[...], g1_ref[...])
        out_ref[...] = (mine + got_ref[...]).astype(BF16)

    blk = pl.BlockSpec((tm, tn), lambda i, j, core_ref: (i, j))
    return pl.pallas_call(
        body,
        name=name,
        grid_spec=pltpu.PrefetchScalarGridSpec(num_scalar_prefetch=1, grid=(rows // tm, cols // tn),
                                               in_specs=[blk, blk, blk], out_specs=blk),
        out_shape=jax.ShapeDtypeStruct((rows, cols), BF16),
        compiler_params=_params("parallel", "parallel"),
    )(core.reshape(1), g0, g1, got)


def _exchange_chip_grads(parts, kinds, sizes):
    n = len(parts)

    def body(*refs):
        ins, outs = refs[:n], refs[n:2 * n]
        send_sems, recv_sems = refs[2 * n:]
        x, y, c = _place()
        sends = []
        for w in range(n):
            for r, (px, py) in enumerate(_other_chips(x, y)):
                cp = pltpu.make_async_remote_copy(
                    src_ref=_window(ins[w], kinds[w], 2 * px + py, sizes[w]), dst_ref=outs[w].at[r],
                    send_sem=send_sems.at[3 * w + r], recv_sem=recv_sems.at[3 * w + r],
                    device_id=(px, py, c), device_id_type=MESH)
                cp.start()
                sends.append(cp)
        for w in range(n):
            for r, (px, py) in enumerate(_other_chips(x, y)):
                got = outs[w].at[r]
                pltpu.make_async_remote_copy(
                    src_ref=got, dst_ref=got, send_sem=send_sems.at[3 * w + r], recv_sem=recv_sems.at[3 * w + r],
                    device_id=(px, py, c), device_id_type=MESH).wait_recv()
        for cp in sends:
            cp.wait_send()

    return pl.pallas_call(
        body,
        name="exchange_chip_grads",
        in_specs=[ANY] * n,
        out_specs=[ANY] * n,
        out_shape=[jax.ShapeDtypeStruct((3,) + tuple(_shard_shape(p, k, s)), p.dtype)
                   for p, k, s in zip(parts, kinds, sizes)],
        scratch_shapes=[pltpu.SemaphoreType.DMA((3 * n,)), pltpu.SemaphoreType.DMA((3 * n,))],
    )(*parts)


def _shard_shape(whole, kind, size):
    if kind == "chip":
        return whole.shape[1:]
    return (size, whole.shape[1]) if kind == "row" else (whole.shape[0], size)


def _sum_chip_partials(name, part, got, kind, size, chip, core):
    rows, cols = _shard_shape(part, kind, size)
    tm = _tile(rows, max(8, (1 << 19) // cols), 16)

    def body(chip_ref, core_ref, part_ref, g0_ref, g1_ref, g2_ref, out_ref):
        total = part_ref[...].astype(F32)
        for ref in (g0_ref, g1_ref, g2_ref):
            total = total + ref[...].astype(F32)
        out_ref[...] = total

    if kind == "chip":
        mine = pl.BlockSpec((None, tm, cols), lambda i, chip_ref, core_ref: (chip_ref[0], i, 0))
    elif kind == "row":
        mine = pl.BlockSpec((tm, cols), lambda i, chip_ref, core_ref: (chip_ref[0] * (rows // tm) + i, 0))
    else:
        mine = pl.BlockSpec((tm, cols), lambda i, chip_ref, core_ref: (i, chip_ref[0]))
    theirs = [pl.BlockSpec((None, tm, cols), functools.partial(lambda r, i, chip_ref, core_ref: (r, i, 0), r))
              for r in range(3)]
    return pl.pallas_call(
        body,
        name=name,
        grid_spec=pltpu.PrefetchScalarGridSpec(
            num_scalar_prefetch=2, grid=(rows // tm,), in_specs=[mine] + theirs,
            out_specs=pl.BlockSpec((None, tm, cols), lambda i, chip_ref, core_ref: (core_ref[0], i, 0))),
        out_shape=jax.ShapeDtypeStruct((2, rows, cols), F32),
        compiler_params=_params("parallel"),
    )(chip.reshape(1), core.reshape(1), part, got, got, got)


def _share_reduced(both):
    n = len(both)

    def body(*refs):
        ins, outs = refs[:n], refs[n:2 * n]
        send_sems, recv_sems = refs[2 * n:]
        x, y, c = _place()
        sends = []
        for w in range(n):
            cp = pltpu.make_async_remote_copy(src_ref=ins[w].at[c], dst_ref=outs[w].at[c], send_sem=send_sems.at[w],
                                              recv_sem=recv_sems.at[w], device_id=(x, y, 1 - c), device_id_type=MESH)
            cp.start()
            sends.append(cp)
        for w in range(n):
            got = outs[w].at[1 - c]
            pltpu.make_async_remote_copy(src_ref=got, dst_ref=got, send_sem=send_sems.at[w], recv_sem=recv_sems.at[w],
                                         device_id=(x, y, 1 - c), device_id_type=MESH).wait_recv()
        for cp in sends:
            cp.wait_send()

    return pl.pallas_call(
        body,
        name="share_reduced",
        in_specs=[ANY] * n,
        out_specs=[ANY] * n,
        out_shape=[jax.ShapeDtypeStruct(a.shape, a.dtype) for a in both],
        input_output_aliases={w: w for w in range(n)},
        scratch_shapes=[pltpu.SemaphoreType.DMA((n,)), pltpu.SemaphoreType.DMA((n,))],
    )(*both)


def _allreduce_small(v):
    rows, width = v.shape

    def body(v_ref, out_ref, slots, send_sems, recv_sems):
        x, y, c = _place()
        me = 4 * x + 2 * y + c
        slots[me] = v_ref[...]
        peers = [(x, y, 1 - c)]
        for px, py in _other_chips(x, y):
            peers += [(px, py, c), (px, py, 1 - c)]
        sends = []
        for r, peer in enumerate(peers):
            cp = pltpu.make_async_remote_copy(
                src_ref=v_ref, dst_ref=slots.at[me], send_sem=send_sems.at[r], recv_sem=recv_sems.at[r],
                device_id=peer, device_id_type=MESH)
            cp.start()
            sends.append(cp)
        for r, (px, py, pc) in enumerate(peers):
            got = slots.at[4 * px + 2 * py + pc]
            pltpu.make_async_remote_copy(
                src_ref=got, dst_ref=got, send_sem=send_sems.at[r], recv_sem=recv_sems.at[r],
                device_id=(px, py, pc), device_id_type=MESH).wait_recv()
        for cp in sends:
            cp.wait_send()
        total = slots[0]
        for n in range(1, 8):
            total = total + slots[n]
        out_ref[...] = total

    vm = pl.BlockSpec(memory_space=pltpu.VMEM)
    return pl.pallas_call(
        body,
        name="allreduce_small",
        in_specs=[vm],
        out_specs=vm,
        out_shape=jax.ShapeDtypeStruct((rows, width), F32),
        scratch_shapes=[pltpu.VMEM((8, rows, width), F32), pltpu.SemaphoreType.DMA((7,)),
                        pltpu.SemaphoreType.DMA((7,))],
    )(v)


def _pad_rows(flat, row_align):
    n = flat.shape[0]
    rows = -(-n // PACK_W)
    rows = -(-rows // row_align) * row_align
    return jnp.pad(flat, (0, rows * PACK_W - n)).reshape(rows, PACK_W)


def _unpack(buf, shapes):
    flat = buf.reshape(-1)
    out, off = [], 0
    for shp in shapes:
        n = math.prod(shp)
        out.append(flat[off:off + n].reshape(shp))
        off += n
    return out


BIG = (("w_in", "chip"), ("w_conv_out", "col"), ("w_attn_out", "col"), ("w_out", "row"), ("w_gate_up", "col"),
       ("w_down", "row"), ("w_ple_gate", "row"), ("w_ple_proj", "col"))


def kernel(x, p, norm_mix_g, w_in, b_forget, conv_w, conv_b, conv_ln_g, conv_ln_b, w_conv_out, w_attn_out, w_out, norm_ffn_g, w_gate_up, w_down, norm_ple_g, w_ple_gate, w_ple_proj, final_g, loss_target, m_norm_mix_g, m_w_in, m_b_forget, m_conv_w, m_conv_b, m_conv_ln_g, m_conv_ln_b, m_w_conv_out, m_w_attn_out, m_w_out, m_norm_ffn_g, m_w_gate_up, m_w_down, m_norm_ple_g, m_w_ple_gate, m_w_ple_proj, m_final_g, v_norm_mix_g, v_w_in, v_b_forget, v_conv_w, v_conv_b, v_conv_ln_g, v_conv_ln_b, v_w_conv_out, v_w_attn_out, v_w_out, v_norm_ffn_g, v_w_gate_up, v_w_down, v_norm_ple_g, v_w_ple_gate, v_w_ple_proj, v_final_g):
    wts = dict(norm_mix_g=norm_mix_g, w_in=w_in, b_forget=b_forget, conv_w=conv_w, conv_b=conv_b,
               conv_ln_g=conv_ln_g, conv_ln_b=conv_ln_b, w_conv_out=w_conv_out, w_attn_out=w_attn_out,
               w_out=w_out, norm_ffn_g=norm_ffn_g, w_gate_up=w_gate_up, w_down=w_down, norm_ple_g=norm_ple_g,
               w_ple_gate=w_ple_gate, w_ple_proj=w_ple_proj, final_g=final_g)
    mom1 = dict(norm_mix_g=m_norm_mix_g, w_in=m_w_in, b_forget=m_b_forget, conv_w=m_conv_w, conv_b=m_conv_b,
                conv_ln_g=m_conv_ln_g, conv_ln_b=m_conv_ln_b, w_conv_out=m_w_conv_out, w_attn_out=m_w_attn_out,
                w_out=m_w_out, norm_ffn_g=m_norm_ffn_g, w_gate_up=m_w_gate_up, w_down=m_w_down,
                norm_ple_g=m_norm_ple_g, w_ple_gate=m_w_ple_gate, w_ple_proj=m_w_ple_proj, final_g=m_final_g)
    mom2 = dict(norm_mix_g=v_norm_mix_g, w_in=v_w_in, b_forget=v_b_forget, conv_w=v_conv_w, conv_b=v_conv_b,
                conv_ln_g=v_conv_ln_g, conv_ln_b=v_conv_ln_b, w_conv_out=v_w_conv_out, w_attn_out=v_w_attn_out,
                w_out=v_w_out, norm_ffn_g=v_norm_ffn_g, w_gate_up=v_w_gate_up, w_down=v_w_down,
                norm_ple_g=v_norm_ple_g, w_ple_gate=v_w_ple_gate, w_ple_proj=v_w_ple_proj, final_g=v_final_g)
    order = list(wts)
    depth = w_in.shape[0]
    assert depth == 2, "the exchanges give one layer to each of a chip's two cores"
    t, d = x.shape[1], x.shape[2]
    c = conv_ln_g.shape[1]
    nh = b_forget.shape[1]
    aw = w_attn_out.shape[1]
    f = N_CHIPS * w_down.shape[1]
    pd = w_ple_proj.shape[1]
    dm = _Dims(t, d, c, nh, aw, f, pd)
    n_split = 2 * c + 3 * aw
    cw = conv_w.shape[2]
    chip = 2 * lax.axis_index("x") + lax.axis_index("y")
    core = lax.axis_index("c")
    big_names = [name for name, _ in BIG]
    big_kinds = [kind for _, kind in BIG]

    gathered = _gather_weights([wts[name].astype(BF16) for name in big_names] + [conv_w], big_kinds + ["chip"])
    full = dict(zip(big_names + ["conv_w"], gathered))

    def layer_weights(l):
        wi = jnp.concatenate([full["w_in"][l, k] for k in range(N_CHIPS)], axis=1)
        cwl = jnp.concatenate([full["conv_w"][l, k] for k in range(N_CHIPS)], axis=1)
        return {
            "main": jnp.concatenate([wi[:, :n_split], wi[:, n_split + nh:]], axis=1),
            "f": jnp.pad(wi[:, n_split:n_split + nh], ((0, 0), (0, LANES - nh))),
            "conv_w": jnp.pad(cwl, ((0, CONV_HALO - CONV_K), (0, 0))),
            "conv_b": conv_b[l][None], "ln_g": conv_ln_g[l][None], "ln_b": conv_ln_b[l][None],
            "b_f": jnp.broadcast_to(b_forget[l][:, None], (nh, LANES)),
            "co": (full["w_conv_out"], 0, (l,)), "ao": (full["w_attn_out"], 0, (l,)), "o": (full["w_out"], 0, (l,)),
            "g": (full["w_gate_up"], 0, (l,)), "u": (full["w_gate_up"], f, (l,)), "d": (full["w_down"], 0, (l,)),
            "pg": (full["w_ple_gate"], 0, (l,)), "pp": (full["w_ple_proj"], 0, (l,)),
            "n_mix": norm_mix_g[l][None], "n_ffn": norm_ffn_g[l][None], "n_ple": norm_ple_g[l][None],
        }

    lw = [layer_weights(l) for l in range(depth)]

    xl = x[0]
    saved = []
    for l in range(depth):
        xl, s = _layer_fwd(dm, xl, p[l, 0], lw[l])
        saved.append(s)
    loss_row, dx, dxb, g_final = _loss_head(xl, final_g[None], loss_target[0], t, d)
    loss = lax.psum(loss_row[0, 0], ("x", "y", "c"))
    lg = [None] * depth
    for l in reversed(range(depth)):
        dx, dxb, lg[l] = _layer_bwd(dm, dx, dxb, saved[l], lw[l])

    def layer_grads(l):
        g = lg[l]
        w_in_g = jnp.concatenate([g["main"][:, :n_split], g["f"][:, :nh], g["main"][:, n_split:]], axis=1)
        return [w_in_g.reshape(d, N_CHIPS, -1).transpose(1, 0, 2), g["co"], g["ao"], g["o"],
                jnp.concatenate([g["g"], g["u"]], axis=1), g["d"], g["pg"], g["pp"]]

    g0, g1 = layer_grads(0), layer_grads(1)
    sizes = [wts[name].shape[1] if kind == "row" else wts[name].shape[2] for name, kind in BIG]
    from_sibling = _swap_layer_grads(g0, g1)

    def flat2(a):
        return a.reshape(-1, a.shape[-1])

    chip_parts = [_add_core_partials("add_core_partials_" + name, flat2(a0), flat2(a1), flat2(got), core)
                  .reshape(got.shape) for name, a0, a1, got in zip(big_names, g0, g1, from_sibling)]
    from_chips = _exchange_chip_grads(chip_parts, big_kinds, sizes)
    reduced = [_sum_chip_partials("sum_chip_partials_" + name, part, got, kind, size, chip, core)
               for name, part, got, kind, size in zip(big_names, chip_parts, from_chips, big_kinds, sizes)]
    grads = dict(zip(big_names, _share_reduced(reduced)))

    def stacked(key):
        return jnp.stack([lg[l][key] for l in range(depth)])

    small = ["norm_mix_g", "b_forget", "conv_b", "conv_ln_g", "conv_ln_b", "norm_ffn_g", "norm_ple_g", "final_g"]
    small_g = {
        "norm_mix_g": jnp.concatenate([lg[l]["n_mix"] for l in range(depth)]),
        "b_forget": stacked("b_f"),
        "conv_b": jnp.concatenate([lg[l]["conv_b"] for l in range(depth)]),
        "conv_ln_g": jnp.concatenate([lg[l]["ln_g"] for l in range(depth)]),
        "conv_ln_b": jnp.concatenate([lg[l]["ln_b"] for l in range(depth)]),
        "norm_ffn_g": jnp.concatenate([lg[l]["n_ffn"] for l in range(depth)]),
        "norm_ple_g": jnp.concatenate([lg[l]["n_ple"] for l in range(depth)]),
        "final_g": g_final[0],
    }
    conv_w_g = jnp.stack([lg[l]["conv_w"][:CONV_K] for l in range(depth)])
    small_shapes = [wts[n].shape for n in small] + [conv_w_g.shape]
    small_pack = _pad_rows(jnp.concatenate([small_g[n].reshape(-1) for n in small] + [conv_w_g.reshape(-1)]), 8)
    small_sum = _unpack(_allreduce_small(small_pack), small_shapes)
    for n, name in enumerate(small):
        grads[name] = small_sum[n]
    grads["conv_w"] = lax.dynamic_slice_in_dim(small_sum[len(small)], chip * cw, cw, axis=2)

    def pack_small(src):
        return _pad_rows(jnp.concatenate([src[n].reshape(-1) for n in small]), 8)

    small_upd = _adamw("adamw_small", pack_small(wts), pack_small(grads), pack_small(mom1), pack_small(mom2))
    small_upd = [_unpack(u, [wts[n].shape for n in small]) for u in small_upd]
    delta, new_m, new_v = {}, {}, {}
    for n, name in enumerate(small):
        delta[name], new_m[name], new_v[name] = small_upd[0][n], small_upd[1][n], small_upd[2][n]
    for name in big_names + ["conv_w"]:
        delta[name], new_m[name], new_v[name] = _adamw("adamw_" + name, wts[name], grads[name], mom1[name],
                                                       mom2[name])

    return (loss, dx[None], *[grads[n] for n in order], *[delta[n] for n in order],
            *[new_m[n] for n in order], *[new_v[n] for n in order])
```

```python
import functools
import math

import jax
import jax.numpy as jnp
from jax import lax
from jax.experimental import pallas as pl
from jax.experimental.pallas import tpu as pltpu

F32 = jnp.float32
BF16 = jnp.bfloat16

EPS = 1e-6
CONV_K = 31
NEG_INF = -1e30
LOG2E = 1.4426950408889634
ADAM_LR = 0.001
ADAM_B1 = 0.9
ADAM_B2 = 0.999
ADAM_EPS = 1e-08
ADAM_WD = 0.01
ADAM_STEP = 10

LANES = 128
VMEM_LIMIT = 60 * 1024 * 1024
PACK_W = 1024
N_CHIPS = 4
CONV_HALO = 32
CUM_BLOCK = 256

MM_TM = 1024
MM_TN = 1024
MM_TK = 2048
ROW_TILE = 256
CONV_TILE = 256
ATTN_TILE = 1024
ATTN_SUB = 1024

MESH = pl.DeviceIdType.MESH
ANY = pl.BlockSpec(memory_space=pl.ANY)


def _params(*sem):
    return pltpu.CompilerParams(dimension_semantics=sem, vmem_limit_bytes=VMEM_LIMIT)


def _tile(dim, pref, align=LANES):
    if dim <= pref:
        return dim
    t = (pref // align) * align
    while t >= align:
        if dim % t == 0:
            return t
        t -= align
    return dim


def _sig(x):
    return 1.0 / (1.0 + jnp.exp(-x))


def _op(a):
    if not isinstance(a, tuple):
        return a, 0, ()
    return a if len(a) == 3 else (a[0], a[1], ())


def _spec(block, index, lead):
    if not lead:
        return pl.BlockSpec(block, index)
    return pl.BlockSpec((None,) * len(lead) + block, lambda *g: tuple(lead) + index(*g))


_DIMS = {
    "nn": (((1,), (0,)), ((), ())),
    "nt": (((1,), (1,)), ((), ())),
    "tn": (((0,), (0,)), ((), ())),
}


def _matmul(name, mode, a_ops, b_ops, terms, n_acc, m, n, k, out_dtypes, epi=None, extras=(),
            tm=None, tn=None, tk=None, side=None):
    tm = tm or _tile(m, MM_TM)
    tn = tn or _tile(n, MM_TN)
    tk = tk or _tile(k, MM_TK)
    ni, nj, nk = m // tm, n // tn, k // tk
    a_ops = [_op(a) for a in a_ops]
    b_ops = [_op(b) for b in b_ops]
    extras = [(kind, _op(e)) for kind, e in extras]
    na, nb, ne, no = len(a_ops), len(b_ops), len(extras), len(out_dtypes)
    n_acc_refs = n_acc if nk > 1 else 0
    s_ins = list(side.ins) if side else []
    s_outs = list(side.out_shapes) if side else []
    s_sems = [pltpu.SemaphoreType.DMA((cnt,)) for cnt in side.sem_counts] if side else []

    def a_spec(off, lead):
        if mode == "tn":
            assert off % tm == 0
            return _spec((tk, tm), lambda i, j, kk: (kk, i + off // tm), lead)
        assert off % tk == 0
        return _spec((tm, tk), lambda i, j, kk: (i, kk + off // tk), lead)

    def b_spec(off, lead):
        if mode == "nt":
            assert off % tk == 0
            return _spec((tn, tk), lambda i, j, kk: (j, kk + off // tk), lead)
        assert off % tn == 0
        return _spec((tk, tn), lambda i, j, kk: (kk, j + off // tn), lead)

    def e_spec(kind, off, lead):
        assert off % tn == 0
        if kind == "n":
            return _spec((1, tn), lambda i, j, kk: (0, j + off // tn), lead)
        return _spec((tm, tn), lambda i, j, kk: (i, j + off // tn), lead)

    def body(*refs):
        refs = list(refs)
        a_refs, b_refs, e_refs, si_refs, o_refs, so_refs, acc_refs, sem_refs = (
            [refs.pop(0) for _ in range(cnt)]
            for cnt in (na, nb, ne, len(s_ins), no, len(s_outs), n_acc_refs, len(s_sems)))
        i, j, kk = pl.program_id(0), pl.program_id(1), pl.program_id(2)
        if side:
            @pl.when((i == 0) & (j == 0) & (kk == 0))
            def _():
                side.start(si_refs, so_refs, sem_refs)

        av = [r[...].astype(BF16) for r in a_refs]
        bv = [r[...].astype(BF16) for r in b_refs]
        sums = [None] * n_acc
        for ai, bi, ci in terms:
            part = lax.dot_general(av[ai], bv[bi], _DIMS[mode], preferred_element_type=F32)
            sums[ci] = part if sums[ci] is None else sums[ci] + part

        def finish(accs):
            outs = epi(accs, [e[...] for e in e_refs]) if epi is not None else accs
            for o, val in zip(o_refs, outs):
                o[...] = val.astype(o.dtype)

        if nk == 1:
            finish(sums)
        else:
            @pl.when(kk == 0)
            def _():
                for acc, part in zip(acc_refs, sums):
                    acc[...] = part

            @pl.when(kk > 0)
            def _():
                for acc, part in zip(acc_refs, sums):
                    acc[...] += part

            @pl.when(kk == nk - 1)
            def _():
                finish([acc[...] for acc in acc_refs])

        if side:
            @pl.when((i == ni - 1) & (j == nj - 1) & (kk == nk - 1))
            def _():
                side.finish(si_refs, so_refs, sem_refs)

    order = ("arbitrary",) * 3 if side else ("parallel", "parallel", "arbitrary")
    outs = pl.pallas_call(
        body,
        name=name,
        grid=(ni, nj, nk),
        in_specs=[a_spec(off, lead) for _, off, lead in a_ops] + [b_spec(off, lead) for _, off, lead in b_ops]
        + [e_spec(kind, off, lead) for kind, (_, off, lead) in extras] + [ANY] * len(s_ins),
        out_specs=[pl.BlockSpec((tm, tn), lambda i, j, kk: (i, j)) for _ in out_dtypes] + [ANY] * len(s_outs),
        out_shape=[jax.ShapeDtypeStruct((m, n), dt) for dt in out_dtypes] + s_outs,
        scratch_shapes=[pltpu.VMEM((tm, tn), F32) for _ in range(n_acc_refs)] + s_sems,
        compiler_params=_params(*order),
    )(*[a[0] for a in a_ops], *[b[0] for b in b_ops], *[e[0] for _, e in extras], *s_ins)
    return outs


def _mm(name, mode, a, b, m, n, k, out_dtype, **kw):
    return _matmul(name, mode, [a], [b], [(0, 0, 0)], 1, m, n, k, [out_dtype], **kw)[0]


def _ew(name, fn, ins, m, n, out_dtypes, tm, tn=None):
    tn = tn or n
    ins = [_op(a) for a in ins]
    ni = len(ins)

    def spec(off, lead):
        assert off % tn == 0
        return _spec((tm, tn), lambda i, j: (i, j + off // tn), lead)

    def body(*refs):
        outs = fn(*[r[...] for r in refs[:ni]])
        for o, val in zip(refs[ni:], outs):
            o[...] = val.astype(o.dtype)

    return pl.pallas_call(
        body,
        name=name,
        grid=(m // tm, n // tn),
        in_specs=[spec(off, lead) for _, off, lead in ins],
        out_specs=[pl.BlockSpec((tm, tn), lambda i, j: (i, j)) for _ in out_dtypes],
        out_shape=[jax.ShapeDtypeStruct((m, n), dt) for dt in out_dtypes],
        compiler_params=_params("parallel", "parallel"),
    )(*[a[0] for a in ins])


def _rms_fwd(name, x, g, t, d):
    tr = _tile(t, ROW_TILE, 8)

    def body(x_ref, g_ref, h_ref):
        xv = x_ref[...]
        r = lax.rsqrt(jnp.mean(xv * xv, axis=1, keepdims=True) + EPS)
        h_ref[...] = (xv * r * g_ref[...]).astype(BF16)

    return pl.pallas_call(
        body,
        name=name,
        grid=(t // tr,),
        in_specs=[pl.BlockSpec((tr, d), lambda i: (i, 0)), pl.BlockSpec((1, d), lambda i: (0, 0))],
        out_specs=pl.BlockSpec((tr, d), lambda i: (i, 0)),
        out_shape=jax.ShapeDtypeStruct((t, d), BF16),
        compiler_params=_params("parallel"),
    )(x, g)


def _rms_bwd_rows(dh, xv, g):
    r = lax.rsqrt(jnp.mean(xv * xv, axis=1, keepdims=True) + EPS)
    xhat = xv * r
    dxh = dh * g
    dx = r * (dxh - xhat * jnp.mean(dxh * xhat, axis=1, keepdims=True))
    return dx, dh * xhat


def _rms_bwd(name, dh, x, g, dres, t, d):
    tr = _tile(t, ROW_TILE, 8)

    def body(dh_ref, x_ref, g_ref, dres_ref, dx_ref, dxb_ref, dg_ref):
        @pl.when(pl.program_id(0) == 0)
        def _():
            dg_ref[...] = jnp.zeros(dg_ref.shape, F32)

        dx, dg_rows = _rms_bwd_rows(dh_ref[...].astype(F32), x_ref[...], g_ref[...])
        dx = dx + dres_ref[...]
        dx_ref[...] = dx
        dxb_ref[...] = dx.astype(BF16)
        dg_ref[...] += jnp.sum(dg_rows, axis=0, keepdims=True)

    row = pl.BlockSpec((tr, d), lambda i: (i, 0))
    vec = pl.BlockSpec((1, d), lambda i: (0, 0))
    return pl.pallas_call(
        body,
        name=name,
        grid=(t // tr,),
        in_specs=[row, row, vec, row],
        out_specs=[row, row, vec],
        out_shape=[jax.ShapeDtypeStruct((t, d), F32), jax.ShapeDtypeStruct((t, d), BF16),
                   jax.ShapeDtypeStruct((1, d), F32)],
        compiler_params=_params("arbitrary"),
    )(dh, x, g, dres)


def _loss_head(x, g, target, t, d):
    tr = _tile(t, ROW_TILE, 8)

    def body(x_ref, g_ref, tgt_ref, loss_ref, dx_ref, dxb_ref, dg_ref):
        @pl.when(pl.program_id(0) == 0)
        def _():
            dg_ref[...] = jnp.zeros(dg_ref.shape, F32)
            loss_ref[...] = jnp.zeros(loss_ref.shape, F32)

        xv = x_ref[...]
        gv = g_ref[...]
        r = lax.rsqrt(jnp.mean(xv * xv, axis=1, keepdims=True) + EPS)
        err = xv * r * gv - tgt_ref[...]
        loss_ref[...] += (0.5 / d) * jnp.sum(err * err)
        dx, dg_rows = _rms_bwd_rows(err * (1.0 / d), xv, gv)
        dx_ref[...] = dx
        dxb_ref[...] = dx.astype(BF16)
        dg_ref[...] += jnp.sum(dg_rows, axis=0, keepdims=True)

    row = pl.BlockSpec((tr, d), lambda i: (i, 0))
    vec = pl.BlockSpec((1, d), lambda i: (0, 0))
    one = pl.BlockSpec((1, LANES), lambda i: (0, 0))
    return pl.pallas_call(
        body,
        name="loss_head",
        grid=(t // tr,),
        in_specs=[row, vec, row],
        out_specs=[one, row, row, vec],
        out_shape=[jax.ShapeDtypeStruct((1, LANES), F32), jax.ShapeDtypeStruct((t, d), F32),
                   jax.ShapeDtypeStruct((t, d), BF16), jax.ShapeDtypeStruct((1, d), F32)],
        compiler_params=_params("arbitrary"),
    )(x, g, target)


def _layernorm_rows(cv, g, b):
    mu = jnp.mean(cv, axis=1, keepdims=True)
    xc = cv - mu
    rstd = lax.rsqrt(jnp.mean(xc * xc, axis=1, keepdims=True) + EPS)
    xhat = xc * rstd
    return xhat, rstd, xhat * g + b


def _conv_fwd(proj, conv_w, conv_b, ln_g, ln_b, t, c, a_off, g_off):
    tm = _tile(t, CONV_TILE, CONV_HALO)
    per = tm // CONV_HALO
    ab, gb = a_off // c, g_off // c

    def body(a_ref, g_ref, ap_ref, gp_ref, w_ref, cb_ref, lg_ref, lb_ref, u2_ref, c_ref, upad):
        i = pl.program_id(0)
        u_prev = ap_ref[...].astype(F32) * _sig(gp_ref[...].astype(F32))
        upad[pl.ds(0, CONV_HALO), :] = jnp.where(i > 0, u_prev, 0.0)
        upad[pl.ds(CONV_HALO, tm), :] = a_ref[...].astype(F32) * _sig(g_ref[...].astype(F32))
        acc = jnp.zeros((tm, c), F32) + cb_ref[...]
        for k in range(CONV_K):
            acc = acc + w_ref[pl.ds(k, 1), :] * upad[pl.ds(CONV_HALO - (CONV_K - 1) + k, tm), :]
        c_ref[...] = acc
        _, _, z = _layernorm_rows(acc, lg_ref[...], lb_ref[...])
        u2_ref[...] = (z * _sig(z)).astype(BF16)

    vec = pl.BlockSpec((1, c), lambda i: (0, 0))
    return pl.pallas_call(
        body,
        name="conv_fwd",
        grid=(t // tm,),
        in_specs=[
            pl.BlockSpec((tm, c), lambda i: (i, ab)),
            pl.BlockSpec((tm, c), lambda i: (i, gb)),
            pl.BlockSpec((CONV_HALO, c), lambda i: (jnp.maximum(i * per - 1, 0), ab)),
            pl.BlockSpec((CONV_HALO, c), lambda i: (jnp.maximum(i * per - 1, 0), gb)),
            pl.BlockSpec((CONV_HALO, c), lambda i: (0, 0)), vec, vec, vec,
        ],
        out_specs=[pl.BlockSpec((tm, c), lambda i: (i, 0)), pl.BlockSpec((tm, c), lambda i: (i, 0))],
        out_shape=[jax.ShapeDtypeStruct((t, c), BF16), jax.ShapeDtypeStruct((t, c), F32)],
        scratch_shapes=[pltpu.VMEM((CONV_HALO + tm, c), F32)],
        compiler_params=_params("parallel"),
    )(proj, proj, proj, proj, conv_w, conv_b, ln_g, ln_b)


def _conv_bwd(proj, cpre, du2, conv_w, ln_g, ln_b, t, c, a_off, g_off):
    tm = _tile(t, CONV_TILE, CONV_HALO)
    per = tm // CONV_HALO
    nt = t // tm
    last_halo = t // CONV_HALO - 1
    ab, gb = a_off // c, g_off // c

    def body(a_ref, g_ref, ap_ref, gp_ref, c_ref, cn_ref, du_ref, dun_ref, w_ref, lg_ref, lb_ref,
             da_ref, dg_ref, gw_ref, gcb_ref, glg_ref, glb_ref, upad, dpad):
        i = pl.program_id(0)

        @pl.when(i == 0)
        def _():
            gw_ref[...] = jnp.zeros(gw_ref.shape, F32)
            gcb_ref[...] = jnp.zeros(gcb_ref.shape, F32)
            glg_ref[...] = jnp.zeros(glg_ref.shape, F32)
            glb_ref[...] = jnp.zeros(glb_ref.shape, F32)

        lg = lg_ref[...]
        lb = lb_ref[...]

        def ln_bwd(cv, duv):
            xhat, rstd, z = _layernorm_rows(cv, lg, lb)
            sz = _sig(z)
            dz = duv * (sz * (1.0 + z * (1.0 - sz)))
            dxh = dz * lg
            dc = rstd * (dxh - jnp.mean(dxh, axis=1, keepdims=True)
                         - xhat * jnp.mean(dxh * xhat, axis=1, keepdims=True))
            return dc, dz, xhat

        dc, dz, xhat = ln_bwd(c_ref[...], du_ref[...].astype(F32))
        dc_next, _, _ = ln_bwd(cn_ref[...], dun_ref[...].astype(F32))
        glg_ref[...] += jnp.sum(dz * xhat, axis=0, keepdims=True)
        glb_ref[...] += jnp.sum(dz, axis=0, keepdims=True)
        gcb_ref[...] += jnp.sum(dc, axis=0, keepdims=True)
        dpad[pl.ds(0, tm), :] = dc
        dpad[pl.ds(tm, CONV_HALO), :] = jnp.where(i < nt - 1, dc_next, 0.0)

        av = a_ref[...].astype(F32)
        sg = _sig(g_ref[...].astype(F32))
        u_prev = ap_ref[...].astype(F32) * _sig(gp_ref[...].astype(F32))
        upad[pl.ds(0, CONV_HALO), :] = jnp.where(i > 0, u_prev, 0.0)
        upad[pl.ds(CONV_HALO, tm), :] = av * sg

        du = jnp.zeros((tm, c), F32)
        for k in range(CONV_K):
            du = du + w_ref[pl.ds(k, 1), :] * dpad[pl.ds(CONV_K - 1 - k, tm), :]
            gw_ref[pl.ds(k, 1), :] += jnp.sum(
                dc * upad[pl.ds(CONV_HALO - (CONV_K - 1) + k, tm), :], axis=0, keepdims=True)
        da_ref[...] = (du * sg).astype(BF16)
        dg_ref[...] = (du * av * sg * (1.0 - sg)).astype(BF16)

    vec = pl.BlockSpec((1, c), lambda i: (0, 0))
    cur = pl.BlockSpec((tm, c), lambda i: (i, 0))
    nxt = pl.BlockSpec((CONV_HALO, c), lambda i: (jnp.minimum((i + 1) * per, last_halo), 0))
    wsp = pl.BlockSpec((CONV_HALO, c), lambda i: (0, 0))
    return pl.pallas_call(
        body,
        name="conv_bwd",
        grid=(nt,),
        in_specs=[
            pl.BlockSpec((tm, c), lambda i: (i, ab)),
            pl.BlockSpec((tm, c), lambda i: (i, gb)),
            pl.BlockSpec((CONV_HALO, c), lambda i: (jnp.maximum(i * per - 1, 0), ab)),
            pl.BlockSpec((CONV_HALO, c), lambda i: (jnp.maximum(i * per - 1, 0), gb)),
            cur, nxt, cur, nxt, wsp, vec, vec,
        ],
        out_specs=[cur, cur, wsp, vec, vec, vec],
        out_shape=[jax.ShapeDtypeStruct((t, c), BF16), jax.ShapeDtypeStruct((t, c), BF16),
                   jax.ShapeDtypeStruct((CONV_HALO, c), F32), jax.ShapeDtypeStruct((1, c), F32),
                   jax.ShapeDtypeStruct((1, c), F32), jax.ShapeDtypeStruct((1, c), F32)],
        scratch_shapes=[pltpu.VMEM((CONV_HALO + tm, c), F32), pltpu.VMEM((tm + CONV_HALO, c), F32)],
        compiler_params=_params("arbitrary"),
    )(proj, proj, proj, proj, cpre, cpre, du2, du2, conv_w, ln_g, ln_b)


def _split3(x):
    hi = x.astype(BF16).astype(F32)
    r1 = x - hi
    mid = r1.astype(BF16).astype(F32)
    lo = (r1 - mid).astype(BF16).astype(F32)
    return hi, mid, lo


def _split3_dot(x, tri):
    dot = functools.partial(jnp.dot, preferred_element_type=F32)
    hi, mid, lo = _split3(x)
    return dot(hi.astype(BF16), tri) + dot(mid.astype(BF16), tri) + dot(lo.astype(BF16), tri)


def _to_blocks(a, nb, blk):
    return a.reshape(a.shape[0], nb, blk).transpose(1, 0, 2)


def _from_blocks(a):
    return a.transpose(1, 0, 2).reshape(a.shape[1], -1)


def _forget_fwd(f_t, b_col, nh, t):
    blk = _tile(t, CUM_BLOCK)
    nb = t // blk

    def body(f_ref, b_ref, c_ref):
        ri = lax.broadcasted_iota(jnp.int32, (blk, blk), 0)
        ci = lax.broadcasted_iota(jnp.int32, (blk, blk), 1)
        tri = (ri <= ci).astype(BF16)

        def step(bi, carry):
            xv = f_ref[bi] + b_ref[:, :1]
            lf = jnp.minimum(xv, 0.0) - jnp.log(1.0 + jnp.exp(-jnp.abs(xv)))
            cs = _split3_dot(lf, tri) + carry
            c_ref[bi] = cs
            return cs[:, blk - 1:blk]

        lax.fori_loop(0, nb, step, jnp.zeros((nh, 1), F32))

    out = pl.pallas_call(
        body,
        name="forget_fwd",
        out_shape=jax.ShapeDtypeStruct((nb, nh, blk), F32),
        compiler_params=pltpu.CompilerParams(vmem_limit_bytes=VMEM_LIMIT),
    )(_to_blocks(f_t, nb, blk), b_col)
    return _from_blocks(out)


def _forget_bwd(dc, f_t, b_col, nh, t):
    blk = _tile(t, CUM_BLOCK)
    nb = t // blk

    def body(dc_ref, f_ref, b_ref, df_ref, db_ref):
        ri = lax.broadcasted_iota(jnp.int32, (blk, blk), 0)
        ci = lax.broadcasted_iota(jnp.int32, (blk, blk), 1)
        tri = (ri >= ci).astype(BF16)

        def step(n, carry):
            tail, db = carry
            bi = nb - 1 - n
            rc = _split3_dot(dc_ref[bi], tri) + tail
            df = rc * _sig(-(f_ref[bi] + b_ref[:, :1]))
            df_ref[bi] = df
            return rc[:, 0:1], db + jnp.sum(df, axis=1, keepdims=True)

        _, db = lax.fori_loop(0, nb, step, (jnp.zeros((nh, 1), F32), jnp.zeros((nh, 1), F32)))
        db_ref[...] = jnp.broadcast_to(db, db_ref.shape)

    df, db = pl.pallas_call(
        body,
        name="forget_bwd",
        out_shape=[jax.ShapeDtypeStruct((nb, nh, blk), F32), jax.ShapeDtypeStruct((nh, LANES), F32)],
        compiler_params=pltpu.CompilerParams(vmem_limit_bytes=VMEM_LIMIT),
    )(_to_blocks(dc, nb, blk), _to_blocks(f_t, nb, blk), b_col)
    return _from_blocks(df), db


def _lanes(parts, rows):
    lane = lax.broadcasted_iota(jnp.int32, (rows, LANES), 1)
    out = jnp.zeros((rows, LANES), F32)
    for n, part in enumerate(parts):
        out = jnp.where(lane == n, part, out)
    return out


def _attn_prep_fwd(proj, cs, t, nh, hd, q_off, k_off, v_off):
    tr = _tile(t, ATTN_TILE, 16)
    qb, kb, vb = q_off // hd, k_off // hd, v_off // hd

    def body(q_ref, k_ref, v_ref, cs_ref, qa_ref, ka_ref, va_ref):
        hi, mid, lo = _split3(cs_ref[0][:, :1])
        qa_ref[0, :, :hd] = q_ref[...]
        qa_ref[0, :, hd:] = _lanes([1.0, 1.0, 1.0, hi, mid, lo], tr).astype(BF16)
        ka_ref[0, :, :hd] = k_ref[...]
        ka_ref[0, :, hd:] = _lanes([-hi, -mid, -lo] + [1.0] * 6, tr).astype(BF16)
        va_ref[0, :, :hd] = v_ref[...]
        va_ref[0, :, hd:] = _lanes([-1.0, -1.0, -1.0], tr).astype(BF16)

    wide = pl.BlockSpec((1, tr, 2 * hd), lambda h, i: (h, i, 0))
    return pl.pallas_call(
        body,
        name="attn_prep_fwd",
        grid=(nh, t // tr),
        in_specs=[pl.BlockSpec((tr, hd), lambda h, i: (i, qb + h)), pl.BlockSpec((tr, hd), lambda h, i: (i, kb + h)),
                  pl.BlockSpec((tr, hd), lambda h, i: (i, vb + h)), pl.BlockSpec((1, tr, LANES), lambda h, i: (h, i, 0))],
        out_specs=[wide, wide, wide],
        out_shape=[jax.ShapeDtypeStruct((nh, t, 2 * hd), BF16)] * 3,
        compiler_params=_params("parallel", "parallel"),
    )(proj, proj, proj, cs)


def _attn_prep_bwd(qa, lse, o, do, t, nh, hd):
    tr = _tile(t, ATTN_TILE, 16)
    inv_scale = math.sqrt(hd)

    def body(qa_ref, lse_ref, o_ref, do_ref, qb_ref, da_ref):
        l_hi, l_mid, l_lo = _split3(lse_ref[0][:, :1] * (-inv_scale))
        lane = lax.broadcasted_iota(jnp.int32, (tr, LANES), 1)
        extra = qa_ref[0, :, hd:].astype(F32)
        extra = jnp.where(lane == 6, l_hi, jnp.where(lane == 7, l_mid, jnp.where(lane == 8, l_lo, extra)))
        qb_ref[0, :, :hd] = qa_ref[0, :, :hd]
        qb_ref[0, :, hd:] = extra.astype(BF16)
        dov = do_ref[...]
        delta = jnp.sum(dov.astype(F32) * o_ref[...].astype(F32), axis=1, keepdims=True)
        da_ref[0, :, :hd] = dov
        da_ref[0, :, hd:] = _lanes(list(_split3(delta)), tr).astype(BF16)

    wide = pl.BlockSpec((1, tr, 2 * hd), lambda h, i: (h, i, 0))
    head = pl.BlockSpec((tr, hd), lambda h, i: (i, h))
    return pl.pallas_call(
        body,
        name="attn_prep_bwd",
        grid=(nh, t // tr),
        in_specs=[wide, pl.BlockSpec((1, tr, LANES), lambda h, i: (h, i, 0)), head, head],
        out_specs=[wide, wide],
        out_shape=[jax.ShapeDtypeStruct((nh, t, 2 * hd), BF16)] * 2,
        compiler_params=_params("parallel", "parallel"),
    )(qa, lse, o, do)


def _causal(s, row0, rows, cols):
    row = lax.broadcasted_iota(jnp.int32, (rows, cols), 0) + row0
    col = lax.broadcasted_iota(jnp.int32, (rows, cols), 1)
    return jnp.where(col <= row, s, NEG_INF)


def _attn_fwd(qa, ka, proj, t, nh, hd, v_off):
    tq = _tile(t, ATTN_TILE)
    sub = _tile(tq, ATTN_SUB)
    ns = tq // sub
    scale = 1.0 / math.sqrt(hd)
    vb = v_off // hd

    def body(qa_ref, ka_ref, v_ref, o_ref, lse_ref):
        i = pl.program_id(1)

        def tile(j, carry, masked):
            rows = pl.ds(pl.multiple_of(j * tq, tq), tq)
            kj = ka_ref[0, rows, :]
            vj = v_ref[rows, :]
            new = []
            for r in range(ns):
                m, l, acc = carry[r]
                s = lax.dot_general(qa_ref[0, pl.ds(r * sub, sub), :], kj, _DIMS["nt"], preferred_element_type=F32)
                if masked:
                    s = _causal(s, r * sub, sub, tq)
                m_new = jnp.maximum(m, jnp.max(s, axis=1, keepdims=True))
                p = jnp.exp2((s - m_new) * (scale * LOG2E))
                alpha = jnp.exp2((m - m_new) * (scale * LOG2E))
                l = alpha * l + jnp.sum(p, axis=1, keepdims=True)
                acc = alpha * acc + jnp.dot(p.astype(BF16), vj, preferred_element_type=F32)
                new.append((m_new, l, acc))
            return tuple(new)

        init = tuple((jnp.full((sub, 1), NEG_INF, F32), jnp.zeros((sub, 1), F32), jnp.zeros((sub, hd), F32))
                     for _ in range(ns))
        carry = lax.fori_loop(0, i, lambda j, cr: tile(j, cr, False), init)
        carry = tile(i, carry, True)
        for r in range(ns):
            m, l, acc = carry[r]
            o_ref[pl.ds(r * sub, sub), :] = (acc / l).astype(BF16)
            lse_ref[0, pl.ds(r * sub, sub), :] = jnp.broadcast_to(m * scale + jnp.log(l), (sub, LANES))

    return pl.pallas_call(
        body,
        name="attn_fwd",
        grid=(nh, t // tq),
        in_specs=[pl.BlockSpec((1, tq, 2 * hd), lambda h, i: (h, i, 0)),
                  pl.BlockSpec((1, t, 2 * hd), lambda h, i: (h, 0, 0)),
                  pl.BlockSpec((t, hd), lambda h, i: (0, vb + h))],
        out_specs=[pl.BlockSpec((tq, hd), lambda h, i: (i, h)),
                   pl.BlockSpec((1, tq, LANES), lambda h, i: (h, i, 0))],
        out_shape=[jax.ShapeDtypeStruct((t, nh * hd), BF16), jax.ShapeDtypeStruct((nh, t, LANES), F32)],
        compiler_params=_params("parallel", "parallel"),
    )(qa, ka, proj)


def _attn_bwd(qb, ka, va, da, t, nh, hd):
    tq = _tile(t, ATTN_TILE)
    nq = t // tq
    sub = _tile(tq, ATTN_SUB)
    ns = tq // sub
    scale = 1.0 / math.sqrt(hd)

    def body(qb_ref, ka_ref, va_ref, da_ref, dq_ref, dk_ref, dv_ref, dck_ref, dcq_ref, dq_all):
        j = pl.program_id(1)

        @pl.when(j == 0)
        def _():
            dq_all[...] = jnp.zeros(dq_all.shape, F32)
            dcq_ref[...] = jnp.zeros(dcq_ref.shape, F32)

        kaj = ka_ref[0]
        vaj = va_ref[0]
        kj = kaj[:, :hd]

        def tile(i, carry, masked):
            dk, dv, dck = carry
            for r in range(ns):
                rows = pl.ds(pl.multiple_of(i * tq + r * sub, sub), sub)
                qr = qb_ref[0, rows, :]
                dr = da_ref[0, rows, :]
                s = lax.dot_general(qr, kaj, _DIMS["nt"], preferred_element_type=F32)
                if masked:
                    s = _causal(s, r * sub, sub, tq)
                p = jnp.exp2(s * (scale * LOG2E))
                ds = p * lax.dot_general(dr, vaj, _DIMS["nt"], preferred_element_type=F32)
                dsb = ds.astype(BF16)
                dv = dv + lax.dot_general(p.astype(BF16), dr[:, :hd], _DIMS["tn"], preferred_element_type=F32)
                dk = dk + lax.dot_general(dsb, qr[:, :hd], _DIMS["tn"], preferred_element_type=F32)
                dck = dck - jnp.sum(ds, axis=0, keepdims=True)
                dq_all[rows, :] += jnp.dot(dsb, kj, preferred_element_type=F32)
                dcq_ref[0, rows, :] += jnp.sum(ds, axis=1, keepdims=True)
            return dk, dv, dck

        carry = (jnp.zeros((tq, hd), F32), jnp.zeros((tq, hd), F32), jnp.zeros((1, tq), F32))
        carry = tile(j, carry, True)
        dk, dv, dck = lax.fori_loop(j + 1, nq, lambda i, cr: tile(i, cr, False), carry)
        dk_ref[...] = (dk * scale).astype(BF16)
        dv_ref[...] = dv.astype(BF16)
        dck_ref[0] = dck

        @pl.when(j == nq - 1)
        def _():
            dq_ref[...] = (dq_all[...] * scale).astype(BF16)

    whole = pl.BlockSpec((1, t, 2 * hd), lambda h, j: (h, 0, 0))
    block = pl.BlockSpec((1, tq, 2 * hd), lambda h, j: (h, j, 0))
    return pl.pallas_call(
        body,
        name="attn_bwd",
        grid=(nh, nq),
        in_specs=[whole, block, block, whole],
        out_specs=[
            pl.BlockSpec((t, hd), lambda h, j: (0, h)),
            pl.BlockSpec((tq, hd), lambda h, j: (j, h)),
            pl.BlockSpec((tq, hd), lambda h, j: (j, h)),
            pl.BlockSpec((1, 1, tq), lambda h, j: (h, 0, j)),
            pl.BlockSpec((1, t, LANES), lambda h, j: (h, 0, 0)),
        ],
        out_shape=[jax.ShapeDtypeStruct((t, nh * hd), BF16), jax.ShapeDtypeStruct((t, nh * hd), BF16),
                   jax.ShapeDtypeStruct((t, nh * hd), BF16), jax.ShapeDtypeStruct((nh, 1, t), F32),
                   jax.ShapeDtypeStruct((nh, t, LANES), F32)],
        scratch_shapes=[pltpu.VMEM((t, hd), F32)],
        compiler_params=_params("arbitrary", "arbitrary"),
    )(qb, ka, va, da)


class _Dims:
    def __init__(self, t, d, c, nh, aw, f, pd):
        self.t, self.d, self.c, self.nh, self.aw, self.f, self.pd = t, d, c, nh, aw, f, pd
        self.hd = aw // nh
        self.a_off, self.g_off = 0, c
        self.q_off, self.k_off, self.v_off = 2 * c, 2 * c + aw, 2 * c + 2 * aw
        self.gc_off = 2 * c + 3 * aw
        self.ga_off = self.gc_off + d
        self.n_main = self.ga_off + d


def _ffn_tn(f):
    return _tile(f, 1536)


class _Riders:
    def __init__(self, plan=None):
        self.plan = plan or {}

    def host(self, name, n_out, call):
        if name not in self.plan:
            return call()
        make_side, take = self.plan[name]
        outs = call(side=make_side())
        take(outs[n_out:])
        return outs[:n_out]


def _layer_fwd(dm, x, p, w, riders):
    t, d, c, f = dm.t, dm.d, dm.c, dm.f
    s = {"x": x}
    h = _rms_fwd("rms_mix", x, w["n_mix"], t, d)
    proj, = riders.host("mm_in", 1, functools.partial(
        _matmul, "mm_in", "nn", [h], [w["main"]], [(0, 0, 0)], 1, t, dm.n_main, d, [BF16]))
    fl = _mm("mm_forget", "nn", h, w["f"], t, LANES, d, F32)
    u2, cpre = _conv_fwd(proj, w["conv_w"], w["conv_b"], w["ln_g"], w["ln_b"], t, c, dm.a_off, dm.g_off)
    f_t = fl[:, :dm.nh].T
    cum = _forget_fwd(f_t, w["b_f"], dm.nh, t)
    cs = jnp.broadcast_to((cum * math.sqrt(dm.hd))[:, :, None], (dm.nh, t, LANES))
    qa, ka, va = _attn_prep_fwd(proj, cs, t, dm.nh, dm.hd, dm.q_off, dm.k_off, dm.v_off)
    o, lse = _attn_fwd(qa, ka, proj, t, dm.nh, dm.hd, dm.v_off)

    def epi_conv(accs, ex):
        return accs[0], _sig(ex[0].astype(F32)) * accs[0]

    yc, m1 = _matmul("mm_conv_out", "nn", [u2], [w["co"]], [(0, 0, 0)], 1, t, d, c, [BF16, BF16],
                     epi=epi_conv, extras=[("mn", (proj, dm.gc_off))], tm=512)

    def epi_attn(accs, ex):
        return accs[0], ex[1].astype(F32) + _sig(ex[0].astype(F32)) * accs[0]

    ya, merged = _matmul("mm_attn_out", "nn", [o], [w["ao"]], [(0, 0, 0)], 1, t, d, dm.aw, [BF16, BF16],
                         epi=epi_attn, extras=[("mn", (proj, dm.ga_off)), ("mn", m1)], tm=512)
    x1 = _matmul("mm_out", "nn", [merged], [w["o"]], [(0, 0, 0)], 1, t, d, d, [F32],
                 epi=lambda accs, ex: [ex[0] + accs[0]], extras=[("mn", x)], tm=512)[0]

    hf = _rms_fwd("rms_ffn", x1, w["n_ffn"], t, d)

    def epi_glu(accs, ex):
        gate, up = accs
        return gate, up, gate * _sig(gate) * up

    gate, up, act = riders.host("mm_gate_up", 3, functools.partial(
        _matmul, "mm_gate_up", "nn", [hf], [w["g"], w["u"]], [(0, 0, 0), (0, 1, 1)], 2, t, f, d, [BF16, BF16, BF16],
        epi=epi_glu, tm=512, tn=_ffn_tn(f), tk=_tile(d, 1024)))
    x2, = riders.host("mm_down", 1, functools.partial(
        _matmul, "mm_down", "nn", [act], [w["d"]], [(0, 0, 0)], 1, t, d, f, [F32],
        epi=lambda accs, ex: [ex[0] + accs[0]], extras=[("mn", x1)], tm=512))

    hp = _rms_fwd("rms_ple", x2, w["n_ple"], t, d)
    pp = _mm("mm_ple_proj", "nn", p, w["pp"], t, d, dm.pd, BF16, tm=512)

    def epi_ple(accs, ex):
        sg = _sig(accs[0])
        return sg, ex[1] + sg * ex[0].astype(F32)

    sg, x3 = _matmul("mm_ple_gate", "nn", [hp], [w["pg"]], [(0, 0, 0)], 1, t, d, d, [BF16, F32],
                     epi=epi_ple, extras=[("mn", pp), ("mn", x2)], tm=512)
    s.update(h=h, proj=proj, f_t=f_t, qa=qa, ka=ka, va=va, u2=u2, cpre=cpre, o=o, lse=lse, yc=yc, ya=ya,
             merged=merged, x1=x1, hf=hf, gate=gate, up=up, act=act, x2=x2, hp=hp, pp=pp, sg=sg, p=p)
    return x3, s


def _layer_bwd(dm, dx3, dx3b, s, w, riders):
    t, d, c, f = dm.t, dm.d, dm.c, dm.f
    g = {}

    def ple_ew(dxv, sgv, ppv):
        sgf, ppf = sgv.astype(F32), ppv.astype(F32)
        return dxv * sgf, dxv * ppf * sgf * (1.0 - sgf)

    d_pp, d_z = _ew("ple_bwd", ple_ew, [dx3, s["sg"], s["pp"]], t, d, [BF16, BF16], _tile(t, ROW_TILE, 8))
    g["pp"] = _mm("gw_ple_proj", "tn", s["p"], d_pp, dm.pd, d, t, F32)
    g["pg"] = _mm("gw_ple_gate", "tn", s["hp"], d_z, d, d, t, F32)
    d_hp = _mm("dx_ple_gate", "nt", d_z, w["pg"], t, d, d, BF16)
    dx2, dx2b, g["n_ple"] = _rms_bwd("rms_ple_bwd", d_hp, s["x2"], w["n_ple"], dx3, t, d)

    def epi_dglu(accs, ex):
        gate, up = ex[0].astype(F32), ex[1].astype(F32)
        sg = _sig(gate)
        return accs[0] * up * (sg * (1.0 + gate * (1.0 - sg))), accs[0] * gate * sg

    d_gate, d_up = riders.host("dx_down", 2, functools.partial(
        _matmul, "dx_down", "nt", [dx2b], [w["d"]], [(0, 0, 0)], 1, t, f, d, [BF16, BF16], epi=epi_dglu,
        extras=[("mn", s["gate"]), ("mn", s["up"])], tm=512, tn=_ffn_tn(f), tk=_tile(d, 1024)))
    g["d"] = _mm("gw_down", "tn", s["act"], dx2b, f, d, t, F32, tm=_ffn_tn(f))
    g["g"] = _mm("gw_gate", "tn", s["hf"], d_gate, d, f, t, F32, tn=_ffn_tn(f))
    g["u"] = _mm("gw_up", "tn", s["hf"], d_up, d, f, t, F32, tn=_ffn_tn(f))
    d_hf, = riders.host("dx_gate_up", 1, functools.partial(
        _matmul, "dx_gate_up", "nt", [d_gate, d_up], [w["g"], w["u"]], [(0, 0, 0), (1, 1, 0)], 1, t, d, f, [BF16]))
    dx1, dx1b, g["n_ffn"] = _rms_bwd("rms_ffn_bwd", d_hf, s["x1"], w["n_ffn"], dx2, t, d)

    g["o"] = _mm("gw_out", "tn", s["merged"], dx1b, d, d, t, F32)

    def epi_dmerge(accs, ex):
        dmv = accs[0]
        sgc, sga = _sig(ex[0].astype(F32)), _sig(ex[1].astype(F32))
        ycv, yav = ex[2].astype(F32), ex[3].astype(F32)
        return dmv * sgc, dmv * sga, dmv * ycv * sgc * (1.0 - sgc), dmv * yav * sga * (1.0 - sga)

    d_yc, d_ya, d_gc, d_ga = _matmul(
        "dx_out", "nt", [dx1b], [w["o"]], [(0, 0, 0)], 1, t, d, d, [BF16] * 4, epi=epi_dmerge,
        extras=[("mn", (s["proj"], dm.gc_off)), ("mn", (s["proj"], dm.ga_off)), ("mn", s["yc"]), ("mn", s["ya"])],
        tm=512, tn=_tile(d, 512))
    g["co"] = _mm("gw_conv_out", "tn", s["u2"], d_yc, c, d, t, F32)
    d_u2 = _mm("dx_conv_out", "nt", d_yc, w["co"], t, c, d, BF16)
    g["ao"] = _mm("gw_attn_out", "tn", s["o"], d_ya, dm.aw, d, t, F32)
    d_o = _mm("dx_attn_out", "nt", d_ya, w["ao"], t, dm.aw, d, BF16)

    qb, da = _attn_prep_bwd(s["qa"], s["lse"], s["o"], d_o, t, dm.nh, dm.hd)
    dq, dk, dv, dck, dcq = _attn_bwd(qb, s["ka"], s["va"], da, t, dm.nh, dm.hd)
    d_ft, g_bf = _forget_bwd(dck.reshape(dm.nh, t) + dcq[:, :, 0], s["f_t"], w["b_f"], dm.nh, t)
    g["b_f"] = g_bf[:, 0]
    d_f = jnp.pad(d_ft.T, ((0, 0), (0, LANES - dm.nh))).astype(BF16)

    d_a, d_gg, g["conv_w"], g["conv_b"], g["ln_g"], g["ln_b"] = _conv_bwd(
        s["proj"], s["cpre"], d_u2, w["conv_w"], w["ln_g"], w["ln_b"], t, c, dm.a_off, dm.g_off)

    d_proj = jnp.concatenate([d_a, d_gg, dq, dk, dv, d_gc, d_ga], axis=1)
    g["main"], = riders.host("gw_in", 1, functools.partial(
        _matmul, "gw_in", "tn", [s["h"]], [d_proj], [(0, 0, 0)], 1, d, dm.n_main, t, [F32]))
    g["f"] = _mm("gw_forget", "tn", s["h"], d_f, d, LANES, t, F32)
    d_h_f = _mm("dx_forget", "nt", d_f, w["f"], t, d, LANES, BF16)
    d_h = _matmul("dx_in", "nt", [d_proj], [w["main"]], [(0, 0, 0)], 1, t, d, dm.n_main, [BF16],
                  epi=lambda accs, ex: [accs[0] + ex[0].astype(F32)], extras=[("mn", d_h_f)])[0]
    dx, dxb, g["n_mix"] = _rms_bwd("rms_mix_bwd", d_h, s["x"], w["n_mix"], dx1, t, d)
    return dx, dxb, g


def _adamw_tiles(wv, gv, mv, vv):
    m_new = ADAM_B1 * mv + (1.0 - ADAM_B1) * gv
    v_new = ADAM_B2 * vv + (1.0 - ADAM_B2) * (gv * gv)
    m_hat = m_new / (1.0 - ADAM_B1 ** ADAM_STEP)
    v_hat = v_new / (1.0 - ADAM_B2 ** ADAM_STEP)
    delta = -ADAM_LR * (m_hat / (jnp.sqrt(v_hat) + ADAM_EPS) + ADAM_WD * wv)
    return delta, m_new, v_new


def _adamw(name, wv, gv, mv, vv):
    shape = wv.shape
    cols = shape[-1]
    rows = wv.size // cols
    tm = _tile(rows, max(8, (1 << 18) // cols), 8)
    flat = [a.reshape(rows, cols) for a in (wv, gv, mv, vv)]
    outs = _ew(name, _adamw_tiles, flat, rows, cols, [F32, F32, F32], tm)
    return [o.reshape(shape) for o in outs]


def _place():
    return lax.axis_index("x"), lax.axis_index("y"), lax.axis_index("c")


def _other_chips(x, y):
    return [(1 - x, y), (x, 1 - y), (1 - x, 1 - y)]


def _window(ref, kind, chip, size):
    if kind == "chip":
        return ref.at[chip]
    if kind == "row":
        return ref.at[pl.ds(chip * size, size), :]
    return ref.at[:, pl.ds(pl.multiple_of(chip * size, LANES), size)]


class _Side:
    def __init__(self, ins, out_shapes, sem_counts, start, finish):
        self.ins, self.out_shapes, self.sem_counts, self.start, self.finish = ins, out_shapes, sem_counts, start, finish


def _run_side(name, side):
    n_in, n_out = len(side.ins), len(side.out_shapes)

    def body(*refs):
        ins, outs, sems = refs[:n_in], refs[n_in:n_in + n_out], refs[n_in + n_out:]
        side.start(ins, outs, sems)
        side.finish(ins, outs, sems)

    return pl.pallas_call(
        body,
        name=name,
        in_specs=[ANY] * n_in,
        out_specs=[ANY] * n_out,
        out_shape=list(side.out_shapes),
        scratch_shapes=[pltpu.SemaphoreType.DMA((cnt,)) for cnt in side.sem_counts],
    )(*side.ins)


def _full_shape(shard, kind):
    _, a, b = shard.shape
    if kind == "chip":
        return (N_CHIPS, a, b)
    return (N_CHIPS * a, b) if kind == "row" else (a, N_CHIPS * b)


def _gather_side(shards, kinds, layer):
    n = len(shards)
    sizes = [s.shape[1] if k == "row" else s.shape[2] for s, k in zip(shards, kinds)]

    def copies(ins, outs, sems, forwards):
        ici_send, ici_recv, d2d_send, d2d_recv, own_send, own_recv = sems
        x, y, c = _place()
        me = 2 * x + y
        sibling = (x, y, 1 - c)
        own, ici, landed, fwd = [], [], [], []
        for w in range(n):
            mine = _window(outs[w], kinds[w], me, sizes[w])
            own.append(pltpu.make_async_remote_copy(
                src_ref=ins[w].at[layer], dst_ref=mine, send_sem=own_send.at[w], recv_sem=own_recv.at[w],
                device_id=sibling, device_id_type=MESH))
            for r, (px, py) in enumerate(_other_chips(x, y)):
                ici.append(pltpu.make_async_remote_copy(
                    src_ref=ins[w].at[layer], dst_ref=mine, send_sem=ici_send.at[3 * w + r],
                    recv_sem=ici_recv.at[3 * w + r], device_id=(px, py, layer), device_id_type=MESH))
                if forwards:
                    slab = _window(outs[w], kinds[w], 2 * px + py, sizes[w])
                    landed.append(pltpu.make_async_remote_copy(
                        src_ref=slab, dst_ref=slab, send_sem=ici_send.at[3 * w + r], recv_sem=ici_recv.at[3 * w + r],
                        device_id=(px, py, layer), device_id_type=MESH))
                    fwd.append(pltpu.make_async_remote_copy(
                        src_ref=slab, dst_ref=slab, send_sem=d2d_send.at[3 * w + r], recv_sem=d2d_recv.at[3 * w + r],
                        device_id=sibling, device_id_type=MESH))
        return c, own, ici, landed, fwd

    def start(ins, outs, sems):
        c, own, ici, _, _ = copies(ins, outs, sems, False)
        for cp in own:
            cp.start()

        @pl.when(c == layer)
        def _():
            for cp in ici:
                cp.start()

    def finish(ins, outs, sems):
        c, own, ici, landed, fwd = copies(ins, outs, sems, True)

        @pl.when(c == layer)
        def _():
            for got, cp in zip(landed, fwd):
                got.wait_recv()
                cp.start()
            for cp in ici + fwd:
                cp.wait_send()

        @pl.when(c != layer)
        def _():
            for cp in fwd:
                cp.wait_recv()

        for cp in own:
            cp.wait()

    out_shapes = [jax.ShapeDtypeStruct(_full_shape(s, k), s.dtype) for s, k in zip(shards, kinds)]
    return _Side(list(shards), out_shapes, [3 * n] * 4 + [n] * 2, start, finish)


def _swap_side(grads, layer):
    n = len(grads)

    def copies(ins, outs, sems):
        x, y, c = _place()
        return c, [pltpu.make_async_remote_copy(src_ref=ins[w], dst_ref=outs[w], send_sem=sems[0].at[w],
                                                recv_sem=sems[1].at[w], device_id=(x, y, layer), device_id_type=MESH)
                   for w in range(n)]

    def start(ins, outs, sems):
        c, cps = copies(ins, outs, sems)

        @pl.when(c != layer)
        def _():
            for cp in cps:
                cp.start()

    def finish(ins, outs, sems):
        c, cps = copies(ins, outs, sems)

        @pl.when(c != layer)
        def _():
            for cp in cps:
                cp.wait_send()

        @pl.when(c == layer)
        def _():
            for cp in cps:
                cp.wait_recv()

    return _Side(list(grads), [jax.ShapeDtypeStruct(a.shape, a.dtype) for a in grads], [n, n], start, finish)


def _add_core_partials(name, mine, got):
    shape = got.shape
    cols = shape[-1]
    rows = got.size // cols
    tm, tn = _tile(rows, 256, 8), _tile(cols, 2048)
    return _ew(name, lambda a, b: [a + b], [mine.reshape(rows, cols), got.reshape(rows, cols)], rows, cols,
               [BF16], tm, tn)[0].reshape(shape)


def _exchange_side(parts, kinds, sizes, layer):
    n = len(parts)

    def copies(ins, outs, sems):
        x, y, c = _place()
        cps = []
        for w in range(n):
            for r, (px, py) in enumerate(_other_chips(x, y)):
                cps.append(pltpu.make_async_remote_copy(
                    src_ref=_window(ins[w], kinds[w], 2 * px + py, sizes[w]), dst_ref=outs[w].at[r],
                    send_sem=sems[0].at[3 * w + r], recv_sem=sems[1].at[3 * w + r],
                    device_id=(px, py, layer), device_id_type=MESH))
        return c, cps

    def start(ins, outs, sems):
        c, cps = copies(ins, outs, sems)

        @pl.when(c == layer)
        def _():
            for cp in cps:
                cp.start()

    def finish(ins, outs, sems):
        c, cps = copies(ins, outs, sems)

        @pl.when(c == layer)
        def _():
            for cp in cps:
                cp.wait()

    out_shapes = [jax.ShapeDtypeStruct((3,) + tuple(_shard_shape(p, k, s)), p.dtype)
                  for p, k, s in zip(parts, kinds, sizes)]
    return _Side(list(parts), out_shapes, [3 * n, 3 * n], start, finish)


def _shard_shape(whole, kind, size):
    if kind == "chip":
        return whole.shape[1:]
    return (size, whole.shape[1]) if kind == "row" else (whole.shape[0], size)


def _sum_chip_partials(name, part, got, kind, size, chip, layer, both=None):
    rows, cols = _shard_shape(part, kind, size)
    tm = _tile(rows, max(8, (1 << 19) // cols), 16)

    def body(chip_ref, part_ref, g0_ref, g1_ref, g2_ref, *rest):
        total = part_ref[...].astype(F32)
        for ref in (g0_ref, g1_ref, g2_ref):
            total = total + ref[...].astype(F32)
        rest[-1][...] = total

    if kind == "chip":
        mine = pl.BlockSpec((None, tm, cols), lambda i, chip_ref: (chip_ref[0], i, 0))
    elif kind == "row":
        mine = pl.BlockSpec((tm, cols), lambda i, chip_ref: (chip_ref[0] * (rows // tm) + i, 0))
    else:
        mine = pl.BlockSpec((tm, cols), lambda i, chip_ref: (i, chip_ref[0]))
    theirs = [pl.BlockSpec((None, tm, cols), functools.partial(lambda r, i, chip_ref: (r, i, 0), r))
              for r in range(3)]
    kept = [] if both is None else [both]
    return pl.pallas_call(
        body,
        name=name,
        grid_spec=pltpu.PrefetchScalarGridSpec(
            num_scalar_prefetch=1, grid=(rows // tm,), in_specs=[mine] + theirs + [ANY] * len(kept),
            out_specs=pl.BlockSpec((None, tm, cols), lambda i, chip_ref: (layer, i, 0))),
        out_shape=jax.ShapeDtypeStruct((2, rows, cols), F32),
        input_output_aliases={5: 0} if kept else {},
        compiler_params=_params("parallel"),
    )(chip.reshape(1), part, got, got, got, *kept)


def _share_reduced(both):
    n = len(both)

    def body(*refs):
        ins, outs = refs[:n], refs[n:2 * n]
        send_sems, recv_sems = refs[2 * n:]
        x, y, c = _place()
        sends = []
        for w in range(n):
            cp = pltpu.make_async_remote_copy(src_ref=ins[w].at[c], dst_ref=outs[w].at[c], send_sem=send_sems.at[w],
                                              recv_sem=recv_sems.at[w], device_id=(x, y, 1 - c), device_id_type=MESH)
            cp.start()
            sends.append(cp)
        for w in range(n):
            got = outs[w].at[1 - c]
            pltpu.make_async_remote_copy(src_ref=got, dst_ref=got, send_sem=send_sems.at[w], recv_sem=recv_sems.at[w],
                                         device_id=(x, y, 1 - c), device_id_type=MESH).wait_recv()
        for cp in sends:
            cp.wait_send()

    return pl.pallas_call(
        body,
        name="share_reduced",
        in_specs=[ANY] * n,
        out_specs=[ANY] * n,
        out_shape=[jax.ShapeDtypeStruct(a.shape, a.dtype) for a in both],
        input_output_aliases={w: w for w in range(n)},
        scratch_shapes=[pltpu.SemaphoreType.DMA((n,)), pltpu.SemaphoreType.DMA((n,))],
    )(*both)


def _allreduce_small(v):
    rows, width = v.shape

    def body(v_ref, out_ref, slots, send_sems, recv_sems):
        x, y, c = _place()
        me = 4 * x + 2 * y + c
        slots[me] = v_ref[...]
        peers = [(x, y, 1 - c)]
        for px, py in _other_chips(x, y):
            peers += [(px, py, c), (px, py, 1 - c)]
        sends = []
        for r, peer in enumerate(peers):
            cp = pltpu.make_async_remote_copy(
                src_ref=v_ref, dst_ref=slots.at[me], send_sem=send_sems.at[r], recv_sem=recv_sems.at[r],
                device_id=peer, device_id_type=MESH)
            cp.start()
            sends.append(cp)
        for r, (px, py, pc) in enumerate(peers):
            got = slots.at[4 * px + 2 * py + pc]
            pltpu.make_async_remote_copy(
                src_ref=got, dst_ref=got, send_sem=send_sems.at[r], recv_sem=recv_sems.at[r],
                device_id=(px, py, pc), device_id_type=MESH).wait_recv()
        for cp in sends:
            cp.wait_send()
        total = slots[0]
        for n in range(1, 8):
            total = total + slots[n]
        out_ref[...] = total

    vm = pl.BlockSpec(memory_space=pltpu.VMEM)
    return pl.pallas_call(
        body,
        name="allreduce_small",
        in_specs=[vm],
        out_specs=vm,
        out_shape=jax.ShapeDtypeStruct((rows, width), F32),
        scratch_shapes=[pltpu.VMEM((8, rows, width), F32), pltpu.SemaphoreType.DMA((7,)),
                        pltpu.SemaphoreType.DMA((7,))],
    )(v)


def _pad_rows(flat, row_align):
    n = flat.shape[0]
    rows = -(-n // PACK_W)
    rows = -(-rows // row_align) * row_align
    return jnp.pad(flat, (0, rows * PACK_W - n)).reshape(rows, PACK_W)


def _unpack(buf, shapes):
    flat = buf.reshape(-1)
    out, off = [], 0
    for shp in shapes:
        n = math.prod(shp)
        out.append(flat[off:off + n].reshape(shp))
        off += n
    return out


BIG = (("w_in", "chip"), ("w_conv_out", "col"), ("w_attn_out", "col"), ("w_out", "row"), ("w_gate_up", "col"),
       ("w_down", "row"), ("w_ple_gate", "row"), ("w_ple_proj", "col"))


def kernel(x, p, norm_mix_g, w_in, b_forget, conv_w, conv_b, conv_ln_g, conv_ln_b, w_conv_out, w_attn_out, w_out, norm_ffn_g, w_gate_up, w_down, norm_ple_g, w_ple_gate, w_ple_proj, final_g, loss_target, m_norm_mix_g, m_w_in, m_b_forget, m_conv_w, m_conv_b, m_conv_ln_g, m_conv_ln_b, m_w_conv_out, m_w_attn_out, m_w_out, m_norm_ffn_g, m_w_gate_up, m_w_down, m_norm_ple_g, m_w_ple_gate, m_w_ple_proj, m_final_g, v_norm_mix_g, v_w_in, v_b_forget, v_conv_w, v_conv_b, v_conv_ln_g, v_conv_ln_b, v_w_conv_out, v_w_attn_out, v_w_out, v_norm_ffn_g, v_w_gate_up, v_w_down, v_norm_ple_g, v_w_ple_gate, v_w_ple_proj, v_final_g):
    wts = dict(norm_mix_g=norm_mix_g, w_in=w_in, b_forget=b_forget, conv_w=conv_w, conv_b=conv_b,
               conv_ln_g=conv_ln_g, conv_ln_b=conv_ln_b, w_conv_out=w_conv_out, w_attn_out=w_attn_out,
               w_out=w_out, norm_ffn_g=norm_ffn_g, w_gate_up=w_gate_up, w_down=w_down, norm_ple_g=norm_ple_g,
               w_ple_gate=w_ple_gate, w_ple_proj=w_ple_proj, final_g=final_g)
    mom1 = dict(norm_mix_g=m_norm_mix_g, w_in=m_w_in, b_forget=m_b_forget, conv_w=m_conv_w, conv_b=m_conv_b,
                conv_ln_g=m_conv_ln_g, conv_ln_b=m_conv_ln_b, w_conv_out=m_w_conv_out, w_attn_out=m_w_attn_out,
                w_out=m_w_out, norm_ffn_g=m_norm_ffn_g, w_gate_up=m_w_gate_up, w_down=m_w_down,
                norm_ple_g=m_norm_ple_g, w_ple_gate=m_w_ple_gate, w_ple_proj=m_w_ple_proj, final_g=m_final_g)
    mom2 = dict(norm_mix_g=v_norm_mix_g, w_in=v_w_in, b_forget=v_b_forget, conv_w=v_conv_w, conv_b=v_conv_b,
                conv_ln_g=v_conv_ln_g, conv_ln_b=v_conv_ln_b, w_conv_out=v_w_conv_out, w_attn_out=v_w_attn_out,
                w_out=v_w_out, norm_ffn_g=v_norm_ffn_g, w_gate_up=v_w_gate_up, w_down=v_w_down,
                norm_ple_g=v_norm_ple_g, w_ple_gate=v_w_ple_gate, w_ple_proj=v_w_ple_proj, final_g=v_final_g)
    order = list(wts)
    depth = w_in.shape[0]
    assert depth == 2, "the exchanges give one layer to each of a chip's two cores"
    t, d = x.shape[1], x.shape[2]
    c = conv_ln_g.shape[1]
    nh = b_forget.shape[1]
    aw = w_attn_out.shape[1]
    f = N_CHIPS * w_down.shape[1]
    pd = w_ple_proj.shape[1]
    dm = _Dims(t, d, c, nh, aw, f, pd)
    n_split = 2 * c + 3 * aw
    cw = conv_w.shape[2]
    chip = 2 * lax.axis_index("x") + lax.axis_index("y")
    big_names = [name for name, _ in BIG]
    big_kinds = [kind for _, kind in BIG]

    shards = {name: wts[name].astype(BF16) for name in big_names}
    shards["conv_w"] = conv_w
    kind_of = dict(BIG, conv_w="chip")
    full = [{}, {}]

    def gather_rider(names, layer):
        def make_side():
            return _gather_side([shards[n] for n in names], [kind_of[n] for n in names], layer)

        def take(results):
            full[layer].update(zip(names, results))

        return make_side, take

    make_side, take = gather_rider(big_names + ["conv_w"], 0)
    take(_run_side("gather_layer0", make_side()))
    fwd_riders = [_Riders({"mm_in": gather_rider(["w_in", "w_conv_out", "w_out", "conv_w"], 1),
                           "mm_gate_up": gather_rider(["w_gate_up", "w_ple_gate", "w_attn_out"], 1),
                           "mm_down": gather_rider(["w_down", "w_ple_proj"], 1)}),
                  _Riders()]

    def layer_weights(l):
        fw = full[l]
        wi = jnp.concatenate([fw["w_in"][k] for k in range(N_CHIPS)], axis=1)
        cwl = jnp.concatenate([fw["conv_w"][k] for k in range(N_CHIPS)], axis=1)
        return {
            "main": jnp.concatenate([wi[:, :n_split], wi[:, n_split + nh:]], axis=1),
            "f": jnp.pad(wi[:, n_split:n_split + nh], ((0, 0), (0, LANES - nh))),
            "conv_w": jnp.pad(cwl, ((0, CONV_HALO - CONV_K), (0, 0))),
            "conv_b": conv_b[l][None], "ln_g": conv_ln_g[l][None], "ln_b": conv_ln_b[l][None],
            "b_f": jnp.broadcast_to(b_forget[l][:, None], (nh, LANES)),
            "co": fw["w_conv_out"], "ao": fw["w_attn_out"], "o": fw["w_out"],
            "g": (fw["w_gate_up"], 0), "u": (fw["w_gate_up"], f), "d": fw["w_down"],
            "pg": fw["w_ple_gate"], "pp": fw["w_ple_proj"],
            "n_mix": norm_mix_g[l][None], "n_ffn": norm_ffn_g[l][None], "n_ple": norm_ple_g[l][None],
        }

    xl = x[0]
    saved, lw = [], []
    for l in range(depth):
        lw.append(layer_weights(l))
        xl, s = _layer_fwd(dm, xl, p[l, 0], lw[l], fwd_riders[l])
        saved.append(s)
    loss_row, dx, dxb, g_final = _loss_head(xl, final_g[None], loss_target[0], t, d)
    loss = lax.psum(loss_row[0, 0], ("x", "y", "c"))

    sizes = [wts[name].shape[1] if kind == "row" else wts[name].shape[2] for name, kind in BIG]
    everything = list(range(len(BIG)))

    def layer_grads(g):
        w_in_g = jnp.concatenate([g["main"][:, :n_split], g["f"][:, :nh], g["main"][:, n_split:]], axis=1)
        return [w_in_g.reshape(d, N_CHIPS, -1).transpose(1, 0, 2), g["co"], g["ao"], g["o"],
                jnp.concatenate([g["g"], g["u"]], axis=1), g["d"], g["pg"], g["pp"]]

    def add_partials(layer, mine, got):
        return [_add_core_partials(f"add_core_partials_l{layer}_{name}", a, b)
                for name, a, b in zip(big_names, mine, got)]

    def exchange(parts, group, layer):
        return _exchange_side([parts[w] for w in group], [big_kinds[w] for w in group], [sizes[w] for w in group], layer)

    def sum_partials(layer, parts, got, both):
        return [_sum_chip_partials(f"sum_chip_partials_l{layer}_{big_names[w]}", parts[w], got[w], big_kinds[w],
                                   sizes[w], chip, layer, both=None if both is None else both[w])
                for w in everything]

    lg = [None] * depth
    dx, dxb, lg[1] = _layer_bwd(dm, dx, dxb, saved[1], lw[1], _Riders())
    g1 = layer_grads(lg[1])
    stage = {"got": {}}

    def take_swap(results):
        stage["parts"] = add_partials(1, g1, results)

    def exchange_rider(group):
        return (lambda: exchange(stage["parts"], group, 1)), (lambda results: stage["got"].update(zip(group, results)))

    bwd_riders = _Riders({"dx_down": (lambda: _swap_side(g1, 1), take_swap),
                          "dx_gate_up": exchange_rider([4, 5, 6, 7]), "gw_in": exchange_rider([0, 1, 2, 3])})
    dx, dxb, lg[0] = _layer_bwd(dm, dx, dxb, saved[0], lw[0], bwd_riders)
    both = sum_partials(1, stage["parts"], stage["got"], None)

    g0 = layer_grads(lg[0])
    parts0 = add_partials(0, g0, _run_side("swap_layer0_grads", _swap_side(g0, 0)))
    got0 = _run_side("exchange_layer0_grads", exchange(parts0, everything, 0))
    both = sum_partials(0, parts0, got0, both)
    grads = dict(zip(big_names, _share_reduced(both)))

    def stacked(key):
        return jnp.stack([lg[l][key] for l in range(depth)])

    small = ["norm_mix_g", "b_forget", "conv_b", "conv_ln_g", "conv_ln_b", "norm_ffn_g", "norm_ple_g", "final_g"]
    small_g = {
        "norm_mix_g": jnp.concatenate([lg[l]["n_mix"] for l in range(depth)]),
        "b_forget": stacked("b_f"),
        "conv_b": jnp.concatenate([lg[l]["conv_b"] for l in range(depth)]),
        "conv_ln_g": jnp.concatenate([lg[l]["ln_g"] for l in range(depth)]),
        "conv_ln_b": jnp.concatenate([lg[l]["ln_b"] for l in range(depth)]),
        "norm_ffn_g": jnp.concatenate([lg[l]["n_ffn"] for l in range(depth)]),
        "norm_ple_g": jnp.concatenate([lg[l]["n_ple"] for l in range(depth)]),
        "final_g": g_final[0],
    }
    conv_w_g = jnp.stack([lg[l]["conv_w"][:CONV_K] for l in range(depth)])
    small_shapes = [wts[n].shape for n in small] + [conv_w_g.shape]
    small_pack = _pad_rows(jnp.concatenate([small_g[n].reshape(-1) for n in small] + [conv_w_g.reshape(-1)]), 8)
    small_sum = _unpack(_allreduce_small(small_pack), small_shapes)
    for n, name in enumerate(small):
        grads[name] = small_sum[n]
    grads["conv_w"] = lax.dynamic_slice_in_dim(small_sum[len(small)], chip * cw, cw, axis=2)

    def pack_small(src):
        return _pad_rows(jnp.concatenate([src[n].reshape(-1) for n in small]), 8)

    small_upd = _adamw("adamw_small", pack_small(wts), pack_small(grads), pack_small(mom1), pack_small(mom2))
    small_upd = [_unpack(u, [wts[n].shape for n in small]) for u in small_upd]
    delta, new_m, new_v = {}, {}, {}
    for n, name in enumerate(small):
        delta[name], new_m[name], new_v[name] = small_upd[0][n], small_upd[1][n], small_upd[2][n]
    for name in big_names + ["conv_w"]:
        delta[name], new_m[name], new_v[name] = _adamw("adamw_" + name, wts[name], grads[name], mom1[name],
                                                       mom2[name])

    return (loss, dx[None], *[grads[n] for n in order], *[delta[n] for n in order],
            *[new_m[n] for n in order], *[new_v[n] for n in order])
```

```python
import functools
import math

import jax
import jax.numpy as jnp
from jax import lax
from jax.experimental import pallas as pl
from jax.experimental.pallas import tpu as pltpu

F32 = jnp.float32
BF16 = jnp.bfloat16

EPS = 1e-6
CONV_K = 31
NEG_INF = -1e30
LOG2E = 1.4426950408889634
ADAM_LR = 0.001
ADAM_B1 = 0.9
ADAM_B2 = 0.999
ADAM_EPS = 1e-08
ADAM_WD = 0.01
ADAM_STEP = 10

LANES = 128
VMEM_LIMIT = 60 * 1024 * 1024
PACK_W = 1024
N_CHIPS = 4
CONV_HALO = 32
CUM_BLOCK = 256

MM_TM = 1024
MM_TN = 1024
MM_TK = 2048
ROW_TILE = 256
CONV_TILE = 256
ATTN_TILE = 1024
ATTN_SUB = 1024

MESH = pl.DeviceIdType.MESH
ANY = pl.BlockSpec(memory_space=pl.ANY)


def _params(*sem):
    return pltpu.CompilerParams(dimension_semantics=sem, vmem_limit_bytes=VMEM_LIMIT)


def _tile(dim, pref, align=LANES):
    if dim <= pref:
        return dim
    t = (pref // align) * align
    while t >= align:
        if dim % t == 0:
            return t
        t -= align
    return dim


def _sig(x):
    return 1.0 / (1.0 + jnp.exp(-x))


def _op(a):
    if not isinstance(a, tuple):
        return a, 0, ()
    return a if len(a) == 3 else (a[0], a[1], ())


def _spec(block, index, lead):
    if not lead:
        return pl.BlockSpec(block, index)
    return pl.BlockSpec((None,) * len(lead) + block, lambda *g: tuple(lead) + index(*g))


_DIMS = {
    "nn": (((1,), (0,)), ((), ())),
    "nt": (((1,), (1,)), ((), ())),
    "tn": (((0,), (0,)), ((), ())),
}


def _matmul(name, mode, a_ops, b_ops, terms, n_acc, m, n, k, out_dtypes, epi=None, extras=(),
            tm=None, tn=None, tk=None, side=None):
    tm = tm or _tile(m, MM_TM)
    tn = tn or _tile(n, MM_TN)
    tk = tk or _tile(k, MM_TK)
    ni, nj, nk = m // tm, n // tn, k // tk
    a_ops = [_op(a) for a in a_ops]
    b_ops = [_op(b) for b in b_ops]
    extras = [(kind, _op(e)) for kind, e in extras]
    na, nb, ne, no = len(a_ops), len(b_ops), len(extras), len(out_dtypes)
    n_acc_refs = n_acc if nk > 1 else 0
    s_ins = list(side.ins) if side else []
    s_outs = list(side.out_shapes) if side else []
    s_sems = [pltpu.SemaphoreType.DMA((cnt,)) for cnt in side.sem_counts] if side else []

    def a_spec(off, lead):
        if mode == "tn":
            assert off % tm == 0
            return _spec((tk, tm), lambda i, j, kk: (kk, i + off // tm), lead)
        assert off % tk == 0
        return _spec((tm, tk), lambda i, j, kk: (i, kk + off // tk), lead)

    def b_spec(off, lead):
        if mode == "nt":
            assert off % tk == 0
            return _spec((tn, tk), lambda i, j, kk: (j, kk + off // tk), lead)
        assert off % tn == 0
        return _spec((tk, tn), lambda i, j, kk: (kk, j + off // tn), lead)

    def e_spec(kind, off, lead):
        assert off % tn == 0
        if kind == "n":
            return _spec((1, tn), lambda i, j, kk: (0, j + off // tn), lead)
        return _spec((tm, tn), lambda i, j, kk: (i, j + off // tn), lead)

    def body(*refs):
        refs = list(refs)
        a_refs, b_refs, e_refs, si_refs, o_refs, so_refs, acc_refs, sem_refs = (
            [refs.pop(0) for _ in range(cnt)]
            for cnt in (na, nb, ne, len(s_ins), no, len(s_outs), n_acc_refs, len(s_sems)))
        i, j, kk = pl.program_id(0), pl.program_id(1), pl.program_id(2)
        if side:
            @pl.when((i == 0) & (j == 0) & (kk == 0))
            def _():
                side.start(si_refs, so_refs, sem_refs)

        av = [r[...].astype(BF16) for r in a_refs]
        bv = [r[...].astype(BF16) for r in b_refs]
        sums = [None] * n_acc
        for ai, bi, ci in terms:
            part = lax.dot_general(av[ai], bv[bi], _DIMS[mode], preferred_element_type=F32)
            sums[ci] = part if sums[ci] is None else sums[ci] + part

        def finish(accs):
            outs = epi(accs, [e[...] for e in e_refs]) if epi is not None else accs
            for o, val in zip(o_refs, outs):
                o[...] = val.astype(o.dtype)

        if nk == 1:
            finish(sums)
        else:
            @pl.when(kk == 0)
            def _():
                for acc, part in zip(acc_refs, sums):
                    acc[...] = part

            @pl.when(kk > 0)
            def _():
                for acc, part in zip(acc_refs, sums):
                    acc[...] += part

            @pl.when(kk == nk - 1)
            def _():
                finish([acc[...] for acc in acc_refs])

        if side:
            @pl.when((i == ni - 1) & (j == nj - 1) & (kk == nk - 1))
            def _():
                side.finish(si_refs, so_refs, sem_refs)

    order = ("arbitrary",) * 3 if side else ("parallel", "parallel", "arbitrary")
    outs = pl.pallas_call(
        body,
        name=name,
        grid=(ni, nj, nk),
        in_specs=[a_spec(off, lead) for _, off, lead in a_ops] + [b_spec(off, lead) for _, off, lead in b_ops]
        + [e_spec(kind, off, lead) for kind, (_, off, lead) in extras] + [ANY] * len(s_ins),
        out_specs=[pl.BlockSpec((tm, tn), lambda i, j, kk: (i, j)) for _ in out_dtypes] + [ANY] * len(s_outs),
        out_shape=[jax.ShapeDtypeStruct((m, n), dt) for dt in out_dtypes] + s_outs,
        scratch_shapes=[pltpu.VMEM((tm, tn), F32) for _ in range(n_acc_refs)] + s_sems,
        compiler_params=_params(*order),
    )(*[a[0] for a in a_ops], *[b[0] for b in b_ops], *[e[0] for _, e in extras], *s_ins)
    return outs


def _mm(name, mode, a, b, m, n, k, out_dtype, **kw):
    return _matmul(name, mode, [a], [b], [(0, 0, 0)], 1, m, n, k, [out_dtype], **kw)[0]


def _ew(name, fn, ins, m, n, out_dtypes, tm, tn=None):
    tn = tn or n
    ins = [_op(a) for a in ins]
    ni = len(ins)

    def spec(off, lead):
        assert off % tn == 0
        return _spec((tm, tn), lambda i, j: (i, j + off // tn), lead)

    def body(*refs):
        outs = fn(*[r[...] for r in refs[:ni]])
        for o, val in zip(refs[ni:], outs):
            o[...] = val.astype(o.dtype)

    return pl.pallas_call(
        body,
        name=name,
        grid=(m // tm, n // tn),
        in_specs=[spec(off, lead) for _, off, lead in ins],
        out_specs=[pl.BlockSpec((tm, tn), lambda i, j: (i, j)) for _ in out_dtypes],
        out_shape=[jax.ShapeDtypeStruct((m, n), dt) for dt in out_dtypes],
        compiler_params=_params("parallel", "parallel"),
    )(*[a[0] for a in ins])


def _rms_fwd(name, x, g, t, d):
    tr = _tile(t, ROW_TILE, 8)

    def body(x_ref, g_ref, h_ref):
        xv = x_ref[...]
        r = lax.rsqrt(jnp.mean(xv * xv, axis=1, keepdims=True) + EPS)
        h_ref[...] = (xv * r * g_ref[...]).astype(BF16)

    return pl.pallas_call(
        body,
        name=name,
        grid=(t // tr,),
        in_specs=[pl.BlockSpec((tr, d), lambda i: (i, 0)), pl.BlockSpec((1, d), lambda i: (0, 0))],
        out_specs=pl.BlockSpec((tr, d), lambda i: (i, 0)),
        out_shape=jax.ShapeDtypeStruct((t, d), BF16),
        compiler_params=_params("parallel"),
    )(x, g)


def _rms_bwd_rows(dh, xv, g):
    r = lax.rsqrt(jnp.mean(xv * xv, axis=1, keepdims=True) + EPS)
    xhat = xv * r
    dxh = dh * g
    dx = r * (dxh - xhat * jnp.mean(dxh * xhat, axis=1, keepdims=True))
    return dx, dh * xhat


def _rms_bwd(name, dh, x, g, dres, t, d):
    tr = _tile(t, ROW_TILE, 8)

    def body(dh_ref, x_ref, g_ref, dres_ref, dx_ref, dxb_ref, dg_ref):
        @pl.when(pl.program_id(0) == 0)
        def _():
            dg_ref[...] = jnp.zeros(dg_ref.shape, F32)

        dx, dg_rows = _rms_bwd_rows(dh_ref[...].astype(F32), x_ref[...], g_ref[...])
        dx = dx + dres_ref[...]
        dx_ref[...] = dx
        dxb_ref[...] = dx.astype(BF16)
        dg_ref[...] += jnp.sum(dg_rows, axis=0, keepdims=True)

    row = pl.BlockSpec((tr, d), lambda i: (i, 0))
    vec = pl.BlockSpec((1, d), lambda i: (0, 0))
    return pl.pallas_call(
        body,
        name=name,
        grid=(t // tr,),
        in_specs=[row, row, vec, row],
        out_specs=[row, row, vec],
        out_shape=[jax.ShapeDtypeStruct((t, d), F32), jax.ShapeDtypeStruct((t, d), BF16),
                   jax.ShapeDtypeStruct((1, d), F32)],
        compiler_params=_params("arbitrary"),
    )(dh, x, g, dres)


def _loss_head(x, g, target, t, d):
    tr = _tile(t, ROW_TILE, 8)

    def body(x_ref, g_ref, tgt_ref, loss_ref, dx_ref, dxb_ref, dg_ref):
        @pl.when(pl.program_id(0) == 0)
        def _():
            dg_ref[...] = jnp.zeros(dg_ref.shape, F32)
            loss_ref[...] = jnp.zeros(loss_ref.shape, F32)

        xv = x_ref[...]
        gv = g_ref[...]
        r = lax.rsqrt(jnp.mean(xv * xv, axis=1, keepdims=True) + EPS)
        err = xv * r * gv - tgt_ref[...]
        loss_ref[...] += (0.5 / d) * jnp.sum(err * err)
        dx, dg_rows = _rms_bwd_rows(err * (1.0 / d), xv, gv)
        dx_ref[...] = dx
        dxb_ref[...] = dx.astype(BF16)
        dg_ref[...] += jnp.sum(dg_rows, axis=0, keepdims=True)

    row = pl.BlockSpec((tr, d), lambda i: (i, 0))
    vec = pl.BlockSpec((1, d), lambda i: (0, 0))
    one = pl.BlockSpec((1, LANES), lambda i: (0, 0))
    return pl.pallas_call(
        body,
        name="loss_head",
        grid=(t // tr,),
        in_specs=[row, vec, row],
        out_specs=[one, row, row, vec],
        out_shape=[jax.ShapeDtypeStruct((1, LANES), F32), jax.ShapeDtypeStruct((t, d), F32),
                   jax.ShapeDtypeStruct((t, d), BF16), jax.ShapeDtypeStruct((1, d), F32)],
        compiler_params=_params("arbitrary"),
    )(x, g, target)


def _layernorm_rows(cv, g, b):
    mu = jnp.mean(cv, axis=1, keepdims=True)
    xc = cv - mu
    rstd = lax.rsqrt(jnp.mean(xc * xc, axis=1, keepdims=True) + EPS)
    xhat = xc * rstd
    return xhat, rstd, xhat * g + b


def _conv_fwd(proj, conv_w, conv_b, ln_g, ln_b, t, c, a_off, g_off):
    tm = _tile(t, CONV_TILE, CONV_HALO)
    per = tm // CONV_HALO
    ab, gb = a_off // c, g_off // c

    def body(a_ref, g_ref, ap_ref, gp_ref, w_ref, cb_ref, lg_ref, lb_ref, u2_ref, c_ref, upad):
        i = pl.program_id(0)
        u_prev = ap_ref[...].astype(F32) * _sig(gp_ref[...].astype(F32))
        upad[pl.ds(0, CONV_HALO), :] = jnp.where(i > 0, u_prev, 0.0)
        upad[pl.ds(CONV_HALO, tm), :] = a_ref[...].astype(F32) * _sig(g_ref[...].astype(F32))
        acc = jnp.zeros((tm, c), F32) + cb_ref[...]
        for k in range(CONV_K):
            acc = acc + w_ref[pl.ds(k, 1), :] * upad[pl.ds(CONV_HALO - (CONV_K - 1) + k, tm), :]
        c_ref[...] = acc
        _, _, z = _layernorm_rows(acc, lg_ref[...], lb_ref[...])
        u2_ref[...] = (z * _sig(z)).astype(BF16)

    vec = pl.BlockSpec((1, c), lambda i: (0, 0))
    return pl.pallas_call(
        body,
        name="conv_fwd",
        grid=(t // tm,),
        in_specs=[
            pl.BlockSpec((tm, c), lambda i: (i, ab)),
            pl.BlockSpec((tm, c), lambda i: (i, gb)),
            pl.BlockSpec((CONV_HALO, c), lambda i: (jnp.maximum(i * per - 1, 0), ab)),
            pl.BlockSpec((CONV_HALO, c), lambda i: (jnp.maximum(i * per - 1, 0), gb)),
            pl.BlockSpec((CONV_HALO, c), lambda i: (0, 0)), vec, vec, vec,
        ],
        out_specs=[pl.BlockSpec((tm, c), lambda i: (i, 0)), pl.BlockSpec((tm, c), lambda i: (i, 0))],
        out_shape=[jax.ShapeDtypeStruct((t, c), BF16), jax.ShapeDtypeStruct((t, c), F32)],
        scratch_shapes=[pltpu.VMEM((CONV_HALO + tm, c), F32)],
        compiler_params=_params("parallel"),
    )(proj, proj, proj, proj, conv_w, conv_b, ln_g, ln_b)


def _conv_bwd(proj, cpre, du2, conv_w, ln_g, ln_b, t, c, a_off, g_off):
    tm = _tile(t, CONV_TILE, CONV_HALO)
    per = tm // CONV_HALO
    nt = t // tm
    last_halo = t // CONV_HALO - 1
    ab, gb = a_off // c, g_off // c

    def body(a_ref, g_ref, ap_ref, gp_ref, c_ref, cn_ref, du_ref, dun_ref, w_ref, lg_ref, lb_ref,
             da_ref, dg_ref, gw_ref, gcb_ref, glg_ref, glb_ref, upad, dpad):
        i = pl.program_id(0)

        @pl.when(i == 0)
        def _():
            gw_ref[...] = jnp.zeros(gw_ref.shape, F32)
            gcb_ref[...] = jnp.zeros(gcb_ref.shape, F32)
            glg_ref[...] = jnp.zeros(glg_ref.shape, F32)
            glb_ref[...] = jnp.zeros(glb_ref.shape, F32)

        lg = lg_ref[...]
        lb = lb_ref[...]

        def ln_bwd(cv, duv):
            xhat, rstd, z = _layernorm_rows(cv, lg, lb)
            sz = _sig(z)
            dz = duv * (sz * (1.0 + z * (1.0 - sz)))
            dxh = dz * lg
            dc = rstd * (dxh - jnp.mean(dxh, axis=1, keepdims=True)
                         - xhat * jnp.mean(dxh * xhat, axis=1, keepdims=True))
            return dc, dz, xhat

        dc, dz, xhat = ln_bwd(c_ref[...], du_ref[...].astype(F32))
        dc_next, _, _ = ln_bwd(cn_ref[...], dun_ref[...].astype(F32))
        glg_ref[...] += jnp.sum(dz * xhat, axis=0, keepdims=True)
        glb_ref[...] += jnp.sum(dz, axis=0, keepdims=True)
        gcb_ref[...] += jnp.sum(dc, axis=0, keepdims=True)
        dpad[pl.ds(0, tm), :] = dc
        dpad[pl.ds(tm, CONV_HALO), :] = jnp.where(i < nt - 1, dc_next, 0.0)

        av = a_ref[...].astype(F32)
        sg = _sig(g_ref[...].astype(F32))
        u_prev = ap_ref[...].astype(F32) * _sig(gp_ref[...].astype(F32))
        upad[pl.ds(0, CONV_HALO), :] = jnp.where(i > 0, u_prev, 0.0)
        upad[pl.ds(CONV_HALO, tm), :] = av * sg

        du = jnp.zeros((tm, c), F32)
        for k in range(CONV_K):
            du = du + w_ref[pl.ds(k, 1), :] * dpad[pl.ds(CONV_K - 1 - k, tm), :]
            gw_ref[pl.ds(k, 1), :] += jnp.sum(
                dc * upad[pl.ds(CONV_HALO - (CONV_K - 1) + k, tm), :], axis=0, keepdims=True)
        da_ref[...] = (du * sg).astype(BF16)
        dg_ref[...] = (du * av * sg * (1.0 - sg)).astype(BF16)

    vec = pl.BlockSpec((1, c), lambda i: (0, 0))
    cur = pl.BlockSpec((tm, c), lambda i: (i, 0))
    nxt = pl.BlockSpec((CONV_HALO, c), lambda i: (jnp.minimum((i + 1) * per, last_halo), 0))
    wsp = pl.BlockSpec((CONV_HALO, c), lambda i: (0, 0))
    return pl.pallas_call(
        body,
        name="conv_bwd",
        grid=(nt,),
        in_specs=[
            pl.BlockSpec((tm, c), lambda i: (i, ab)),
            pl.BlockSpec((tm, c), lambda i: (i, gb)),
            pl.BlockSpec((CONV_HALO, c), lambda i: (jnp.maximum(i * per - 1, 0), ab)),
            pl.BlockSpec((CONV_HALO, c), lambda i: (jnp.maximum(i * per - 1, 0), gb)),
            cur, nxt, cur, nxt, wsp, vec, vec,
        ],
        out_specs=[cur, cur, wsp, vec, vec, vec],
        out_shape=[jax.ShapeDtypeStruct((t, c), BF16), jax.ShapeDtypeStruct((t, c), BF16),
                   jax.ShapeDtypeStruct((CONV_HALO, c), F32), jax.ShapeDtypeStruct((1, c), F32),
                   jax.ShapeDtypeStruct((1, c), F32), jax.ShapeDtypeStruct((1, c), F32)],
        scratch_shapes=[pltpu.VMEM((CONV_HALO + tm, c), F32), pltpu.VMEM((tm + CONV_HALO, c), F32)],
        compiler_params=_params("arbitrary"),
    )(proj, proj, proj, proj, cpre, cpre, du2, du2, conv_w, ln_g, ln_b)


def _split3(x):
    hi = x.astype(BF16).astype(F32)
    r1 = x - hi
    mid = r1.astype(BF16).astype(F32)
    lo = (r1 - mid).astype(BF16).astype(F32)
    return hi, mid, lo


def _split3_dot(x, tri):
    dot = functools.partial(jnp.dot, preferred_element_type=F32)
    hi, mid, lo = _split3(x)
    return dot(hi.astype(BF16), tri) + dot(mid.astype(BF16), tri) + dot(lo.astype(BF16), tri)


def _to_blocks(a, nb, blk):
    return a.reshape(a.shape[0], nb, blk).transpose(1, 0, 2)


def _from_blocks(a):
    return a.transpose(1, 0, 2).reshape(a.shape[1], -1)


def _forget_fwd(f_t, b_col, nh, t):
    blk = _tile(t, CUM_BLOCK)
    nb = t // blk

    def body(f_ref, b_ref, c_ref):
        ri = lax.broadcasted_iota(jnp.int32, (blk, blk), 0)
        ci = lax.broadcasted_iota(jnp.int32, (blk, blk), 1)
        tri = (ri <= ci).astype(BF16)

        def step(bi, carry):
            xv = f_ref[bi] + b_ref[:, :1]
            lf = jnp.minimum(xv, 0.0) - jnp.log(1.0 + jnp.exp(-jnp.abs(xv)))
            cs = _split3_dot(lf, tri) + carry
            c_ref[bi] = cs
            return cs[:, blk - 1:blk]

        lax.fori_loop(0, nb, step, jnp.zeros((nh, 1), F32))

    out = pl.pallas_call(
        body,
        name="forget_fwd",
        out_shape=jax.ShapeDtypeStruct((nb, nh, blk), F32),
        compiler_params=pltpu.CompilerParams(vmem_limit_bytes=VMEM_LIMIT),
    )(_to_blocks(f_t, nb, blk), b_col)
    return _from_blocks(out)


def _forget_bwd(dc, f_t, b_col, nh, t):
    blk = _tile(t, CUM_BLOCK)
    nb = t // blk

    def body(dc_ref, f_ref, b_ref, df_ref, db_ref):
        ri = lax.broadcasted_iota(jnp.int32, (blk, blk), 0)
        ci = lax.broadcasted_iota(jnp.int32, (blk, blk), 1)
        tri = (ri >= ci).astype(BF16)

        def step(n, carry):
            tail, db = carry
            bi = nb - 1 - n
            rc = _split3_dot(dc_ref[bi], tri) + tail
            df = rc * _sig(-(f_ref[bi] + b_ref[:, :1]))
            df_ref[bi] = df
            return rc[:, 0:1], db + jnp.sum(df, axis=1, keepdims=True)

        _, db = lax.fori_loop(0, nb, step, (jnp.zeros((nh, 1), F32), jnp.zeros((nh, 1), F32)))
        db_ref[...] = jnp.broadcast_to(db, db_ref.shape)

    df, db = pl.pallas_call(
        body,
        name="forget_bwd",
        out_shape=[jax.ShapeDtypeStruct((nb, nh, blk), F32), jax.ShapeDtypeStruct((nh, LANES), F32)],
        compiler_params=pltpu.CompilerParams(vmem_limit_bytes=VMEM_LIMIT),
    )(_to_blocks(dc, nb, blk), _to_blocks(f_t, nb, blk), b_col)
    return _from_blocks(df), db


def _lanes(parts, rows):
    lane = lax.broadcasted_iota(jnp.int32, (rows, LANES), 1)
    out = jnp.zeros((rows, LANES), F32)
    for n, part in enumerate(parts):
        out = jnp.where(lane == n, part, out)
    return out


def _attn_prep_fwd(proj, cs, t, nh, hd, q_off, k_off, v_off):
    tr = _tile(t, ATTN_TILE, 16)
    qb, kb, vb = q_off // hd, k_off // hd, v_off // hd

    def body(q_ref, k_ref, v_ref, cs_ref, qa_ref, ka_ref, va_ref):
        hi, mid, lo = _split3(cs_ref[0][:, :1])
        qa_ref[0, :, :hd] = q_ref[...]
        qa_ref[0, :, hd:] = _lanes([1.0, 1.0, 1.0, hi, mid, lo], tr).astype(BF16)
        ka_ref[0, :, :hd] = k_ref[...]
        ka_ref[0, :, hd:] = _lanes([-hi, -mid, -lo] + [1.0] * 6, tr).astype(BF16)
        va_ref[0, :, :hd] = v_ref[...]
        va_ref[0, :, hd:] = _lanes([-1.0, -1.0, -1.0], tr).astype(BF16)

    wide = pl.BlockSpec((1, tr, 2 * hd), lambda h, i: (h, i, 0))
    return pl.pallas_call(
        body,
        name="attn_prep_fwd",
        grid=(nh, t // tr),
        in_specs=[pl.BlockSpec((tr, hd), lambda h, i: (i, qb + h)), pl.BlockSpec((tr, hd), lambda h, i: (i, kb + h)),
                  pl.BlockSpec((tr, hd), lambda h, i: (i, vb + h)), pl.BlockSpec((1, tr, LANES), lambda h, i: (h, i, 0))],
        out_specs=[wide, wide, wide],
        out_shape=[jax.ShapeDtypeStruct((nh, t, 2 * hd), BF16)] * 3,
        compiler_params=_params("parallel", "parallel"),
    )(proj, proj, proj, cs)


def _attn_prep_bwd(qa, lse, o, do, t, nh, hd):
    tr = _tile(t, ATTN_TILE, 16)
    inv_scale = math.sqrt(hd)

    def body(qa_ref, lse_ref, o_ref, do_ref, qb_ref, da_ref):
        l_hi, l_mid, l_lo = _split3(lse_ref[0][:, :1] * (-inv_scale))
        lane = lax.broadcasted_iota(jnp.int32, (tr, LANES), 1)
        extra = qa_ref[0, :, hd:].astype(F32)
        extra = jnp.where(lane == 6, l_hi, jnp.where(lane == 7, l_mid, jnp.where(lane == 8, l_lo, extra)))
        qb_ref[0, :, :hd] = qa_ref[0, :, :hd]
        qb_ref[0, :, hd:] = extra.astype(BF16)
        dov = do_ref[...]
        delta = jnp.sum(dov.astype(F32) * o_ref[...].astype(F32), axis=1, keepdims=True)
        da_ref[0, :, :hd] = dov
        da_ref[0, :, hd:] = _lanes(list(_split3(delta)), tr).astype(BF16)

    wide = pl.BlockSpec((1, tr, 2 * hd), lambda h, i: (h, i, 0))
    head = pl.BlockSpec((tr, hd), lambda h, i: (i, h))
    return pl.pallas_call(
        body,
        name="attn_prep_bwd",
        grid=(nh, t // tr),
        in_specs=[wide, pl.BlockSpec((1, tr, LANES), lambda h, i: (h, i, 0)), head, head],
        out_specs=[wide, wide],
        out_shape=[jax.ShapeDtypeStruct((nh, t, 2 * hd), BF16)] * 2,
        compiler_params=_params("parallel", "parallel"),
    )(qa, lse, o, do)


def _causal(s, row0, rows, cols):
    row = lax.broadcasted_iota(jnp.int32, (rows, cols), 0) + row0
    col = lax.broadcasted_iota(jnp.int32, (rows, cols), 1)
    return jnp.where(col <= row, s, NEG_INF)


def _side_refs(side):
    if side is None:
        return [], [], []
    return list(side.ins), list(side.out_shapes), [pltpu.SemaphoreType.DMA((cnt,)) for cnt in side.sem_counts]


def _attn_fwd(qa, ka, proj, t, nh, hd, v_off, side=None):
    tq = _tile(t, ATTN_TILE)
    nq = t // tq
    sub = _tile(tq, ATTN_SUB)
    ns = tq // sub
    scale = 1.0 / math.sqrt(hd)
    vb = v_off // hd
    s_ins, s_outs, s_sems = _side_refs(side)

    def body(qa_ref, ka_ref, v_ref, *rest):
        si_refs, (o_ref, lse_ref) = rest[:len(s_ins)], rest[len(s_ins):len(s_ins) + 2]
        so_refs, sem_refs = rest[len(s_ins) + 2:len(s_ins) + 2 + len(s_outs)], rest[len(s_ins) + 2 + len(s_outs):]
        h, i = pl.program_id(0), pl.program_id(1)
        if side:
            @pl.when((h == 0) & (i == 0))
            def _():
                side.start(si_refs, so_refs, sem_refs)

        def tile(j, carry, masked):
            rows = pl.ds(pl.multiple_of(j * tq, tq), tq)
            kj = ka_ref[0, rows, :]
            vj = v_ref[rows, :]
            new = []
            for r in range(ns):
                m, l, acc = carry[r]
                s = lax.dot_general(qa_ref[0, pl.ds(r * sub, sub), :], kj, _DIMS["nt"], preferred_element_type=F32)
                if masked:
                    s = _causal(s, r * sub, sub, tq)
                m_new = jnp.maximum(m, jnp.max(s, axis=1, keepdims=True))
                p = jnp.exp2((s - m_new) * (scale * LOG2E))
                alpha = jnp.exp2((m - m_new) * (scale * LOG2E))
                l = alpha * l + jnp.sum(p, axis=1, keepdims=True)
                acc = alpha * acc + jnp.dot(p.astype(BF16), vj, preferred_element_type=F32)
                new.append((m_new, l, acc))
            return tuple(new)

        init = tuple((jnp.full((sub, 1), NEG_INF, F32), jnp.zeros((sub, 1), F32), jnp.zeros((sub, hd), F32))
                     for _ in range(ns))
        carry = lax.fori_loop(0, i, lambda j, cr: tile(j, cr, False), init)
        carry = tile(i, carry, True)
        for r in range(ns):
            m, l, acc = carry[r]
            o_ref[pl.ds(r * sub, sub), :] = (acc / l).astype(BF16)
            lse_ref[0, pl.ds(r * sub, sub), :] = jnp.broadcast_to(m * scale + jnp.log(l), (sub, LANES))
        if side:
            @pl.when((h == nh - 1) & (i == nq - 1))
            def _():
                side.finish(si_refs, so_refs, sem_refs)

    return pl.pallas_call(
        body,
        name="attn_fwd",
        grid=(nh, nq),
        in_specs=[pl.BlockSpec((1, tq, 2 * hd), lambda h, i: (h, i, 0)),
                  pl.BlockSpec((1, t, 2 * hd), lambda h, i: (h, 0, 0)),
                  pl.BlockSpec((t, hd), lambda h, i: (0, vb + h))] + [ANY] * len(s_ins),
        out_specs=[pl.BlockSpec((tq, hd), lambda h, i: (i, h)),
                   pl.BlockSpec((1, tq, LANES), lambda h, i: (h, i, 0))] + [ANY] * len(s_outs),
        out_shape=[jax.ShapeDtypeStruct((t, nh * hd), BF16), jax.ShapeDtypeStruct((nh, t, LANES), F32)] + s_outs,
        scratch_shapes=s_sems,
        compiler_params=_params(*(("arbitrary",) * 2 if side else ("parallel",) * 2)),
    )(qa, ka, proj, *s_ins)


def _attn_bwd(qb, ka, va, da, t, nh, hd, side=None):
    tq = _tile(t, ATTN_TILE)
    nq = t // tq
    sub = _tile(tq, ATTN_SUB)
    ns = tq // sub
    scale = 1.0 / math.sqrt(hd)
    s_ins, s_outs, s_sems = _side_refs(side)

    def body(qb_ref, ka_ref, va_ref, da_ref, *rest):
        si_refs, rest = rest[:len(s_ins)], rest[len(s_ins):]
        (dq_ref, dk_ref, dv_ref, dck_ref, dcq_ref), rest = rest[:5], rest[5:]
        so_refs, (dq_all, *sem_refs) = rest[:len(s_outs)], rest[len(s_outs):]
        h, j = pl.program_id(0), pl.program_id(1)
        if side:
            @pl.when((h == 0) & (j == 0))
            def _():
                side.start(si_refs, so_refs, sem_refs)

        @pl.when(j == 0)
        def _():
            dq_all[...] = jnp.zeros(dq_all.shape, F32)
            dcq_ref[...] = jnp.zeros(dcq_ref.shape, F32)

        kaj = ka_ref[0]
        vaj = va_ref[0]
        kj = kaj[:, :hd]

        def tile(i, carry, masked):
            dk, dv, dck = carry
            for r in range(ns):
                rows = pl.ds(pl.multiple_of(i * tq + r * sub, sub), sub)
                qr = qb_ref[0, rows, :]
                dr = da_ref[0, rows, :]
                s = lax.dot_general(qr, kaj, _DIMS["nt"], preferred_element_type=F32)
                if masked:
                    s = _causal(s, r * sub, sub, tq)
                p = jnp.exp2(s * (scale * LOG2E))
                ds = p * lax.dot_general(dr, vaj, _DIMS["nt"], preferred_element_type=F32)
                dsb = ds.astype(BF16)
                dv = dv + lax.dot_general(p.astype(BF16), dr[:, :hd], _DIMS["tn"], preferred_element_type=F32)
                dk = dk + lax.dot_general(dsb, qr[:, :hd], _DIMS["tn"], preferred_element_type=F32)
                dck = dck - jnp.sum(ds, axis=0, keepdims=True)
                dq_all[rows, :] += jnp.dot(dsb, kj, preferred_element_type=F32)
                dcq_ref[0, rows, :] += jnp.sum(ds, axis=1, keepdims=True)
            return dk, dv, dck

        carry = (jnp.zeros((tq, hd), F32), jnp.zeros((tq, hd), F32), jnp.zeros((1, tq), F32))
        carry = tile(j, carry, True)
        dk, dv, dck = lax.fori_loop(j + 1, nq, lambda i, cr: tile(i, cr, False), carry)
        dk_ref[...] = (dk * scale).astype(BF16)
        dv_ref[...] = dv.astype(BF16)
        dck_ref[0] = dck

        @pl.when(j == nq - 1)
        def _():
            dq_ref[...] = (dq_all[...] * scale).astype(BF16)

        if side:
            @pl.when((h == nh - 1) & (j == nq - 1))
            def _():
                side.finish(si_refs, so_refs, sem_refs)

    whole = pl.BlockSpec((1, t, 2 * hd), lambda h, j: (h, 0, 0))
    block = pl.BlockSpec((1, tq, 2 * hd), lambda h, j: (h, j, 0))
    return pl.pallas_call(
        body,
        name="attn_bwd",
        grid=(nh, nq),
        in_specs=[whole, block, block, whole] + [ANY] * len(s_ins),
        out_specs=[
            pl.BlockSpec((t, hd), lambda h, j: (0, h)),
            pl.BlockSpec((tq, hd), lambda h, j: (j, h)),
            pl.BlockSpec((tq, hd), lambda h, j: (j, h)),
            pl.BlockSpec((1, 1, tq), lambda h, j: (h, 0, j)),
            pl.BlockSpec((1, t, LANES), lambda h, j: (h, 0, 0)),
        ] + [ANY] * len(s_outs),
        out_shape=[jax.ShapeDtypeStruct((t, nh * hd), BF16), jax.ShapeDtypeStruct((t, nh * hd), BF16),
                   jax.ShapeDtypeStruct((t, nh * hd), BF16), jax.ShapeDtypeStruct((nh, 1, t), F32),
                   jax.ShapeDtypeStruct((nh, t, LANES), F32)] + s_outs,
        scratch_shapes=[pltpu.VMEM((t, hd), F32)] + s_sems,
        compiler_params=_params("arbitrary", "arbitrary"),
    )(qb, ka, va, da, *s_ins)


class _Dims:
    def __init__(self, t, d, c, nh, aw, f, pd):
        self.t, self.d, self.c, self.nh, self.aw, self.f, self.pd = t, d, c, nh, aw, f, pd
        self.hd = aw // nh
        self.a_off, self.g_off = 0, c
        self.q_off, self.k_off, self.v_off = 2 * c, 2 * c + aw, 2 * c + 2 * aw
        self.gc_off = 2 * c + 3 * aw
        self.ga_off = self.gc_off + d
        self.n_main = self.ga_off + d


def _ffn_tn(f):
    return _tile(f, 1536)


class _Riders:
    def __init__(self, plan=None):
        self.plan = plan or {}

    def host(self, name, n_out, call, ctx=None):
        if name not in self.plan:
            return call()
        make_side, take = self.plan[name]
        outs = call(side=make_side(ctx))
        take(outs[n_out:])
        return outs[:n_out]


class _Weights:
    def __init__(self, fixed, gathered, views):
        self.fixed, self.gathered, self.views = fixed, gathered, views

    def __getitem__(self, key):
        if key in self.fixed:
            return self.fixed[key]
        name, off = self.views[key]
        return self.gathered[name], off


def _layer_fwd(dm, x, p, w, riders):
    t, d, c, f = dm.t, dm.d, dm.c, dm.f
    s = {"x": x}
    h = _rms_fwd("rms_mix", x, w["n_mix"], t, d)
    proj, = riders.host("mm_in", 1, functools.partial(
        _matmul, "mm_in", "nn", [h], [w["main"]], [(0, 0, 0)], 1, t, dm.n_main, d, [BF16]))
    fl = _mm("mm_forget", "nn", h, w["f"], t, LANES, d, F32)
    u2, cpre = _conv_fwd(proj, w["conv_w"], w["conv_b"], w["ln_g"], w["ln_b"], t, c, dm.a_off, dm.g_off)
    f_t = fl[:, :dm.nh].T
    cum = _forget_fwd(f_t, w["b_f"], dm.nh, t)
    cs = jnp.broadcast_to((cum * math.sqrt(dm.hd))[:, :, None], (dm.nh, t, LANES))
    qa, ka, va = _attn_prep_fwd(proj, cs, t, dm.nh, dm.hd, dm.q_off, dm.k_off, dm.v_off)
    o, lse = riders.host("attn_fwd", 2, functools.partial(_attn_fwd, qa, ka, proj, t, dm.nh, dm.hd, dm.v_off))

    def epi_conv(accs, ex):
        return accs[0], _sig(ex[0].astype(F32)) * accs[0]

    yc, m1 = _matmul("mm_conv_out", "nn", [u2], [w["co"]], [(0, 0, 0)], 1, t, d, c, [BF16, BF16],
                     epi=epi_conv, extras=[("mn", (proj, dm.gc_off))], tm=512)

    def epi_attn(accs, ex):
        return accs[0], ex[1].astype(F32) + _sig(ex[0].astype(F32)) * accs[0]

    ya, merged = _matmul("mm_attn_out", "nn", [o], [w["ao"]], [(0, 0, 0)], 1, t, d, dm.aw, [BF16, BF16],
                         epi=epi_attn, extras=[("mn", (proj, dm.ga_off)), ("mn", m1)], tm=512)
    x1 = _matmul("mm_out", "nn", [merged], [w["o"]], [(0, 0, 0)], 1, t, d, d, [F32],
                 epi=lambda accs, ex: [ex[0] + accs[0]], extras=[("mn", x)], tm=512)[0]

    hf = _rms_fwd("rms_ffn", x1, w["n_ffn"], t, d)

    def epi_glu(accs, ex):
        gate, up = accs
        return gate, up, gate * _sig(gate) * up

    gate, up, act = riders.host("mm_gate_up", 3, functools.partial(
        _matmul, "mm_gate_up", "nn", [hf], [w["g"], w["u"]], [(0, 0, 0), (0, 1, 1)], 2, t, f, d, [BF16, BF16, BF16],
        epi=epi_glu, tm=512, tn=_ffn_tn(f), tk=_tile(d, 1024)))
    x2, = riders.host("mm_down", 1, functools.partial(
        _matmul, "mm_down", "nn", [act], [w["d"]], [(0, 0, 0)], 1, t, d, f, [F32],
        epi=lambda accs, ex: [ex[0] + accs[0]], extras=[("mn", x1)], tm=512))

    hp = _rms_fwd("rms_ple", x2, w["n_ple"], t, d)
    pp = _mm("mm_ple_proj", "nn", p, w["pp"], t, d, dm.pd, BF16, tm=512)

    def epi_ple(accs, ex):
        sg = _sig(accs[0])
        return sg, ex[1] + sg * ex[0].astype(F32)

    sg, x3 = _matmul("mm_ple_gate", "nn", [hp], [w["pg"]], [(0, 0, 0)], 1, t, d, d, [BF16, F32],
                     epi=epi_ple, extras=[("mn", pp), ("mn", x2)], tm=512)
    s.update(h=h, proj=proj, f_t=f_t, qa=qa, ka=ka, va=va, u2=u2, cpre=cpre, o=o, lse=lse, yc=yc, ya=ya,
             merged=merged, x1=x1, hf=hf, gate=gate, up=up, act=act, x2=x2, hp=hp, pp=pp, sg=sg, p=p)
    return x3, s


def _layer_bwd(dm, dx3, dx3b, s, w, riders):
    t, d, c, f = dm.t, dm.d, dm.c, dm.f
    g = {}

    def ple_ew(dxv, sgv, ppv):
        sgf, ppf = sgv.astype(F32), ppv.astype(F32)
        return dxv * sgf, dxv * ppf * sgf * (1.0 - sgf)

    d_pp, d_z = _ew("ple_bwd", ple_ew, [dx3, s["sg"], s["pp"]], t, d, [BF16, BF16], _tile(t, ROW_TILE, 8))
    g["pp"] = _mm("gw_ple_proj", "tn", s["p"], d_pp, dm.pd, d, t, F32)
    g["pg"] = _mm("gw_ple_gate", "tn", s["hp"], d_z, d, d, t, F32)
    d_hp = _mm("dx_ple_gate", "nt", d_z, w["pg"], t, d, d, BF16)
    dx2, dx2b, g["n_ple"] = _rms_bwd("rms_ple_bwd", d_hp, s["x2"], w["n_ple"], dx3, t, d)

    def epi_dglu(accs, ex):
        gate, up = ex[0].astype(F32), ex[1].astype(F32)
        sg = _sig(gate)
        return accs[0] * up * (sg * (1.0 + gate * (1.0 - sg))), accs[0] * gate * sg

    d_gate, d_up = riders.host("dx_down", 2, functools.partial(
        _matmul, "dx_down", "nt", [dx2b], [w["d"]], [(0, 0, 0)], 1, t, f, d, [BF16, BF16], epi=epi_dglu,
        extras=[("mn", s["gate"]), ("mn", s["up"])], tm=512, tn=_ffn_tn(f), tk=_tile(d, 1024)))
    g["d"] = _mm("gw_down", "tn", s["act"], dx2b, f, d, t, F32, tm=_ffn_tn(f))
    g["g"] = _mm("gw_gate", "tn", s["hf"], d_gate, d, f, t, F32, tn=_ffn_tn(f))
    g["u"] = _mm("gw_up", "tn", s["hf"], d_up, d, f, t, F32, tn=_ffn_tn(f))
    d_hf, = riders.host("dx_gate_up", 1, functools.partial(
        _matmul, "dx_gate_up", "nt", [d_gate, d_up], [w["g"], w["u"]], [(0, 0, 0), (1, 1, 0)], 1, t, d, f, [BF16]))
    dx1, dx1b, g["n_ffn"] = _rms_bwd("rms_ffn_bwd", d_hf, s["x1"], w["n_ffn"], dx2, t, d)

    g["o"] = _mm("gw_out", "tn", s["merged"], dx1b, d, d, t, F32)

    def epi_dmerge(accs, ex):
        dmv = accs[0]
        sgc, sga = _sig(ex[0].astype(F32)), _sig(ex[1].astype(F32))
        ycv, yav = ex[2].astype(F32), ex[3].astype(F32)
        return dmv * sgc, dmv * sga, dmv * ycv * sgc * (1.0 - sgc), dmv * yav * sga * (1.0 - sga)

    d_yc, d_ya, d_gc, d_ga = _matmul(
        "dx_out", "nt", [dx1b], [w["o"]], [(0, 0, 0)], 1, t, d, d, [BF16] * 4, epi=epi_dmerge,
        extras=[("mn", (s["proj"], dm.gc_off)), ("mn", (s["proj"], dm.ga_off)), ("mn", s["yc"]), ("mn", s["ya"])],
        tm=512, tn=_tile(d, 512))
    g["co"] = _mm("gw_conv_out", "tn", s["u2"], d_yc, c, d, t, F32)
    d_u2 = _mm("dx_conv_out", "nt", d_yc, w["co"], t, c, d, BF16)
    g["ao"] = _mm("gw_attn_out", "tn", s["o"], d_ya, dm.aw, d, t, F32)
    d_o = _mm("dx_attn_out", "nt", d_ya, w["ao"], t, dm.aw, d, BF16)

    qb, da = _attn_prep_bwd(s["qa"], s["lse"], s["o"], d_o, t, dm.nh, dm.hd)
    dq, dk, dv, dck, dcq = riders.host(
        "attn_bwd", 5, functools.partial(_attn_bwd, qb, s["ka"], s["va"], da, t, dm.nh, dm.hd), ctx=g)
    d_ft, g_bf = _forget_bwd(dck.reshape(dm.nh, t) + dcq[:, :, 0], s["f_t"], w["b_f"], dm.nh, t)
    g["b_f"] = g_bf[:, 0]
    d_f = jnp.pad(d_ft.T, ((0, 0), (0, LANES - dm.nh))).astype(BF16)

    d_a, d_gg, g["conv_w"], g["conv_b"], g["ln_g"], g["ln_b"] = _conv_bwd(
        s["proj"], s["cpre"], d_u2, w["conv_w"], w["ln_g"], w["ln_b"], t, c, dm.a_off, dm.g_off)

    d_proj = jnp.concatenate([d_a, d_gg, dq, dk, dv, d_gc, d_ga], axis=1)
    g["main"], = riders.host("gw_in", 1, functools.partial(
        _matmul, "gw_in", "tn", [s["h"]], [d_proj], [(0, 0, 0)], 1, d, dm.n_main, t, [F32]))
    g["f"] = _mm("gw_forget", "tn", s["h"], d_f, d, LANES, t, F32)
    d_h_f = _mm("dx_forget", "nt", d_f, w["f"], t, d, LANES, BF16)
    d_h, = riders.host("dx_in", 1, functools.partial(
        _matmul, "dx_in", "nt", [d_proj], [w["main"]], [(0, 0, 0)], 1, t, d, dm.n_main, [BF16],
        epi=lambda accs, ex: [accs[0] + ex[0].astype(F32)], extras=[("mn", d_h_f)]))
    dx, dxb, g["n_mix"] = _rms_bwd("rms_mix_bwd", d_h, s["x"], w["n_mix"], dx1, t, d)
    return dx, dxb, g


def _adamw_tiles(wv, gv, mv, vv):
    m_new = ADAM_B1 * mv + (1.0 - ADAM_B1) * gv
    v_new = ADAM_B2 * vv + (1.0 - ADAM_B2) * (gv * gv)
    m_hat = m_new / (1.0 - ADAM_B1 ** ADAM_STEP)
    v_hat = v_new / (1.0 - ADAM_B2 ** ADAM_STEP)
    delta = -ADAM_LR * (m_hat / (jnp.sqrt(v_hat) + ADAM_EPS) + ADAM_WD * wv)
    return delta, m_new, v_new


def _adamw(name, wv, gv, mv, vv):
    shape = wv.shape
    cols = shape[-1]
    rows = wv.size // cols
    tm = _tile(rows, max(8, (1 << 18) // cols), 8)
    flat = [a.reshape(rows, cols) for a in (wv, gv, mv, vv)]
    outs = _ew(name, _adamw_tiles, flat, rows, cols, [F32, F32, F32], tm)
    return [o.reshape(shape) for o in outs]


def _place():
    return lax.axis_index("x"), lax.axis_index("y"), lax.axis_index("c")


def _other_chips(x, y):
    return [(1 - x, y), (x, 1 - y), (1 - x, 1 - y)]


def _window(ref, kind, chip, size):
    if kind == "chip":
        return ref.at[chip]
    if kind == "row":
        return ref.at[pl.ds(chip * size, size), :]
    return ref.at[:, pl.ds(pl.multiple_of(chip * size, LANES), size)]


class _Side:
    def __init__(self, ins, out_shapes, sem_counts, start, finish):
        self.ins, self.out_shapes, self.sem_counts, self.start, self.finish = ins, out_shapes, sem_counts, start, finish

    def join(self, other):
        ni, no, ns = len(self.ins), len(self.out_shapes), len(self.sem_counts)

        def both(first, second):
            def run(ins, outs, sems):
                first(ins[:ni], outs[:no], sems[:ns])
                second(ins[ni:], outs[no:], sems[ns:])
            return run

        return _Side(self.ins + other.ins, self.out_shapes + other.out_shapes, self.sem_counts + other.sem_counts,
                     both(self.start, other.start), both(self.finish, other.finish))


def _run_side(name, side):
    n_in, n_out = len(side.ins), len(side.out_shapes)

    def body(*refs):
        ins, outs, sems = refs[:n_in], refs[n_in:n_in + n_out], refs[n_in + n_out:]
        side.start(ins, outs, sems)
        side.finish(ins, outs, sems)

    return pl.pallas_call(
        body,
        name=name,
        in_specs=[ANY] * n_in,
        out_specs=[ANY] * n_out,
        out_shape=list(side.out_shapes),
        scratch_shapes=[pltpu.SemaphoreType.DMA((cnt,)) for cnt in side.sem_counts],
    )(*side.ins)


def _full_shape(shard, kind):
    _, a, b = shard.shape
    if kind == "chip":
        return (N_CHIPS, a, b)
    return (N_CHIPS * a, b) if kind == "row" else (a, N_CHIPS * b)


def _gather_side(shards, kinds, layer):
    n = len(shards)
    sizes = [s.shape[1] if k == "row" else s.shape[2] for s, k in zip(shards, kinds)]

    def copies(ins, outs, sems, forwards):
        ici_send, ici_recv, d2d_send, d2d_recv, own_send, own_recv = sems
        x, y, c = _place()
        me = 2 * x + y
        sibling = (x, y, 1 - c)
        own, ici, landed, fwd = [], [], [], []
        for w in range(n):
            mine = _window(outs[w], kinds[w], me, sizes[w])
            own.append(pltpu.make_async_remote_copy(
                src_ref=ins[w].at[layer], dst_ref=mine, send_sem=own_send.at[w], recv_sem=own_recv.at[w],
                device_id=sibling, device_id_type=MESH))
            for r, (px, py) in enumerate(_other_chips(x, y)):
                ici.append(pltpu.make_async_remote_copy(
                    src_ref=ins[w].at[layer], dst_ref=mine, send_sem=ici_send.at[3 * w + r],
                    recv_sem=ici_recv.at[3 * w + r], device_id=(px, py, layer), device_id_type=MESH))
                if forwards:
                    slab = _window(outs[w], kinds[w], 2 * px + py, sizes[w])
                    landed.append(pltpu.make_async_remote_copy(
                        src_ref=slab, dst_ref=slab, send_sem=ici_send.at[3 * w + r], recv_sem=ici_recv.at[3 * w + r],
                        device_id=(px, py, layer), device_id_type=MESH))
                    fwd.append(pltpu.make_async_remote_copy(
                        src_ref=slab, dst_ref=slab, send_sem=d2d_send.at[3 * w + r], recv_sem=d2d_recv.at[3 * w + r],
                        device_id=sibling, device_id_type=MESH))
        return c, own, ici, landed, fwd

    def start(ins, outs, sems):
        c, own, ici, _, _ = copies(ins, outs, sems, False)
        for cp in own:
            cp.start()

        @pl.when(c == layer)
        def _():
            for cp in ici:
                cp.start()

    def finish(ins, outs, sems):
        c, own, ici, landed, fwd = copies(ins, outs, sems, True)

        @pl.when(c == layer)
        def _():
            for got, cp in zip(landed, fwd):
                got.wait_recv()
                cp.start()
            for cp in ici + fwd:
                cp.wait_send()

        @pl.when(c != layer)
        def _():
            for cp in fwd:
                cp.wait_recv()

        for cp in own:
            cp.wait()

    out_shapes = [jax.ShapeDtypeStruct(_full_shape(s, k), s.dtype) for s, k in zip(shards, kinds)]
    return _Side(list(shards), out_shapes, [3 * n] * 4 + [n] * 2, start, finish)


def _swap_side(grads, layer):
    n = len(grads)

    def copies(ins, outs, sems):
        x, y, c = _place()
        return c, [pltpu.make_async_remote_copy(src_ref=ins[w], dst_ref=outs[w], send_sem=sems[0].at[w],
                                                recv_sem=sems[1].at[w], device_id=(x, y, layer), device_id_type=MESH)
                   for w in range(n)]

    def start(ins, outs, sems):
        c, cps = copies(ins, outs, sems)

        @pl.when(c != layer)
        def _():
            for cp in cps:
                cp.start()

    def finish(ins, outs, sems):
        c, cps = copies(ins, outs, sems)

        @pl.when(c != layer)
        def _():
            for cp in cps:
                cp.wait_send()

        @pl.when(c == layer)
        def _():
            for cp in cps:
                cp.wait_recv()

    return _Side(list(grads), [jax.ShapeDtypeStruct(a.shape, a.dtype) for a in grads], [n, n], start, finish)


def _add_core_partials(name, mine, got):
    shape = got.shape
    cols = shape[-1]
    rows = got.size // cols
    tm, tn = _tile(rows, 256, 8), _tile(cols, 2048)
    return _ew(name, lambda a, b: [a + b], [mine.reshape(rows, cols), got.reshape(rows, cols)], rows, cols,
               [BF16], tm, tn)[0].reshape(shape)


def _exchange_side(parts, kinds, sizes, layer):
    n = len(parts)

    def copies(ins, outs, sems):
        x, y, c = _place()
        cps = []
        for w in range(n):
            for r, (px, py) in enumerate(_other_chips(x, y)):
                cps.append(pltpu.make_async_remote_copy(
                    src_ref=_window(ins[w], kinds[w], 2 * px + py, sizes[w]), dst_ref=outs[w].at[r],
                    send_sem=sems[0].at[3 * w + r], recv_sem=sems[1].at[3 * w + r],
                    device_id=(px, py, layer), device_id_type=MESH))
        return c, cps

    def start(ins, outs, sems):
        c, cps = copies(ins, outs, sems)

        @pl.when(c == layer)
        def _():
            for cp in cps:
                cp.start()

    def finish(ins, outs, sems):
        c, cps = copies(ins, outs, sems)

        @pl.when(c == layer)
        def _():
            for cp in cps:
                cp.wait()

    out_shapes = [jax.ShapeDtypeStruct((3,) + tuple(_shard_shape(p, k, s)), p.dtype)
                  for p, k, s in zip(parts, kinds, sizes)]
    return _Side(list(parts), out_shapes, [3 * n, 3 * n], start, finish)


def _shard_shape(whole, kind, size):
    if kind == "chip":
        return whole.shape[1:]
    return (size, whole.shape[1]) if kind == "row" else (whole.shape[0], size)


def _sum_chip_partials(name, part, got, kind, size, chip, layer, both=None):
    rows, cols = _shard_shape(part, kind, size)
    tm = _tile(rows, max(8, (1 << 19) // cols), 16)

    def body(chip_ref, part_ref, g0_ref, g1_ref, g2_ref, *rest):
        total = part_ref[...].astype(F32)
        for ref in (g0_ref, g1_ref, g2_ref):
            total = total + ref[...].astype(F32)
        rest[-1][...] = total

    if kind == "chip":
        mine = pl.BlockSpec((None, tm, cols), lambda i, chip_ref: (chip_ref[0], i, 0))
    elif kind == "row":
        mine = pl.BlockSpec((tm, cols), lambda i, chip_ref: (chip_ref[0] * (rows // tm) + i, 0))
    else:
        mine = pl.BlockSpec((tm, cols), lambda i, chip_ref: (i, chip_ref[0]))
    theirs = [pl.BlockSpec((None, tm, cols), functools.partial(lambda r, i, chip_ref: (r, i, 0), r))
              for r in range(3)]
    kept = [] if both is None else [both]
    return pl.pallas_call(
        body,
        name=name,
        grid_spec=pltpu.PrefetchScalarGridSpec(
            num_scalar_prefetch=1, grid=(rows // tm,), in_specs=[mine] + theirs + [ANY] * len(kept),
            out_specs=pl.BlockSpec((None, tm, cols), lambda i, chip_ref: (layer, i, 0))),
        out_shape=jax.ShapeDtypeStruct((2, rows, cols), F32),
        input_output_aliases={5: 0} if kept else {},
        compiler_params=_params("parallel"),
    )(chip.reshape(1), part, got, got, got, *kept)


def _share_reduced(both):
    n = len(both)

    def body(*refs):
        ins, outs = refs[:n], refs[n:2 * n]
        send_sems, recv_sems = refs[2 * n:]
        x, y, c = _place()
        sends = []
        for w in range(n):
            cp = pltpu.make_async_remote_copy(src_ref=ins[w].at[c], dst_ref=outs[w].at[c], send_sem=send_sems.at[w],
                                              recv_sem=recv_sems.at[w], device_id=(x, y, 1 - c), device_id_type=MESH)
            cp.start()
            sends.append(cp)
        for w in range(n):
            got = outs[w].at[1 - c]
            pltpu.make_async_remote_copy(src_ref=got, dst_ref=got, send_sem=send_sems.at[w], recv_sem=recv_sems.at[w],
                                         device_id=(x, y, 1 - c), device_id_type=MESH).wait_recv()
        for cp in sends:
            cp.wait_send()

    return pl.pallas_call(
        body,
        name="share_reduced",
        in_specs=[ANY] * n,
        out_specs=[ANY] * n,
        out_shape=[jax.ShapeDtypeStruct(a.shape, a.dtype) for a in both],
        input_output_aliases={w: w for w in range(n)},
        scratch_shapes=[pltpu.SemaphoreType.DMA((n,)), pltpu.SemaphoreType.DMA((n,))],
    )(*both)


def _allreduce_small(v):
    rows, width = v.shape

    def body(v_ref, out_ref, slots, send_sems, recv_sems):
        x, y, c = _place()
        me = 4 * x + 2 * y + c
        slots[me] = v_ref[...]
        peers = [(x, y, 1 - c)]
        for px, py in _other_chips(x, y):
            peers += [(px, py, c), (px, py, 1 - c)]
        sends = []
        for r, peer in enumerate(peers):
            cp = pltpu.make_async_remote_copy(
                src_ref=v_ref, dst_ref=slots.at[me], send_sem=send_sems.at[r], recv_sem=recv_sems.at[r],
                device_id=peer, device_id_type=MESH)
            cp.start()
            sends.append(cp)
        for r, (px, py, pc) in enumerate(peers):
            got = slots.at[4 * px + 2 * py + pc]
            pltpu.make_async_remote_copy(
                src_ref=got, dst_ref=got, send_sem=send_sems.at[r], recv_sem=recv_sems.at[r],
                device_id=(px, py, pc), device_id_type=MESH).wait_recv()
        for cp in sends:
            cp.wait_send()
        total = slots[0]
        for n in range(1, 8):
            total = total + slots[n]
        out_ref[...] = total

    vm = pl.BlockSpec(memory_space=pltpu.VMEM)
    return pl.pallas_call(
        body,
        name="allreduce_small",
        in_specs=[vm],
        out_specs=vm,
        out_shape=jax.ShapeDtypeStruct((rows, width), F32),
        scratch_shapes=[pltpu.VMEM((8, rows, width), F32), pltpu.SemaphoreType.DMA((7,)),
                        pltpu.SemaphoreType.DMA((7,))],
    )(v)


def _pad_rows(flat, row_align):
    n = flat.shape[0]
    rows = -(-n // PACK_W)
    rows = -(-rows // row_align) * row_align
    return jnp.pad(flat, (0, rows * PACK_W - n)).reshape(rows, PACK_W)


def _unpack(buf, shapes):
    flat = buf.reshape(-1)
    out, off = [], 0
    for shp in shapes:
        n = math.prod(shp)
        out.append(flat[off:off + n].reshape(shp))
        off += n
    return out


BIG = (("w_in", "chip"), ("w_conv_out", "col"), ("w_attn_out", "col"), ("w_out", "row"), ("w_gate_up", "col"),
       ("w_down", "row"), ("w_ple_gate", "row"), ("w_ple_proj", "col"))


def kernel(x, p, norm_mix_g, w_in, b_forget, conv_w, conv_b, conv_ln_g, conv_ln_b, w_conv_out, w_attn_out, w_out, norm_ffn_g, w_gate_up, w_down, norm_ple_g, w_ple_gate, w_ple_proj, final_g, loss_target, m_norm_mix_g, m_w_in, m_b_forget, m_conv_w, m_conv_b, m_conv_ln_g, m_conv_ln_b, m_w_conv_out, m_w_attn_out, m_w_out, m_norm_ffn_g, m_w_gate_up, m_w_down, m_norm_ple_g, m_w_ple_gate, m_w_ple_proj, m_final_g, v_norm_mix_g, v_w_in, v_b_forget, v_conv_w, v_conv_b, v_conv_ln_g, v_conv_ln_b, v_w_conv_out, v_w_attn_out, v_w_out, v_norm_ffn_g, v_w_gate_up, v_w_down, v_norm_ple_g, v_w_ple_gate, v_w_ple_proj, v_final_g):
    wts = dict(norm_mix_g=norm_mix_g, w_in=w_in, b_forget=b_forget, conv_w=conv_w, conv_b=conv_b,
               conv_ln_g=conv_ln_g, conv_ln_b=conv_ln_b, w_conv_out=w_conv_out, w_attn_out=w_attn_out,
               w_out=w_out, norm_ffn_g=norm_ffn_g, w_gate_up=w_gate_up, w_down=w_down, norm_ple_g=norm_ple_g,
               w_ple_gate=w_ple_gate, w_ple_proj=w_ple_proj, final_g=final_g)
    mom1 = dict(norm_mix_g=m_norm_mix_g, w_in=m_w_in, b_forget=m_b_forget, conv_w=m_conv_w, conv_b=m_conv_b,
                conv_ln_g=m_conv_ln_g, conv_ln_b=m_conv_ln_b, w_conv_out=m_w_conv_out, w_attn_out=m_w_attn_out,
                w_out=m_w_out, norm_ffn_g=m_norm_ffn_g, w_gate_up=m_w_gate_up, w_down=m_w_down,
                norm_ple_g=m_norm_ple_g, w_ple_gate=m_w_ple_gate, w_ple_proj=m_w_ple_proj, final_g=m_final_g)
    mom2 = dict(norm_mix_g=v_norm_mix_g, w_in=v_w_in, b_forget=v_b_forget, conv_w=v_conv_w, conv_b=v_conv_b,
                conv_ln_g=v_conv_ln_g, conv_ln_b=v_conv_ln_b, w_conv_out=v_w_conv_out, w_attn_out=v_w_attn_out,
                w_out=v_w_out, norm_ffn_g=v_norm_ffn_g, w_gate_up=v_w_gate_up, w_down=v_w_down,
                norm_ple_g=v_norm_ple_g, w_ple_gate=v_w_ple_gate, w_ple_proj=v_w_ple_proj, final_g=v_final_g)
    order = list(wts)
    depth = w_in.shape[0]
    assert depth == 2, "the exchanges give one layer to each of a chip's two cores"
    t, d = x.shape[1], x.shape[2]
    c = conv_ln_g.shape[1]
    nh = b_forget.shape[1]
    aw = w_attn_out.shape[1]
    f = N_CHIPS * w_down.shape[1]
    pd = w_ple_proj.shape[1]
    dm = _Dims(t, d, c, nh, aw, f, pd)
    n_split = 2 * c + 3 * aw
    cw = conv_w.shape[2]
    chip = 2 * lax.axis_index("x") + lax.axis_index("y")
    big_names = [name for name, _ in BIG]
    big_kinds = [kind for _, kind in BIG]

    shards = {name: wts[name].astype(BF16) for name in big_names}
    shards["conv_w"] = conv_w
    kind_of = dict(BIG, conv_w="chip")
    full = [{}, {}]

    def gather_rider(names, layer):
        def make_side(ctx=None):
            return _gather_side([shards[n] for n in names], [kind_of[n] for n in names], layer)

        def take(results):
            full[layer].update(zip(names, results))

        return make_side, take

    make_side, take = gather_rider(["w_in", "conv_w"], 0)
    take(_run_side("gather_layer0_in", make_side()))
    fwd_riders = [_Riders({"attn_fwd": gather_rider(big_names[1:], 0),
                           "mm_in": gather_rider(["w_in", "w_conv_out", "w_out", "conv_w"], 1),
                           "mm_gate_up": gather_rider(["w_gate_up", "w_ple_gate", "w_attn_out"], 1),
                           "mm_down": gather_rider(["w_down", "w_ple_proj"], 1)}),
                  _Riders()]
    views = {"co": ("w_conv_out", 0), "ao": ("w_attn_out", 0), "o": ("w_out", 0), "g": ("w_gate_up", 0),
             "u": ("w_gate_up", f), "d": ("w_down", 0), "pg": ("w_ple_gate", 0), "pp": ("w_ple_proj", 0)}

    def layer_weights(l):
        fw = full[l]
        wi = jnp.concatenate([fw["w_in"][k] for k in range(N_CHIPS)], axis=1)
        cwl = jnp.concatenate([fw["conv_w"][k] for k in range(N_CHIPS)], axis=1)
        return _Weights({
            "main": jnp.concatenate([wi[:, :n_split], wi[:, n_split + nh:]], axis=1),
            "f": jnp.pad(wi[:, n_split:n_split + nh], ((0, 0), (0, LANES - nh))),
            "conv_w": jnp.pad(cwl, ((0, CONV_HALO - CONV_K), (0, 0))),
            "conv_b": conv_b[l][None], "ln_g": conv_ln_g[l][None], "ln_b": conv_ln_b[l][None],
            "b_f": jnp.broadcast_to(b_forget[l][:, None], (nh, LANES)),
            "n_mix": norm_mix_g[l][None], "n_ffn": norm_ffn_g[l][None], "n_ple": norm_ple_g[l][None],
        }, fw, views)

    xl = x[0]
    saved, lw = [], []
    for l in range(depth):
        lw.append(layer_weights(l))
        xl, s = _layer_fwd(dm, xl, p[l, 0], lw[l], fwd_riders[l])
        saved.append(s)
    loss_row, dx, dxb, g_final = _loss_head(xl, final_g[None], loss_target[0], t, d)
    loss = lax.psum(loss_row[0, 0], ("x", "y", "c"))

    sizes = [wts[name].shape[1] if kind == "row" else wts[name].shape[2] for name, kind in BIG]
    everything = list(range(len(BIG)))
    plain = {1: "co", 2: "ao", 3: "o", 5: "d", 6: "pg", 7: "pp"}

    def grad_of(g, w):
        if w == 0:
            w_in_g = jnp.concatenate([g["main"][:, :n_split], g["f"][:, :nh], g["main"][:, n_split:]], axis=1)
            return w_in_g.reshape(d, N_CHIPS, -1).transpose(1, 0, 2)
        return jnp.concatenate([g["g"], g["u"]], axis=1) if w == 4 else g[plain[w]]

    def add_partials(layer, group, mine, got):
        return {w: _add_core_partials(f"add_core_partials_l{layer}_{big_names[w]}", a, b)
                for w, a, b in zip(group, mine, got)}

    def exchange(parts, group, layer):
        return _exchange_side([parts[w] for w in group], [big_kinds[w] for w in group], [sizes[w] for w in group], layer)

    def sum_partials(layer, parts, got, both):
        return [_sum_chip_partials(f"sum_chip_partials_l{layer}_{big_names[w]}", parts[w], got[w], big_kinds[w],
                                   sizes[w], chip, layer, both=None if both is None else both[w])
                for w in everything]

    lg = [None] * depth
    dx, dxb, lg[1] = _layer_bwd(dm, dx, dxb, saved[1], lw[1], _Riders())
    g1 = [grad_of(lg[1], w) for w in everything]
    parts = [{}, {}]
    got = [{}, {}]
    early = everything[1:]

    def exchange_rider(group, layer):
        return (lambda ctx: exchange(parts[layer], group, layer)), (lambda res: got[layer].update(zip(group, res)))

    early_grads = []

    def attn_bwd_side(g):
        early_grads.extend(grad_of(g, w) for w in early)
        return exchange(parts[1], [0, 1, 2, 3], 1).join(_swap_side(early_grads, 0))

    def attn_bwd_take(res):
        got[1].update(zip([0, 1, 2, 3], res[:4]))
        parts[0].update(add_partials(0, early, early_grads, res[4:]))

    bwd_riders = _Riders({
        "dx_down": (lambda ctx: _swap_side(g1, 1), lambda res: parts[1].update(add_partials(1, everything, g1, res))),
        "dx_gate_up": exchange_rider([4, 5, 6, 7], 1),
        "attn_bwd": (attn_bwd_side, attn_bwd_take),
        "gw_in": exchange_rider([4], 0),
        "dx_in": exchange_rider([1, 2, 3, 5, 6, 7], 0)})
    dx, dxb, lg[0] = _layer_bwd(dm, dx, dxb, saved[0], lw[0], bwd_riders)
    both = sum_partials(1, parts[1], got[1], None)

    w_in_g = [grad_of(lg[0], 0)]
    parts[0].update(add_partials(0, [0], w_in_g, _run_side("swap_layer0_in_grads", _swap_side(w_in_g, 0))))
    got[0].update(zip([0], _run_side("exchange_layer0_in_grads", exchange(parts[0], [0], 0))))
    both = sum_partials(0, parts[0], got[0], both)
    grads = dict(zip(big_names, _share_reduced(both)))

    def stacked(key):
        return jnp.stack([lg[l][key] for l in range(depth)])

    small = ["norm_mix_g", "b_forget", "conv_b", "conv_ln_g", "conv_ln_b", "norm_ffn_g", "norm_ple_g", "final_g"]
    small_g = {
        "norm_mix_g": jnp.concatenate([lg[l]["n_mix"] for l in range(depth)]),
        "b_forget": stacked("b_f"),
        "conv_b": jnp.concatenate([lg[l]["conv_b"] for l in range(depth)]),
        "conv_ln_g": jnp.concatenate([lg[l]["ln_g"] for l in range(depth)]),
        "conv_ln_b": jnp.concatenate([lg[l]["ln_b"] for l in range(depth)]),
        "norm_ffn_g": jnp.concatenate([lg[l]["n_ffn"] for l in range(depth)]),
        "norm_ple_g": jnp.concatenate([lg[l]["n_ple"] for l in range(depth)]),
        "final_g": g_final[0],
    }
    conv_w_g = jnp.stack([lg[l]["conv_w"][:CONV_K] for l in range(depth)])
    small_shapes = [wts[n].shape for n in small] + [conv_w_g.shape]
    small_pack = _pad_rows(jnp.concatenate([small_g[n].reshape(-1) for n in small] + [conv_w_g.reshape(-1)]), 8)
    small_sum = _unpack(_allreduce_small(small_pack), small_shapes)
    for n, name in enumerate(small):
        grads[name] = small_sum[n]
    grads["conv_w"] = lax.dynamic_slice_in_dim(small_sum[len(small)], chip * cw, cw, axis=2)

    def pack_small(src):
        return _pad_rows(jnp.concatenate([src[n].reshape(-1) for n in small]), 8)

    small_upd = _adamw("adamw_small", pack_small(wts), pack_small(grads), pack_small(mom1), pack_small(mom2))
    small_upd = [_unpack(u, [wts[n].shape for n in small]) for u in small_upd]
    delta, new_m, new_v = {}, {}, {}
    for n, name in enumerate(small):
        delta[name], new_m[name], new_v[name] = small_upd[0][n], small_upd[1][n], small_upd[2][n]
    for name in big_names + ["conv_w"]:
        delta[name], new_m[name], new_v[name] = _adamw("adamw_" + name, wts[name], grads[name], mom1[name],
                                                       mom2[name])

    return (loss, dx[None], *[grads[n] for n in order], *[delta[n] for n in order],
            *[new_m[n] for n in order], *[new_v[n] for n in order])
```

```python
import functools
import math

import jax
import jax.numpy as jnp
from jax import lax
from jax.experimental import pallas as pl
from jax.experimental.pallas import tpu as pltpu

F32 = jnp.float32
BF16 = jnp.bfloat16

EPS = 1e-6
CONV_K = 31
NEG_INF = -1e30
LOG2E = 1.4426950408889634
ADAM_LR = 0.001
ADAM_B1 = 0.9
ADAM_B2 = 0.999
ADAM_EPS = 1e-08
ADAM_WD = 0.01
ADAM_STEP = 10

LANES = 128
VMEM_LIMIT = 60 * 1024 * 1024
PACK_W = 1024
N_CHIPS = 4
CONV_HALO = 32
CUM_BLOCK = 256

MM_TM = 1024
MM_TN = 1024
MM_TK = 2048
ROW_TILE = 256
CONV_TILE = 256
ATTN_TILE = 1024
ATTN_SUB = 1024

MESH = pl.DeviceIdType.MESH
ANY = pl.BlockSpec(memory_space=pl.ANY)


def _params(*sem):
    return pltpu.CompilerParams(dimension_semantics=sem, vmem_limit_bytes=VMEM_LIMIT)


def _tile(dim, pref, align=LANES):
    if dim <= pref:
        return dim
    t = (pref // align) * align
    while t >= align:
        if dim % t == 0:
            return t
        t -= align
    return dim


def _sig(x):
    return 1.0 / (1.0 + jnp.exp(-x))


def _op(a):
    if not isinstance(a, tuple):
        return a, 0, ()
    return a if len(a) == 3 else (a[0], a[1], ())


def _spec(block, index, lead):
    if not lead:
        return pl.BlockSpec(block, index)
    return pl.BlockSpec((None,) * len(lead) + block, lambda *g: tuple(lead) + index(*g))


_DIMS = {
    "nn": (((1,), (0,)), ((), ())),
    "nt": (((1,), (1,)), ((), ())),
    "tn": (((0,), (0,)), ((), ())),
}


def _matmul(name, mode, a_ops, b_ops, terms, n_acc, m, n, k, out_dtypes, epi=None, extras=(),
            tm=None, tn=None, tk=None, side=None, cols_outer=False):
    tm = tm or _tile(m, MM_TM)
    tn = tn or _tile(n, MM_TN)
    tk = tk or _tile(k, MM_TK)
    ni, nj, nk = m // tm, n // tn, k // tk
    a_ops = [_op(a) for a in a_ops]
    b_ops = [_op(b) for b in b_ops]
    extras = [(kind, _op(e)) for kind, e in extras]
    na, nb, ne, no = len(a_ops), len(b_ops), len(extras), len(out_dtypes)
    n_acc_refs = n_acc if nk > 1 else 0
    s_ins = list(side.ins) if side else []
    s_outs = list(side.out_shapes) if side else []
    s_sems = [pltpu.SemaphoreType.DMA((cnt,)) for cnt in side.sem_counts] if side else []

    def at(index):
        return (lambda jj, ii, kk: index(ii, jj, kk)) if cols_outer else index

    def a_spec(off, lead):
        if mode == "tn":
            assert off % tm == 0
            return _spec((tk, tm), at(lambda i, j, kk: (kk, i + off // tm)), lead)
        assert off % tk == 0
        return _spec((tm, tk), at(lambda i, j, kk: (i, kk + off // tk)), lead)

    def b_spec(off, lead):
        if mode == "nt":
            assert off % tk == 0
            return _spec((tn, tk), at(lambda i, j, kk: (j, kk + off // tk)), lead)
        assert off % tn == 0
        return _spec((tk, tn), at(lambda i, j, kk: (kk, j + off // tn)), lead)

    def e_spec(kind, off, lead):
        assert off % tn == 0
        if kind == "n":
            return _spec((1, tn), at(lambda i, j, kk: (0, j + off // tn)), lead)
        return _spec((tm, tn), at(lambda i, j, kk: (i, j + off // tn)), lead)

    def body(*refs):
        refs = list(refs)
        a_refs, b_refs, e_refs, si_refs, o_refs, so_refs, acc_refs, sem_refs = (
            [refs.pop(0) for _ in range(cnt)]
            for cnt in (na, nb, ne, len(s_ins), no, len(s_outs), n_acc_refs, len(s_sems)))
        outer, inner, kk = pl.program_id(0), pl.program_id(1), pl.program_id(2)
        if side:
            @pl.when((outer == 0) & (inner == 0) & (kk == 0))
            def _():
                side.start(si_refs, so_refs, sem_refs)

        av = [r[...].astype(BF16) for r in a_refs]
        bv = [r[...].astype(BF16) for r in b_refs]
        sums = [None] * n_acc
        for ai, bi, ci in terms:
            part = lax.dot_general(av[ai], bv[bi], _DIMS[mode], preferred_element_type=F32)
            sums[ci] = part if sums[ci] is None else sums[ci] + part

        def finish(accs):
            outs = epi(accs, [e[...] for e in e_refs]) if epi is not None else accs
            for o, val in zip(o_refs, outs):
                o[...] = val.astype(o.dtype)

        if nk == 1:
            finish(sums)
        else:
            @pl.when(kk == 0)
            def _():
                for acc, part in zip(acc_refs, sums):
                    acc[...] = part

            @pl.when(kk > 0)
            def _():
                for acc, part in zip(acc_refs, sums):
                    acc[...] += part

            @pl.when(kk == nk - 1)
            def _():
                finish([acc[...] for acc in acc_refs])

        if side:
            @pl.when((outer == grid[0] - 1) & (inner == grid[1] - 1) & (kk == nk - 1))
            def _():
                side.finish(si_refs, so_refs, sem_refs)

    grid = (nj, ni, nk) if cols_outer else (ni, nj, nk)
    order = ("arbitrary",) * 3 if side else ("parallel", "parallel", "arbitrary")
    outs = pl.pallas_call(
        body,
        name=name,
        grid=grid,
        in_specs=[a_spec(off, lead) for _, off, lead in a_ops] + [b_spec(off, lead) for _, off, lead in b_ops]
        + [e_spec(kind, off, lead) for kind, (_, off, lead) in extras] + [ANY] * len(s_ins),
        out_specs=[pl.BlockSpec((tm, tn), at(lambda i, j, kk: (i, j))) for _ in out_dtypes] + [ANY] * len(s_outs),
        out_shape=[jax.ShapeDtypeStruct((m, n), dt) for dt in out_dtypes] + s_outs,
        scratch_shapes=[pltpu.VMEM((tm, tn), F32) for _ in range(n_acc_refs)] + s_sems,
        compiler_params=_params(*order),
    )(*[a[0] for a in a_ops], *[b[0] for b in b_ops], *[e[0] for _, e in extras], *s_ins)
    return outs


def _mm(name, mode, a, b, m, n, k, out_dtype, **kw):
    return _matmul(name, mode, [a], [b], [(0, 0, 0)], 1, m, n, k, [out_dtype], **kw)[0]


def _ew(name, fn, ins, m, n, out_dtypes, tm, tn=None):
    tn = tn or n
    ins = [_op(a) for a in ins]
    ni = len(ins)

    def spec(off, lead):
        assert off % tn == 0
        return _spec((tm, tn), lambda i, j: (i, j + off // tn), lead)

    def body(*refs):
        outs = fn(*[r[...] for r in refs[:ni]])
        for o, val in zip(refs[ni:], outs):
            o[...] = val.astype(o.dtype)

    return pl.pallas_call(
        body,
        name=name,
        grid=(m // tm, n // tn),
        in_specs=[spec(off, lead) for _, off, lead in ins],
        out_specs=[pl.BlockSpec((tm, tn), lambda i, j: (i, j)) for _ in out_dtypes],
        out_shape=[jax.ShapeDtypeStruct((m, n), dt) for dt in out_dtypes],
        compiler_params=_params("parallel", "parallel"),
    )(*[a[0] for a in ins])


def _rms_fwd(name, x, g, t, d):
    tr = _tile(t, ROW_TILE, 8)

    def body(x_ref, g_ref, h_ref):
        xv = x_ref[...]
        r = lax.rsqrt(jnp.mean(xv * xv, axis=1, keepdims=True) + EPS)
        h_ref[...] = (xv * r * g_ref[...]).astype(BF16)

    return pl.pallas_call(
        body,
        name=name,
        grid=(t // tr,),
        in_specs=[pl.BlockSpec((tr, d), lambda i: (i, 0)), pl.BlockSpec((1, d), lambda i: (0, 0))],
        out_specs=pl.BlockSpec((tr, d), lambda i: (i, 0)),
        out_shape=jax.ShapeDtypeStruct((t, d), BF16),
        compiler_params=_params("parallel"),
    )(x, g)


def _rms_bwd_rows(dh, xv, g):
    r = lax.rsqrt(jnp.mean(xv * xv, axis=1, keepdims=True) + EPS)
    xhat = xv * r
    dxh = dh * g
    dx = r * (dxh - xhat * jnp.mean(dxh * xhat, axis=1, keepdims=True))
    return dx, dh * xhat


def _rms_bwd(name, dh, x, g, dres, t, d):
    tr = _tile(t, ROW_TILE, 8)

    def body(dh_ref, x_ref, g_ref, dres_ref, dx_ref, dxb_ref, dg_ref):
        @pl.when(pl.program_id(0) == 0)
        def _():
            dg_ref[...] = jnp.zeros(dg_ref.shape, F32)

        dx, dg_rows = _rms_bwd_rows(dh_ref[...].astype(F32), x_ref[...], g_ref[...])
        dx = dx + dres_ref[...]
        dx_ref[...] = dx
        dxb_ref[...] = dx.astype(BF16)
        dg_ref[...] += jnp.sum(dg_rows, axis=0, keepdims=True)

    row = pl.BlockSpec((tr, d), lambda i: (i, 0))
    vec = pl.BlockSpec((1, d), lambda i: (0, 0))
    return pl.pallas_call(
        body,
        name=name,
        grid=(t // tr,),
        in_specs=[row, row, vec, row],
        out_specs=[row, row, vec],
        out_shape=[jax.ShapeDtypeStruct((t, d), F32), jax.ShapeDtypeStruct((t, d), BF16),
                   jax.ShapeDtypeStruct((1, d), F32)],
        compiler_params=_params("arbitrary"),
    )(dh, x, g, dres)


def _loss_head(x, g, target, t, d):
    tr = _tile(t, ROW_TILE, 8)

    def body(x_ref, g_ref, tgt_ref, loss_ref, dx_ref, dxb_ref, dg_ref):
        @pl.when(pl.program_id(0) == 0)
        def _():
            dg_ref[...] = jnp.zeros(dg_ref.shape, F32)
            loss_ref[...] = jnp.zeros(loss_ref.shape, F32)

        xv = x_ref[...]
        gv = g_ref[...]
        r = lax.rsqrt(jnp.mean(xv * xv, axis=1, keepdims=True) + EPS)
        err = xv * r * gv - tgt_ref[...]
        loss_ref[...] += (0.5 / d) * jnp.sum(err * err)
        dx, dg_rows = _rms_bwd_rows(err * (1.0 / d), xv, gv)
        dx_ref[...] = dx
        dxb_ref[...] = dx.astype(BF16)
        dg_ref[...] += jnp.sum(dg_rows, axis=0, keepdims=True)

    row = pl.BlockSpec((tr, d), lambda i: (i, 0))
    vec = pl.BlockSpec((1, d), lambda i: (0, 0))
    one = pl.BlockSpec((1, LANES), lambda i: (0, 0))
    return pl.pallas_call(
        body,
        name="loss_head",
        grid=(t // tr,),
        in_specs=[row, vec, row],
        out_specs=[one, row, row, vec],
        out_shape=[jax.ShapeDtypeStruct((1, LANES), F32), jax.ShapeDtypeStruct((t, d), F32),
                   jax.ShapeDtypeStruct((t, d), BF16), jax.ShapeDtypeStruct((1, d), F32)],
        compiler_params=_params("arbitrary"),
    )(x, g, target)


def _layernorm_rows(cv, g, b):
    mu = jnp.mean(cv, axis=1, keepdims=True)
    xc = cv - mu
    rstd = lax.rsqrt(jnp.mean(xc * xc, axis=1, keepdims=True) + EPS)
    xhat = xc * rstd
    return xhat, rstd, xhat * g + b


def _conv_fwd(proj, conv_w, conv_b, ln_g, ln_b, t, c, a_off, g_off):
    tm = _tile(t, CONV_TILE, CONV_HALO)
    per = tm // CONV_HALO
    ab, gb = a_off // c, g_off // c

    def body(a_ref, g_ref, ap_ref, gp_ref, w_ref, cb_ref, lg_ref, lb_ref, u2_ref, c_ref, upad):
        i = pl.program_id(0)
        u_prev = ap_ref[...].astype(F32) * _sig(gp_ref[...].astype(F32))
        upad[pl.ds(0, CONV_HALO), :] = jnp.where(i > 0, u_prev, 0.0)
        upad[pl.ds(CONV_HALO, tm), :] = a_ref[...].astype(F32) * _sig(g_ref[...].astype(F32))
        acc = jnp.zeros((tm, c), F32) + cb_ref[...]
        for k in range(CONV_K):
            acc = acc + w_ref[pl.ds(k, 1), :] * upad[pl.ds(CONV_HALO - (CONV_K - 1) + k, tm), :]
        c_ref[...] = acc
        _, _, z = _layernorm_rows(acc, lg_ref[...], lb_ref[...])
        u2_ref[...] = (z * _sig(z)).astype(BF16)

    vec = pl.BlockSpec((1, c), lambda i: (0, 0))
    return pl.pallas_call(
        body,
        name="conv_fwd",
        grid=(t // tm,),
        in_specs=[
            pl.BlockSpec((tm, c), lambda i: (i, ab)),
            pl.BlockSpec((tm, c), lambda i: (i, gb)),
            pl.BlockSpec((CONV_HALO, c), lambda i: (jnp.maximum(i * per - 1, 0), ab)),
            pl.BlockSpec((CONV_HALO, c), lambda i: (jnp.maximum(i * per - 1, 0), gb)),
            pl.BlockSpec((CONV_HALO, c), lambda i: (0, 0)), vec, vec, vec,
        ],
        out_specs=[pl.BlockSpec((tm, c), lambda i: (i, 0)), pl.BlockSpec((tm, c), lambda i: (i, 0))],
        out_shape=[jax.ShapeDtypeStruct((t, c), BF16), jax.ShapeDtypeStruct((t, c), F32)],
        scratch_shapes=[pltpu.VMEM((CONV_HALO + tm, c), F32)],
        compiler_params=_params("parallel"),
    )(proj, proj, proj, proj, conv_w, conv_b, ln_g, ln_b)


def _conv_bwd(proj, cpre, du2, conv_w, ln_g, ln_b, t, c, a_off, g_off):
    tm = _tile(t, CONV_TILE, CONV_HALO)
    per = tm // CONV_HALO
    nt = t // tm
    last_halo = t // CONV_HALO - 1
    ab, gb = a_off // c, g_off // c

    def body(a_ref, g_ref, ap_ref, gp_ref, c_ref, cn_ref, du_ref, dun_ref, w_ref, lg_ref, lb_ref,
             da_ref, dg_ref, gw_ref, gcb_ref, glg_ref, glb_ref, upad, dpad):
        i = pl.program_id(0)

        @pl.when(i == 0)
        def _():
            gw_ref[...] = jnp.zeros(gw_ref.shape, F32)
            gcb_ref[...] = jnp.zeros(gcb_ref.shape, F32)
            glg_ref[...] = jnp.zeros(glg_ref.shape, F32)
            glb_ref[...] = jnp.zeros(glb_ref.shape, F32)

        lg = lg_ref[...]
        lb = lb_ref[...]

        def ln_bwd(cv, duv):
            xhat, rstd, z = _layernorm_rows(cv, lg, lb)
            sz = _sig(z)
            dz = duv * (sz * (1.0 + z * (1.0 - sz)))
            dxh = dz * lg
            dc = rstd * (dxh - jnp.mean(dxh, axis=1, keepdims=True)
                         - xhat * jnp.mean(dxh * xhat, axis=1, keepdims=True))
            return dc, dz, xhat

        dc, dz, xhat = ln_bwd(c_ref[...], du_ref[...].astype(F32))
        dc_next, _, _ = ln_bwd(cn_ref[...], dun_ref[...].astype(F32))
        glg_ref[...] += jnp.sum(dz * xhat, axis=0, keepdims=True)
        glb_ref[...] += jnp.sum(dz, axis=0, keepdims=True)
        gcb_ref[...] += jnp.sum(dc, axis=0, keepdims=True)
        dpad[pl.ds(0, tm), :] = dc
        dpad[pl.ds(tm, CONV_HALO), :] = jnp.where(i < nt - 1, dc_next, 0.0)

        av = a_ref[...].astype(F32)
        sg = _sig(g_ref[...].astype(F32))
        u_prev = ap_ref[...].astype(F32) * _sig(gp_ref[...].astype(F32))
        upad[pl.ds(0, CONV_HALO), :] = jnp.where(i > 0, u_prev, 0.0)
        upad[pl.ds(CONV_HALO, tm), :] = av * sg

        du = jnp.zeros((tm, c), F32)
        for k in range(CONV_K):
            du = du + w_ref[pl.ds(k, 1), :] * dpad[pl.ds(CONV_K - 1 - k, tm), :]
            gw_ref[pl.ds(k, 1), :] += jnp.sum(
                dc * upad[pl.ds(CONV_HALO - (CONV_K - 1) + k, tm), :], axis=0, keepdims=True)
        da_ref[...] = (du * sg).astype(BF16)
        dg_ref[...] = (du * av * sg * (1.0 - sg)).astype(BF16)

    vec = pl.BlockSpec((1, c), lambda i: (0, 0))
    cur = pl.BlockSpec((tm, c), lambda i: (i, 0))
    nxt = pl.BlockSpec((CONV_HALO, c), lambda i: (jnp.minimum((i + 1) * per, last_halo), 0))
    wsp = pl.BlockSpec((CONV_HALO, c), lambda i: (0, 0))
    return pl.pallas_call(
        body,
        name="conv_bwd",
        grid=(nt,),
        in_specs=[
            pl.BlockSpec((tm, c), lambda i: (i, ab)),
            pl.BlockSpec((tm, c), lambda i: (i, gb)),
            pl.BlockSpec((CONV_HALO, c), lambda i: (jnp.maximum(i * per - 1, 0), ab)),
            pl.BlockSpec((CONV_HALO, c), lambda i: (jnp.maximum(i * per - 1, 0), gb)),
            cur, nxt, cur, nxt, wsp, vec, vec,
        ],
        out_specs=[cur, cur, wsp, vec, vec, vec],
        out_shape=[jax.ShapeDtypeStruct((t, c), BF16), jax.ShapeDtypeStruct((t, c), BF16),
                   jax.ShapeDtypeStruct((CONV_HALO, c), F32), jax.ShapeDtypeStruct((1, c), F32),
                   jax.ShapeDtypeStruct((1, c), F32), jax.ShapeDtypeStruct((1, c), F32)],
        scratch_shapes=[pltpu.VMEM((CONV_HALO + tm, c), F32), pltpu.VMEM((tm + CONV_HALO, c), F32)],
        compiler_params=_params("arbitrary"),
    )(proj, proj, proj, proj, cpre, cpre, du2, du2, conv_w, ln_g, ln_b)


def _split3(x):
    hi = x.astype(BF16).astype(F32)
    r1 = x - hi
    mid = r1.astype(BF16).astype(F32)
    lo = (r1 - mid).astype(BF16).astype(F32)
    return hi, mid, lo


def _split3_dot(x, tri):
    dot = functools.partial(jnp.dot, preferred_element_type=F32)
    hi, mid, lo = _split3(x)
    return dot(hi.astype(BF16), tri) + dot(mid.astype(BF16), tri) + dot(lo.astype(BF16), tri)


def _to_blocks(a, nb, blk):
    return a.reshape(a.shape[0], nb, blk).transpose(1, 0, 2)


def _from_blocks(a):
    return a.transpose(1, 0, 2).reshape(a.shape[1], -1)


def _forget_fwd(f_t, b_col, nh, t):
    blk = _tile(t, CUM_BLOCK)
    nb = t // blk

    def body(f_ref, b_ref, c_ref):
        ri = lax.broadcasted_iota(jnp.int32, (blk, blk), 0)
        ci = lax.broadcasted_iota(jnp.int32, (blk, blk), 1)
        tri = (ri <= ci).astype(BF16)

        def step(bi, carry):
            xv = f_ref[bi] + b_ref[:, :1]
            lf = jnp.minimum(xv, 0.0) - jnp.log(1.0 + jnp.exp(-jnp.abs(xv)))
            cs = _split3_dot(lf, tri) + carry
            c_ref[bi] = cs
            return cs[:, blk - 1:blk]

        lax.fori_loop(0, nb, step, jnp.zeros((nh, 1), F32))

    out = pl.pallas_call(
        body,
        name="forget_fwd",
        out_shape=jax.ShapeDtypeStruct((nb, nh, blk), F32),
        compiler_params=pltpu.CompilerParams(vmem_limit_bytes=VMEM_LIMIT),
    )(_to_blocks(f_t, nb, blk), b_col)
    return _from_blocks(out)


def _forget_bwd(dc, f_t, b_col, nh, t):
    blk = _tile(t, CUM_BLOCK)
    nb = t // blk

    def body(dc_ref, f_ref, b_ref, df_ref, db_ref):
        ri = lax.broadcasted_iota(jnp.int32, (blk, blk), 0)
        ci = lax.broadcasted_iota(jnp.int32, (blk, blk), 1)
        tri = (ri >= ci).astype(BF16)

        def step(n, carry):
            tail, db = carry
            bi = nb - 1 - n
            rc = _split3_dot(dc_ref[bi], tri) + tail
            df = rc * _sig(-(f_ref[bi] + b_ref[:, :1]))
            df_ref[bi] = df
            return rc[:, 0:1], db + jnp.sum(df, axis=1, keepdims=True)

        _, db = lax.fori_loop(0, nb, step, (jnp.zeros((nh, 1), F32), jnp.zeros((nh, 1), F32)))
        db_ref[...] = jnp.broadcast_to(db, db_ref.shape)

    df, db = pl.pallas_call(
        body,
        name="forget_bwd",
        out_shape=[jax.ShapeDtypeStruct((nb, nh, blk), F32), jax.ShapeDtypeStruct((nh, LANES), F32)],
        compiler_params=pltpu.CompilerParams(vmem_limit_bytes=VMEM_LIMIT),
    )(_to_blocks(dc, nb, blk), _to_blocks(f_t, nb, blk), b_col)
    return _from_blocks(df), db


def _lanes(parts, rows):
    lane = lax.broadcasted_iota(jnp.int32, (rows, LANES), 1)
    out = jnp.zeros((rows, LANES), F32)
    for n, part in enumerate(parts):
        out = jnp.where(lane == n, part, out)
    return out


def _attn_prep_fwd(proj, cs, t, nh, hd, q_off, k_off, v_off):
    tr = _tile(t, ATTN_TILE, 16)
    qb, kb, vb = q_off // hd, k_off // hd, v_off // hd

    def body(q_ref, k_ref, v_ref, cs_ref, qa_ref, ka_ref, va_ref):
        hi, mid, lo = _split3(cs_ref[0][:, :1])
        qa_ref[0, :, :hd] = q_ref[...]
        qa_ref[0, :, hd:] = _lanes([1.0, 1.0, 1.0, hi, mid, lo], tr).astype(BF16)
        ka_ref[0, :, :hd] = k_ref[...]
        ka_ref[0, :, hd:] = _lanes([-hi, -mid, -lo] + [1.0] * 6, tr).astype(BF16)
        va_ref[0, :, :hd] = v_ref[...]
        va_ref[0, :, hd:] = _lanes([-1.0, -1.0, -1.0], tr).astype(BF16)

    wide = pl.BlockSpec((1, tr, 2 * hd), lambda h, i: (h, i, 0))
    return pl.pallas_call(
        body,
        name="attn_prep_fwd",
        grid=(nh, t // tr),
        in_specs=[pl.BlockSpec((tr, hd), lambda h, i: (i, qb + h)), pl.BlockSpec((tr, hd), lambda h, i: (i, kb + h)),
                  pl.BlockSpec((tr, hd), lambda h, i: (i, vb + h)), pl.BlockSpec((1, tr, LANES), lambda h, i: (h, i, 0))],
        out_specs=[wide, wide, wide],
        out_shape=[jax.ShapeDtypeStruct((nh, t, 2 * hd), BF16)] * 3,
        compiler_params=_params("parallel", "parallel"),
    )(proj, proj, proj, cs)


def _attn_prep_bwd(qa, lse, o, do, t, nh, hd):
    tr = _tile(t, ATTN_TILE, 16)
    inv_scale = math.sqrt(hd)

    def body(qa_ref, lse_ref, o_ref, do_ref, qb_ref, da_ref):
        l_hi, l_mid, l_lo = _split3(lse_ref[0][:, :1] * (-inv_scale))
        lane = lax.broadcasted_iota(jnp.int32, (tr, LANES), 1)
        extra = qa_ref[0, :, hd:].astype(F32)
        extra = jnp.where(lane == 6, l_hi, jnp.where(lane == 7, l_mid, jnp.where(lane == 8, l_lo, extra)))
        qb_ref[0, :, :hd] = qa_ref[0, :, :hd]
        qb_ref[0, :, hd:] = extra.astype(BF16)
        dov = do_ref[...]
        delta = jnp.sum(dov.astype(F32) * o_ref[...].astype(F32), axis=1, keepdims=True)
        da_ref[0, :, :hd] = dov
        da_ref[0, :, hd:] = _lanes(list(_split3(delta)), tr).astype(BF16)

    wide = pl.BlockSpec((1, tr, 2 * hd), lambda h, i: (h, i, 0))
    head = pl.BlockSpec((tr, hd), lambda h, i: (i, h))
    return pl.pallas_call(
        body,
        name="attn_prep_bwd",
        grid=(nh, t // tr),
        in_specs=[wide, pl.BlockSpec((1, tr, LANES), lambda h, i: (h, i, 0)), head, head],
        out_specs=[wide, wide],
        out_shape=[jax.ShapeDtypeStruct((nh, t, 2 * hd), BF16)] * 2,
        compiler_params=_params("parallel", "parallel"),
    )(qa, lse, o, do)


def _causal(s, row0, rows, cols):
    row = lax.broadcasted_iota(jnp.int32, (rows, cols), 0) + row0
    col = lax.broadcasted_iota(jnp.int32, (rows, cols), 1)
    return jnp.where(col <= row, s, NEG_INF)


def _side_refs(side):
    if side is None:
        return [], [], []
    return list(side.ins), list(side.out_shapes), [pltpu.SemaphoreType.DMA((cnt,)) for cnt in side.sem_counts]


def _attn_fwd(qa, ka, proj, t, nh, hd, v_off, side=None):
    tq = _tile(t, ATTN_TILE)
    nq = t // tq
    sub = _tile(tq, ATTN_SUB)
    ns = tq // sub
    scale = 1.0 / math.sqrt(hd)
    vb = v_off // hd
    s_ins, s_outs, s_sems = _side_refs(side)

    def body(qa_ref, ka_ref, v_ref, *rest):
        si_refs, (o_ref, lse_ref) = rest[:len(s_ins)], rest[len(s_ins):len(s_ins) + 2]
        so_refs, sem_refs = rest[len(s_ins) + 2:len(s_ins) + 2 + len(s_outs)], rest[len(s_ins) + 2 + len(s_outs):]
        h, i = pl.program_id(0), pl.program_id(1)
        if side:
            @pl.when((h == 0) & (i == 0))
            def _():
                side.start(si_refs, so_refs, sem_refs)

        def tile(j, carry, masked):
            rows = pl.ds(pl.multiple_of(j * tq, tq), tq)
            kj = ka_ref[0, rows, :]
            vj = v_ref[rows, :]
            new = []
            for r in range(ns):
                m, l, acc = carry[r]
                s = lax.dot_general(qa_ref[0, pl.ds(r * sub, sub), :], kj, _DIMS["nt"], preferred_element_type=F32)
                if masked:
                    s = _causal(s, r * sub, sub, tq)
                m_new = jnp.maximum(m, jnp.max(s, axis=1, keepdims=True))
                p = jnp.exp2((s - m_new) * (scale * LOG2E))
                alpha = jnp.exp2((m - m_new) * (scale * LOG2E))
                l = alpha * l + jnp.sum(p, axis=1, keepdims=True)
                acc = alpha * acc + jnp.dot(p.astype(BF16), vj, preferred_element_type=F32)
                new.append((m_new, l, acc))
            return tuple(new)

        init = tuple((jnp.full((sub, 1), NEG_INF, F32), jnp.zeros((sub, 1), F32), jnp.zeros((sub, hd), F32))
                     for _ in range(ns))
        carry = lax.fori_loop(0, i, lambda j, cr: tile(j, cr, False), init)
        carry = tile(i, carry, True)
        for r in range(ns):
            m, l, acc = carry[r]
            o_ref[pl.ds(r * sub, sub), :] = (acc / l).astype(BF16)
            lse_ref[0, pl.ds(r * sub, sub), :] = jnp.broadcast_to(m * scale + jnp.log(l), (sub, LANES))
        if side:
            @pl.when((h == nh - 1) & (i == nq - 1))
            def _():
                side.finish(si_refs, so_refs, sem_refs)

    return pl.pallas_call(
        body,
        name="attn_fwd",
        grid=(nh, nq),
        in_specs=[pl.BlockSpec((1, tq, 2 * hd), lambda h, i: (h, i, 0)),
                  pl.BlockSpec((1, t, 2 * hd), lambda h, i: (h, 0, 0)),
                  pl.BlockSpec((t, hd), lambda h, i: (0, vb + h))] + [ANY] * len(s_ins),
        out_specs=[pl.BlockSpec((tq, hd), lambda h, i: (i, h)),
                   pl.BlockSpec((1, tq, LANES), lambda h, i: (h, i, 0))] + [ANY] * len(s_outs),
        out_shape=[jax.ShapeDtypeStruct((t, nh * hd), BF16), jax.ShapeDtypeStruct((nh, t, LANES), F32)] + s_outs,
        scratch_shapes=s_sems,
        compiler_params=_params(*(("arbitrary",) * 2 if side else ("parallel",) * 2)),
    )(qa, ka, proj, *s_ins)


def _attn_bwd(qb, ka, va, da, t, nh, hd, side=None):
    tq = _tile(t, ATTN_TILE)
    nq = t // tq
    sub = _tile(tq, ATTN_SUB)
    ns = tq // sub
    scale = 1.0 / math.sqrt(hd)
    s_ins, s_outs, s_sems = _side_refs(side)

    def body(qb_ref, ka_ref, va_ref, da_ref, *rest):
        si_refs, rest = rest[:len(s_ins)], rest[len(s_ins):]
        (dq_ref, dk_ref, dv_ref, dck_ref, dcq_ref), rest = rest[:5], rest[5:]
        so_refs, (dq_all, *sem_refs) = rest[:len(s_outs)], rest[len(s_outs):]
        h, j = pl.program_id(0), pl.program_id(1)
        if side:
            @pl.when((h == 0) & (j == 0))
            def _():
                side.start(si_refs, so_refs, sem_refs)

        @pl.when(j == 0)
        def _():
            dq_all[...] = jnp.zeros(dq_all.shape, F32)
            dcq_ref[...] = jnp.zeros(dcq_ref.shape, F32)

        kaj = ka_ref[0]
        vaj = va_ref[0]
        kj = kaj[:, :hd]

        def tile(i, carry, masked):
            dk, dv, dck = carry
            for r in range(ns):
                rows = pl.ds(pl.multiple_of(i * tq + r * sub, sub), sub)
                qr = qb_ref[0, rows, :]
                dr = da_ref[0, rows, :]
                s = lax.dot_general(qr, kaj, _DIMS["nt"], preferred_element_type=F32)
                if masked:
                    s = _causal(s, r * sub, sub, tq)
                p = jnp.exp2(s * (scale * LOG2E))
                ds = p * lax.dot_general(dr, vaj, _DIMS["nt"], preferred_element_type=F32)
                dsb = ds.astype(BF16)
                dv = dv + lax.dot_general(p.astype(BF16), dr[:, :hd], _DIMS["tn"], preferred_element_type=F32)
                dk = dk + lax.dot_general(dsb, qr[:, :hd], _DIMS["tn"], preferred_element_type=F32)
                dck = dck - jnp.sum(ds, axis=0, keepdims=True)
                dq_all[rows, :] += jnp.dot(dsb, kj, preferred_element_type=F32)
                dcq_ref[0, rows, :] += jnp.sum(ds, axis=1, keepdims=True)
            return dk, dv, dck

        carry = (jnp.zeros((tq, hd), F32), jnp.zeros((tq, hd), F32), jnp.zeros((1, tq), F32))
        carry = tile(j, carry, True)
        dk, dv, dck = lax.fori_loop(j + 1, nq, lambda i, cr: tile(i, cr, False), carry)
        dk_ref[...] = (dk * scale).astype(BF16)
        dv_ref[...] = dv.astype(BF16)
        dck_ref[0] = dck

        @pl.when(j == nq - 1)
        def _():
            dq_ref[...] = (dq_all[...] * scale).astype(BF16)

        if side:
            @pl.when((h == nh - 1) & (j == nq - 1))
            def _():
                side.finish(si_refs, so_refs, sem_refs)

    whole = pl.BlockSpec((1, t, 2 * hd), lambda h, j: (h, 0, 0))
    block = pl.BlockSpec((1, tq, 2 * hd), lambda h, j: (h, j, 0))
    return pl.pallas_call(
        body,
        name="attn_bwd",
        grid=(nh, nq),
        in_specs=[whole, block, block, whole] + [ANY] * len(s_ins),
        out_specs=[
            pl.BlockSpec((t, hd), lambda h, j: (0, h)),
            pl.BlockSpec((tq, hd), lambda h, j: (j, h)),
            pl.BlockSpec((tq, hd), lambda h, j: (j, h)),
            pl.BlockSpec((1, 1, tq), lambda h, j: (h, 0, j)),
            pl.BlockSpec((1, t, LANES), lambda h, j: (h, 0, 0)),
        ] + [ANY] * len(s_outs),
        out_shape=[jax.ShapeDtypeStruct((t, nh * hd), BF16), jax.ShapeDtypeStruct((t, nh * hd), BF16),
                   jax.ShapeDtypeStruct((t, nh * hd), BF16), jax.ShapeDtypeStruct((nh, 1, t), F32),
                   jax.ShapeDtypeStruct((nh, t, LANES), F32)] + s_outs,
        scratch_shapes=[pltpu.VMEM((t, hd), F32)] + s_sems,
        compiler_params=_params("arbitrary", "arbitrary"),
    )(qb, ka, va, da, *s_ins)


class _Dims:
    def __init__(self, t, d, c, nh, aw, f, pd):
        self.t, self.d, self.c, self.nh, self.aw, self.f, self.pd = t, d, c, nh, aw, f, pd
        self.hd = aw // nh
        self.a_off, self.g_off = 0, c
        self.q_off, self.k_off, self.v_off = 2 * c, 2 * c + aw, 2 * c + 2 * aw
        self.gc_off = 2 * c + 3 * aw
        self.ga_off = self.gc_off + d
        self.n_main = self.ga_off + d


def _ffn_tn(f):
    return _tile(f, 1536)


class _Riders:
    def __init__(self, plan=None):
        self.plan = plan or {}

    def host(self, name, n_out, call, ctx=None):
        if name not in self.plan:
            return call()
        make_side, take = self.plan[name]
        outs = call(side=make_side(ctx))
        take(outs[n_out:])
        return outs[:n_out]


class _Weights:
    def __init__(self, fixed, gathered, views):
        self.fixed, self.gathered, self.views = fixed, gathered, views

    def __getitem__(self, key):
        if key in self.fixed:
            return self.fixed[key]
        name, off = self.views[key]
        return self.gathered[name], off


def _layer_fwd(dm, x, p, w, riders):
    t, d, c, f = dm.t, dm.d, dm.c, dm.f
    s = {"x": x}
    h = _rms_fwd("rms_mix", x, w["n_mix"], t, d)
    proj, = riders.host("mm_in", 1, functools.partial(
        _matmul, "mm_in", "nn", [h], [w["main"]], [(0, 0, 0)], 1, t, dm.n_main, d, [BF16]))
    fl = _mm("mm_forget", "nn", h, w["f"], t, LANES, d, F32)
    u2, cpre = _conv_fwd(proj, w["conv_w"], w["conv_b"], w["ln_g"], w["ln_b"], t, c, dm.a_off, dm.g_off)
    f_t = fl[:, :dm.nh].T
    cum = _forget_fwd(f_t, w["b_f"], dm.nh, t)
    cs = jnp.broadcast_to((cum * math.sqrt(dm.hd))[:, :, None], (dm.nh, t, LANES))
    qa, ka, va = _attn_prep_fwd(proj, cs, t, dm.nh, dm.hd, dm.q_off, dm.k_off, dm.v_off)
    o, lse = riders.host("attn_fwd", 2, functools.partial(_attn_fwd, qa, ka, proj, t, dm.nh, dm.hd, dm.v_off))

    def epi_conv(accs, ex):
        return accs[0], _sig(ex[0].astype(F32)) * accs[0]

    yc, m1 = _matmul("mm_conv_out", "nn", [u2], [w["co"]], [(0, 0, 0)], 1, t, d, c, [BF16, BF16],
                     epi=epi_conv, extras=[("mn", (proj, dm.gc_off))], tm=512)

    def epi_attn(accs, ex):
        return accs[0], ex[1].astype(F32) + _sig(ex[0].astype(F32)) * accs[0]

    ya, merged = _matmul("mm_attn_out", "nn", [o], [w["ao"]], [(0, 0, 0)], 1, t, d, dm.aw, [BF16, BF16],
                         epi=epi_attn, extras=[("mn", (proj, dm.ga_off)), ("mn", m1)], tm=512)
    x1 = _matmul("mm_out", "nn", [merged], [w["o"]], [(0, 0, 0)], 1, t, d, d, [F32],
                 epi=lambda accs, ex: [ex[0] + accs[0]], extras=[("mn", x)], tm=512)[0]

    hf = _rms_fwd("rms_ffn", x1, w["n_ffn"], t, d)

    def epi_glu(accs, ex):
        gate, up = accs
        return gate, up, gate * _sig(gate) * up

    gate, up, act = riders.host("mm_gate_up", 3, functools.partial(
        _matmul, "mm_gate_up", "nn", [hf], [w["g"], w["u"]], [(0, 0, 0), (0, 1, 1)], 2, t, f, d, [BF16, BF16, BF16],
        epi=epi_glu, tm=512, tn=_ffn_tn(f), cols_outer=True))
    x2, = riders.host("mm_down", 1, functools.partial(
        _matmul, "mm_down", "nn", [act], [w["d"]], [(0, 0, 0)], 1, t, d, f, [F32],
        epi=lambda accs, ex: [ex[0] + accs[0]], extras=[("mn", x1)], tm=512, tk=f, cols_outer=True))

    hp = _rms_fwd("rms_ple", x2, w["n_ple"], t, d)
    pp = _mm("mm_ple_proj", "nn", p, w["pp"], t, d, dm.pd, BF16, tm=512)

    def epi_ple(accs, ex):
        sg = _sig(accs[0])
        return sg, ex[1] + sg * ex[0].astype(F32)

    sg, x3 = _matmul("mm_ple_gate", "nn", [hp], [w["pg"]], [(0, 0, 0)], 1, t, d, d, [BF16, F32],
                     epi=epi_ple, extras=[("mn", pp), ("mn", x2)], tm=512)
    s.update(h=h, proj=proj, f_t=f_t, qa=qa, ka=ka, va=va, u2=u2, cpre=cpre, o=o, lse=lse, yc=yc, ya=ya,
             merged=merged, x1=x1, hf=hf, gate=gate, up=up, act=act, x2=x2, hp=hp, pp=pp, sg=sg, p=p)
    return x3, s


def _layer_bwd(dm, dx3, dx3b, s, w, riders):
    t, d, c, f = dm.t, dm.d, dm.c, dm.f
    g = {}

    def ple_ew(dxv, sgv, ppv):
        sgf, ppf = sgv.astype(F32), ppv.astype(F32)
        return dxv * sgf, dxv * ppf * sgf * (1.0 - sgf)

    d_pp, d_z = _ew("ple_bwd", ple_ew, [dx3, s["sg"], s["pp"]], t, d, [BF16, BF16], _tile(t, ROW_TILE, 8))
    g["pp"] = _mm("gw_ple_proj", "tn", s["p"], d_pp, dm.pd, d, t, F32)
    g["pg"] = _mm("gw_ple_gate", "tn", s["hp"], d_z, d, d, t, F32)
    d_hp = _mm("dx_ple_gate", "nt", d_z, w["pg"], t, d, d, BF16)
    dx2, dx2b, g["n_ple"] = _rms_bwd("rms_ple_bwd", d_hp, s["x2"], w["n_ple"], dx3, t, d)

    def epi_dglu(accs, ex):
        gate, up = ex[0].astype(F32), ex[1].astype(F32)
        sg = _sig(gate)
        return accs[0] * up * (sg * (1.0 + gate * (1.0 - sg))), accs[0] * gate * sg

    d_gate, d_up = riders.host("dx_down", 2, functools.partial(
        _matmul, "dx_down", "nt", [dx2b], [w["d"]], [(0, 0, 0)], 1, t, f, d, [BF16, BF16], epi=epi_dglu,
        extras=[("mn", s["gate"]), ("mn", s["up"])], tm=512, tn=_ffn_tn(f), cols_outer=True))
    g["d"] = _mm("gw_down", "tn", s["act"], dx2b, f, d, t, F32, tm=_ffn_tn(f))
    g["g"] = _mm("gw_gate", "tn", s["hf"], d_gate, d, f, t, F32, tn=_ffn_tn(f))
    g["u"] = _mm("gw_up", "tn", s["hf"], d_up, d, f, t, F32, tn=_ffn_tn(f))
    d_hf, = riders.host("dx_gate_up", 1, functools.partial(
        _matmul, "dx_gate_up", "nt", [d_gate, d_up], [w["g"], w["u"]], [(0, 0, 0), (1, 1, 0)], 1, t, d, f, [BF16]))
    dx1, dx1b, g["n_ffn"] = _rms_bwd("rms_ffn_bwd", d_hf, s["x1"], w["n_ffn"], dx2, t, d)

    g["o"] = _mm("gw_out", "tn", s["merged"], dx1b, d, d, t, F32)

    def epi_dmerge(accs, ex):
        dmv = accs[0]
        sgc, sga = _sig(ex[0].astype(F32)), _sig(ex[1].astype(F32))
        ycv, yav = ex[2].astype(F32), ex[3].astype(F32)
        return dmv * sgc, dmv * sga, dmv * ycv * sgc * (1.0 - sgc), dmv * yav * sga * (1.0 - sga)

    d_yc, d_ya, d_gc, d_ga = _matmul(
        "dx_out", "nt", [dx1b], [w["o"]], [(0, 0, 0)], 1, t, d, d, [BF16] * 4, epi=epi_dmerge,
        extras=[("mn", (s["proj"], dm.gc_off)), ("mn", (s["proj"], dm.ga_off)), ("mn", s["yc"]), ("mn", s["ya"])],
        tm=512, tn=_tile(d, 512))
    g["co"] = _mm("gw_conv_out", "tn", s["u2"], d_yc, c, d, t, F32)
    d_u2 = _mm("dx_conv_out", "nt", d_yc, w["co"], t, c, d, BF16)
    g["ao"] = _mm("gw_attn_out", "tn", s["o"], d_ya, dm.aw, d, t, F32)
    d_o = _mm("dx_attn_out", "nt", d_ya, w["ao"], t, dm.aw, d, BF16)

    qb, da = _attn_prep_bwd(s["qa"], s["lse"], s["o"], d_o, t, dm.nh, dm.hd)
    dq, dk, dv, dck, dcq = riders.host(
        "attn_bwd", 5, functools.partial(_attn_bwd, qb, s["ka"], s["va"], da, t, dm.nh, dm.hd), ctx=g)
    d_ft, g_bf = _forget_bwd(dck.reshape(dm.nh, t) + dcq[:, :, 0], s["f_t"], w["b_f"], dm.nh, t)
    g["b_f"] = g_bf[:, 0]
    d_f = jnp.pad(d_ft.T, ((0, 0), (0, LANES - dm.nh))).astype(BF16)

    d_a, d_gg, g["conv_w"], g["conv_b"], g["ln_g"], g["ln_b"] = _conv_bwd(
        s["proj"], s["cpre"], d_u2, w["conv_w"], w["ln_g"], w["ln_b"], t, c, dm.a_off, dm.g_off)

    d_proj = jnp.concatenate([d_a, d_gg, dq, dk, dv, d_gc, d_ga], axis=1)
    g["main"], = riders.host("gw_in", 1, functools.partial(
        _matmul, "gw_in", "tn", [s["h"]], [d_proj], [(0, 0, 0)], 1, d, dm.n_main, t, [F32]))
    g["f"] = _mm("gw_forget", "tn", s["h"], d_f, d, LANES, t, F32)
    d_h_f = _mm("dx_forget", "nt", d_f, w["f"], t, d, LANES, BF16)
    d_h, = riders.host("dx_in", 1, functools.partial(
        _matmul, "dx_in", "nt", [d_proj], [w["main"]], [(0, 0, 0)], 1, t, d, dm.n_main, [BF16],
        epi=lambda accs, ex: [accs[0] + ex[0].astype(F32)], extras=[("mn", d_h_f)]))
    dx, dxb, g["n_mix"] = _rms_bwd("rms_mix_bwd", d_h, s["x"], w["n_mix"], dx1, t, d)
    return dx, dxb, g


def _adamw_tiles(wv, gv, mv, vv):
    m_new = ADAM_B1 * mv + (1.0 - ADAM_B1) * gv
    v_new = ADAM_B2 * vv + (1.0 - ADAM_B2) * (gv * gv)
    m_hat = m_new / (1.0 - ADAM_B1 ** ADAM_STEP)
    v_hat = v_new / (1.0 - ADAM_B2 ** ADAM_STEP)
    delta = -ADAM_LR * (m_hat / (jnp.sqrt(v_hat) + ADAM_EPS) + ADAM_WD * wv)
    return delta, m_new, v_new


def _adamw(name, wv, gv, mv, vv):
    shape = wv.shape
    cols = shape[-1]
    rows = wv.size // cols
    tm = _tile(rows, max(8, (1 << 18) // cols), 8)
    flat = [a.reshape(rows, cols) for a in (wv, gv, mv, vv)]
    outs = _ew(name, _adamw_tiles, flat, rows, cols, [F32, F32, F32], tm)
    return [o.reshape(shape) for o in outs]


def _place():
    return lax.axis_index("x"), lax.axis_index("y"), lax.axis_index("c")


def _other_chips(x, y):
    return [(1 - x, y), (x, 1 - y), (1 - x, 1 - y)]


def _window(ref, kind, chip, size):
    if kind == "chip":
        return ref.at[chip]
    if kind == "row":
        return ref.at[pl.ds(chip * size, size), :]
    return ref.at[:, pl.ds(pl.multiple_of(chip * size, LANES), size)]


class _Side:
    def __init__(self, ins, out_shapes, sem_counts, start, finish):
        self.ins, self.out_shapes, self.sem_counts, self.start, self.finish = ins, out_shapes, sem_counts, start, finish

    def join(self, other):
        ni, no, ns = len(self.ins), len(self.out_shapes), len(self.sem_counts)

        def both(first, second):
            def run(ins, outs, sems):
                first(ins[:ni], outs[:no], sems[:ns])
                second(ins[ni:], outs[no:], sems[ns:])
            return run

        return _Side(self.ins + other.ins, self.out_shapes + other.out_shapes, self.sem_counts + other.sem_counts,
                     both(self.start, other.start), both(self.finish, other.finish))


def _run_side(name, side):
    n_in, n_out = len(side.ins), len(side.out_shapes)

    def body(*refs):
        ins, outs, sems = refs[:n_in], refs[n_in:n_in + n_out], refs[n_in + n_out:]
        side.start(ins, outs, sems)
        side.finish(ins, outs, sems)

    return pl.pallas_call(
        body,
        name=name,
        in_specs=[ANY] * n_in,
        out_specs=[ANY] * n_out,
        out_shape=list(side.out_shapes),
        scratch_shapes=[pltpu.SemaphoreType.DMA((cnt,)) for cnt in side.sem_counts],
    )(*side.ins)


def _full_shape(shard, kind):
    _, a, b = shard.shape
    if kind == "chip":
        return (N_CHIPS, a, b)
    return (N_CHIPS * a, b) if kind == "row" else (a, N_CHIPS * b)


def _gather_side(shards, kinds, layer):
    n = len(shards)
    sizes = [s.shape[1] if k == "row" else s.shape[2] for s, k in zip(shards, kinds)]

    def copies(ins, outs, sems, forwards):
        ici_send, ici_recv, d2d_send, d2d_recv, own_send, own_recv = sems
        x, y, c = _place()
        me = 2 * x + y
        sibling = (x, y, 1 - c)
        own, ici, landed, fwd = [], [], [], []
        for w in range(n):
            mine = _window(outs[w], kinds[w], me, sizes[w])
            own.append(pltpu.make_async_remote_copy(
                src_ref=ins[w].at[layer], dst_ref=mine, send_sem=own_send.at[w], recv_sem=own_recv.at[w],
                device_id=sibling, device_id_type=MESH))
            for r, (px, py) in enumerate(_other_chips(x, y)):
                ici.append(pltpu.make_async_remote_copy(
                    src_ref=ins[w].at[layer], dst_ref=mine, send_sem=ici_send.at[3 * w + r],
                    recv_sem=ici_recv.at[3 * w + r], device_id=(px, py, layer), device_id_type=MESH))
                if forwards:
                    slab = _window(outs[w], kinds[w], 2 * px + py, sizes[w])
                    landed.append(pltpu.make_async_remote_copy(
                        src_ref=slab, dst_ref=slab, send_sem=ici_send.at[3 * w + r], recv_sem=ici_recv.at[3 * w + r],
                        device_id=(px, py, layer), device_id_type=MESH))
                    fwd.append(pltpu.make_async_remote_copy(
                        src_ref=slab, dst_ref=slab, send_sem=d2d_send.at[3 * w + r], recv_sem=d2d_recv.at[3 * w + r],
                        device_id=sibling, device_id_type=MESH))
        return c, own, ici, landed, fwd

    def start(ins, outs, sems):
        c, own, ici, _, _ = copies(ins, outs, sems, False)
        for cp in own:
            cp.start()

        @pl.when(c == layer)
        def _():
            for cp in ici:
                cp.start()

    def finish(ins, outs, sems):
        c, own, ici, landed, fwd = copies(ins, outs, sems, True)

        @pl.when(c == layer)
        def _():
            for got, cp in zip(landed, fwd):
                got.wait_recv()
                cp.start()
            for cp in ici + fwd:
                cp.wait_send()

        @pl.when(c != layer)
        def _():
            for cp in fwd:
                cp.wait_recv()

        for cp in own:
            cp.wait()

    out_shapes = [jax.ShapeDtypeStruct(_full_shape(s, k), s.dtype) for s, k in zip(shards, kinds)]
    return _Side(list(shards), out_shapes, [3 * n] * 4 + [n] * 2, start, finish)


def _swap_side(grads, layer):
    n = len(grads)

    def copies(ins, outs, sems):
        x, y, c = _place()
        return c, [pltpu.make_async_remote_copy(src_ref=ins[w], dst_ref=outs[w], send_sem=sems[0].at[w],
                                                recv_sem=sems[1].at[w], device_id=(x, y, layer), device_id_type=MESH)
                   for w in range(n)]

    def start(ins, outs, sems):
        c, cps = copies(ins, outs, sems)

        @pl.when(c != layer)
        def _():
            for cp in cps:
                cp.start()

    def finish(ins, outs, sems):
        c, cps = copies(ins, outs, sems)

        @pl.when(c != layer)
        def _():
            for cp in cps:
                cp.wait_send()

        @pl.when(c == layer)
        def _():
            for cp in cps:
                cp.wait_recv()

    return _Side(list(grads), [jax.ShapeDtypeStruct(a.shape, a.dtype) for a in grads], [n, n], start, finish)


def _add_core_partials(name, mine, got):
    shape = got.shape
    cols = shape[-1]
    rows = got.size // cols
    tm, tn = _tile(rows, 256, 8), _tile(cols, 2048)
    return _ew(name, lambda a, b: [a + b], [mine.reshape(rows, cols), got.reshape(rows, cols)], rows, cols,
               [BF16], tm, tn)[0].reshape(shape)


def _exchange_side(parts, kinds, sizes, layer):
    n = len(parts)

    def copies(ins, outs, sems):
        x, y, c = _place()
        cps = []
        for w in range(n):
            for r, (px, py) in enumerate(_other_chips(x, y)):
                cps.append(pltpu.make_async_remote_copy(
                    src_ref=_window(ins[w], kinds[w], 2 * px + py, sizes[w]), dst_ref=outs[w].at[r],
                    send_sem=sems[0].at[3 * w + r], recv_sem=sems[1].at[3 * w + r],
                    device_id=(px, py, layer), device_id_type=MESH))
        return c, cps

    def start(ins, outs, sems):
        c, cps = copies(ins, outs, sems)

        @pl.when(c == layer)
        def _():
            for cp in cps:
                cp.start()

    def finish(ins, outs, sems):
        c, cps = copies(ins, outs, sems)

        @pl.when(c == layer)
        def _():
            for cp in cps:
                cp.wait()

    out_shapes = [jax.ShapeDtypeStruct((3,) + tuple(_shard_shape(p, k, s)), p.dtype)
                  for p, k, s in zip(parts, kinds, sizes)]
    return _Side(list(parts), out_shapes, [3 * n, 3 * n], start, finish)


def _shard_shape(whole, kind, size):
    if kind == "chip":
        return whole.shape[1:]
    return (size, whole.shape[1]) if kind == "row" else (whole.shape[0], size)


def _sum_chip_partials(name, part, got, kind, size, chip, layer, both=None):
    rows, cols = _shard_shape(part, kind, size)
    tm = _tile(rows, max(8, (1 << 19) // cols), 16)

    def body(chip_ref, part_ref, g0_ref, g1_ref, g2_ref, *rest):
        total = part_ref[...].astype(F32)
        for ref in (g0_ref, g1_ref, g2_ref):
            total = total + ref[...].astype(F32)
        rest[-1][...] = total

    if kind == "chip":
        mine = pl.BlockSpec((None, tm, cols), lambda i, chip_ref: (chip_ref[0], i, 0))
    elif kind == "row":
        mine = pl.BlockSpec((tm, cols), lambda i, chip_ref: (chip_ref[0] * (rows // tm) + i, 0))
    else:
        mine = pl.BlockSpec((tm, cols), lambda i, chip_ref: (i, chip_ref[0]))
    theirs = [pl.BlockSpec((None, tm, cols), functools.partial(lambda r, i, chip_ref: (r, i, 0), r))
              for r in range(3)]
    kept = [] if both is None else [both]
    return pl.pallas_call(
        body,
        name=name,
        grid_spec=pltpu.PrefetchScalarGridSpec(
            num_scalar_prefetch=1, grid=(rows // tm,), in_specs=[mine] + theirs + [ANY] * len(kept),
            out_specs=pl.BlockSpec((None, tm, cols), lambda i, chip_ref: (layer, i, 0))),
        out_shape=jax.ShapeDtypeStruct((2, rows, cols), F32),
        input_output_aliases={5: 0} if kept else {},
        compiler_params=_params("parallel"),
    )(chip.reshape(1), part, got, got, got, *kept)


def _share_reduced(both):
    n = len(both)

    def body(*refs):
        ins, outs = refs[:n], refs[n:2 * n]
        send_sems, recv_sems = refs[2 * n:]
        x, y, c = _place()
        sends = []
        for w in range(n):
            cp = pltpu.make_async_remote_copy(src_ref=ins[w].at[c], dst_ref=outs[w].at[c], send_sem=send_sems.at[w],
                                              recv_sem=recv_sems.at[w], device_id=(x, y, 1 - c), device_id_type=MESH)
            cp.start()
            sends.append(cp)
        for w in range(n):
            got = outs[w].at[1 - c]
            pltpu.make_async_remote_copy(src_ref=got, dst_ref=got, send_sem=send_sems.at[w], recv_sem=recv_sems.at[w],
                                         device_id=(x, y, 1 - c), device_id_type=MESH).wait_recv()
        for cp in sends:
            cp.wait_send()

    return pl.pallas_call(
        body,
        name="share_reduced",
        in_specs=[ANY] * n,
        out_specs=[ANY] * n,
        out_shape=[jax.ShapeDtypeStruct(a.shape, a.dtype) for a in both],
        input_output_aliases={w: w for w in range(n)},
        scratch_shapes=[pltpu.SemaphoreType.DMA((n,)), pltpu.SemaphoreType.DMA((n,))],
    )(*both)


def _allreduce_small(v):
    rows, width = v.shape

    def body(v_ref, out_ref, slots, send_sems, recv_sems):
        x, y, c = _place()
        me = 4 * x + 2 * y + c
        slots[me] = v_ref[...]
        peers = [(x, y, 1 - c)]
        for px, py in _other_chips(x, y):
            peers += [(px, py, c), (px, py, 1 - c)]
        sends = []
        for r, peer in enumerate(peers):
            cp = pltpu.make_async_remote_copy(
                src_ref=v_ref, dst_ref=slots.at[me], send_sem=send_sems.at[r], recv_sem=recv_sems.at[r],
                device_id=peer, device_id_type=MESH)
            cp.start()
            sends.append(cp)
        for r, (px, py, pc) in enumerate(peers):
            got = slots.at[4 * px + 2 * py + pc]
            pltpu.make_async_remote_copy(
                src_ref=got, dst_ref=got, send_sem=send_sems.at[r], recv_sem=recv_sems.at[r],
                device_id=(px, py, pc), device_id_type=MESH).wait_recv()
        for cp in sends:
            cp.wait_send()
        total = slots[0]
        for n in range(1, 8):
            total = total + slots[n]
        out_ref[...] = total

    vm = pl.BlockSpec(memory_space=pltpu.VMEM)
    return pl.pallas_call(
        body,
        name="allreduce_small",
        in_specs=[vm],
        out_specs=vm,
        out_shape=jax.ShapeDtypeStruct((rows, width), F32),
        scratch_shapes=[pltpu.VMEM((8, rows, width), F32), pltpu.SemaphoreType.DMA((7,)),
                        pltpu.SemaphoreType.DMA((7,))],
    )(v)


def _pad_rows(flat, row_align):
    n = flat.shape[0]
    rows = -(-n // PACK_W)
    rows = -(-rows // row_align) * row_align
    return jnp.pad(flat, (0, rows * PACK_W - n)).reshape(rows, PACK_W)


def _unpack(buf, shapes):
    flat = buf.reshape(-1)
    out, off = [], 0
    for shp in shapes:
        n = math.prod(shp)
        out.append(flat[off:off + n].reshape(shp))
        off += n
    return out


BIG = (("w_in", "chip"), ("w_conv_out", "col"), ("w_attn_out", "col"), ("w_out", "row"), ("w_gate_up", "col"),
       ("w_down", "row"), ("w_ple_gate", "row"), ("w_ple_proj", "col"))


def kernel(x, p, norm_mix_g, w_in, b_forget, conv_w, conv_b, conv_ln_g, conv_ln_b, w_conv_out, w_attn_out, w_out, norm_ffn_g, w_gate_up, w_down, norm_ple_g, w_ple_gate, w_ple_proj, final_g, loss_target, m_norm_mix_g, m_w_in, m_b_forget, m_conv_w, m_conv_b, m_conv_ln_g, m_conv_ln_b, m_w_conv_out, m_w_attn_out, m_w_out, m_norm_ffn_g, m_w_gate_up, m_w_down, m_norm_ple_g, m_w_ple_gate, m_w_ple_proj, m_final_g, v_norm_mix_g, v_w_in, v_b_forget, v_conv_w, v_conv_b, v_conv_ln_g, v_conv_ln_b, v_w_conv_out, v_w_attn_out, v_w_out, v_norm_ffn_g, v_w_gate_up, v_w_down, v_norm_ple_g, v_w_ple_gate, v_w_ple_proj, v_final_g):
    wts = dict(norm_mix_g=norm_mix_g, w_in=w_in, b_forget=b_forget, conv_w=conv_w, conv_b=conv_b,
               conv_ln_g=conv_ln_g, conv_ln_b=conv_ln_b, w_conv_out=w_conv_out, w_attn_out=w_attn_out,
               w_out=w_out, norm_ffn_g=norm_ffn_g, w_gate_up=w_gate_up, w_down=w_down, norm_ple_g=norm_ple_g,
               w_ple_gate=w_ple_gate, w_ple_proj=w_ple_proj, final_g=final_g)
    mom1 = dict(norm_mix_g=m_norm_mix_g, w_in=m_w_in, b_forget=m_b_forget, conv_w=m_conv_w, conv_b=m_conv_b,
                conv_ln_g=m_conv_ln_g, conv_ln_b=m_conv_ln_b, w_conv_out=m_w_conv_out, w_attn_out=m_w_attn_out,
                w_out=m_w_out, norm_ffn_g=m_norm_ffn_g, w_gate_up=m_w_gate_up, w_down=m_w_down,
                norm_ple_g=m_norm_ple_g, w_ple_gate=m_w_ple_gate, w_ple_proj=m_w_ple_proj, final_g=m_final_g)
    mom2 = dict(norm_mix_g=v_norm_mix_g, w_in=v_w_in, b_forget=v_b_forget, conv_w=v_conv_w, conv_b=v_conv_b,
                conv_ln_g=v_conv_ln_g, conv_ln_b=v_conv_ln_b, w_conv_out=v_w_conv_out, w_attn_out=v_w_attn_out,
                w_out=v_w_out, norm_ffn_g=v_norm_ffn_g, w_gate_up=v_w_gate_up, w_down=v_w_down,
                norm_ple_g=v_norm_ple_g, w_ple_gate=v_w_ple_gate, w_ple_proj=v_w_ple_proj, final_g=v_final_g)
    order = list(wts)
    depth = w_in.shape[0]
    assert depth == 2, "the exchanges give one layer to each of a chip's two cores"
    t, d = x.shape[1], x.shape[2]
    c = conv_ln_g.shape[1]
    nh = b_forget.shape[1]
    aw = w_attn_out.shape[1]
    f = N_CHIPS * w_down.shape[1]
    pd = w_ple_proj.shape[1]
    dm = _Dims(t, d, c, nh, aw, f, pd)
    n_split = 2 * c + 3 * aw
    cw = conv_w.shape[2]
    chip = 2 * lax.axis_index("x") + lax.axis_index("y")
    big_names = [name for name, _ in BIG]
    big_kinds = [kind for _, kind in BIG]

    shards = {name: wts[name].astype(BF16) for name in big_names}
    shards["conv_w"] = conv_w
    kind_of = dict(BIG, conv_w="chip")
    full = [{}, {}]

    def gather_rider(names, layer):
        def make_side(ctx=None):
            return _gather_side([shards[n] for n in names], [kind_of[n] for n in names], layer)

        def take(results):
            full[layer].update(zip(names, results))

        return make_side, take

    make_side, take = gather_rider(["w_in", "conv_w"], 0)
    take(_run_side("gather_layer0_in", make_side()))
    fwd_riders = [_Riders({"attn_fwd": gather_rider(big_names[1:], 0),
                           "mm_in": gather_rider(["w_in", "w_conv_out", "w_out", "conv_w"], 1),
                           "mm_gate_up": gather_rider(["w_gate_up", "w_ple_gate", "w_attn_out"], 1),
                           "mm_down": gather_rider(["w_down", "w_ple_proj"], 1)}),
                  _Riders()]
    views = {"co": ("w_conv_out", 0), "ao": ("w_attn_out", 0), "o": ("w_out", 0), "g": ("w_gate_up", 0),
             "u": ("w_gate_up", f), "d": ("w_down", 0), "pg": ("w_ple_gate", 0), "pp": ("w_ple_proj", 0)}

    def layer_weights(l):
        fw = full[l]
        wi = jnp.concatenate([fw["w_in"][k] for k in range(N_CHIPS)], axis=1)
        cwl = jnp.concatenate([fw["conv_w"][k] for k in range(N_CHIPS)], axis=1)
        return _Weights({
            "main": jnp.concatenate([wi[:, :n_split], wi[:, n_split + nh:]], axis=1),
            "f": jnp.pad(wi[:, n_split:n_split + nh], ((0, 0), (0, LANES - nh))),
            "conv_w": jnp.pad(cwl, ((0, CONV_HALO - CONV_K), (0, 0))),
            "conv_b": conv_b[l][None], "ln_g": conv_ln_g[l][None], "ln_b": conv_ln_b[l][None],
            "b_f": jnp.broadcast_to(b_forget[l][:, None], (nh, LANES)),
            "n_mix": norm_mix_g[l][None], "n_ffn": norm_ffn_g[l][None], "n_ple": norm_ple_g[l][None],
        }, fw, views)

    xl = x[0]
    saved, lw = [], []
    for l in range(depth):
        lw.append(layer_weights(l))
        xl, s = _layer_fwd(dm, xl, p[l, 0], lw[l], fwd_riders[l])
        saved.append(s)
    loss_row, dx, dxb, g_final = _loss_head(xl, final_g[None], loss_target[0], t, d)
    loss = lax.psum(loss_row[0, 0], ("x", "y", "c"))

    sizes = [wts[name].shape[1] if kind == "row" else wts[name].shape[2] for name, kind in BIG]
    everything = list(range(len(BIG)))
    plain = {1: "co", 2: "ao", 3: "o", 5: "d", 6: "pg", 7: "pp"}

    def grad_of(g, w):
        if w == 0:
            w_in_g = jnp.concatenate([g["main"][:, :n_split], g["f"][:, :nh], g["main"][:, n_split:]], axis=1)
            return w_in_g.reshape(d, N_CHIPS, -1).transpose(1, 0, 2)
        return jnp.concatenate([g["g"], g["u"]], axis=1) if w == 4 else g[plain[w]]

    def add_partials(layer, group, mine, got):
        return {w: _add_core_partials(f"add_core_partials_l{layer}_{big_names[w]}", a, b)
                for w, a, b in zip(group, mine, got)}

    def exchange(parts, group, layer):
        return _exchange_side([parts[w] for w in group], [big_kinds[w] for w in group], [sizes[w] for w in group], layer)

    def sum_partials(layer, parts, got, both):
        return [_sum_chip_partials(f"sum_chip_partials_l{layer}_{big_names[w]}", parts[w], got[w], big_kinds[w],
                                   sizes[w], chip, layer, both=None if both is None else both[w])
                for w in everything]

    lg = [None] * depth
    dx, dxb, lg[1] = _layer_bwd(dm, dx, dxb, saved[1], lw[1], _Riders())
    g1 = [grad_of(lg[1], w) for w in everything]
    parts = [{}, {}]
    got = [{}, {}]
    early = everything[1:]

    def exchange_rider(group, layer):
        return (lambda ctx: exchange(parts[layer], group, layer)), (lambda res: got[layer].update(zip(group, res)))

    early_grads = []

    def attn_bwd_side(g):
        early_grads.extend(grad_of(g, w) for w in early)
        return exchange(parts[1], [0, 1, 2, 3], 1).join(_swap_side(early_grads, 0))

    def attn_bwd_take(res):
        got[1].update(zip([0, 1, 2, 3], res[:4]))
        parts[0].update(add_partials(0, early, early_grads, res[4:]))

    bwd_riders = _Riders({
        "dx_down": (lambda ctx: _swap_side(g1, 1), lambda res: parts[1].update(add_partials(1, everything, g1, res))),
        "dx_gate_up": exchange_rider([4, 5, 6, 7], 1),
        "attn_bwd": (attn_bwd_side, attn_bwd_take),
        "gw_in": exchange_rider([4], 0),
        "dx_in": exchange_rider([1, 2, 3, 5, 6, 7], 0)})
    dx, dxb, lg[0] = _layer_bwd(dm, dx, dxb, saved[0], lw[0], bwd_riders)
    both = sum_partials(1, parts[1], got[1], None)

    w_in_g = [grad_of(lg[0], 0)]
    parts[0].update(add_partials(0, [0], w_in_g, _run_side("swap_layer0_in_grads", _swap_side(w_in_g, 0))))
    got[0].update(zip([0], _run_side("exchange_layer0_in_grads", exchange(parts[0], [0], 0))))
    both = sum_partials(0, parts[0], got[0], both)
    grads = dict(zip(big_names, _share_reduced(both)))

    def stacked(key):
        return jnp.stack([lg[l][key] for l in range(depth)])

    small = ["norm_mix_g", "b_forget", "conv_b", "conv_ln_g", "conv_ln_b", "norm_ffn_g", "norm_ple_g", "final_g"]
    small_g = {
        "norm_mix_g": jnp.concatenate([lg[l]["n_mix"] for l in range(depth)]),
        "b_forget": stacked("b_f"),
        "conv_b": jnp.concatenate([lg[l]["conv_b"] for l in range(depth)]),
        "conv_ln_g": jnp.concatenate([lg[l]["ln_g"] for l in range(depth)]),
        "conv_ln_b": jnp.concatenate([lg[l]["ln_b"] for l in range(depth)]),
        "norm_ffn_g": jnp.concatenate([lg[l]["n_ffn"] for l in range(depth)]),
        "norm_ple_g": jnp.concatenate([lg[l]["n_ple"] for l in range(depth)]),
        "final_g": g_final[0],
    }
    conv_w_g = jnp.stack([lg[l]["conv_w"][:CONV_K] for l in range(depth)])
    small_shapes = [wts[n].shape for n in small] + [conv_w_g.shape]
    small_pack = _pad_rows(jnp.concatenate([small_g[n].reshape(-1) for n in small] + [conv_w_g.reshape(-1)]), 8)
    small_sum = _unpack(_allreduce_small(small_pack), small_shapes)
    for n, name in enumerate(small):
        grads[name] = small_sum[n]
    grads["conv_w"] = lax.dynamic_slice_in_dim(small_sum[len(small)], chip * cw, cw, axis=2)

    def pack_small(src):
        return _pad_rows(jnp.concatenate([src[n].reshape(-1) for n in small]), 8)

    small_upd = _adamw("adamw_small", pack_small(wts), pack_small(grads), pack_small(mom1), pack_small(mom2))
    small_upd = [_unpack(u, [wts[n].shape for n in small]) for u in small_upd]
    delta, new_m, new_v = {}, {}, {}
    for n, name in enumerate(small):
        delta[name], new_m[name], new_v[name] = small_upd[0][n], small_upd[1][n], small_upd[2][n]
    for name in big_names + ["conv_w"]:
        delta[name], new_m[name], new_v[name] = _adamw("adamw_" + name, wts[name], grads[name], mom1[name],
                                                       mom2[name])

    return (loss, dx[None], *[grads[n] for n in order], *[delta[n] for n in order],
            *[new_m[n] for n in order], *[new_v[n] for n in order])
```

```python
import functools
import math

import jax
import jax.numpy as jnp
from jax import lax
from jax.experimental import pallas as pl
from jax.experimental.pallas import tpu as pltpu

F32 = jnp.float32
BF16 = jnp.bfloat16

EPS = 1e-6
CONV_K = 31
NEG_INF = -1e30
LOG2E = 1.4426950408889634
ADAM_LR = 0.001
ADAM_B1 = 0.9
ADAM_B2 = 0.999
ADAM_EPS = 1e-08
ADAM_WD = 0.01
ADAM_STEP = 10

LANES = 128
VMEM_LIMIT = 60 * 1024 * 1024
PACK_W = 1024
N_CHIPS = 4
CONV_HALO = 32
CUM_BLOCK = 256

MM_TM = 1024
MM_TN = 1024
MM_TK = 2048
ROW_TILE = 256
CONV_TILE = 256
ATTN_TILE = 1024
ATTN_SUB = 1024

MESH = pl.DeviceIdType.MESH
ANY = pl.BlockSpec(memory_space=pl.ANY)


def _params(*sem):
    return pltpu.CompilerParams(dimension_semantics=sem, vmem_limit_bytes=VMEM_LIMIT)


def _tile(dim, pref, align=LANES):
    if dim <= pref:
        return dim
    t = (pref // align) * align
    while t >= align:
        if dim % t == 0:
            return t
        t -= align
    return dim


def _sig(x):
    return 1.0 / (1.0 + jnp.exp(-x))


def _op(a):
    if not isinstance(a, tuple):
        return a, 0, ()
    return a if len(a) == 3 else (a[0], a[1], ())


def _spec(block, index, lead):
    if not lead:
        return pl.BlockSpec(block, index)
    return pl.BlockSpec((None,) * len(lead) + block, lambda *g: tuple(lead) + index(*g))


_DIMS = {
    "nn": (((1,), (0,)), ((), ())),
    "nt": (((1,), (1,)), ((), ())),
    "tn": (((0,), (0,)), ((), ())),
}


def _matmul(name, mode, a_ops, b_ops, terms, n_acc, m, n, k, out_dtypes, epi=None, extras=(),
            tm=None, tn=None, tk=None, side=None, cols_outer=False):
    tm = tm or _tile(m, MM_TM)
    tn = tn or _tile(n, MM_TN)
    tk = tk or _tile(k, MM_TK)
    ni, nj, nk = m // tm, n // tn, k // tk
    a_ops = [_op(a) for a in a_ops]
    b_ops = [_op(b) for b in b_ops]
    extras = [(kind, _op(e)) for kind, e in extras]
    na, nb, ne, no = len(a_ops), len(b_ops), len(extras), len(out_dtypes)
    n_acc_refs = n_acc if nk > 1 else 0
    s_ins = list(side.ins) if side else []
    s_outs = list(side.out_shapes) if side else []
    s_sems = [pltpu.SemaphoreType.DMA((cnt,)) for cnt in side.sem_counts] if side else []

    def at(index):
        return (lambda jj, ii, kk: index(ii, jj, kk)) if cols_outer else index

    def a_spec(off, lead):
        if mode == "tn":
            assert off % tm == 0
            return _spec((tk, tm), at(lambda i, j, kk: (kk, i + off // tm)), lead)
        assert off % tk == 0
        return _spec((tm, tk), at(lambda i, j, kk: (i, kk + off // tk)), lead)

    def b_spec(off, lead):
        if mode == "nt":
            assert off % tk == 0
            return _spec((tn, tk), at(lambda i, j, kk: (j, kk + off // tk)), lead)
        assert off % tn == 0
        return _spec((tk, tn), at(lambda i, j, kk: (kk, j + off // tn)), lead)

    def e_spec(kind, off, lead):
        assert off % tn == 0
        if kind == "n":
            return _spec((1, tn), at(lambda i, j, kk: (0, j + off // tn)), lead)
        return _spec((tm, tn), at(lambda i, j, kk: (i, j + off // tn)), lead)

    def body(*refs):
        refs = list(refs)
        a_refs, b_refs, e_refs, si_refs, o_refs, so_refs, acc_refs, sem_refs = (
            [refs.pop(0) for _ in range(cnt)]
            for cnt in (na, nb, ne, len(s_ins), no, len(s_outs), n_acc_refs, len(s_sems)))
        outer, inner, kk = pl.program_id(0), pl.program_id(1), pl.program_id(2)
        if side:
            @pl.when((outer == 0) & (inner == 0) & (kk == 0))
            def _():
                side.start(si_refs, so_refs, sem_refs)

        av = [r[...].astype(BF16) for r in a_refs]
        bv = [r[...].astype(BF16) for r in b_refs]
        sums = [None] * n_acc
        for ai, bi, ci in terms:
            part = lax.dot_general(av[ai], bv[bi], _DIMS[mode], preferred_element_type=F32)
            sums[ci] = part if sums[ci] is None else sums[ci] + part

        def finish(accs):
            outs = epi(accs, [e[...] for e in e_refs]) if epi is not None else accs
            for o, val in zip(o_refs, outs):
                o[...] = val.astype(o.dtype)

        if nk == 1:
            finish(sums)
        else:
            @pl.when(kk == 0)
            def _():
                for acc, part in zip(acc_refs, sums):
                    acc[...] = part

            @pl.when(kk > 0)
            def _():
                for acc, part in zip(acc_refs, sums):
                    acc[...] += part

            @pl.when(kk == nk - 1)
            def _():
                finish([acc[...] for acc in acc_refs])

        if side:
            @pl.when((outer == grid[0] - 1) & (inner == grid[1] - 1) & (kk == nk - 1))
            def _():
                side.finish(si_refs, so_refs, sem_refs)

    grid = (nj, ni, nk) if cols_outer else (ni, nj, nk)
    order = ("arbitrary",) * 3 if side else ("parallel", "parallel", "arbitrary")
    outs = pl.pallas_call(
        body,
        name=name,
        grid=grid,
        in_specs=[a_spec(off, lead) for _, off, lead in a_ops] + [b_spec(off, lead) for _, off, lead in b_ops]
        + [e_spec(kind, off, lead) for kind, (_, off, lead) in extras] + [ANY] * len(s_ins),
        out_specs=[pl.BlockSpec((tm, tn), at(lambda i, j, kk: (i, j))) for _ in out_dtypes] + [ANY] * len(s_outs),
        out_shape=[jax.ShapeDtypeStruct((m, n), dt) for dt in out_dtypes] + s_outs,
        scratch_shapes=[pltpu.VMEM((tm, tn), F32) for _ in range(n_acc_refs)] + s_sems,
        compiler_params=_params(*order),
    )(*[a[0] for a in a_ops], *[b[0] for b in b_ops], *[e[0] for _, e in extras], *s_ins)
    return outs


def _mm(name, mode, a, b, m, n, k, out_dtype, **kw):
    return _matmul(name, mode, [a], [b], [(0, 0, 0)], 1, m, n, k, [out_dtype], **kw)[0]


def _ew(name, fn, ins, m, n, out_dtypes, tm, tn=None):
    tn = tn or n
    ins = [_op(a) for a in ins]
    ni = len(ins)

    def spec(off, lead):
        assert off % tn == 0
        return _spec((tm, tn), lambda i, j: (i, j + off // tn), lead)

    def body(*refs):
        outs = fn(*[r[...] for r in refs[:ni]])
        for o, val in zip(refs[ni:], outs):
            o[...] = val.astype(o.dtype)

    return pl.pallas_call(
        body,
        name=name,
        grid=(m // tm, n // tn),
        in_specs=[spec(off, lead) for _, off, lead in ins],
        out_specs=[pl.BlockSpec((tm, tn), lambda i, j: (i, j)) for _ in out_dtypes],
        out_shape=[jax.ShapeDtypeStruct((m, n), dt) for dt in out_dtypes],
        compiler_params=_params("parallel", "parallel"),
    )(*[a[0] for a in ins])


def _rms_fwd(name, x, g, t, d):
    tr = _tile(t, ROW_TILE, 8)

    def body(x_ref, g_ref, h_ref):
        xv = x_ref[...]
        r = lax.rsqrt(jnp.mean(xv * xv, axis=1, keepdims=True) + EPS)
        h_ref[...] = (xv * r * g_ref[...]).astype(BF16)

    return pl.pallas_call(
        body,
        name=name,
        grid=(t // tr,),
        in_specs=[pl.BlockSpec((tr, d), lambda i: (i, 0)), pl.BlockSpec((1, d), lambda i: (0, 0))],
        out_specs=pl.BlockSpec((tr, d), lambda i: (i, 0)),
        out_shape=jax.ShapeDtypeStruct((t, d), BF16),
        compiler_params=_params("parallel"),
    )(x, g)


def _rms_bwd_rows(dh, xv, g):
    r = lax.rsqrt(jnp.mean(xv * xv, axis=1, keepdims=True) + EPS)
    xhat = xv * r
    dxh = dh * g
    dx = r * (dxh - xhat * jnp.mean(dxh * xhat, axis=1, keepdims=True))
    return dx, dh * xhat


def _rms_bwd(name, dh, x, g, dres, t, d):
    tr = _tile(t, ROW_TILE, 8)

    def body(dh_ref, x_ref, g_ref, dres_ref, dx_ref, dxb_ref, dg_ref):
        @pl.when(pl.program_id(0) == 0)
        def _():
            dg_ref[...] = jnp.zeros(dg_ref.shape, F32)

        dx, dg_rows = _rms_bwd_rows(dh_ref[...].astype(F32), x_ref[...], g_ref[...])
        dx = dx + dres_ref[...]
        dx_ref[...] = dx
        dxb_ref[...] = dx.astype(BF16)
        dg_ref[...] += jnp.sum(dg_rows, axis=0, keepdims=True)

    row = pl.BlockSpec((tr, d), lambda i: (i, 0))
    vec = pl.BlockSpec((1, d), lambda i: (0, 0))
    return pl.pallas_call(
        body,
        name=name,
        grid=(t // tr,),
        in_specs=[row, row, vec, row],
        out_specs=[row, row, vec],
        out_shape=[jax.ShapeDtypeStruct((t, d), F32), jax.ShapeDtypeStruct((t, d), BF16),
                   jax.ShapeDtypeStruct((1, d), F32)],
        compiler_params=_params("arbitrary"),
    )(dh, x, g, dres)


def _loss_head(x, g, target, t, d):
    tr = _tile(t, ROW_TILE, 8)

    def body(x_ref, g_ref, tgt_ref, loss_ref, dx_ref, dxb_ref, dg_ref):
        @pl.when(pl.program_id(0) == 0)
        def _():
            dg_ref[...] = jnp.zeros(dg_ref.shape, F32)
            loss_ref[...] = jnp.zeros(loss_ref.shape, F32)

        xv = x_ref[...]
        gv = g_ref[...]
        r = lax.rsqrt(jnp.mean(xv * xv, axis=1, keepdims=True) + EPS)
        err = xv * r * gv - tgt_ref[...]
        loss_ref[...] += (0.5 / d) * jnp.sum(err * err)
        dx, dg_rows = _rms_bwd_rows(err * (1.0 / d), xv, gv)
        dx_ref[...] = dx
        dxb_ref[...] = dx.astype(BF16)
        dg_ref[...] += jnp.sum(dg_rows, axis=0, keepdims=True)

    row = pl.BlockSpec((tr, d), lambda i: (i, 0))
    vec = pl.BlockSpec((1, d), lambda i: (0, 0))
    one = pl.BlockSpec((1, LANES), lambda i: (0, 0))
    return pl.pallas_call(
        body,
        name="loss_head",
        grid=(t // tr,),
        in_specs=[row, vec, row],
        out_specs=[one, row, row, vec],
        out_shape=[jax.ShapeDtypeStruct((1, LANES), F32), jax.ShapeDtypeStruct((t, d), F32),
                   jax.ShapeDtypeStruct((t, d), BF16), jax.ShapeDtypeStruct((1, d), F32)],
        compiler_params=_params("arbitrary"),
    )(x, g, target)


def _layernorm_rows(cv, g, b):
    mu = jnp.mean(cv, axis=1, keepdims=True)
    xc = cv - mu
    rstd = lax.rsqrt(jnp.mean(xc * xc, axis=1, keepdims=True) + EPS)
    xhat = xc * rstd
    return xhat, rstd, xhat * g + b


SUBLANES = 8


def _shift_copies(pad, shifted, span):
    for b in range(1, SUBLANES):
        shifted[b - 1, pl.ds(0, span), :] = pad[pl.ds(b, span), :]


def _tap(pad, shifted, offset, rows):
    b = offset % SUBLANES
    if b == 0:
        return pad[pl.ds(offset, rows), :]
    return shifted[b - 1, pl.ds(offset - b, rows), :]


def _conv_fwd(proj, conv_w, conv_b, ln_g, ln_b, t, c, a_off, g_off):
    tm = _tile(t, CONV_TILE, CONV_HALO)
    per = tm // CONV_HALO
    ab, gb = a_off // c, g_off // c
    span = tm + CONV_HALO - SUBLANES

    def body(a_ref, g_ref, ap_ref, gp_ref, w_ref, cb_ref, lg_ref, lb_ref, u2_ref, c_ref, upad, ushift):
        i = pl.program_id(0)
        u_prev = ap_ref[...].astype(F32) * _sig(gp_ref[...].astype(F32))
        upad[pl.ds(0, CONV_HALO), :] = jnp.where(i > 0, u_prev, 0.0)
        upad[pl.ds(CONV_HALO, tm), :] = a_ref[...].astype(F32) * _sig(g_ref[...].astype(F32))
        _shift_copies(upad, ushift, span)
        acc = jnp.zeros((tm, c), F32) + cb_ref[...]
        for k in range(CONV_K):
            acc = acc + w_ref[pl.ds(k, 1), :] * _tap(upad, ushift, CONV_HALO - (CONV_K - 1) + k, tm)
        c_ref[...] = acc
        _, _, z = _layernorm_rows(acc, lg_ref[...], lb_ref[...])
        u2_ref[...] = (z * _sig(z)).astype(BF16)

    vec = pl.BlockSpec((1, c), lambda i: (0, 0))
    return pl.pallas_call(
        body,
        name="conv_fwd",
        grid=(t // tm,),
        in_specs=[
            pl.BlockSpec((tm, c), lambda i: (i, ab)),
            pl.BlockSpec((tm, c), lambda i: (i, gb)),
            pl.BlockSpec((CONV_HALO, c), lambda i: (jnp.maximum(i * per - 1, 0), ab)),
            pl.BlockSpec((CONV_HALO, c), lambda i: (jnp.maximum(i * per - 1, 0), gb)),
            pl.BlockSpec((CONV_HALO, c), lambda i: (0, 0)), vec, vec, vec,
        ],
        out_specs=[pl.BlockSpec((tm, c), lambda i: (i, 0)), pl.BlockSpec((tm, c), lambda i: (i, 0))],
        out_shape=[jax.ShapeDtypeStruct((t, c), BF16), jax.ShapeDtypeStruct((t, c), F32)],
        scratch_shapes=[pltpu.VMEM((CONV_HALO + tm, c), F32), pltpu.VMEM((SUBLANES - 1, span, c), F32)],
        compiler_params=_params("parallel"),
    )(proj, proj, proj, proj, conv_w, conv_b, ln_g, ln_b)


def _conv_bwd(proj, cpre, du2, conv_w, ln_g, ln_b, t, c, a_off, g_off):
    tm = _tile(t, CONV_TILE, CONV_HALO)
    per = tm // CONV_HALO
    nt = t // tm
    last_halo = t // CONV_HALO - 1
    ab, gb = a_off // c, g_off // c
    span = tm + CONV_HALO - SUBLANES

    def body(a_ref, g_ref, ap_ref, gp_ref, c_ref, cn_ref, du_ref, dun_ref, w_ref, lg_ref, lb_ref,
             da_ref, dg_ref, gw_ref, gcb_ref, glg_ref, glb_ref, upad, dpad, ushift, dshift):
        i = pl.program_id(0)

        @pl.when(i == 0)
        def _():
            gw_ref[...] = jnp.zeros(gw_ref.shape, F32)
            gcb_ref[...] = jnp.zeros(gcb_ref.shape, F32)
            glg_ref[...] = jnp.zeros(glg_ref.shape, F32)
            glb_ref[...] = jnp.zeros(glb_ref.shape, F32)

        lg = lg_ref[...]
        lb = lb_ref[...]

        def ln_bwd(cv, duv):
            xhat, rstd, z = _layernorm_rows(cv, lg, lb)
            sz = _sig(z)
            dz = duv * (sz * (1.0 + z * (1.0 - sz)))
            dxh = dz * lg
            dc = rstd * (dxh - jnp.mean(dxh, axis=1, keepdims=True)
                         - xhat * jnp.mean(dxh * xhat, axis=1, keepdims=True))
            return dc, dz, xhat

        dc, dz, xhat = ln_bwd(c_ref[...], du_ref[...].astype(F32))
        dc_next, _, _ = ln_bwd(cn_ref[...], dun_ref[...].astype(F32))
        glg_ref[...] += jnp.sum(dz * xhat, axis=0, keepdims=True)
        glb_ref[...] += jnp.sum(dz, axis=0, keepdims=True)
        gcb_ref[...] += jnp.sum(dc, axis=0, keepdims=True)
        dpad[pl.ds(0, tm), :] = dc
        dpad[pl.ds(tm, CONV_HALO), :] = jnp.where(i < nt - 1, dc_next, 0.0)

        av = a_ref[...].astype(F32)
        sg = _sig(g_ref[...].astype(F32))
        u_prev = ap_ref[...].astype(F32) * _sig(gp_ref[...].astype(F32))
        upad[pl.ds(0, CONV_HALO), :] = jnp.where(i > 0, u_prev, 0.0)
        upad[pl.ds(CONV_HALO, tm), :] = av * sg
        _shift_copies(upad, ushift, span)
        _shift_copies(dpad, dshift, span)

        du = jnp.zeros((tm, c), F32)
        for k in range(CONV_K):
            du = du + w_ref[pl.ds(k, 1), :] * _tap(dpad, dshift, CONV_K - 1 - k, tm)
            gw_ref[pl.ds(k, 1), :] += jnp.sum(
                dc * _tap(upad, ushift, CONV_HALO - (CONV_K - 1) + k, tm), axis=0, keepdims=True)
        da_ref[...] = (du * sg).astype(BF16)
        dg_ref[...] = (du * av * sg * (1.0 - sg)).astype(BF16)

    vec = pl.BlockSpec((1, c), lambda i: (0, 0))
    cur = pl.BlockSpec((tm, c), lambda i: (i, 0))
    nxt = pl.BlockSpec((CONV_HALO, c), lambda i: (jnp.minimum((i + 1) * per, last_halo), 0))
    wsp = pl.BlockSpec((CONV_HALO, c), lambda i: (0, 0))
    return pl.pallas_call(
        body,
        name="conv_bwd",
        grid=(nt,),
        in_specs=[
            pl.BlockSpec((tm, c), lambda i: (i, ab)),
            pl.BlockSpec((tm, c), lambda i: (i, gb)),
            pl.BlockSpec((CONV_HALO, c), lambda i: (jnp.maximum(i * per - 1, 0), ab)),
            pl.BlockSpec((CONV_HALO, c), lambda i: (jnp.maximum(i * per - 1, 0), gb)),
            cur, nxt, cur, nxt, wsp, vec, vec,
        ],
        out_specs=[cur, cur, wsp, vec, vec, vec],
        out_shape=[jax.ShapeDtypeStruct((t, c), BF16), jax.ShapeDtypeStruct((t, c), BF16),
                   jax.ShapeDtypeStruct((CONV_HALO, c), F32), jax.ShapeDtypeStruct((1, c), F32),
                   jax.ShapeDtypeStruct((1, c), F32), jax.ShapeDtypeStruct((1, c), F32)],
        scratch_shapes=[pltpu.VMEM((CONV_HALO + tm, c), F32), pltpu.VMEM((tm + CONV_HALO, c), F32),
                        pltpu.VMEM((SUBLANES - 1, span, c), F32), pltpu.VMEM((SUBLANES - 1, span, c), F32)],
        compiler_params=_params("arbitrary"),
    )(proj, proj, proj, proj, cpre, cpre, du2, du2, conv_w, ln_g, ln_b)


def _split3(x):
    hi = x.astype(BF16).astype(F32)
    r1 = x - hi
    mid = r1.astype(BF16).astype(F32)
    lo = (r1 - mid).astype(BF16).astype(F32)
    return hi, mid, lo


def _split3_dot(x, tri):
    dot = functools.partial(jnp.dot, preferred_element_type=F32)
    hi, mid, lo = _split3(x)
    return dot(hi.astype(BF16), tri) + dot(mid.astype(BF16), tri) + dot(lo.astype(BF16), tri)


def _to_blocks(a, nb, blk):
    return a.reshape(a.shape[0], nb, blk).transpose(1, 0, 2)


def _from_blocks(a):
    return a.transpose(1, 0, 2).reshape(a.shape[1], -1)


def _forget_fwd(f_t, b_col, nh, t):
    blk = _tile(t, CUM_BLOCK)
    nb = t // blk

    def body(f_ref, b_ref, c_ref):
        ri = lax.broadcasted_iota(jnp.int32, (blk, blk), 0)
        ci = lax.broadcasted_iota(jnp.int32, (blk, blk), 1)
        tri = (ri <= ci).astype(BF16)

        def step(bi, carry):
            xv = f_ref[bi] + b_ref[:, :1]
            lf = jnp.minimum(xv, 0.0) - jnp.log(1.0 + jnp.exp(-jnp.abs(xv)))
            cs = _split3_dot(lf, tri) + carry
            c_ref[bi] = cs
            return cs[:, blk - 1:blk]

        lax.fori_loop(0, nb, step, jnp.zeros((nh, 1), F32))

    out = pl.pallas_call(
        body,
        name="forget_fwd",
        out_shape=jax.ShapeDtypeStruct((nb, nh, blk), F32),
        compiler_params=pltpu.CompilerParams(vmem_limit_bytes=VMEM_LIMIT),
    )(_to_blocks(f_t, nb, blk), b_col)
    return _from_blocks(out)


def _forget_bwd(dc, f_t, b_col, nh, t):
    blk = _tile(t, CUM_BLOCK)
    nb = t // blk

    def body(dc_ref, f_ref, b_ref, df_ref, db_ref):
        ri = lax.broadcasted_iota(jnp.int32, (blk, blk), 0)
        ci = lax.broadcasted_iota(jnp.int32, (blk, blk), 1)
        tri = (ri >= ci).astype(BF16)

        def step(n, carry):
            tail, db = carry
            bi = nb - 1 - n
            rc = _split3_dot(dc_ref[bi], tri) + tail
            df = rc * _sig(-(f_ref[bi] + b_ref[:, :1]))
            df_ref[bi] = df
            return rc[:, 0:1], db + jnp.sum(df, axis=1, keepdims=True)

        _, db = lax.fori_loop(0, nb, step, (jnp.zeros((nh, 1), F32), jnp.zeros((nh, 1), F32)))
        db_ref[...] = jnp.broadcast_to(db, db_ref.shape)

    df, db = pl.pallas_call(
        body,
        name="forget_bwd",
        out_shape=[jax.ShapeDtypeStruct((nb, nh, blk), F32), jax.ShapeDtypeStruct((nh, LANES), F32)],
        compiler_params=pltpu.CompilerParams(vmem_limit_bytes=VMEM_LIMIT),
    )(_to_blocks(dc, nb, blk), _to_blocks(f_t, nb, blk), b_col)
    return _from_blocks(df), db


def _lanes(parts, rows):
    lane = lax.broadcasted_iota(jnp.int32, (rows, LANES), 1)
    out = jnp.zeros((rows, LANES), F32)
    for n, part in enumerate(parts):
        out = jnp.where(lane == n, part, out)
    return out


def _attn_prep_fwd(proj, cs, t, nh, hd, q_off, k_off, v_off):
    tr = _tile(t, ATTN_TILE, 16)
    qb, kb, vb = q_off // hd, k_off // hd, v_off // hd

    def body(q_ref, k_ref, v_ref, cs_ref, qa_ref, ka_ref, va_ref):
        hi, mid, lo = _split3(cs_ref[0][:, :1])
        qa_ref[0, :, :hd] = q_ref[...]
        qa_ref[0, :, hd:] = _lanes([1.0, 1.0, 1.0, hi, mid, lo], tr).astype(BF16)
        ka_ref[0, :, :hd] = k_ref[...]
        ka_ref[0, :, hd:] = _lanes([-hi, -mid, -lo] + [1.0] * 6, tr).astype(BF16)
        va_ref[0, :, :hd] = v_ref[...]
        va_ref[0, :, hd:] = _lanes([-1.0, -1.0, -1.0], tr).astype(BF16)

    wide = pl.BlockSpec((1, tr, 2 * hd), lambda h, i: (h, i, 0))
    return pl.pallas_call(
        body,
        name="attn_prep_fwd",
        grid=(nh, t // tr),
        in_specs=[pl.BlockSpec((tr, hd), lambda h, i: (i, qb + h)), pl.BlockSpec((tr, hd), lambda h, i: (i, kb + h)),
                  pl.BlockSpec((tr, hd), lambda h, i: (i, vb + h)), pl.BlockSpec((1, tr, LANES), lambda h, i: (h, i, 0))],
        out_specs=[wide, wide, wide],
        out_shape=[jax.ShapeDtypeStruct((nh, t, 2 * hd), BF16)] * 3,
        compiler_params=_params("parallel", "parallel"),
    )(proj, proj, proj, cs)


def _attn_prep_bwd(qa, lse, o, do, t, nh, hd):
    tr = _tile(t, ATTN_TILE, 16)
    inv_scale = math.sqrt(hd)

    def body(qa_ref, lse_ref, o_ref, do_ref, qb_ref, da_ref):
        l_hi, l_mid, l_lo = _split3(lse_ref[0][:, :1] * (-inv_scale))
        lane = lax.broadcasted_iota(jnp.int32, (tr, LANES), 1)
        extra = qa_ref[0, :, hd:].astype(F32)
        extra = jnp.where(lane == 6, l_hi, jnp.where(lane == 7, l_mid, jnp.where(lane == 8, l_lo, extra)))
        qb_ref[0, :, :hd] = qa_ref[0, :, :hd]
        qb_ref[0, :, hd:] = extra.astype(BF16)
        dov = do_ref[...]
        delta = jnp.sum(dov.astype(F32) * o_ref[...].astype(F32), axis=1, keepdims=True)
        da_ref[0, :, :hd] = dov
        da_ref[0, :, hd:] = _lanes(list(_split3(delta)), tr).astype(BF16)

    wide = pl.BlockSpec((1, tr, 2 * hd), lambda h, i: (h, i, 0))
    head = pl.BlockSpec((tr, hd), lambda h, i: (i, h))
    return pl.pallas_call(
        body,
        name="attn_prep_bwd",
        grid=(nh, t // tr),
        in_specs=[wide, pl.BlockSpec((1, tr, LANES), lambda h, i: (h, i, 0)), head, head],
        out_specs=[wide, wide],
        out_shape=[jax.ShapeDtypeStruct((nh, t, 2 * hd), BF16)] * 2,
        compiler_params=_params("parallel", "parallel"),
    )(qa, lse, o, do)


def _causal(s, row0, rows, cols):
    row = lax.broadcasted_iota(jnp.int32, (rows, cols), 0) + row0
    col = lax.broadcasted_iota(jnp.int32, (rows, cols), 1)
    return jnp.where(col <= row, s, NEG_INF)


def _side_refs(side):
    if side is None:
        return [], [], []
    return list(side.ins), list(side.out_shapes), [pltpu.SemaphoreType.DMA((cnt,)) for cnt in side.sem_counts]


def _attn_fwd(qa, ka, proj, t, nh, hd, v_off, side=None):
    tq = _tile(t, ATTN_TILE)
    nq = t // tq
    sub = _tile(tq, ATTN_SUB)
    ns = tq // sub
    scale = 1.0 / math.sqrt(hd)
    vb = v_off // hd
    s_ins, s_outs, s_sems = _side_refs(side)

    def body(qa_ref, ka_ref, v_ref, *rest):
        si_refs, (o_ref, lse_ref) = rest[:len(s_ins)], rest[len(s_ins):len(s_ins) + 2]
        so_refs, sem_refs = rest[len(s_ins) + 2:len(s_ins) + 2 + len(s_outs)], rest[len(s_ins) + 2 + len(s_outs):]
        h, i = pl.program_id(0), pl.program_id(1)
        if side:
            @pl.when((h == 0) & (i == 0))
            def _():
                side.start(si_refs, so_refs, sem_refs)

        def tile(j, carry, masked):
            rows = pl.ds(pl.multiple_of(j * tq, tq), tq)
            kj = ka_ref[0, rows, :]
            vj = v_ref[rows, :]
            new = []
            for r in range(ns):
                m, l, acc = carry[r]
                s = lax.dot_general(qa_ref[0, pl.ds(r * sub, sub), :], kj, _DIMS["nt"], preferred_element_type=F32)
                if masked:
                    s = _causal(s, r * sub, sub, tq)
                m_new = jnp.maximum(m, jnp.max(s, axis=1, keepdims=True))
                p = jnp.exp2((s - m_new) * (scale * LOG2E))
                alpha = jnp.exp2((m - m_new) * (scale * LOG2E))
                l = alpha * l + jnp.sum(p, axis=1, keepdims=True)
                acc = alpha * acc + jnp.dot(p.astype(BF16), vj, preferred_element_type=F32)
                new.append((m_new, l, acc))
            return tuple(new)

        init = tuple((jnp.full((sub, 1), NEG_INF, F32), jnp.zeros((sub, 1), F32), jnp.zeros((sub, hd), F32))
                     for _ in range(ns))
        carry = lax.fori_loop(0, i, lambda j, cr: tile(j, cr, False), init)
        carry = tile(i, carry, True)
        for r in range(ns):
            m, l, acc = carry[r]
            o_ref[pl.ds(r * sub, sub), :] = (acc / l).astype(BF16)
            lse_ref[0, pl.ds(r * sub, sub), :] = jnp.broadcast_to(m * scale + jnp.log(l), (sub, LANES))
        if side:
            @pl.when((h == nh - 1) & (i == nq - 1))
            def _():
                side.finish(si_refs, so_refs, sem_refs)

    return pl.pallas_call(
        body,
        name="attn_fwd",
        grid=(nh, nq),
        in_specs=[pl.BlockSpec((1, tq, 2 * hd), lambda h, i: (h, i, 0)),
                  pl.BlockSpec((1, t, 2 * hd), lambda h, i: (h, 0, 0)),
                  pl.BlockSpec((t, hd), lambda h, i: (0, vb + h))] + [ANY] * len(s_ins),
        out_specs=[pl.BlockSpec((tq, hd), lambda h, i: (i, h)),
                   pl.BlockSpec((1, tq, LANES), lambda h, i: (h, i, 0))] + [ANY] * len(s_outs),
        out_shape=[jax.ShapeDtypeStruct((t, nh * hd), BF16), jax.ShapeDtypeStruct((nh, t, LANES), F32)] + s_outs,
        scratch_shapes=s_sems,
        compiler_params=_params(*(("arbitrary",) * 2 if side else ("parallel",) * 2)),
    )(qa, ka, proj, *s_ins)


def _attn_bwd(qb, ka, va, da, t, nh, hd, side=None):
    tq = _tile(t, ATTN_TILE)
    nq = t // tq
    sub = _tile(tq, ATTN_SUB)
    ns = tq // sub
    scale = 1.0 / math.sqrt(hd)
    s_ins, s_outs, s_sems = _side_refs(side)

    def body(qb_ref, ka_ref, va_ref, da_ref, *rest):
        si_refs, rest = rest[:len(s_ins)], rest[len(s_ins):]
        (dq_ref, dk_ref, dv_ref, dck_ref, dcq_ref), rest = rest[:5], rest[5:]
        so_refs, (dq_all, *sem_refs) = rest[:len(s_outs)], rest[len(s_outs):]
        h, j = pl.program_id(0), pl.program_id(1)
        if side:
            @pl.when((h == 0) & (j == 0))
            def _():
                side.start(si_refs, so_refs, sem_refs)

        @pl.when(j == 0)
        def _():
            dq_all[...] = jnp.zeros(dq_all.shape, F32)
            dcq_ref[...] = jnp.zeros(dcq_ref.shape, F32)

        kaj = ka_ref[0]
        vaj = va_ref[0]
        kj = kaj[:, :hd]

        def tile(i, carry, masked):
            dk, dv, dck = carry
            for r in range(ns):
                rows = pl.ds(pl.multiple_of(i * tq + r * sub, sub), sub)
                qr = qb_ref[0, rows, :]
                dr = da_ref[0, rows, :]
                s = lax.dot_general(qr, kaj, _DIMS["nt"], preferred_element_type=F32)
                if masked:
                    s = _causal(s, r * sub, sub, tq)
                p = jnp.exp2(s * (scale * LOG2E))
                ds = p * lax.dot_general(dr, vaj, _DIMS["nt"], preferred_element_type=F32)
                dsb = ds.astype(BF16)
                dv = dv + lax.dot_general(p.astype(BF16), dr[:, :hd], _DIMS["tn"], preferred_element_type=F32)
                dk = dk + lax.dot_general(dsb, qr[:, :hd], _DIMS["tn"], preferred_element_type=F32)
                dck = dck - jnp.sum(ds, axis=0, keepdims=True)
                dq_all[rows, :] += jnp.dot(dsb, kj, preferred_element_type=F32)
                dcq_ref[0, rows, :] += jnp.sum(ds, axis=1, keepdims=True)
            return dk, dv, dck

        carry = (jnp.zeros((tq, hd), F32), jnp.zeros((tq, hd), F32), jnp.zeros((1, tq), F32))
        carry = tile(j, carry, True)
        dk, dv, dck = lax.fori_loop(j + 1, nq, lambda i, cr: tile(i, cr, False), carry)
        dk_ref[...] = (dk * scale).astype(BF16)
        dv_ref[...] = dv.astype(BF16)
        dck_ref[0] = dck

        @pl.when(j == nq - 1)
        def _():
            dq_ref[...] = (dq_all[...] * scale).astype(BF16)

        if side:
            @pl.when((h == nh - 1) & (j == nq - 1))
            def _():
                side.finish(si_refs, so_refs, sem_refs)

    whole = pl.BlockSpec((1, t, 2 * hd), lambda h, j: (h, 0, 0))
    block = pl.BlockSpec((1, tq, 2 * hd), lambda h, j: (h, j, 0))
    return pl.pallas_call(
        body,
        name="attn_bwd",
        grid=(nh, nq),
        in_specs=[whole, block, block, whole] + [ANY] * len(s_ins),
        out_specs=[
            pl.BlockSpec((t, hd), lambda h, j: (0, h)),
            pl.BlockSpec((tq, hd), lambda h, j: (j, h)),
            pl.BlockSpec((tq, hd), lambda h, j: (j, h)),
            pl.BlockSpec((1, 1, tq), lambda h, j: (h, 0, j)),
            pl.BlockSpec((1, t, LANES), lambda h, j: (h, 0, 0)),
        ] + [ANY] * len(s_outs),
        out_shape=[jax.ShapeDtypeStruct((t, nh * hd), BF16), jax.ShapeDtypeStruct((t, nh * hd), BF16),
                   jax.ShapeDtypeStruct((t, nh * hd), BF16), jax.ShapeDtypeStruct((nh, 1, t), F32),
                   jax.ShapeDtypeStruct((nh, t, LANES), F32)] + s_outs,
        scratch_shapes=[pltpu.VMEM((t, hd), F32)] + s_sems,
        compiler_params=_params("arbitrary", "arbitrary"),
    )(qb, ka, va, da, *s_ins)


class _Dims:
    def __init__(self, t, d, c, nh, aw, f, pd):
        self.t, self.d, self.c, self.nh, self.aw, self.f, self.pd = t, d, c, nh, aw, f, pd
        self.hd = aw // nh
        self.a_off, self.g_off = 0, c
        self.q_off, self.k_off, self.v_off = 2 * c, 2 * c + aw, 2 * c + 2 * aw
        self.gc_off = 2 * c + 3 * aw
        self.ga_off = self.gc_off + d
        self.n_main = self.ga_off + d


def _ffn_tn(f):
    return _tile(f, 1536)


class _Riders:
    def __init__(self, plan=None):
        self.plan = plan or {}

    def host(self, name, n_out, call, ctx=None):
        if name not in self.plan:
            return call()
        make_side, take = self.plan[name]
        outs = call(side=make_side(ctx))
        take(outs[n_out:])
        return outs[:n_out]


class _Weights:
    def __init__(self, fixed, gathered, views):
        self.fixed, self.gathered, self.views = fixed, gathered, views

    def __getitem__(self, key):
        if key in self.fixed:
            return self.fixed[key]
        name, off = self.views[key]
        return self.gathered[name], off


def _layer_fwd(dm, x, p, w, riders):
    t, d, c, f = dm.t, dm.d, dm.c, dm.f
    s = {"x": x}
    h = _rms_fwd("rms_mix", x, w["n_mix"], t, d)
    proj, = riders.host("mm_in", 1, functools.partial(
        _matmul, "mm_in", "nn", [h], [w["main"]], [(0, 0, 0)], 1, t, dm.n_main, d, [BF16]))
    fl = _mm("mm_forget", "nn", h, w["f"], t, LANES, d, F32)
    u2, cpre = _conv_fwd(proj, w["conv_w"], w["conv_b"], w["ln_g"], w["ln_b"], t, c, dm.a_off, dm.g_off)
    f_t = fl[:, :dm.nh].T
    cum = _forget_fwd(f_t, w["b_f"], dm.nh, t)
    cs = jnp.broadcast_to((cum * math.sqrt(dm.hd))[:, :, None], (dm.nh, t, LANES))
    qa, ka, va = _attn_prep_fwd(proj, cs, t, dm.nh, dm.hd, dm.q_off, dm.k_off, dm.v_off)
    o, lse = riders.host("attn_fwd", 2, functools.partial(_attn_fwd, qa, ka, proj, t, dm.nh, dm.hd, dm.v_off))

    def epi_conv(accs, ex):
        return accs[0], _sig(ex[0].astype(F32)) * accs[0]

    yc, m1 = _matmul("mm_conv_out", "nn", [u2], [w["co"]], [(0, 0, 0)], 1, t, d, c, [BF16, BF16],
                     epi=epi_conv, extras=[("mn", (proj, dm.gc_off))], tm=512)

    def epi_attn(accs, ex):
        return accs[0], ex[1].astype(F32) + _sig(ex[0].astype(F32)) * accs[0]

    ya, merged = _matmul("mm_attn_out", "nn", [o], [w["ao"]], [(0, 0, 0)], 1, t, d, dm.aw, [BF16, BF16],
                         epi=epi_attn, extras=[("mn", (proj, dm.ga_off)), ("mn", m1)], tm=512)
    x1 = _matmul("mm_out", "nn", [merged], [w["o"]], [(0, 0, 0)], 1, t, d, d, [F32],
                 epi=lambda accs, ex: [ex[0] + accs[0]], extras=[("mn", x)], tm=512)[0]

    hf = _rms_fwd("rms_ffn", x1, w["n_ffn"], t, d)

    def epi_glu(accs, ex):
        gate, up = accs
        return gate, up, gate * _sig(gate) * up

    gate, up, act = riders.host("mm_gate_up", 3, functools.partial(
        _matmul, "mm_gate_up", "nn", [hf], [w["g"], w["u"]], [(0, 0, 0), (0, 1, 1)], 2, t, f, d, [BF16, BF16, BF16],
        epi=epi_glu, tm=512, tn=_ffn_tn(f), cols_outer=True))
    x2, = riders.host("mm_down", 1, functools.partial(
        _matmul, "mm_down", "nn", [act], [w["d"]], [(0, 0, 0)], 1, t, d, f, [F32],
        epi=lambda accs, ex: [ex[0] + accs[0]], extras=[("mn", x1)], tm=512, tk=f, cols_outer=True))

    hp = _rms_fwd("rms_ple", x2, w["n_ple"], t, d)
    pp = _mm("mm_ple_proj", "nn", p, w["pp"], t, d, dm.pd, BF16, tm=512)

    def epi_ple(accs, ex):
        sg = _sig(accs[0])
        return sg, ex[1] + sg * ex[0].astype(F32)

    sg, x3 = _matmul("mm_ple_gate", "nn", [hp], [w["pg"]], [(0, 0, 0)], 1, t, d, d, [BF16, F32],
                     epi=epi_ple, extras=[("mn", pp), ("mn", x2)], tm=512)
    s.update(h=h, proj=proj, f_t=f_t, qa=qa, ka=ka, va=va, u2=u2, cpre=cpre, o=o, lse=lse, yc=yc, ya=ya,
             merged=merged, x1=x1, hf=hf, gate=gate, up=up, act=act, x2=x2, hp=hp, pp=pp, sg=sg, p=p)
    return x3, s


def _layer_bwd(dm, dx3, dx3b, s, w, riders):
    t, d, c, f = dm.t, dm.d, dm.c, dm.f
    g = {}

    def ple_ew(dxv, sgv, ppv):
        sgf, ppf = sgv.astype(F32), ppv.astype(F32)
        return dxv * sgf, dxv * ppf * sgf * (1.0 - sgf)

    d_pp, d_z = _ew("ple_bwd", ple_ew, [dx3, s["sg"], s["pp"]], t, d, [BF16, BF16], _tile(t, ROW_TILE, 8))
    g["pp"] = _mm("gw_ple_proj", "tn", s["p"], d_pp, dm.pd, d, t, F32)
    g["pg"] = _mm("gw_ple_gate", "tn", s["hp"], d_z, d, d, t, F32)
    d_hp = _mm("dx_ple_gate", "nt", d_z, w["pg"], t, d, d, BF16)
    dx2, dx2b, g["n_ple"] = _rms_bwd("rms_ple_bwd", d_hp, s["x2"], w["n_ple"], dx3, t, d)

    def epi_dglu(accs, ex):
        gate, up = ex[0].astype(F32), ex[1].astype(F32)
        sg = _sig(gate)
        return accs[0] * up * (sg * (1.0 + gate * (1.0 - sg))), accs[0] * gate * sg

    d_gate, d_up = riders.host("dx_down", 2, functools.partial(
        _matmul, "dx_down", "nt", [dx2b], [w["d"]], [(0, 0, 0)], 1, t, f, d, [BF16, BF16], epi=epi_dglu,
        extras=[("mn", s["gate"]), ("mn", s["up"])], tm=512, tn=_ffn_tn(f), cols_outer=True))
    g["d"] = _mm("gw_down", "tn", s["act"], dx2b, f, d, t, F32, tm=_ffn_tn(f))
    g["g"] = _mm("gw_gate", "tn", s["hf"], d_gate, d, f, t, F32, tn=_ffn_tn(f))
    g["u"] = _mm("gw_up", "tn", s["hf"], d_up, d, f, t, F32, tn=_ffn_tn(f))
    d_hf, = riders.host("dx_gate_up", 1, functools.partial(
        _matmul, "dx_gate_up", "nt", [d_gate, d_up], [w["g"], w["u"]], [(0, 0, 0), (1, 1, 0)], 1, t, d, f, [BF16]))
    dx1, dx1b, g["n_ffn"] = _rms_bwd("rms_ffn_bwd", d_hf, s["x1"], w["n_ffn"], dx2, t, d)

    g["o"] = _mm("gw_out", "tn", s["merged"], dx1b, d, d, t, F32)

    def epi_dmerge(accs, ex):
        dmv = accs[0]
        sgc, sga = _sig(ex[0].astype(F32)), _sig(ex[1].astype(F32))
        ycv, yav = ex[2].astype(F32), ex[3].astype(F32)
        return dmv * sgc, dmv * sga, dmv * ycv * sgc * (1.0 - sgc), dmv * yav * sga * (1.0 - sga)

    d_yc, d_ya, d_gc, d_ga = _matmul(
        "dx_out", "nt", [dx1b], [w["o"]], [(0, 0, 0)], 1, t, d, d, [BF16] * 4, epi=epi_dmerge,
        extras=[("mn", (s["proj"], dm.gc_off)), ("mn", (s["proj"], dm.ga_off)), ("mn", s["yc"]), ("mn", s["ya"])],
        tm=512, tn=_tile(d, 512))
    g["co"] = _mm("gw_conv_out", "tn", s["u2"], d_yc, c, d, t, F32)
    d_u2 = _mm("dx_conv_out", "nt", d_yc, w["co"], t, c, d, BF16)
    g["ao"] = _mm("gw_attn_out", "tn", s["o"], d_ya, dm.aw, d, t, F32)
    d_o = _mm("dx_attn_out", "nt", d_ya, w["ao"], t, dm.aw, d, BF16)

    qb, da = _attn_prep_bwd(s["qa"], s["lse"], s["o"], d_o, t, dm.nh, dm.hd)
    dq, dk, dv, dck, dcq = riders.host(
        "attn_bwd", 5, functools.partial(_attn_bwd, qb, s["ka"], s["va"], da, t, dm.nh, dm.hd), ctx=g)
    d_ft, g_bf = _forget_bwd(dck.reshape(dm.nh, t) + dcq[:, :, 0], s["f_t"], w["b_f"], dm.nh, t)
    g["b_f"] = g_bf[:, 0]
    d_f = jnp.pad(d_ft.T, ((0, 0), (0, LANES - dm.nh))).astype(BF16)

    d_a, d_gg, g["conv_w"], g["conv_b"], g["ln_g"], g["ln_b"] = _conv_bwd(
        s["proj"], s["cpre"], d_u2, w["conv_w"], w["ln_g"], w["ln_b"], t, c, dm.a_off, dm.g_off)

    d_proj = jnp.concatenate([d_a, d_gg, dq, dk, dv, d_gc, d_ga], axis=1)
    g["main"], = riders.host("gw_in", 1, functools.partial(
        _matmul, "gw_in", "tn", [s["h"]], [d_proj], [(0, 0, 0)], 1, d, dm.n_main, t, [F32]))
    g["f"] = _mm("gw_forget", "tn", s["h"], d_f, d, LANES, t, F32)
    d_h_f = _mm("dx_forget", "nt", d_f, w["f"], t, d, LANES, BF16)
    d_h, = riders.host("dx_in", 1, functools.partial(
        _matmul, "dx_in", "nt", [d_proj], [w["main"]], [(0, 0, 0)], 1, t, d, dm.n_main, [BF16],
        epi=lambda accs, ex: [accs[0] + ex[0].astype(F32)], extras=[("mn", d_h_f)]))
    dx, dxb, g["n_mix"] = _rms_bwd("rms_mix_bwd", d_h, s["x"], w["n_mix"], dx1, t, d)
    return dx, dxb, g


def _adamw_tiles(wv, gv, mv, vv):
    m_new = ADAM_B1 * mv + (1.0 - ADAM_B1) * gv
    v_new = ADAM_B2 * vv + (1.0 - ADAM_B2) * (gv * gv)
    m_hat = m_new / (1.0 - ADAM_B1 ** ADAM_STEP)
    v_hat = v_new / (1.0 - ADAM_B2 ** ADAM_STEP)
    delta = -ADAM_LR * (m_hat / (jnp.sqrt(v_hat) + ADAM_EPS) + ADAM_WD * wv)
    return delta, m_new, v_new


def _adamw(name, wv, gv, mv, vv):
    shape = wv.shape
    cols = shape[-1]
    rows = wv.size // cols
    tm = _tile(rows, max(8, (1 << 18) // cols), 8)
    flat = [a.reshape(rows, cols) for a in (wv, gv, mv, vv)]
    outs = _ew(name, _adamw_tiles, flat, rows, cols, [F32, F32, F32], tm)
    return [o.reshape(shape) for o in outs]


def _place():
    return lax.axis_index("x"), lax.axis_index("y"), lax.axis_index("c")


def _other_chips(x, y):
    return [(1 - x, y), (x, 1 - y), (1 - x, 1 - y)]


def _window(ref, kind, chip, size):
    if kind == "chip":
        return ref.at[chip]
    if kind == "row":
        return ref.at[pl.ds(chip * size, size), :]
    return ref.at[:, pl.ds(pl.multiple_of(chip * size, LANES), size)]


class _Side:
    def __init__(self, ins, out_shapes, sem_counts, start, finish):
        self.ins, self.out_shapes, self.sem_counts, self.start, self.finish = ins, out_shapes, sem_counts, start, finish

    def join(self, other):
        ni, no, ns = len(self.ins), len(self.out_shapes), len(self.sem_counts)

        def both(first, second):
            def run(ins, outs, sems):
                first(ins[:ni], outs[:no], sems[:ns])
                second(ins[ni:], outs[no:], sems[ns:])
            return run

        return _Side(self.ins + other.ins, self.out_shapes + other.out_shapes, self.sem_counts + other.sem_counts,
                     both(self.start, other.start), both(self.finish, other.finish))


def _run_side(name, side):
    n_in, n_out = len(side.ins), len(side.out_shapes)

    def body(*refs):
        ins, outs, sems = refs[:n_in], refs[n_in:n_in + n_out], refs[n_in + n_out:]
        side.start(ins, outs, sems)
        side.finish(ins, outs, sems)

    return pl.pallas_call(
        body,
        name=name,
        in_specs=[ANY] * n_in,
        out_specs=[ANY] * n_out,
        out_shape=list(side.out_shapes),
        scratch_shapes=[pltpu.SemaphoreType.DMA((cnt,)) for cnt in side.sem_counts],
    )(*side.ins)


def _full_shape(shard, kind):
    _, a, b = shard.shape
    if kind == "chip":
        return (N_CHIPS, a, b)
    return (N_CHIPS * a, b) if kind == "row" else (a, N_CHIPS * b)


def _gather_side(shards, kinds, layer):
    n = len(shards)
    sizes = [s.shape[1] if k == "row" else s.shape[2] for s, k in zip(shards, kinds)]

    def copies(ins, outs, sems, forwards):
        ici_send, ici_recv, d2d_send, d2d_recv, own_send, own_recv = sems
        x, y, c = _place()
        me = 2 * x + y
        sibling = (x, y, 1 - c)
        own, ici, landed, fwd = [], [], [], []
        for w in range(n):
            mine = _window(outs[w], kinds[w], me, sizes[w])
            own.append(pltpu.make_async_remote_copy(
                src_ref=ins[w].at[layer], dst_ref=mine, send_sem=own_send.at[w], recv_sem=own_recv.at[w],
                device_id=sibling, device_id_type=MESH))
            for r, (px, py) in enumerate(_other_chips(x, y)):
                ici.append(pltpu.make_async_remote_copy(
                    src_ref=ins[w].at[layer], dst_ref=mine, send_sem=ici_send.at[3 * w + r],
                    recv_sem=ici_recv.at[3 * w + r], device_id=(px, py, layer), device_id_type=MESH))
                if forwards:
                    slab = _window(outs[w], kinds[w], 2 * px + py, sizes[w])
                    landed.append(pltpu.make_async_remote_copy(
                        src_ref=slab, dst_ref=slab, send_sem=ici_send.at[3 * w + r], recv_sem=ici_recv.at[3 * w + r],
                        device_id=(px, py, layer), device_id_type=MESH))
                    fwd.append(pltpu.make_async_remote_copy(
                        src_ref=slab, dst_ref=slab, send_sem=d2d_send.at[3 * w + r], recv_sem=d2d_recv.at[3 * w + r],
                        device_id=sibling, device_id_type=MESH))
        return c, own, ici, landed, fwd

    def start(ins, outs, sems):
        c, own, ici, _, _ = copies(ins, outs, sems, False)
        for cp in own:
            cp.start()

        @pl.when(c == layer)
        def _():
            for cp in ici:
                cp.start()

    def finish(ins, outs, sems):
        c, own, ici, landed, fwd = copies(ins, outs, sems, True)

        @pl.when(c == layer)
        def _():
            for got, cp in zip(landed, fwd):
                got.wait_recv()
                cp.start()
            for cp in ici + fwd:
                cp.wait_send()

        @pl.when(c != layer)
        def _():
            for cp in fwd:
                cp.wait_recv()

        for cp in own:
            cp.wait()

    out_shapes = [jax.ShapeDtypeStruct(_full_shape(s, k), s.dtype) for s, k in zip(shards, kinds)]
    return _Side(list(shards), out_shapes, [3 * n] * 4 + [n] * 2, start, finish)


def _swap_side(grads, layer):
    n = len(grads)

    def copies(ins, outs, sems):
        x, y, c = _place()
        return c, [pltpu.make_async_remote_copy(src_ref=ins[w], dst_ref=outs[w], send_sem=sems[0].at[w],
                                                recv_sem=sems[1].at[w], device_id=(x, y, layer), device_id_type=MESH)
                   for w in range(n)]

    def start(ins, outs, sems):
        c, cps = copies(ins, outs, sems)

        @pl.when(c != layer)
        def _():
            for cp in cps:
                cp.start()

    def finish(ins, outs, sems):
        c, cps = copies(ins, outs, sems)

        @pl.when(c != layer)
        def _():
            for cp in cps:
                cp.wait_send()

        @pl.when(c == layer)
        def _():
            for cp in cps:
                cp.wait_recv()

    return _Side(list(grads), [jax.ShapeDtypeStruct(a.shape, a.dtype) for a in grads], [n, n], start, finish)


def _add_core_partials(name, mine, got):
    shape = got.shape
    cols = shape[-1]
    rows = got.size // cols
    tm, tn = _tile(rows, 256, 8), _tile(cols, 2048)
    return _ew(name, lambda a, b: [a + b], [mine.reshape(rows, cols), got.reshape(rows, cols)], rows, cols,
               [BF16], tm, tn)[0].reshape(shape)


def _exchange_side(parts, kinds, sizes, layer):
    n = len(parts)

    def copies(ins, outs, sems):
        x, y, c = _place()
        cps = []
        for w in range(n):
            for r, (px, py) in enumerate(_other_chips(x, y)):
                cps.append(pltpu.make_async_remote_copy(
                    src_ref=_window(ins[w], kinds[w], 2 * px + py, sizes[w]), dst_ref=outs[w].at[r],
                    send_sem=sems[0].at[3 * w + r], recv_sem=sems[1].at[3 * w + r],
                    device_id=(px, py, layer), device_id_type=MESH))
        return c, cps

    def start(ins, outs, sems):
        c, cps = copies(ins, outs, sems)

        @pl.when(c == layer)
        def _():
            for cp in cps:
                cp.start()

    def finish(ins, outs, sems):
        c, cps = copies(ins, outs, sems)

        @pl.when(c == layer)
        def _():
            for cp in cps:
                cp.wait()

    out_shapes = [jax.ShapeDtypeStruct((3,) + tuple(_shard_shape(p, k, s)), p.dtype)
                  for p, k, s in zip(parts, kinds, sizes)]
    return _Side(list(parts), out_shapes, [3 * n, 3 * n], start, finish)


def _shard_shape(whole, kind, size):
    if kind == "chip":
        return whole.shape[1:]
    return (size, whole.shape[1]) if kind == "row" else (whole.shape[0], size)


def _sum_chip_partials(name, part, got, kind, size, chip, layer, both=None):
    rows, cols = _shard_shape(part, kind, size)
    tm = _tile(rows, max(8, (1 << 19) // cols), 16)

    def body(chip_ref, part_ref, g0_ref, g1_ref, g2_ref, *rest):
        total = part_ref[...].astype(F32)
        for ref in (g0_ref, g1_ref, g2_ref):
            total = total + ref[...].astype(F32)
        rest[-1][...] = total

    if kind == "chip":
        mine = pl.BlockSpec((None, tm, cols), lambda i, chip_ref: (chip_ref[0], i, 0))
    elif kind == "row":
        mine = pl.BlockSpec((tm, cols), lambda i, chip_ref: (chip_ref[0] * (rows // tm) + i, 0))
    else:
        mine = pl.BlockSpec((tm, cols), lambda i, chip_ref: (i, chip_ref[0]))
    theirs = [pl.BlockSpec((None, tm, cols), functools.partial(lambda r, i, chip_ref: (r, i, 0), r))
              for r in range(3)]
    kept = [] if both is None else [both]
    return pl.pallas_call(
        body,
        name=name,
        grid_spec=pltpu.PrefetchScalarGridSpec(
            num_scalar_prefetch=1, grid=(rows // tm,), in_specs=[mine] + theirs + [ANY] * len(kept),
            out_specs=pl.BlockSpec((None, tm, cols), lambda i, chip_ref: (layer, i, 0))),
        out_shape=jax.ShapeDtypeStruct((2, rows, cols), F32),
        input_output_aliases={5: 0} if kept else {},
        compiler_params=_params("parallel"),
    )(chip.reshape(1), part, got, got, got, *kept)


def _share_reduced(both):
    n = len(both)

    def body(*refs):
        ins, outs = refs[:n], refs[n:2 * n]
        send_sems, recv_sems = refs[2 * n:]
        x, y, c = _place()
        sends = []
        for w in range(n):
            cp = pltpu.make_async_remote_copy(src_ref=ins[w].at[c], dst_ref=outs[w].at[c], send_sem=send_sems.at[w],
                                              recv_sem=recv_sems.at[w], device_id=(x, y, 1 - c), device_id_type=MESH)
            cp.start()
            sends.append(cp)
        for w in range(n):
            got = outs[w].at[1 - c]
            pltpu.make_async_remote_copy(src_ref=got, dst_ref=got, send_sem=send_sems.at[w], recv_sem=recv_sems.at[w],
                                         device_id=(x, y, 1 - c), device_id_type=MESH).wait_recv()
        for cp in sends:
            cp.wait_send()

    return pl.pallas_call(
        body,
        name="share_reduced",
        in_specs=[ANY] * n,
        out_specs=[ANY] * n,
        out_shape=[jax.ShapeDtypeStruct(a.shape, a.dtype) for a in both],
        input_output_aliases={w: w for w in range(n)},
        scratch_shapes=[pltpu.SemaphoreType.DMA((n,)), pltpu.SemaphoreType.DMA((n,))],
    )(*both)


def _allreduce_small(v):
    rows, width = v.shape

    def body(v_ref, out_ref, slots, send_sems, recv_sems):
        x, y, c = _place()
        me = 4 * x + 2 * y + c
        slots[me] = v_ref[...]
        peers = [(x, y, 1 - c)]
        for px, py in _other_chips(x, y):
            peers += [(px, py, c), (px, py, 1 - c)]
        sends = []
        for r, peer in enumerate(peers):
            cp = pltpu.make_async_remote_copy(
                src_ref=v_ref, dst_ref=slots.at[me], send_sem=send_sems.at[r], recv_sem=recv_sems.at[r],
                device_id=peer, device_id_type=MESH)
            cp.start()
            sends.append(cp)
        for r, (px, py, pc) in enumerate(peers):
            got = slots.at[4 * px + 2 * py + pc]
            pltpu.make_async_remote_copy(
                src_ref=got, dst_ref=got, send_sem=send_sems.at[r], recv_sem=recv_sems.at[r],
                device_id=(px, py, pc), device_id_type=MESH).wait_recv()
        for cp in sends:
            cp.wait_send()
        total = slots[0]
        for n in range(1, 8):
            total = total + slots[n]
        out_ref[...] = total

    vm = pl.BlockSpec(memory_space=pltpu.VMEM)
    return pl.pallas_call(
        body,
        name="allreduce_small",
        in_specs=[vm],
        out_specs=vm,
        out_shape=jax.ShapeDtypeStruct((rows, width), F32),
        scratch_shapes=[pltpu.VMEM((8, rows, width), F32), pltpu.SemaphoreType.DMA((7,)),
                        pltpu.SemaphoreType.DMA((7,))],
    )(v)


def _pad_rows(flat, row_align):
    n = flat.shape[0]
    rows = -(-n // PACK_W)
    rows = -(-rows // row_align) * row_align
    return jnp.pad(flat, (0, rows * PACK_W - n)).reshape(rows, PACK_W)


def _unpack(buf, shapes):
    flat = buf.reshape(-1)
    out, off = [], 0
    for shp in shapes:
        n = math.prod(shp)
        out.append(flat[off:off + n].reshape(shp))
        off += n
    return out


BIG = (("w_in", "chip"), ("w_conv_out", "col"), ("w_attn_out", "col"), ("w_out", "row"), ("w_gate_up", "col"),
       ("w_down", "row"), ("w_ple_gate", "row"), ("w_ple_proj", "col"))


def kernel(x, p, norm_mix_g, w_in, b_forget, conv_w, conv_b, conv_ln_g, conv_ln_b, w_conv_out, w_attn_out, w_out, norm_ffn_g, w_gate_up, w_down, norm_ple_g, w_ple_gate, w_ple_proj, final_g, loss_target, m_norm_mix_g, m_w_in, m_b_forget, m_conv_w, m_conv_b, m_conv_ln_g, m_conv_ln_b, m_w_conv_out, m_w_attn_out, m_w_out, m_norm_ffn_g, m_w_gate_up, m_w_down, m_norm_ple_g, m_w_ple_gate, m_w_ple_proj, m_final_g, v_norm_mix_g, v_w_in, v_b_forget, v_conv_w, v_conv_b, v_conv_ln_g, v_conv_ln_b, v_w_conv_out, v_w_attn_out, v_w_out, v_norm_ffn_g, v_w_gate_up, v_w_down, v_norm_ple_g, v_w_ple_gate, v_w_ple_proj, v_final_g):
    wts = dict(norm_mix_g=norm_mix_g, w_in=w_in, b_forget=b_forget, conv_w=conv_w, conv_b=conv_b,
               conv_ln_g=conv_ln_g, conv_ln_b=conv_ln_b, w_conv_out=w_conv_out, w_attn_out=w_attn_out,
               w_out=w_out, norm_ffn_g=norm_ffn_g, w_gate_up=w_gate_up, w_down=w_down, norm_ple_g=norm_ple_g,
               w_ple_gate=w_ple_gate, w_ple_proj=w_ple_proj, final_g=final_g)
    mom1 = dict(norm_mix_g=m_norm_mix_g, w_in=m_w_in, b_forget=m_b_forget, conv_w=m_conv_w, conv_b=m_conv_b,
                conv_ln_g=m_conv_ln_g, conv_ln_b=m_conv_ln_b, w_conv_out=m_w_conv_out, w_attn_out=m_w_attn_out,
                w_out=m_w_out, norm_ffn_g=m_norm_ffn_g, w_gate_up=m_w_gate_up, w_down=m_w_down,
                norm_ple_g=m_norm_ple_g, w_ple_gate=m_w_ple_gate, w_ple_proj=m_w_ple_proj, final_g=m_final_g)
    mom2 = dict(norm_mix_g=v_norm_mix_g, w_in=v_w_in, b_forget=v_b_forget, conv_w=v_conv_w, conv_b=v_conv_b,
                conv_ln_g=v_conv_ln_g, conv_ln_b=v_conv_ln_b, w_conv_out=v_w_conv_out, w_attn_out=v_w_attn_out,
                w_out=v_w_out, norm_ffn_g=v_norm_ffn_g, w_gate_up=v_w_gate_up, w_down=v_w_down,
                norm_ple_g=v_norm_ple_g, w_ple_gate=v_w_ple_gate, w_ple_proj=v_w_ple_proj, final_g=v_final_g)
    order = list(wts)
    depth = w_in.shape[0]
    assert depth == 2, "the exchanges give one layer to each of a chip's two cores"
    t, d = x.shape[1], x.shape[2]
    c = conv_ln_g.shape[1]
    nh = b_forget.shape[1]
    aw = w_attn_out.shape[1]
    f = N_CHIPS * w_down.shape[1]
    pd = w_ple_proj.shape[1]
    dm = _Dims(t, d, c, nh, aw, f, pd)
    n_split = 2 * c + 3 * aw
    cw = conv_w.shape[2]
    chip = 2 * lax.axis_index("x") + lax.axis_index("y")
    big_names = [name for name, _ in BIG]
    big_kinds = [kind for _, kind in BIG]

    shards = {name: wts[name].astype(BF16) for name in big_names}
    shards["conv_w"] = conv_w
    kind_of = dict(BIG, conv_w="chip")
    full = [{}, {}]

    def gather_rider(names, layer):
        def make_side(ctx=None):
            return _gather_side([shards[n] for n in names], [kind_of[n] for n in names], layer)

        def take(results):
            full[layer].update(zip(names, results))

        return make_side, take

    make_side, take = gather_rider(["w_in", "conv_w"], 0)
    take(_run_side("gather_layer0_in", make_side()))
    fwd_riders = [_Riders({"attn_fwd": gather_rider(big_names[1:], 0),
                           "mm_in": gather_rider(["w_in", "w_conv_out", "w_out", "conv_w"], 1),
                           "mm_gate_up": gather_rider(["w_gate_up", "w_ple_gate", "w_attn_out"], 1),
                           "mm_down": gather_rider(["w_down", "w_ple_proj"], 1)}),
                  _Riders()]
    views = {"co": ("w_conv_out", 0), "ao": ("w_attn_out", 0), "o": ("w_out", 0), "g": ("w_gate_up", 0),
             "u": ("w_gate_up", f), "d": ("w_down", 0), "pg": ("w_ple_gate", 0), "pp": ("w_ple_proj", 0)}

    def layer_weights(l):
        fw = full[l]
        wi = jnp.concatenate([fw["w_in"][k] for k in range(N_CHIPS)], axis=1)
        cwl = jnp.concatenate([fw["conv_w"][k] for k in range(N_CHIPS)], axis=1)
        return _Weights({
            "main": jnp.concatenate([wi[:, :n_split], wi[:, n_split + nh:]], axis=1),
            "f": jnp.pad(wi[:, n_split:n_split + nh], ((0, 0), (0, LANES - nh))),
            "conv_w": jnp.pad(cwl, ((0, CONV_HALO - CONV_K), (0, 0))),
            "conv_b": conv_b[l][None], "ln_g": conv_ln_g[l][None], "ln_b": conv_ln_b[l][None],
            "b_f": jnp.broadcast_to(b_forget[l][:, None], (nh, LANES)),
            "n_mix": norm_mix_g[l][None], "n_ffn": norm_ffn_g[l][None], "n_ple": norm_ple_g[l][None],
        }, fw, views)

    xl = x[0]
    saved, lw = [], []
    for l in range(depth):
        lw.append(layer_weights(l))
        xl, s = _layer_fwd(dm, xl, p[l, 0], lw[l], fwd_riders[l])
        saved.append(s)
    loss_row, dx, dxb, g_final = _loss_head(xl, final_g[None], loss_target[0], t, d)
    loss = lax.psum(loss_row[0, 0], ("x", "y", "c"))

    sizes = [wts[name].shape[1] if kind == "row" else wts[name].shape[2] for name, kind in BIG]
    everything = list(range(len(BIG)))
    plain = {1: "co", 2: "ao", 3: "o", 5: "d", 6: "pg", 7: "pp"}

    def grad_of(g, w):
        if w == 0:
            w_in_g = jnp.concatenate([g["main"][:, :n_split], g["f"][:, :nh], g["main"][:, n_split:]], axis=1)
            return w_in_g.reshape(d, N_CHIPS, -1).transpose(1, 0, 2)
        return jnp.concatenate([g["g"], g["u"]], axis=1) if w == 4 else g[plain[w]]

    def add_partials(layer, group, mine, got):
        return {w: _add_core_partials(f"add_core_partials_l{layer}_{big_names[w]}", a, b)
                for w, a, b in zip(group, mine, got)}

    def exchange(parts, group, layer):
        return _exchange_side([parts[w] for w in group], [big_kinds[w] for w in group], [sizes[w] for w in group], layer)

    def sum_partials(layer, parts, got, both):
        return [_sum_chip_partials(f"sum_chip_partials_l{layer}_{big_names[w]}", parts[w], got[w], big_kinds[w],
                                   sizes[w], chip, layer, both=None if both is None else both[w])
                for w in everything]

    lg = [None] * depth
    dx, dxb, lg[1] = _layer_bwd(dm, dx, dxb, saved[1], lw[1], _Riders())
    g1 = [grad_of(lg[1], w) for w in everything]
    parts = [{}, {}]
    got = [{}, {}]
    early = everything[1:]

    def exchange_rider(group, layer):
        return (lambda ctx: exchange(parts[layer], group, layer)), (lambda res: got[layer].update(zip(group, res)))

    early_grads = []

    def attn_bwd_side(g):
        early_grads.extend(grad_of(g, w) for w in early)
        return exchange(parts[1], [0, 1, 2, 3], 1).join(_swap_side(early_grads, 0))

    def attn_bwd_take(res):
        got[1].update(zip([0, 1, 2, 3], res[:4]))
        parts[0].update(add_partials(0, early, early_grads, res[4:]))

    bwd_riders = _Riders({
        "dx_down": (lambda ctx: _swap_side(g1, 1), lambda res: parts[1].update(add_partials(1, everything, g1, res))),
        "dx_gate_up": exchange_rider([4, 5, 6, 7], 1),
        "attn_bwd": (attn_bwd_side, attn_bwd_take),
        "gw_in": exchange_rider([4], 0),
        "dx_in": exchange_rider([1, 2, 3, 5, 6, 7], 0)})
    dx, dxb, lg[0] = _layer_bwd(dm, dx, dxb, saved[0], lw[0], bwd_riders)
    both = sum_partials(1, parts[1], got[1], None)

    w_in_g = [grad_of(lg[0], 0)]
    parts[0].update(add_partials(0, [0], w_in_g, _run_side("swap_layer0_in_grads", _swap_side(w_in_g, 0))))
    got[0].update(zip([0], _run_side("exchange_layer0_in_grads", exchange(parts[0], [0], 0))))
    both = sum_partials(0, parts[0], got[0], both)
    grads = dict(zip(big_names, _share_reduced(both)))

    def stacked(key):
        return jnp.stack([lg[l][key] for l in range(depth)])

    small = ["norm_mix_g", "b_forget", "conv_b", "conv_ln_g", "conv_ln_b", "norm_ffn_g", "norm_ple_g", "final_g"]
    small_g = {
        "norm_mix_g": jnp.concatenate([lg[l]["n_mix"] for l in range(depth)]),
        "b_forget": stacked("b_f"),
        "conv_b": jnp.concatenate([lg[l]["conv_b"] for l in range(depth)]),
        "conv_ln_g": jnp.concatenate([lg[l]["ln_g"] for l in range(depth)]),
        "conv_ln_b": jnp.concatenate([lg[l]["ln_b"] for l in range(depth)]),
        "norm_ffn_g": jnp.concatenate([lg[l]["n_ffn"] for l in range(depth)]),
        "norm_ple_g": jnp.concatenate([lg[l]["n_ple"] for l in range(depth)]),
        "final_g": g_final[0],
    }
    conv_w_g = jnp.stack([lg[l]["conv_w"][:CONV_K] for l in range(depth)])
    small_shapes = [wts[n].shape for n in small] + [conv_w_g.shape]
    small_pack = _pad_rows(jnp.concatenate([small_g[n].reshape(-1) for n in small] + [conv_w_g.reshape(-1)]), 8)
    small_sum = _unpack(_allreduce_small(small_pack), small_shapes)
    for n, name in enumerate(small):
        grads[name] = small_sum[n]
    grads["conv_w"] = lax.dynamic_slice_in_dim(small_sum[len(small)], chip * cw, cw, axis=2)

    def pack_small(src):
        return _pad_rows(jnp.concatenate([src[n].reshape(-1) for n in small]), 8)

    small_upd = _adamw("adamw_small", pack_small(wts), pack_small(grads), pack_small(mom1), pack_small(mom2))
    small_upd = [_unpack(u, [wts[n].shape for n in small]) for u in small_upd]
    delta, new_m, new_v = {}, {}, {}
    for n, name in enumerate(small):
        delta[name], new_m[name], new_v[name] = small_upd[0][n], small_upd[1][n], small_upd[2][n]
    for name in big_names + ["conv_w"]:
        delta[name], new_m[name], new_v[name] = _adamw("adamw_" + name, wts[name], grads[name], mom1[name],
                                                       mom2[name])

    return (loss, dx[None], *[grads[n] for n in order], *[delta[n] for n in order],
            *[new_m[n] for n in order], *[new_v[n] for n in order])
```

```python
import functools
import math

import jax
import jax.numpy as jnp
from jax import lax
from jax.experimental import pallas as pl
from jax.experimental.pallas import tpu as pltpu

F32 = jnp.float32
BF16 = jnp.bfloat16

EPS = 1e-6
CONV_K = 31
NEG_INF = -1e30
LOG2E = 1.4426950408889634
ADAM_LR = 0.001
ADAM_B1 = 0.9
ADAM_B2 = 0.999
ADAM_EPS = 1e-08
ADAM_WD = 0.01
ADAM_STEP = 10

LANES = 128
VMEM_LIMIT = 60 * 1024 * 1024
PACK_W = 1024
N_CHIPS = 4
CONV_HALO = 32
CUM_BLOCK = 256

MM_TM = 1024
MM_TN = 1024
MM_TK = 2048
ROW_TILE = 256
CONV_TILE = 256
ATTN_TILE = 1024
ATTN_SUB = 1024

MESH = pl.DeviceIdType.MESH
ANY = pl.BlockSpec(memory_space=pl.ANY)


def _params(*sem):
    return pltpu.CompilerParams(dimension_semantics=sem, vmem_limit_bytes=VMEM_LIMIT)


def _tile(dim, pref, align=LANES):
    if dim <= pref:
        return dim
    t = (pref // align) * align
    while t >= align:
        if dim % t == 0:
            return t
        t -= align
    return dim


def _sig(x):
    return 1.0 / (1.0 + jnp.exp(-x))


def _op(a):
    if not isinstance(a, tuple):
        return a, 0, ()
    return a if len(a) == 3 else (a[0], a[1], ())


def _spec(block, index, lead):
    if not lead:
        return pl.BlockSpec(block, index)
    return pl.BlockSpec((None,) * len(lead) + block, lambda *g: tuple(lead) + index(*g))


_DIMS = {
    "nn": (((1,), (0,)), ((), ())),
    "nt": (((1,), (1,)), ((), ())),
    "tn": (((0,), (0,)), ((), ())),
}


def _matmul(name, mode, a_ops, b_ops, terms, n_acc, m, n, k, out_dtypes, epi=None, extras=(),
            tm=None, tn=None, tk=None, side=None, cols_outer=False):
    tm = tm or _tile(m, MM_TM)
    tn = tn or _tile(n, MM_TN)
    tk = tk or _tile(k, MM_TK)
    ni, nj, nk = m // tm, n // tn, k // tk
    a_ops = [_op(a) for a in a_ops]
    b_ops = [_op(b) for b in b_ops]
    extras = [(kind, _op(e)) for kind, e in extras]
    na, nb, ne, no = len(a_ops), len(b_ops), len(extras), len(out_dtypes)
    n_acc_refs = n_acc if nk > 1 else 0
    s_ins = list(side.ins) if side else []
    s_outs = list(side.out_shapes) if side else []
    s_sems = [pltpu.SemaphoreType.DMA((cnt,)) for cnt in side.sem_counts] if side else []

    def at(index):
        return (lambda jj, ii, kk: index(ii, jj, kk)) if cols_outer else index

    def a_spec(off, lead):
        if mode == "tn":
            assert off % tm == 0
            return _spec((tk, tm), at(lambda i, j, kk: (kk, i + off // tm)), lead)
        assert off % tk == 0
        return _spec((tm, tk), at(lambda i, j, kk: (i, kk + off // tk)), lead)

    def b_spec(off, lead):
        if mode == "nt":
            assert off % tk == 0
            return _spec((tn, tk), at(lambda i, j, kk: (j, kk + off // tk)), lead)
        assert off % tn == 0
        return _spec((tk, tn), at(lambda i, j, kk: (kk, j + off // tn)), lead)

    def e_spec(kind, off, lead):
        assert off % tn == 0
        if kind == "n":
            return _spec((1, tn), at(lambda i, j, kk: (0, j + off // tn)), lead)
        return _spec((tm, tn), at(lambda i, j, kk: (i, j + off // tn)), lead)

    def body(*refs):
        refs = list(refs)
        a_refs, b_refs, e_refs, si_refs, o_refs, so_refs, acc_refs, sem_refs = (
            [refs.pop(0) for _ in range(cnt)]
            for cnt in (na, nb, ne, len(s_ins), no, len(s_outs), n_acc_refs, len(s_sems)))
        outer, inner, kk = pl.program_id(0), pl.program_id(1), pl.program_id(2)
        if side:
            @pl.when((outer == 0) & (inner == 0) & (kk == 0))
            def _():
                side.start(si_refs, so_refs, sem_refs)

        av = [r[...].astype(BF16) for r in a_refs]
        bv = [r[...].astype(BF16) for r in b_refs]
        sums = [None] * n_acc
        for ai, bi, ci in terms:
            part = lax.dot_general(av[ai], bv[bi], _DIMS[mode], preferred_element_type=F32)
            sums[ci] = part if sums[ci] is None else sums[ci] + part

        def finish(accs):
            outs = epi(accs, [e[...] for e in e_refs]) if epi is not None else accs
            for o, val in zip(o_refs, outs):
                o[...] = val.astype(o.dtype)

        if nk == 1:
            finish(sums)
        else:
            @pl.when(kk == 0)
            def _():
                for acc, part in zip(acc_refs, sums):
                    acc[...] = part

            @pl.when(kk > 0)
            def _():
                for acc, part in zip(acc_refs, sums):
                    acc[...] += part

            @pl.when(kk == nk - 1)
            def _():
                finish([acc[...] for acc in acc_refs])

        if side:
            @pl.when((outer == grid[0] - 1) & (inner == grid[1] - 1) & (kk == nk - 1))
            def _():
                side.finish(si_refs, so_refs, sem_refs)

    grid = (nj, ni, nk) if cols_outer else (ni, nj, nk)
    order = ("arbitrary",) * 3 if side else ("parallel", "parallel", "arbitrary")
    outs = pl.pallas_call(
        body,
        name=name,
        grid=grid,
        in_specs=[a_spec(off, lead) for _, off, lead in a_ops] + [b_spec(off, lead) for _, off, lead in b_ops]
        + [e_spec(kind, off, lead) for kind, (_, off, lead) in extras] + [ANY] * len(s_ins),
        out_specs=[pl.BlockSpec((tm, tn), at(lambda i, j, kk: (i, j))) for _ in out_dtypes] + [ANY] * len(s_outs),
        out_shape=[jax.ShapeDtypeStruct((m, n), dt) for dt in out_dtypes] + s_outs,
        scratch_shapes=[pltpu.VMEM((tm, tn), F32) for _ in range(n_acc_refs)] + s_sems,
        compiler_params=_params(*order),
    )(*[a[0] for a in a_ops], *[b[0] for b in b_ops], *[e[0] for _, e in extras], *s_ins)
    return outs


def _mm(name, mode, a, b, m, n, k, out_dtype, **kw):
    return _matmul(name, mode, [a], [b], [(0, 0, 0)], 1, m, n, k, [out_dtype], **kw)[0]


def _ew(name, fn, ins, m, n, out_dtypes, tm, tn=None):
    tn = tn or n
    ins = [_op(a) for a in ins]
    ni = len(ins)

    def spec(off, lead):
        assert off % tn == 0
        return _spec((tm, tn), lambda i, j: (i, j + off // tn), lead)

    def body(*refs):
        outs = fn(*[r[...] for r in refs[:ni]])
        for o, val in zip(refs[ni:], outs):
            o[...] = val.astype(o.dtype)

    return pl.pallas_call(
        body,
        name=name,
        grid=(m // tm, n // tn),
        in_specs=[spec(off, lead) for _, off, lead in ins],
        out_specs=[pl.BlockSpec((tm, tn), lambda i, j: (i, j)) for _ in out_dtypes],
        out_shape=[jax.ShapeDtypeStruct((m, n), dt) for dt in out_dtypes],
        compiler_params=_params("parallel", "parallel"),
    )(*[a[0] for a in ins])


def _rms_fwd(name, x, g, t, d):
    tr = _tile(t, ROW_TILE, 8)

    def body(x_ref, g_ref, h_ref):
        xv = x_ref[...]
        r = lax.rsqrt(jnp.mean(xv * xv, axis=1, keepdims=True) + EPS)
        h_ref[...] = (xv * r * g_ref[...]).astype(BF16)

    return pl.pallas_call(
        body,
        name=name,
        grid=(t // tr,),
        in_specs=[pl.BlockSpec((tr, d), lambda i: (i, 0)), pl.BlockSpec((1, d), lambda i: (0, 0))],
        out_specs=pl.BlockSpec((tr, d), lambda i: (i, 0)),
        out_shape=jax.ShapeDtypeStruct((t, d), BF16),
        compiler_params=_params("parallel"),
    )(x, g)


def _rms_bwd_rows(dh, xv, g):
    r = lax.rsqrt(jnp.mean(xv * xv, axis=1, keepdims=True) + EPS)
    xhat = xv * r
    dxh = dh * g
    dx = r * (dxh - xhat * jnp.mean(dxh * xhat, axis=1, keepdims=True))
    return dx, dh * xhat


def _rms_bwd(name, dh, x, g, dres, t, d):
    tr = _tile(t, ROW_TILE, 8)

    def body(dh_ref, x_ref, g_ref, dres_ref, dx_ref, dxb_ref, dg_ref):
        @pl.when(pl.program_id(0) == 0)
        def _():
            dg_ref[...] = jnp.zeros(dg_ref.shape, F32)

        dx, dg_rows = _rms_bwd_rows(dh_ref[...].astype(F32), x_ref[...], g_ref[...])
        dx = dx + dres_ref[...]
        dx_ref[...] = dx
        dxb_ref[...] = dx.astype(BF16)
        dg_ref[...] += jnp.sum(dg_rows, axis=0, keepdims=True)

    row = pl.BlockSpec((tr, d), lambda i: (i, 0))
    vec = pl.BlockSpec((1, d), lambda i: (0, 0))
    return pl.pallas_call(
        body,
        name=name,
        grid=(t // tr,),
        in_specs=[row, row, vec, row],
        out_specs=[row, row, vec],
        out_shape=[jax.ShapeDtypeStruct((t, d), F32), jax.ShapeDtypeStruct((t, d), BF16),
                   jax.ShapeDtypeStruct((1, d), F32)],
        compiler_params=_params("arbitrary"),
    )(dh, x, g, dres)


def _loss_head(x, g, target, t, d):
    tr = _tile(t, ROW_TILE, 8)

    def body(x_ref, g_ref, tgt_ref, loss_ref, dx_ref, dxb_ref, dg_ref):
        @pl.when(pl.program_id(0) == 0)
        def _():
            dg_ref[...] = jnp.zeros(dg_ref.shape, F32)
            loss_ref[...] = jnp.zeros(loss_ref.shape, F32)

        xv = x_ref[...]
        gv = g_ref[...]
        r = lax.rsqrt(jnp.mean(xv * xv, axis=1, keepdims=True) + EPS)
        err = xv * r * gv - tgt_ref[...]
        loss_ref[...] += (0.5 / d) * jnp.sum(err * err)
        dx, dg_rows = _rms_bwd_rows(err * (1.0 / d), xv, gv)
        dx_ref[...] = dx
        dxb_ref[...] = dx.astype(BF16)
        dg_ref[...] += jnp.sum(dg_rows, axis=0, keepdims=True)

    row = pl.BlockSpec((tr, d), lambda i: (i, 0))
    vec = pl.BlockSpec((1, d), lambda i: (0, 0))
    one = pl.BlockSpec((1, LANES), lambda i: (0, 0))
    return pl.pallas_call(
        body,
        name="loss_head",
        grid=(t // tr,),
        in_specs=[row, vec, row],
        out_specs=[one, row, row, vec],
        out_shape=[jax.ShapeDtypeStruct((1, LANES), F32), jax.ShapeDtypeStruct((t, d), F32),
                   jax.ShapeDtypeStruct((t, d), BF16), jax.ShapeDtypeStruct((1, d), F32)],
        compiler_params=_params("arbitrary"),
    )(x, g, target)


def _layernorm_rows(cv, g, b):
    mu = jnp.mean(cv, axis=1, keepdims=True)
    xc = cv - mu
    rstd = lax.rsqrt(jnp.mean(xc * xc, axis=1, keepdims=True) + EPS)
    xhat = xc * rstd
    return xhat, rstd, xhat * g + b


SUBLANES = 8


def _shift_copies(pad, shifted, span):
    for b in range(1, SUBLANES):
        shifted[b - 1, pl.ds(0, span), :] = pad[pl.ds(b, span), :]


def _tap(pad, shifted, offset, rows):
    b = offset % SUBLANES
    if b == 0:
        return pad[pl.ds(offset, rows), :]
    return shifted[b - 1, pl.ds(offset - b, rows), :]


def _conv_fwd(proj, conv_w, conv_b, ln_g, ln_b, t, c, a_off, g_off):
    tm = _tile(t, CONV_TILE, CONV_HALO)
    per = tm // CONV_HALO
    ab, gb = a_off // c, g_off // c
    span = tm + CONV_HALO - SUBLANES

    def body(a_ref, g_ref, ap_ref, gp_ref, w_ref, cb_ref, lg_ref, lb_ref, u2_ref, c_ref, upad, ushift):
        i = pl.program_id(0)
        u_prev = ap_ref[...].astype(F32) * _sig(gp_ref[...].astype(F32))
        upad[pl.ds(0, CONV_HALO), :] = jnp.where(i > 0, u_prev, 0.0)
        upad[pl.ds(CONV_HALO, tm), :] = a_ref[...].astype(F32) * _sig(g_ref[...].astype(F32))
        _shift_copies(upad, ushift, span)
        acc = jnp.zeros((tm, c), F32) + cb_ref[...]
        for k in range(CONV_K):
            acc = acc + w_ref[pl.ds(k, 1), :] * _tap(upad, ushift, CONV_HALO - (CONV_K - 1) + k, tm)
        c_ref[...] = acc
        _, _, z = _layernorm_rows(acc, lg_ref[...], lb_ref[...])
        u2_ref[...] = (z * _sig(z)).astype(BF16)

    vec = pl.BlockSpec((1, c), lambda i: (0, 0))
    return pl.pallas_call(
        body,
        name="conv_fwd",
        grid=(t // tm,),
        in_specs=[
            pl.BlockSpec((tm, c), lambda i: (i, ab)),
            pl.BlockSpec((tm, c), lambda i: (i, gb)),
            pl.BlockSpec((CONV_HALO, c), lambda i: (jnp.maximum(i * per - 1, 0), ab)),
            pl.BlockSpec((CONV_HALO, c), lambda i: (jnp.maximum(i * per - 1, 0), gb)),
            pl.BlockSpec((CONV_HALO, c), lambda i: (0, 0)), vec, vec, vec,
        ],
        out_specs=[pl.BlockSpec((tm, c), lambda i: (i, 0)), pl.BlockSpec((tm, c), lambda i: (i, 0))],
        out_shape=[jax.ShapeDtypeStruct((t, c), BF16), jax.ShapeDtypeStruct((t, c), F32)],
        scratch_shapes=[pltpu.VMEM((CONV_HALO + tm, c), F32), pltpu.VMEM((SUBLANES - 1, span, c), F32)],
        compiler_params=_params("parallel"),
    )(proj, proj, proj, proj, conv_w, conv_b, ln_g, ln_b)


def _conv_bwd(proj, cpre, du2, conv_w, ln_g, ln_b, t, c, a_off, g_off):
    tm = _tile(t, CONV_TILE, CONV_HALO)
    per = tm // CONV_HALO
    nt = t // tm
    last_halo = t // CONV_HALO - 1
    ab, gb = a_off // c, g_off // c
    span = tm + CONV_HALO - SUBLANES

    def body(a_ref, g_ref, ap_ref, gp_ref, c_ref, cn_ref, du_ref, dun_ref, w_ref, lg_ref, lb_ref,
             da_ref, dg_ref, gw_ref, gcb_ref, glg_ref, glb_ref, upad, dpad, ushift, dshift):
        i = pl.program_id(0)

        @pl.when(i == 0)
        def _():
            gw_ref[...] = jnp.zeros(gw_ref.shape, F32)
            gcb_ref[...] = jnp.zeros(gcb_ref.shape, F32)
            glg_ref[...] = jnp.zeros(glg_ref.shape, F32)
            glb_ref[...] = jnp.zeros(glb_ref.shape, F32)

        lg = lg_ref[...]
        lb = lb_ref[...]

        def ln_bwd(cv, duv):
            xhat, rstd, z = _layernorm_rows(cv, lg, lb)
            sz = _sig(z)
            dz = duv * (sz * (1.0 + z * (1.0 - sz)))
            dxh = dz * lg
            dc = rstd * (dxh - jnp.mean(dxh, axis=1, keepdims=True)
                         - xhat * jnp.mean(dxh * xhat, axis=1, keepdims=True))
            return dc, dz, xhat

        dc, dz, xhat = ln_bwd(c_ref[...], du_ref[...].astype(F32))
        dc_next, _, _ = ln_bwd(cn_ref[...], dun_ref[...].astype(F32))
        glg_ref[...] += jnp.sum(dz * xhat, axis=0, keepdims=True)
        glb_ref[...] += jnp.sum(dz, axis=0, keepdims=True)
        gcb_ref[...] += jnp.sum(dc, axis=0, keepdims=True)
        dpad[pl.ds(0, tm), :] = dc
        dpad[pl.ds(tm, CONV_HALO), :] = jnp.where(i < nt - 1, dc_next, 0.0)

        av = a_ref[...].astype(F32)
        sg = _sig(g_ref[...].astype(F32))
        u_prev = ap_ref[...].astype(F32) * _sig(gp_ref[...].astype(F32))
        upad[pl.ds(0, CONV_HALO), :] = jnp.where(i > 0, u_prev, 0.0)
        upad[pl.ds(CONV_HALO, tm), :] = av * sg
        _shift_copies(upad, ushift, span)
        _shift_copies(dpad, dshift, span)

        du = jnp.zeros((tm, c), F32)
        for k in range(CONV_K):
            du = du + w_ref[pl.ds(k, 1), :] * _tap(dpad, dshift, CONV_K - 1 - k, tm)
            gw_ref[pl.ds(k, 1), :] += jnp.sum(
                dc * _tap(upad, ushift, CONV_HALO - (CONV_K - 1) + k, tm), axis=0, keepdims=True)
        da_ref[...] = (du * sg).astype(BF16)
        dg_ref[...] = (du * av * sg * (1.0 - sg)).astype(BF16)

    vec = pl.BlockSpec((1, c), lambda i: (0, 0))
    cur = pl.BlockSpec((tm, c), lambda i: (i, 0))
    nxt = pl.BlockSpec((CONV_HALO, c), lambda i: (jnp.minimum((i + 1) * per, last_halo), 0))
    wsp = pl.BlockSpec((CONV_HALO, c), lambda i: (0, 0))
    return pl.pallas_call(
        body,
        name="conv_bwd",
        grid=(nt,),
        in_specs=[
            pl.BlockSpec((tm, c), lambda i: (i, ab)),
            pl.BlockSpec((tm, c), lambda i: (i, gb)),
            pl.BlockSpec((CONV_HALO, c), lambda i: (jnp.maximum(i * per - 1, 0), ab)),
            pl.BlockSpec((CONV_HALO, c), lambda i: (jnp.maximum(i * per - 1, 0), gb)),
            cur, nxt, cur, nxt, wsp, vec, vec,
        ],
        out_specs=[cur, cur, wsp, vec, vec, vec],
        out_shape=[jax.ShapeDtypeStruct((t, c), BF16), jax.ShapeDtypeStruct((t, c), BF16),
                   jax.ShapeDtypeStruct((CONV_HALO, c), F32), jax.ShapeDtypeStruct((1, c), F32),
                   jax.ShapeDtypeStruct((1, c), F32), jax.ShapeDtypeStruct((1, c), F32)],
        scratch_shapes=[pltpu.VMEM((CONV_HALO + tm, c), F32), pltpu.VMEM((tm + CONV_HALO, c), F32),
                        pltpu.VMEM((SUBLANES - 1, span, c), F32), pltpu.VMEM((SUBLANES - 1, span, c), F32)],
        compiler_params=_params("arbitrary"),
    )(proj, proj, proj, proj, cpre, cpre, du2, du2, conv_w, ln_g, ln_b)


def _split3(x):
    hi = x.astype(BF16).astype(F32)
    r1 = x - hi
    mid = r1.astype(BF16).astype(F32)
    lo = (r1 - mid).astype(BF16).astype(F32)
    return hi, mid, lo


def _split3_dot(x, tri):
    dot = functools.partial(jnp.dot, preferred_element_type=F32)
    hi, mid, lo = _split3(x)
    return dot(hi.astype(BF16), tri) + dot(mid.astype(BF16), tri) + dot(lo.astype(BF16), tri)


def _to_blocks(a, nb, blk):
    return a.reshape(a.shape[0], nb, blk).transpose(1, 0, 2)


def _from_blocks(a):
    return a.transpose(1, 0, 2).reshape(a.shape[1], -1)


def _forget_fwd(f_t, b_col, nh, t):
    blk = _tile(t, CUM_BLOCK)
    nb = t // blk

    def body(f_ref, b_ref, c_ref):
        ri = lax.broadcasted_iota(jnp.int32, (blk, blk), 0)
        ci = lax.broadcasted_iota(jnp.int32, (blk, blk), 1)
        tri = (ri <= ci).astype(BF16)

        def step(bi, carry):
            xv = f_ref[bi] + b_ref[:, :1]
            lf = jnp.minimum(xv, 0.0) - jnp.log(1.0 + jnp.exp(-jnp.abs(xv)))
            cs = _split3_dot(lf, tri) + carry
            c_ref[bi] = cs
            return cs[:, blk - 1:blk]

        lax.fori_loop(0, nb, step, jnp.zeros((nh, 1), F32))

    out = pl.pallas_call(
        body,
        name="forget_fwd",
        out_shape=jax.ShapeDtypeStruct((nb, nh, blk), F32),
        compiler_params=pltpu.CompilerParams(vmem_limit_bytes=VMEM_LIMIT),
    )(_to_blocks(f_t, nb, blk), b_col)
    return _from_blocks(out)


def _forget_bwd(dc, f_t, b_col, nh, t):
    blk = _tile(t, CUM_BLOCK)
    nb = t // blk

    def body(dc_ref, f_ref, b_ref, df_ref, db_ref):
        ri = lax.broadcasted_iota(jnp.int32, (blk, blk), 0)
        ci = lax.broadcasted_iota(jnp.int32, (blk, blk), 1)
        tri = (ri >= ci).astype(BF16)

        def step(n, carry):
            tail, db = carry
            bi = nb - 1 - n
            rc = _split3_dot(dc_ref[bi], tri) + tail
            df = rc * _sig(-(f_ref[bi] + b_ref[:, :1]))
            df_ref[bi] = df
            return rc[:, 0:1], db + jnp.sum(df, axis=1, keepdims=True)

        _, db = lax.fori_loop(0, nb, step, (jnp.zeros((nh, 1), F32), jnp.zeros((nh, 1), F32)))
        db_ref[...] = jnp.broadcast_to(db, db_ref.shape)

    df, db = pl.pallas_call(
        body,
        name="forget_bwd",
        out_shape=[jax.ShapeDtypeStruct((nb, nh, blk), F32), jax.ShapeDtypeStruct((nh, LANES), F32)],
        compiler_params=pltpu.CompilerParams(vmem_limit_bytes=VMEM_LIMIT),
    )(_to_blocks(dc, nb, blk), _to_blocks(f_t, nb, blk), b_col)
    return _from_blocks(df), db


def _lanes(parts, rows):
    lane = lax.broadcasted_iota(jnp.int32, (rows, LANES), 1)
    out = jnp.zeros((rows, LANES), F32)
    for n, part in enumerate(parts):
        out = jnp.where(lane == n, part, out)
    return out


def _attn_prep_fwd(proj, cs, t, nh, hd, q_off, k_off, v_off):
    tr = _tile(t, ATTN_TILE, 16)
    qb, kb, vb = q_off // hd, k_off // hd, v_off // hd

    def body(q_ref, k_ref, v_ref, cs_ref, qa_ref, ka_ref, va_ref):
        hi, mid, lo = _split3(cs_ref[0][:, :1])
        qa_ref[0, :, :hd] = q_ref[...]
        qa_ref[0, :, hd:] = _lanes([1.0, 1.0, 1.0, hi, mid, lo], tr).astype(BF16)
        ka_ref[0, :, :hd] = k_ref[...]
        ka_ref[0, :, hd:] = _lanes([-hi, -mid, -lo] + [1.0] * 6, tr).astype(BF16)
        va_ref[0, :, :hd] = v_ref[...]
        va_ref[0, :, hd:] = _lanes([-1.0, -1.0, -1.0], tr).astype(BF16)

    wide = pl.BlockSpec((1, tr, 2 * hd), lambda h, i: (h, i, 0))
    return pl.pallas_call(
        body,
        name="attn_prep_fwd",
        grid=(nh, t // tr),
        in_specs=[pl.BlockSpec((tr, hd), lambda h, i: (i, qb + h)), pl.BlockSpec((tr, hd), lambda h, i: (i, kb + h)),
                  pl.BlockSpec((tr, hd), lambda h, i: (i, vb + h)), pl.BlockSpec((1, tr, LANES), lambda h, i: (h, i, 0))],
        out_specs=[wide, wide, wide],
        out_shape=[jax.ShapeDtypeStruct((nh, t, 2 * hd), BF16)] * 3,
        compiler_params=_params("parallel", "parallel"),
    )(proj, proj, proj, cs)


def _attn_prep_bwd(qa, lse, o, do, t, nh, hd):
    tr = _tile(t, ATTN_TILE, 16)
    inv_scale = math.sqrt(hd)

    def body(qa_ref, lse_ref, o_ref, do_ref, qb_ref, da_ref):
        l_hi, l_mid, l_lo = _split3(lse_ref[0][:, :1] * (-inv_scale))
        lane = lax.broadcasted_iota(jnp.int32, (tr, LANES), 1)
        extra = qa_ref[0, :, hd:].astype(F32)
        extra = jnp.where(lane == 6, l_hi, jnp.where(lane == 7, l_mid, jnp.where(lane == 8, l_lo, extra)))
        qb_ref[0, :, :hd] = qa_ref[0, :, :hd]
        qb_ref[0, :, hd:] = extra.astype(BF16)
        dov = do_ref[...]
        delta = jnp.sum(dov.astype(F32) * o_ref[...].astype(F32), axis=1, keepdims=True)
        da_ref[0, :, :hd] = dov
        da_ref[0, :, hd:] = _lanes(list(_split3(delta)), tr).astype(BF16)

    wide = pl.BlockSpec((1, tr, 2 * hd), lambda h, i: (h, i, 0))
    head = pl.BlockSpec((tr, hd), lambda h, i: (i, h))
    return pl.pallas_call(
        body,
        name="attn_prep_bwd",
        grid=(nh, t // tr),
        in_specs=[wide, pl.BlockSpec((1, tr, LANES), lambda h, i: (h, i, 0)), head, head],
        out_specs=[wide, wide],
        out_shape=[jax.ShapeDtypeStruct((nh, t, 2 * hd), BF16)] * 2,
        compiler_params=_params("parallel", "parallel"),
    )(qa, lse, o, do)


def _causal(s, row0, rows, cols):
    row = lax.broadcasted_iota(jnp.int32, (rows, cols), 0) + row0
    col = lax.broadcasted_iota(jnp.int32, (rows, cols), 1)
    return jnp.where(col <= row, s, NEG_INF)


def _side_refs(side):
    if side is None:
        return [], [], []
    return list(side.ins), list(side.out_shapes), [pltpu.SemaphoreType.DMA((cnt,)) for cnt in side.sem_counts]


def _attn_fwd(qa, ka, proj, t, nh, hd, v_off, side=None):
    tq = _tile(t, ATTN_TILE)
    nq = t // tq
    sub = _tile(tq, ATTN_SUB)
    ns = tq // sub
    scale = 1.0 / math.sqrt(hd)
    vb = v_off // hd
    s_ins, s_outs, s_sems = _side_refs(side)

    def body(qa_ref, ka_ref, v_ref, *rest):
        si_refs, (o_ref, lse_ref) = rest[:len(s_ins)], rest[len(s_ins):len(s_ins) + 2]
        so_refs, sem_refs = rest[len(s_ins) + 2:len(s_ins) + 2 + len(s_outs)], rest[len(s_ins) + 2 + len(s_outs):]
        h, i = pl.program_id(0), pl.program_id(1)
        if side:
            @pl.when((h == 0) & (i == 0))
            def _():
                side.start(si_refs, so_refs, sem_refs)

        def tile(j, carry, masked):
            rows = pl.ds(pl.multiple_of(j * tq, tq), tq)
            kj = ka_ref[0, rows, :]
            vj = v_ref[rows, :]
            new = []
            for r in range(ns):
                m, l, acc = carry[r]
                s = lax.dot_general(qa_ref[0, pl.ds(r * sub, sub), :], kj, _DIMS["nt"], preferred_element_type=F32)
                if masked:
                    s = _causal(s, r * sub, sub, tq)
                m_new = jnp.maximum(m, jnp.max(s, axis=1, keepdims=True))
                p = jnp.exp2((s - m_new) * (scale * LOG2E))
                alpha = jnp.exp2((m - m_new) * (scale * LOG2E))
                l = alpha * l + jnp.sum(p, axis=1, keepdims=True)
                acc = alpha * acc + jnp.dot(p.astype(BF16), vj, preferred_element_type=F32)
                new.append((m_new, l, acc))
            return tuple(new)

        init = tuple((jnp.full((sub, 1), NEG_INF, F32), jnp.zeros((sub, 1), F32), jnp.zeros((sub, hd), F32))
                     for _ in range(ns))
        carry = lax.fori_loop(0, i, lambda j, cr: tile(j, cr, False), init)
        carry = tile(i, carry, True)
        for r in range(ns):
            m, l, acc = carry[r]
            o_ref[pl.ds(r * sub, sub), :] = (acc / l).astype(BF16)
            lse_ref[0, pl.ds(r * sub, sub), :] = jnp.broadcast_to(m * scale + jnp.log(l), (sub, LANES))
        if side:
            @pl.when((h == nh - 1) & (i == nq - 1))
            def _():
                side.finish(si_refs, so_refs, sem_refs)

    return pl.pallas_call(
        body,
        name="attn_fwd",
        grid=(nh, nq),
        in_specs=[pl.BlockSpec((1, tq, 2 * hd), lambda h, i: (h, i, 0)),
                  pl.BlockSpec((1, t, 2 * hd), lambda h, i: (h, 0, 0)),
                  pl.BlockSpec((t, hd), lambda h, i: (0, vb + h))] + [ANY] * len(s_ins),
        out_specs=[pl.BlockSpec((tq, hd), lambda h, i: (i, h)),
                   pl.BlockSpec((1, tq, LANES), lambda h, i: (h, i, 0))] + [ANY] * len(s_outs),
        out_shape=[jax.ShapeDtypeStruct((t, nh * hd), BF16), jax.ShapeDtypeStruct((nh, t, LANES), F32)] + s_outs,
        scratch_shapes=s_sems,
        compiler_params=_params(*(("arbitrary",) * 2 if side else ("parallel",) * 2)),
    )(qa, ka, proj, *s_ins)


def _attn_bwd(qb, ka, va, da, t, nh, hd, side=None):
    tq = _tile(t, ATTN_TILE)
    nq = t // tq
    sub = _tile(tq, ATTN_SUB)
    ns = tq // sub
    scale = 1.0 / math.sqrt(hd)
    s_ins, s_outs, s_sems = _side_refs(side)

    def body(qb_ref, ka_ref, va_ref, da_ref, *rest):
        si_refs, rest = rest[:len(s_ins)], rest[len(s_ins):]
        (dq_ref, dk_ref, dv_ref, dck_ref, dcq_ref), rest = rest[:5], rest[5:]
        so_refs, (dq_all, *sem_refs) = rest[:len(s_outs)], rest[len(s_outs):]
        h, j = pl.program_id(0), pl.program_id(1)
        if side:
            @pl.when((h == 0) & (j == 0))
            def _():
                side.start(si_refs, so_refs, sem_refs)

        @pl.when(j == 0)
        def _():
            dq_all[...] = jnp.zeros(dq_all.shape, F32)
            dcq_ref[...] = jnp.zeros(dcq_ref.shape, F32)

        kaj = ka_ref[0]
        vaj = va_ref[0]
        kj = kaj[:, :hd]

        def tile(i, carry, masked):
            dk, dv, dck = carry
            for r in range(ns):
                rows = pl.ds(pl.multiple_of(i * tq + r * sub, sub), sub)
                qr = qb_ref[0, rows, :]
                dr = da_ref[0, rows, :]
                s = lax.dot_general(qr, kaj, _DIMS["nt"], preferred_element_type=F32)
                if masked:
                    s = _causal(s, r * sub, sub, tq)
                p = jnp.exp2(s * (scale * LOG2E))
                ds = p * lax.dot_general(dr, vaj, _DIMS["nt"], preferred_element_type=F32)
                dsb = ds.astype(BF16)
                dv = dv + lax.dot_general(p.astype(BF16), dr[:, :hd], _DIMS["tn"], preferred_element_type=F32)
                dk = dk + lax.dot_general(dsb, qr[:, :hd], _DIMS["tn"], preferred_element_type=F32)
                dck = dck - jnp.sum(ds, axis=0, keepdims=True)
                dq_all[rows, :] += jnp.dot(dsb, kj, preferred_element_type=F32)
                dcq_ref[0, rows, :] += jnp.sum(ds, axis=1, keepdims=True)
            return dk, dv, dck

        carry = (jnp.zeros((tq, hd), F32), jnp.zeros((tq, hd), F32), jnp.zeros((1, tq), F32))
        carry = tile(j, carry, True)
        dk, dv, dck = lax.fori_loop(j + 1, nq, lambda i, cr: tile(i, cr, False), carry)
        dk_ref[...] = (dk * scale).astype(BF16)
        dv_ref[...] = dv.astype(BF16)
        dck_ref[0] = dck

        @pl.when(j == nq - 1)
        def _():
            dq_ref[...] = (dq_all[...] * scale).astype(BF16)

        if side:
            @pl.when((h == nh - 1) & (j == nq - 1))
            def _():
                side.finish(si_refs, so_refs, sem_refs)

    whole = pl.BlockSpec((1, t, 2 * hd), lambda h, j: (h, 0, 0))
    block = pl.BlockSpec((1, tq, 2 * hd), lambda h, j: (h, j, 0))
    return pl.pallas_call(
        body,
        name="attn_bwd",
        grid=(nh, nq),
        in_specs=[whole, block, block, whole] + [ANY] * len(s_ins),
        out_specs=[
            pl.BlockSpec((t, hd), lambda h, j: (0, h)),
            pl.BlockSpec((tq, hd), lambda h, j: (j, h)),
            pl.BlockSpec((tq, hd), lambda h, j: (j, h)),
            pl.BlockSpec((1, 1, tq), lambda h, j: (h, 0, j)),
            pl.BlockSpec((1, t, LANES), lambda h, j: (h, 0, 0)),
        ] + [ANY] * len(s_outs),
        out_shape=[jax.ShapeDtypeStruct((t, nh * hd), BF16), jax.ShapeDtypeStruct((t, nh * hd), BF16),
                   jax.ShapeDtypeStruct((t, nh * hd), BF16), jax.ShapeDtypeStruct((nh, 1, t), F32),
                   jax.ShapeDtypeStruct((nh, t, LANES), F32)] + s_outs,
        scratch_shapes=[pltpu.VMEM((t, hd), F32)] + s_sems,
        compiler_params=_params("arbitrary", "arbitrary"),
    )(qb, ka, va, da, *s_ins)


class _Dims:
    def __init__(self, t, d, c, nh, aw, f, pd):
        self.t, self.d, self.c, self.nh, self.aw, self.f, self.pd = t, d, c, nh, aw, f, pd
        self.hd = aw // nh
        self.a_off, self.g_off = 0, c
        self.q_off, self.k_off, self.v_off = 2 * c, 2 * c + aw, 2 * c + 2 * aw
        self.gc_off = 2 * c + 3 * aw
        self.ga_off = self.gc_off + d
        self.n_main = self.ga_off + d


def _ffn_tn(f):
    return _tile(f, 1536)


class _Riders:
    def __init__(self, plan=None):
        self.plan = plan or {}

    def host(self, name, n_out, call, ctx=None):
        if name not in self.plan:
            return call()
        make_side, take = self.plan[name]
        outs = call(side=make_side(ctx))
        take(outs[n_out:])
        return outs[:n_out]


class _Weights:
    def __init__(self, fixed, gathered, views):
        self.fixed, self.gathered, self.views = fixed, gathered, views

    def __getitem__(self, key):
        if key in self.fixed:
            return self.fixed[key]
        name, off = self.views[key]
        return self.gathered[name], off


def _layer_fwd(dm, x, p, w, riders):
    t, d, c, f = dm.t, dm.d, dm.c, dm.f
    s = {"x": x}
    h = _rms_fwd("rms_mix", x, w["n_mix"], t, d)
    proj, = riders.host("mm_in", 1, functools.partial(
        _matmul, "mm_in", "nn", [h], [w["main"]], [(0, 0, 0)], 1, t, dm.n_main, d, [BF16]))
    fl = _mm("mm_forget", "nn", h, w["f"], t, LANES, d, F32)
    u2, cpre = _conv_fwd(proj, w["conv_w"], w["conv_b"], w["ln_g"], w["ln_b"], t, c, dm.a_off, dm.g_off)
    f_t = fl[:, :dm.nh].T
    cum = _forget_fwd(f_t, w["b_f"], dm.nh, t)
    cs = jnp.broadcast_to((cum * math.sqrt(dm.hd))[:, :, None], (dm.nh, t, LANES))
    qa, ka, va = _attn_prep_fwd(proj, cs, t, dm.nh, dm.hd, dm.q_off, dm.k_off, dm.v_off)
    o, lse = riders.host("attn_fwd", 2, functools.partial(_attn_fwd, qa, ka, proj, t, dm.nh, dm.hd, dm.v_off))

    def epi_conv(accs, ex):
        return accs[0], _sig(ex[0].astype(F32)) * accs[0]

    yc, m1 = _matmul("mm_conv_out", "nn", [u2], [w["co"]], [(0, 0, 0)], 1, t, d, c, [BF16, BF16],
                     epi=epi_conv, extras=[("mn", (proj, dm.gc_off))], tm=512)

    def epi_attn(accs, ex):
        return accs[0], ex[1].astype(F32) + _sig(ex[0].astype(F32)) * accs[0]

    ya, merged = _matmul("mm_attn_out", "nn", [o], [w["ao"]], [(0, 0, 0)], 1, t, d, dm.aw, [BF16, BF16],
                         epi=epi_attn, extras=[("mn", (proj, dm.ga_off)), ("mn", m1)], tm=512)
    x1 = _matmul("mm_out", "nn", [merged], [w["o"]], [(0, 0, 0)], 1, t, d, d, [F32],
                 epi=lambda accs, ex: [ex[0] + accs[0]], extras=[("mn", x)], tm=512)[0]

    hf = _rms_fwd("rms_ffn", x1, w["n_ffn"], t, d)

    def epi_glu(accs, ex):
        gate, up = accs
        return gate, up, gate * _sig(gate) * up

    gate, up, act = riders.host("mm_gate_up", 3, functools.partial(
        _matmul, "mm_gate_up", "nn", [hf], [w["g"], w["u"]], [(0, 0, 0), (0, 1, 1)], 2, t, f, d, [BF16, BF16, BF16],
        epi=epi_glu, tm=512, tn=_ffn_tn(f), cols_outer=True))
    x2, = riders.host("mm_down", 1, functools.partial(
        _matmul, "mm_down", "nn", [act], [w["d"]], [(0, 0, 0)], 1, t, d, f, [F32],
        epi=lambda accs, ex: [ex[0] + accs[0]], extras=[("mn", x1)], tm=512, tk=f, cols_outer=True))

    hp = _rms_fwd("rms_ple", x2, w["n_ple"], t, d)
    pp = _mm("mm_ple_proj", "nn", p, w["pp"], t, d, dm.pd, BF16, tm=512)

    def epi_ple(accs, ex):
        sg = _sig(accs[0])
        return sg, ex[1] + sg * ex[0].astype(F32)

    sg, x3 = _matmul("mm_ple_gate", "nn", [hp], [w["pg"]], [(0, 0, 0)], 1, t, d, d, [BF16, F32],
                     epi=epi_ple, extras=[("mn", pp), ("mn", x2)], tm=512)
    s.update(h=h, proj=proj, f_t=f_t, qa=qa, ka=ka, va=va, u2=u2, cpre=cpre, o=o, lse=lse, yc=yc, ya=ya,
             merged=merged, x1=x1, hf=hf, gate=gate, up=up, act=act, x2=x2, hp=hp, pp=pp, sg=sg, p=p)
    return x3, s


def _layer_bwd(dm, dx3, dx3b, s, w, riders):
    t, d, c, f = dm.t, dm.d, dm.c, dm.f
    g = {}

    def ple_ew(dxv, sgv, ppv):
        sgf, ppf = sgv.astype(F32), ppv.astype(F32)
        return dxv * sgf, dxv * ppf * sgf * (1.0 - sgf)

    d_pp, d_z = _ew("ple_bwd", ple_ew, [dx3, s["sg"], s["pp"]], t, d, [BF16, BF16], _tile(t, ROW_TILE, 8))
    g["pp"] = _mm("gw_ple_proj", "tn", s["p"], d_pp, dm.pd, d, t, F32)
    g["pg"] = _mm("gw_ple_gate", "tn", s["hp"], d_z, d, d, t, F32)
    d_hp = _mm("dx_ple_gate", "nt", d_z, w["pg"], t, d, d, BF16)
    dx2, dx2b, g["n_ple"] = _rms_bwd("rms_ple_bwd", d_hp, s["x2"], w["n_ple"], dx3, t, d)

    def epi_dglu(accs, ex):
        gate, up = ex[0].astype(F32), ex[1].astype(F32)
        sg = _sig(gate)
        return accs[0] * up * (sg * (1.0 + gate * (1.0 - sg))), accs[0] * gate * sg

    d_gate, d_up = riders.host("dx_down", 2, functools.partial(
        _matmul, "dx_down", "nt", [dx2b], [w["d"]], [(0, 0, 0)], 1, t, f, d, [BF16, BF16], epi=epi_dglu,
        extras=[("mn", s["gate"]), ("mn", s["up"])], tm=512, tn=_ffn_tn(f), cols_outer=True))
    g["d"] = _mm("gw_down", "tn", s["act"], dx2b, f, d, t, F32, tm=_ffn_tn(f))
    g["g"] = _mm("gw_gate", "tn", s["hf"], d_gate, d, f, t, F32, tn=_ffn_tn(f))
    g["u"] = _mm("gw_up", "tn", s["hf"], d_up, d, f, t, F32, tn=_ffn_tn(f))
    d_hf, = riders.host("dx_gate_up", 1, functools.partial(
        _matmul, "dx_gate_up", "nt", [d_gate, d_up], [w["g"], w["u"]], [(0, 0, 0), (1, 1, 0)], 1, t, d, f, [BF16]))
    dx1, dx1b, g["n_ffn"] = _rms_bwd("rms_ffn_bwd", d_hf, s["x1"], w["n_ffn"], dx2, t, d)

    g["o"] = _mm("gw_out", "tn", s["merged"], dx1b, d, d, t, F32)

    def epi_dmerge(accs, ex):
        dmv = accs[0]
        sgc, sga = _sig(ex[0].astype(F32)), _sig(ex[1].astype(F32))
        ycv, yav = ex[2].astype(F32), ex[3].astype(F32)
        return dmv * sgc, dmv * sga, dmv * ycv * sgc * (1.0 - sgc), dmv * yav * sga * (1.0 - sga)

    d_yc, d_ya, d_gc, d_ga = _matmul(
        "dx_out", "nt", [dx1b], [w["o"]], [(0, 0, 0)], 1, t, d, d, [BF16] * 4, epi=epi_dmerge,
        extras=[("mn", (s["proj"], dm.gc_off)), ("mn", (s["proj"], dm.ga_off)), ("mn", s["yc"]), ("mn", s["ya"])],
        tm=512, tn=_tile(d, 512))
    g["co"] = _mm("gw_conv_out", "tn", s["u2"], d_yc, c, d, t, F32)
    d_u2 = _mm("dx_conv_out", "nt", d_yc, w["co"], t, c, d, BF16)
    g["ao"] = _mm("gw_attn_out", "tn", s["o"], d_ya, dm.aw, d, t, F32)
    d_o = _mm("dx_attn_out", "nt", d_ya, w["ao"], t, dm.aw, d, BF16)

    qb, da = _attn_prep_bwd(s["qa"], s["lse"], s["o"], d_o, t, dm.nh, dm.hd)
    dq, dk, dv, dck, dcq = riders.host(
        "attn_bwd", 5, functools.partial(_attn_bwd, qb, s["ka"], s["va"], da, t, dm.nh, dm.hd), ctx=g)
    d_ft, g_bf = _forget_bwd(dck.reshape(dm.nh, t) + dcq[:, :, 0], s["f_t"], w["b_f"], dm.nh, t)
    g["b_f"] = g_bf[:, 0]
    d_f = jnp.pad(d_ft.T, ((0, 0), (0, LANES - dm.nh))).astype(BF16)

    d_a, d_gg, g["conv_w"], g["conv_b"], g["ln_g"], g["ln_b"] = _conv_bwd(
        s["proj"], s["cpre"], d_u2, w["conv_w"], w["ln_g"], w["ln_b"], t, c, dm.a_off, dm.g_off)

    d_proj = jnp.concatenate([d_a, d_gg, dq, dk, dv, d_gc, d_ga], axis=1)
    g["main"], = riders.host("gw_in", 1, functools.partial(
        _matmul, "gw_in", "tn", [s["h"]], [d_proj], [(0, 0, 0)], 1, d, dm.n_main, t, [F32], tk=_tile(t, 4096)))
    g["f"] = _mm("gw_forget", "tn", s["h"], d_f, d, LANES, t, F32)
    d_h_f = _mm("dx_forget", "nt", d_f, w["f"], t, d, LANES, BF16)
    d_h, = riders.host("dx_in", 1, functools.partial(
        _matmul, "dx_in", "nt", [d_proj], [w["main"]], [(0, 0, 0)], 1, t, d, dm.n_main, [BF16],
        epi=lambda accs, ex: [accs[0] + ex[0].astype(F32)], extras=[("mn", d_h_f)], tk=_tile(dm.n_main, 3072)))
    dx, dxb, g["n_mix"] = _rms_bwd("rms_mix_bwd", d_h, s["x"], w["n_mix"], dx1, t, d)
    return dx, dxb, g


def _adamw_tiles(wv, gv, mv, vv):
    m_new = ADAM_B1 * mv + (1.0 - ADAM_B1) * gv
    v_new = ADAM_B2 * vv + (1.0 - ADAM_B2) * (gv * gv)
    m_hat = m_new / (1.0 - ADAM_B1 ** ADAM_STEP)
    v_hat = v_new / (1.0 - ADAM_B2 ** ADAM_STEP)
    delta = -ADAM_LR * (m_hat / (jnp.sqrt(v_hat) + ADAM_EPS) + ADAM_WD * wv)
    return delta, m_new, v_new, gv


def _adamw(name, wv, gv, mv, vv):
    shape = wv.shape
    cols = shape[-1]
    rows = wv.size // cols
    tm = _tile(rows, max(8, (1 << 18) // cols), 8)
    flat = [a.reshape(rows, cols) for a in (wv, gv, mv, vv)]
    outs = _ew(name, _adamw_tiles, flat, rows, cols, [F32, F32, F32, F32], tm)
    return [o.reshape(shape) for o in outs]


def _place():
    return lax.axis_index("x"), lax.axis_index("y"), lax.axis_index("c")


def _other_chips(x, y):
    return [(1 - x, y), (x, 1 - y), (1 - x, 1 - y)]


def _window(ref, kind, chip, size):
    if kind == "chip":
        return ref.at[chip]
    if kind == "row":
        return ref.at[pl.ds(chip * size, size), :]
    return ref.at[:, pl.ds(pl.multiple_of(chip * size, LANES), size)]


class _Side:
    def __init__(self, ins, out_shapes, sem_counts, start, finish):
        self.ins, self.out_shapes, self.sem_counts, self.start, self.finish = ins, out_shapes, sem_counts, start, finish

    def join(self, other):
        ni, no, ns = len(self.ins), len(self.out_shapes), len(self.sem_counts)

        def both(first, second):
            def run(ins, outs, sems):
                first(ins[:ni], outs[:no], sems[:ns])
                second(ins[ni:], outs[no:], sems[ns:])
            return run

        return _Side(self.ins + other.ins, self.out_shapes + other.out_shapes, self.sem_counts + other.sem_counts,
                     both(self.start, other.start), both(self.finish, other.finish))


def _run_side(name, side):
    n_in, n_out = len(side.ins), len(side.out_shapes)

    def body(*refs):
        ins, outs, sems = refs[:n_in], refs[n_in:n_in + n_out], refs[n_in + n_out:]
        side.start(ins, outs, sems)
        side.finish(ins, outs, sems)

    return pl.pallas_call(
        body,
        name=name,
        in_specs=[ANY] * n_in,
        out_specs=[ANY] * n_out,
        out_shape=list(side.out_shapes),
        scratch_shapes=[pltpu.SemaphoreType.DMA((cnt,)) for cnt in side.sem_counts],
    )(*side.ins)


def _full_shape(shard, kind):
    _, a, b = shard.shape
    if kind == "chip":
        return (N_CHIPS, a, b)
    return (N_CHIPS * a, b) if kind == "row" else (a, N_CHIPS * b)


def _gather_side(shards, kinds, layer):
    n = len(shards)
    sizes = [s.shape[1] if k == "row" else s.shape[2] for s, k in zip(shards, kinds)]

    def copies(ins, outs, sems, forwards):
        ici_send, ici_recv, d2d_send, d2d_recv, own_send, own_recv = sems
        x, y, c = _place()
        me = 2 * x + y
        sibling = (x, y, 1 - c)
        own, ici, landed, fwd = [], [], [], []
        for w in range(n):
            mine = _window(outs[w], kinds[w], me, sizes[w])
            own.append(pltpu.make_async_remote_copy(
                src_ref=ins[w].at[layer], dst_ref=mine, send_sem=own_send.at[w], recv_sem=own_recv.at[w],
                device_id=sibling, device_id_type=MESH))
            for r, (px, py) in enumerate(_other_chips(x, y)):
                ici.append(pltpu.make_async_remote_copy(
                    src_ref=ins[w].at[layer], dst_ref=mine, send_sem=ici_send.at[3 * w + r],
                    recv_sem=ici_recv.at[3 * w + r], device_id=(px, py, layer), device_id_type=MESH))
                if forwards:
                    slab = _window(outs[w], kinds[w], 2 * px + py, sizes[w])
                    landed.append(pltpu.make_async_remote_copy(
                        src_ref=slab, dst_ref=slab, send_sem=ici_send.at[3 * w + r], recv_sem=ici_recv.at[3 * w + r],
                        device_id=(px, py, layer), device_id_type=MESH))
                    fwd.append(pltpu.make_async_remote_copy(
                        src_ref=slab, dst_ref=slab, send_sem=d2d_send.at[3 * w + r], recv_sem=d2d_recv.at[3 * w + r],
                        device_id=sibling, device_id_type=MESH))
        return c, own, ici, landed, fwd

    def start(ins, outs, sems):
        c, own, ici, _, _ = copies(ins, outs, sems, False)
        for cp in own:
            cp.start()

        @pl.when(c == layer)
        def _():
            for cp in ici:
                cp.start()

    def finish(ins, outs, sems):
        c, own, ici, landed, fwd = copies(ins, outs, sems, True)

        @pl.when(c == layer)
        def _():
            for got, cp in zip(landed, fwd):
                got.wait_recv()
                cp.start()
            for cp in ici + fwd:
                cp.wait_send()

        @pl.when(c != layer)
        def _():
            for cp in fwd:
                cp.wait_recv()

        for cp in own:
            cp.wait()

    out_shapes = [jax.ShapeDtypeStruct(_full_shape(s, k), s.dtype) for s, k in zip(shards, kinds)]
    return _Side(list(shards), out_shapes, [3 * n] * 4 + [n] * 2, start, finish)


def _swap_side(grads, layer):
    n = len(grads)

    def copies(ins, outs, sems):
        x, y, c = _place()
        return c, [pltpu.make_async_remote_copy(src_ref=ins[w], dst_ref=outs[w], send_sem=sems[0].at[w],
                                                recv_sem=sems[1].at[w], device_id=(x, y, layer), device_id_type=MESH)
                   for w in range(n)]

    def start(ins, outs, sems):
        c, cps = copies(ins, outs, sems)

        @pl.when(c != layer)
        def _():
            for cp in cps:
                cp.start()

    def finish(ins, outs, sems):
        c, cps = copies(ins, outs, sems)

        @pl.when(c != layer)
        def _():
            for cp in cps:
                cp.wait_send()

        @pl.when(c == layer)
        def _():
            for cp in cps:
                cp.wait_recv()

    return _Side(list(grads), [jax.ShapeDtypeStruct(a.shape, a.dtype) for a in grads], [n, n], start, finish)


def _add_core_partials(name, mine, got):
    shape = got.shape
    cols = shape[-1]
    rows = got.size // cols
    tm, tn = _tile(rows, 256, 8), _tile(cols, 2048)
    return _ew(name, lambda a, b: [a + b], [mine.reshape(rows, cols), got.reshape(rows, cols)], rows, cols,
               [BF16], tm, tn)[0].reshape(shape)


def _exchange_side(parts, kinds, sizes, layer):
    n = len(parts)

    def copies(ins, outs, sems):
        x, y, c = _place()
        cps = []
        for w in range(n):
            for r, (px, py) in enumerate(_other_chips(x, y)):
                cps.append(pltpu.make_async_remote_copy(
                    src_ref=_window(ins[w], kinds[w], 2 * px + py, sizes[w]), dst_ref=outs[w].at[r],
                    send_sem=sems[0].at[3 * w + r], recv_sem=sems[1].at[3 * w + r],
                    device_id=(px, py, layer), device_id_type=MESH))
        return c, cps

    def start(ins, outs, sems):
        c, cps = copies(ins, outs, sems)

        @pl.when(c == layer)
        def _():
            for cp in cps:
                cp.start()

    def finish(ins, outs, sems):
        c, cps = copies(ins, outs, sems)

        @pl.when(c == layer)
        def _():
            for cp in cps:
                cp.wait()

    out_shapes = [jax.ShapeDtypeStruct((3,) + tuple(_shard_shape(p, k, s)), p.dtype)
                  for p, k, s in zip(parts, kinds, sizes)]
    return _Side(list(parts), out_shapes, [3 * n, 3 * n], start, finish)


def _shard_shape(whole, kind, size):
    if kind == "chip":
        return whole.shape[1:]
    return (size, whole.shape[1]) if kind == "row" else (whole.shape[0], size)


def _sum_chip_partials(name, part, got, kind, size, chip, layer, both=None):
    rows, cols = _shard_shape(part, kind, size)
    tm = _tile(rows, max(8, (1 << 19) // cols), 16)

    def body(chip_ref, part_ref, g0_ref, g1_ref, g2_ref, *rest):
        total = part_ref[...].astype(F32)
        for ref in (g0_ref, g1_ref, g2_ref):
            total = total + ref[...].astype(F32)
        rest[-1][...] = total

    if kind == "chip":
        mine = pl.BlockSpec((None, tm, cols), lambda i, chip_ref: (chip_ref[0], i, 0))
    elif kind == "row":
        mine = pl.BlockSpec((tm, cols), lambda i, chip_ref: (chip_ref[0] * (rows // tm) + i, 0))
    else:
        mine = pl.BlockSpec((tm, cols), lambda i, chip_ref: (i, chip_ref[0]))
    theirs = [pl.BlockSpec((None, tm, cols), functools.partial(lambda r, i, chip_ref: (r, i, 0), r))
              for r in range(3)]
    kept = [] if both is None else [both]
    return pl.pallas_call(
        body,
        name=name,
        grid_spec=pltpu.PrefetchScalarGridSpec(
            num_scalar_prefetch=1, grid=(rows // tm,), in_specs=[mine] + theirs + [ANY] * len(kept),
            out_specs=pl.BlockSpec((None, tm, cols), lambda i, chip_ref: (layer, i, 0))),
        out_shape=jax.ShapeDtypeStruct((2, rows, cols), F32),
        input_output_aliases={5: 0} if kept else {},
        compiler_params=_params("parallel"),
    )(chip.reshape(1), part, got, got, got, *kept)


def _share_reduced(both):
    n = len(both)

    def body(*refs):
        ins, outs = refs[:n], refs[n:2 * n]
        send_sems, recv_sems = refs[2 * n:]
        x, y, c = _place()
        sends = []
        for w in range(n):
            cp = pltpu.make_async_remote_copy(src_ref=ins[w].at[c], dst_ref=outs[w].at[c], send_sem=send_sems.at[w],
                                              recv_sem=recv_sems.at[w], device_id=(x, y, 1 - c), device_id_type=MESH)
            cp.start()
            sends.append(cp)
        for w in range(n):
            got = outs[w].at[1 - c]
            pltpu.make_async_remote_copy(src_ref=got, dst_ref=got, send_sem=send_sems.at[w], recv_sem=recv_sems.at[w],
                                         device_id=(x, y, 1 - c), device_id_type=MESH).wait_recv()
        for cp in sends:
            cp.wait_send()

    return pl.pallas_call(
        body,
        name="share_reduced",
        in_specs=[ANY] * n,
        out_specs=[ANY] * n,
        out_shape=[jax.ShapeDtypeStruct(a.shape, a.dtype) for a in both],
        input_output_aliases={w: w for w in range(n)},
        scratch_shapes=[pltpu.SemaphoreType.DMA((n,)), pltpu.SemaphoreType.DMA((n,))],
    )(*both)


def _allreduce_small(v):
    rows, width = v.shape

    def body(v_ref, out_ref, slots, send_sems, recv_sems):
        x, y, c = _place()
        me = 4 * x + 2 * y + c
        slots[me] = v_ref[...]
        peers = [(x, y, 1 - c)]
        for px, py in _other_chips(x, y):
            peers += [(px, py, c), (px, py, 1 - c)]
        sends = []
        for r, peer in enumerate(peers):
            cp = pltpu.make_async_remote_copy(
                src_ref=v_ref, dst_ref=slots.at[me], send_sem=send_sems.at[r], recv_sem=recv_sems.at[r],
                device_id=peer, device_id_type=MESH)
            cp.start()
            sends.append(cp)
        for r, (px, py, pc) in enumerate(peers):
            got = slots.at[4 * px + 2 * py + pc]
            pltpu.make_async_remote_copy(
                src_ref=got, dst_ref=got, send_sem=send_sems.at[r], recv_sem=recv_sems.at[r],
                device_id=(px, py, pc), device_id_type=MESH).wait_recv()
        for cp in sends:
            cp.wait_send()
        total = slots[0]
        for n in range(1, 8):
            total = total + slots[n]
        out_ref[...] = total

    vm = pl.BlockSpec(memory_space=pltpu.VMEM)
    return pl.pallas_call(
        body,
        name="allreduce_small",
        in_specs=[vm],
        out_specs=vm,
        out_shape=jax.ShapeDtypeStruct((rows, width), F32),
        scratch_shapes=[pltpu.VMEM((8, rows, width), F32), pltpu.SemaphoreType.DMA((7,)),
                        pltpu.SemaphoreType.DMA((7,))],
    )(v)


def _pad_rows(flat, row_align):
    n = flat.shape[0]
    rows = -(-n // PACK_W)
    rows = -(-rows // row_align) * row_align
    return jnp.pad(flat, (0, rows * PACK_W - n)).reshape(rows, PACK_W)


def _unpack(buf, shapes):
    flat = buf.reshape(-1)
    out, off = [], 0
    for shp in shapes:
        n = math.prod(shp)
        out.append(flat[off:off + n].reshape(shp))
        off += n
    return out


BIG = (("w_in", "chip"), ("w_conv_out", "col"), ("w_attn_out", "col"), ("w_out", "row"), ("w_gate_up", "col"),
       ("w_down", "row"), ("w_ple_gate", "row"), ("w_ple_proj", "col"))


def kernel(x, p, norm_mix_g, w_in, b_forget, conv_w, conv_b, conv_ln_g, conv_ln_b, w_conv_out, w_attn_out, w_out, norm_ffn_g, w_gate_up, w_down, norm_ple_g, w_ple_gate, w_ple_proj, final_g, loss_target, m_norm_mix_g, m_w_in, m_b_forget, m_conv_w, m_conv_b, m_conv_ln_g, m_conv_ln_b, m_w_conv_out, m_w_attn_out, m_w_out, m_norm_ffn_g, m_w_gate_up, m_w_down, m_norm_ple_g, m_w_ple_gate, m_w_ple_proj, m_final_g, v_norm_mix_g, v_w_in, v_b_forget, v_conv_w, v_conv_b, v_conv_ln_g, v_conv_ln_b, v_w_conv_out, v_w_attn_out, v_w_out, v_norm_ffn_g, v_w_gate_up, v_w_down, v_norm_ple_g, v_w_ple_gate, v_w_ple_proj, v_final_g):
    wts = dict(norm_mix_g=norm_mix_g, w_in=w_in, b_forget=b_forget, conv_w=conv_w, conv_b=conv_b,
               conv_ln_g=conv_ln_g, conv_ln_b=conv_ln_b, w_conv_out=w_conv_out, w_attn_out=w_attn_out,
               w_out=w_out, norm_ffn_g=norm_ffn_g, w_gate_up=w_gate_up, w_down=w_down, norm_ple_g=norm_ple_g,
               w_ple_gate=w_ple_gate, w_ple_proj=w_ple_proj, final_g=final_g)
    mom1 = dict(norm_mix_g=m_norm_mix_g, w_in=m_w_in, b_forget=m_b_forget, conv_w=m_conv_w, conv_b=m_conv_b,
                conv_ln_g=m_conv_ln_g, conv_ln_b=m_conv_ln_b, w_conv_out=m_w_conv_out, w_attn_out=m_w_attn_out,
                w_out=m_w_out, norm_ffn_g=m_norm_ffn_g, w_gate_up=m_w_gate_up, w_down=m_w_down,
                norm_ple_g=m_norm_ple_g, w_ple_gate=m_w_ple_gate, w_ple_proj=m_w_ple_proj, final_g=m_final_g)
    mom2 = dict(norm_mix_g=v_norm_mix_g, w_in=v_w_in, b_forget=v_b_forget, conv_w=v_conv_w, conv_b=v_conv_b,
                conv_ln_g=v_conv_ln_g, conv_ln_b=v_conv_ln_b, w_conv_out=v_w_conv_out, w_attn_out=v_w_attn_out,
                w_out=v_w_out, norm_ffn_g=v_norm_ffn_g, w_gate_up=v_w_gate_up, w_down=v_w_down,
                norm_ple_g=v_norm_ple_g, w_ple_gate=v_w_ple_gate, w_ple_proj=v_w_ple_proj, final_g=v_final_g)
    order = list(wts)
    depth = w_in.shape[0]
    assert depth == 2, "the exchanges give one layer to each of a chip's two cores"
    t, d = x.shape[1], x.shape[2]
    c = conv_ln_g.shape[1]
    nh = b_forget.shape[1]
    aw = w_attn_out.shape[1]
    f = N_CHIPS * w_down.shape[1]
    pd = w_ple_proj.shape[1]
    dm = _Dims(t, d, c, nh, aw, f, pd)
    n_split = 2 * c + 3 * aw
    cw = conv_w.shape[2]
    chip = 2 * lax.axis_index("x") + lax.axis_index("y")
    big_names = [name for name, _ in BIG]
    big_kinds = [kind for _, kind in BIG]

    shards = {name: wts[name].astype(BF16) for name in big_names}
    shards["conv_w"] = conv_w
    kind_of = dict(BIG, conv_w="chip")
    full = [{}, {}]

    def gather_rider(names, layer):
        def make_side(ctx=None):
            return _gather_side([shards[n] for n in names], [kind_of[n] for n in names], layer)

        def take(results):
            full[layer].update(zip(names, results))

        return make_side, take

    make_side, take = gather_rider(["w_in", "conv_w"], 0)
    take(_run_side("gather_layer0_in", make_side()))
    fwd_riders = [_Riders({"attn_fwd": gather_rider(big_names[1:], 0),
                           "mm_in": gather_rider(["w_in", "w_conv_out", "w_out", "conv_w"], 1),
                           "mm_gate_up": gather_rider(["w_gate_up", "w_ple_gate", "w_attn_out"], 1),
                           "mm_down": gather_rider(["w_down", "w_ple_proj"], 1)}),
                  _Riders()]
    views = {"co": ("w_conv_out", 0), "ao": ("w_attn_out", 0), "o": ("w_out", 0), "g": ("w_gate_up", 0),
             "u": ("w_gate_up", f), "d": ("w_down", 0), "pg": ("w_ple_gate", 0), "pp": ("w_ple_proj", 0)}

    def layer_weights(l):
        fw = full[l]
        wi = jnp.concatenate([fw["w_in"][k] for k in range(N_CHIPS)], axis=1)
        cwl = jnp.concatenate([fw["conv_w"][k] for k in range(N_CHIPS)], axis=1)
        return _Weights({
            "main": jnp.concatenate([wi[:, :n_split], wi[:, n_split + nh:]], axis=1),
            "f": jnp.pad(wi[:, n_split:n_split + nh], ((0, 0), (0, LANES - nh))),
            "conv_w": jnp.pad(cwl, ((0, CONV_HALO - CONV_K), (0, 0))),
            "conv_b": conv_b[l][None], "ln_g": conv_ln_g[l][None], "ln_b": conv_ln_b[l][None],
            "b_f": jnp.broadcast_to(b_forget[l][:, None], (nh, LANES)),
            "n_mix": norm_mix_g[l][None], "n_ffn": norm_ffn_g[l][None], "n_ple": norm_ple_g[l][None],
        }, fw, views)

    xl = x[0]
    saved, lw = [], []
    for l in range(depth):
        lw.append(layer_weights(l))
        xl, s = _layer_fwd(dm, xl, p[l, 0], lw[l], fwd_riders[l])
        saved.append(s)
    loss_row, dx, dxb, g_final = _loss_head(xl, final_g[None], loss_target[0], t, d)
    loss = lax.psum(loss_row[0, 0], ("x", "y", "c"))

    sizes = [wts[name].shape[1] if kind == "row" else wts[name].shape[2] for name, kind in BIG]
    everything = list(range(len(BIG)))
    plain = {1: "co", 2: "ao", 3: "o", 5: "d", 6: "pg", 7: "pp"}

    def grad_of(g, w):
        if w == 0:
            w_in_g = jnp.concatenate([g["main"][:, :n_split], g["f"][:, :nh], g["main"][:, n_split:]], axis=1)
            return w_in_g.reshape(d, N_CHIPS, -1).transpose(1, 0, 2)
        return jnp.concatenate([g["g"], g["u"]], axis=1) if w == 4 else g[plain[w]]

    def add_partials(layer, group, mine, got):
        return {w: _add_core_partials(f"add_core_partials_l{layer}_{big_names[w]}", a, b)
                for w, a, b in zip(group, mine, got)}

    def exchange(parts, group, layer):
        return _exchange_side([parts[w] for w in group], [big_kinds[w] for w in group], [sizes[w] for w in group], layer)

    def sum_partials(layer, parts, got, both):
        return [_sum_chip_partials(f"sum_chip_partials_l{layer}_{big_names[w]}", parts[w], got[w], big_kinds[w],
                                   sizes[w], chip, layer, both=None if both is None else both[w])
                for w in everything]

    lg = [None] * depth
    dx, dxb, lg[1] = _layer_bwd(dm, dx, dxb, saved[1], lw[1], _Riders())
    g1 = [grad_of(lg[1], w) for w in everything]
    parts = [{}, {}]
    got = [{}, {}]
    early = everything[1:]

    def exchange_rider(group, layer):
        return (lambda ctx: exchange(parts[layer], group, layer)), (lambda res: got[layer].update(zip(group, res)))

    early_grads = []

    def attn_bwd_side(g):
        early_grads.extend(grad_of(g, w) for w in early)
        return exchange(parts[1], [0, 1, 2, 3], 1).join(_swap_side(early_grads, 0))

    def attn_bwd_take(res):
        got[1].update(zip([0, 1, 2, 3], res[:4]))
        parts[0].update(add_partials(0, early, early_grads, res[4:]))

    bwd_riders = _Riders({
        "dx_down": (lambda ctx: _swap_side(g1, 1), lambda res: parts[1].update(add_partials(1, everything, g1, res))),
        "dx_gate_up": exchange_rider([4, 5, 6, 7], 1),
        "attn_bwd": (attn_bwd_side, attn_bwd_take),
        "gw_in": exchange_rider([4], 0),
        "dx_in": exchange_rider([1, 2, 3, 5, 6, 7], 0)})
    dx, dxb, lg[0] = _layer_bwd(dm, dx, dxb, saved[0], lw[0], bwd_riders)
    both = sum_partials(1, parts[1], got[1], None)

    w_in_g = [grad_of(lg[0], 0)]
    parts[0].update(add_partials(0, [0], w_in_g, _run_side("swap_layer0_in_grads", _swap_side(w_in_g, 0))))
    got[0].update(zip([0], _run_side("exchange_layer0_in_grads", exchange(parts[0], [0], 0))))
    both = sum_partials(0, parts[0], got[0], both)
    grads = dict(zip(big_names, _share_reduced(both)))

    def stacked(key):
        return jnp.stack([lg[l][key] for l in range(depth)])

    small = ["norm_mix_g", "b_forget", "conv_b", "conv_ln_g", "conv_ln_b", "norm_ffn_g", "norm_ple_g", "final_g"]
    small_g = {
        "norm_mix_g": jnp.concatenate([lg[l]["n_mix"] for l in range(depth)]),
        "b_forget": stacked("b_f"),
        "conv_b": jnp.concatenate([lg[l]["conv_b"] for l in range(depth)]),
        "conv_ln_g": jnp.concatenate([lg[l]["ln_g"] for l in range(depth)]),
        "conv_ln_b": jnp.concatenate([lg[l]["ln_b"] for l in range(depth)]),
        "norm_ffn_g": jnp.concatenate([lg[l]["n_ffn"] for l in range(depth)]),
        "norm_ple_g": jnp.concatenate([lg[l]["n_ple"] for l in range(depth)]),
        "final_g": g_final[0],
    }
    conv_w_g = jnp.stack([lg[l]["conv_w"][:CONV_K] for l in range(depth)])
    small_shapes = [wts[n].shape for n in small] + [conv_w_g.shape]
    small_pack = _pad_rows(jnp.concatenate([small_g[n].reshape(-1) for n in small] + [conv_w_g.reshape(-1)]), 8)
    small_sum = _unpack(_allreduce_small(small_pack), small_shapes)
    for n, name in enumerate(small):
        grads[name] = small_sum[n]
    grads["conv_w"] = lax.dynamic_slice_in_dim(small_sum[len(small)], chip * cw, cw, axis=2)

    def pack_small(src):
        return _pad_rows(jnp.concatenate([src[n].reshape(-1) for n in small]), 8)

    small_upd = _adamw("adamw_small", pack_small(wts), pack_small(grads), pack_small(mom1), pack_small(mom2))
    small_upd = [_unpack(u, [wts[n].shape for n in small]) for u in small_upd]
    delta, new_m, new_v = {}, {}, {}
    for n, name in enumerate(small):
        delta[name], new_m[name], new_v[name] = small_upd[0][n], small_upd[1][n], small_upd[2][n]
    for name in big_names + ["conv_w"]:
        delta[name], new_m[name], new_v[name], grads[name] = _adamw(
            "adamw_" + name, wts[name], grads[name], mom1[name], mom2[name])

    return (loss, dx[None], *[grads[n] for n in order], *[delta[n] for n in order],
            *[new_m[n] for n in order], *[new_v[n] for n in order])
```

```python
import functools
import math

import jax
import jax.numpy as jnp
from jax import lax
from jax.experimental import pallas as pl
from jax.experimental.pallas import tpu as pltpu

F32 = jnp.float32
BF16 = jnp.bfloat16

EPS = 1e-6
CONV_K = 31
NEG_INF = -1e30
LOG2E = 1.4426950408889634
ADAM_LR = 0.001
ADAM_B1 = 0.9
ADAM_B2 = 0.999
ADAM_EPS = 1e-08
ADAM_WD = 0.01
ADAM_STEP = 10

LANES = 128
VMEM_LIMIT = 60 * 1024 * 1024
PACK_W = 1024
N_CHIPS = 4
CONV_HALO = 32
CUM_BLOCK = 256

MM_TM = 1024
MM_TN = 1024
MM_TK = 2048
ROW_TILE = 256
CONV_TILE = 256
ATTN_TILE = 1024
ATTN_SUB = 1024

MESH = pl.DeviceIdType.MESH
ANY = pl.BlockSpec(memory_space=pl.ANY)


def _params(*sem):
    return pltpu.CompilerParams(dimension_semantics=sem, vmem_limit_bytes=VMEM_LIMIT)


def _tile(dim, pref, align=LANES):
    if dim <= pref:
        return dim
    t = (pref // align) * align
    while t >= align:
        if dim % t == 0:
            return t
        t -= align
    return dim


def _sig(x):
    return 1.0 / (1.0 + jnp.exp(-x))


def _op(a):
    if not isinstance(a, tuple):
        return a, 0, ()
    return a if len(a) == 3 else (a[0], a[1], ())


def _spec(block, index, lead):
    if not lead:
        return pl.BlockSpec(block, index)
    return pl.BlockSpec((None,) * len(lead) + block, lambda *g: tuple(lead) + index(*g))


_DIMS = {
    "nn": (((1,), (0,)), ((), ())),
    "nt": (((1,), (1,)), ((), ())),
    "tn": (((0,), (0,)), ((), ())),
}


def _matmul(name, mode, a_ops, b_ops, terms, n_acc, m, n, k, out_dtypes, epi=None, extras=(),
            tm=None, tn=None, tk=None, side=None, cols_outer=False):
    tm = tm or _tile(m, MM_TM)
    tn = tn or _tile(n, MM_TN)
    tk = tk or _tile(k, MM_TK)
    ni, nj, nk = m // tm, n // tn, k // tk
    a_ops = [_op(a) for a in a_ops]
    b_ops = [_op(b) for b in b_ops]
    extras = [(kind, _op(e)) for kind, e in extras]
    na, nb, ne, no = len(a_ops), len(b_ops), len(extras), len(out_dtypes)
    n_acc_refs = n_acc if nk > 1 else 0
    s_ins = list(side.ins) if side else []
    s_outs = list(side.out_shapes) if side else []
    s_sems = [pltpu.SemaphoreType.DMA((cnt,)) for cnt in side.sem_counts] if side else []

    def at(index):
        return (lambda jj, ii, kk: index(ii, jj, kk)) if cols_outer else index

    def a_spec(off, lead):
        if mode == "tn":
            assert off % tm == 0
            return _spec((tk, tm), at(lambda i, j, kk: (kk, i + off // tm)), lead)
        assert off % tk == 0
        return _spec((tm, tk), at(lambda i, j, kk: (i, kk + off // tk)), lead)

    def b_spec(off, lead):
        if mode == "nt":
            assert off % tk == 0
            return _spec((tn, tk), at(lambda i, j, kk: (j, kk + off // tk)), lead)
        assert off % tn == 0
        return _spec((tk, tn), at(lambda i, j, kk: (kk, j + off // tn)), lead)

    def e_spec(kind, off, lead):
        assert off % tn == 0
        if kind == "n":
            return _spec((1, tn), at(lambda i, j, kk: (0, j + off // tn)), lead)
        return _spec((tm, tn), at(lambda i, j, kk: (i, j + off // tn)), lead)

    def body(*refs):
        refs = list(refs)
        a_refs, b_refs, e_refs, si_refs, o_refs, so_refs, acc_refs, sem_refs = (
            [refs.pop(0) for _ in range(cnt)]
            for cnt in (na, nb, ne, len(s_ins), no, len(s_outs), n_acc_refs, len(s_sems)))
        outer, inner, kk = pl.program_id(0), pl.program_id(1), pl.program_id(2)
        if side:
            @pl.when((outer == 0) & (inner == 0) & (kk == 0))
            def _():
                side.start(si_refs, so_refs, sem_refs)

        av = [r[...].astype(BF16) for r in a_refs]
        bv = [r[...].astype(BF16) for r in b_refs]
        sums = [None] * n_acc
        for ai, bi, ci in terms:
            part = lax.dot_general(av[ai], bv[bi], _DIMS[mode], preferred_element_type=F32)
            sums[ci] = part if sums[ci] is None else sums[ci] + part

        def finish(accs):
            outs = epi(accs, [e[...] for e in e_refs]) if epi is not None else accs
            for o, val in zip(o_refs, outs):
                o[...] = val.astype(o.dtype)

        if nk == 1:
            finish(sums)
        else:
            @pl.when(kk == 0)
            def _():
                for acc, part in zip(acc_refs, sums):
                    acc[...] = part

            @pl.when(kk > 0)
            def _():
                for acc, part in zip(acc_refs, sums):
                    acc[...] += part

            @pl.when(kk == nk - 1)
            def _():
                finish([acc[...] for acc in acc_refs])

        if side:
            @pl.when((outer == grid[0] - 1) & (inner == grid[1] - 1) & (kk == nk - 1))
            def _():
                side.finish(si_refs, so_refs, sem_refs)

    grid = (nj, ni, nk) if cols_outer else (ni, nj, nk)
    order = ("arbitrary",) * 3 if side else ("parallel", "parallel", "arbitrary")
    outs = pl.pallas_call(
        body,
        name=name,
        grid=grid,
        in_specs=[a_spec(off, lead) for _, off, lead in a_ops] + [b_spec(off, lead) for _, off, lead in b_ops]
        + [e_spec(kind, off, lead) for kind, (_, off, lead) in extras] + [ANY] * len(s_ins),
        out_specs=[pl.BlockSpec((tm, tn), at(lambda i, j, kk: (i, j))) for _ in out_dtypes] + [ANY] * len(s_outs),
        out_shape=[jax.ShapeDtypeStruct((m, n), dt) for dt in out_dtypes] + s_outs,
        scratch_shapes=[pltpu.VMEM((tm, tn), F32) for _ in range(n_acc_refs)] + s_sems,
        compiler_params=_params(*order),
    )(*[a[0] for a in a_ops], *[b[0] for b in b_ops], *[e[0] for _, e in extras], *s_ins)
    return outs


def _mm(name, mode, a, b, m, n, k, out_dtype, **kw):
    return _matmul(name, mode, [a], [b], [(0, 0, 0)], 1, m, n, k, [out_dtype], **kw)[0]


def _ew(name, fn, ins, m, n, out_dtypes, tm, tn=None):
    tn = tn or n
    ins = [_op(a) for a in ins]
    ni = len(ins)

    def spec(off, lead):
        assert off % tn == 0
        return _spec((tm, tn), lambda i, j: (i, j + off // tn), lead)

    def body(*refs):
        outs = fn(*[r[...] for r in refs[:ni]])
        for o, val in zip(refs[ni:], outs):
            o[...] = val.astype(o.dtype)

    return pl.pallas_call(
        body,
        name=name,
        grid=(m // tm, n // tn),
        in_specs=[spec(off, lead) for _, off, lead in ins],
        out_specs=[pl.BlockSpec((tm, tn), lambda i, j: (i, j)) for _ in out_dtypes],
        out_shape=[jax.ShapeDtypeStruct((m, n), dt) for dt in out_dtypes],
        compiler_params=_params("parallel", "parallel"),
    )(*[a[0] for a in ins])


def _rms_fwd(name, x, g, t, d):
    tr = _tile(t, ROW_TILE, 8)

    def body(x_ref, g_ref, h_ref):
        xv = x_ref[...]
        r = lax.rsqrt(jnp.mean(xv * xv, axis=1, keepdims=True) + EPS)
        h_ref[...] = (xv * r * g_ref[...]).astype(BF16)

    return pl.pallas_call(
        body,
        name=name,
        grid=(t // tr,),
        in_specs=[pl.BlockSpec((tr, d), lambda i: (i, 0)), pl.BlockSpec((1, d), lambda i: (0, 0))],
        out_specs=pl.BlockSpec((tr, d), lambda i: (i, 0)),
        out_shape=jax.ShapeDtypeStruct((t, d), BF16),
        compiler_params=_params("parallel"),
    )(x, g)


def _rms_bwd_rows(dh, xv, g):
    r = lax.rsqrt(jnp.mean(xv * xv, axis=1, keepdims=True) + EPS)
    xhat = xv * r
    dxh = dh * g
    dx = r * (dxh - xhat * jnp.mean(dxh * xhat, axis=1, keepdims=True))
    return dx, dh * xhat


def _rms_bwd(name, dh, x, g, dres, t, d):
    tr = _tile(t, ROW_TILE, 8)

    def body(dh_ref, x_ref, g_ref, dres_ref, dx_ref, dxb_ref, dg_ref):
        @pl.when(pl.program_id(0) == 0)
        def _():
            dg_ref[...] = jnp.zeros(dg_ref.shape, F32)

        dx, dg_rows = _rms_bwd_rows(dh_ref[...].astype(F32), x_ref[...], g_ref[...])
        dx = dx + dres_ref[...]
        dx_ref[...] = dx
        dxb_ref[...] = dx.astype(BF16)
        dg_ref[...] += jnp.sum(dg_rows, axis=0, keepdims=True)

    row = pl.BlockSpec((tr, d), lambda i: (i, 0))
    vec = pl.BlockSpec((1, d), lambda i: (0, 0))
    return pl.pallas_call(
        body,
        name=name,
        grid=(t // tr,),
        in_specs=[row, row, vec, row],
        out_specs=[row, row, vec],
        out_shape=[jax.ShapeDtypeStruct((t, d), F32), jax.ShapeDtypeStruct((t, d), BF16),
                   jax.ShapeDtypeStruct((1, d), F32)],
        compiler_params=_params("arbitrary"),
    )(dh, x, g, dres)


def _loss_head(x, g, target, t, d):
    tr = _tile(t, ROW_TILE, 8)

    def body(x_ref, g_ref, tgt_ref, loss_ref, dx_ref, dxb_ref, dg_ref):
        @pl.when(pl.program_id(0) == 0)
        def _():
            dg_ref[...] = jnp.zeros(dg_ref.shape, F32)
            loss_ref[...] = jnp.zeros(loss_ref.shape, F32)

        xv = x_ref[...]
        gv = g_ref[...]
        r = lax.rsqrt(jnp.mean(xv * xv, axis=1, keepdims=True) + EPS)
        err = xv * r * gv - tgt_ref[...]
        loss_ref[...] += (0.5 / d) * jnp.sum(err * err)
        dx, dg_rows = _rms_bwd_rows(err * (1.0 / d), xv, gv)
        dx_ref[...] = dx
        dxb_ref[...] = dx.astype(BF16)
        dg_ref[...] += jnp.sum(dg_rows, axis=0, keepdims=True)

    row = pl.BlockSpec((tr, d), lambda i: (i, 0))
    vec = pl.BlockSpec((1, d), lambda i: (0, 0))
    one = pl.BlockSpec((1, LANES), lambda i: (0, 0))
    return pl.pallas_call(
        body,
        name="loss_head",
        grid=(t // tr,),
        in_specs=[row, vec, row],
        out_specs=[one, row, row, vec],
        out_shape=[jax.ShapeDtypeStruct((1, LANES), F32), jax.ShapeDtypeStruct((t, d), F32),
                   jax.ShapeDtypeStruct((t, d), BF16), jax.ShapeDtypeStruct((1, d), F32)],
        compiler_params=_params("arbitrary"),
    )(x, g, target)


def _layernorm_rows(cv, g, b):
    mu = jnp.mean(cv, axis=1, keepdims=True)
    xc = cv - mu
    rstd = lax.rsqrt(jnp.mean(xc * xc, axis=1, keepdims=True) + EPS)
    xhat = xc * rstd
    return xhat, rstd, xhat * g + b


SUBLANES = 8


def _shift_copies(pad, shifted, span):
    for b in range(1, SUBLANES):
        shifted[b - 1, pl.ds(0, span), :] = pad[pl.ds(b, span), :]


def _tap(pad, shifted, offset, rows):
    b = offset % SUBLANES
    if b == 0:
        return pad[pl.ds(offset, rows), :]
    return shifted[b - 1, pl.ds(offset - b, rows), :]


def _conv_fwd(proj, conv_w, conv_b, ln_g, ln_b, t, c, a_off, g_off):
    tm = _tile(t, CONV_TILE, CONV_HALO)
    per = tm // CONV_HALO
    ab, gb = a_off // c, g_off // c
    span = tm + CONV_HALO - SUBLANES

    def body(a_ref, g_ref, ap_ref, gp_ref, w_ref, cb_ref, lg_ref, lb_ref, u2_ref, c_ref, upad, ushift):
        i = pl.program_id(0)
        u_prev = ap_ref[...].astype(F32) * _sig(gp_ref[...].astype(F32))
        upad[pl.ds(0, CONV_HALO), :] = jnp.where(i > 0, u_prev, 0.0)
        upad[pl.ds(CONV_HALO, tm), :] = a_ref[...].astype(F32) * _sig(g_ref[...].astype(F32))
        _shift_copies(upad, ushift, span)
        acc = jnp.zeros((tm, c), F32) + cb_ref[...]
        for k in range(CONV_K):
            acc = acc + w_ref[pl.ds(k, 1), :] * _tap(upad, ushift, CONV_HALO - (CONV_K - 1) + k, tm)
        c_ref[...] = acc
        _, _, z = _layernorm_rows(acc, lg_ref[...], lb_ref[...])
        u2_ref[...] = (z * _sig(z)).astype(BF16)

    vec = pl.BlockSpec((1, c), lambda i: (0, 0))
    return pl.pallas_call(
        body,
        name="conv_fwd",
        grid=(t // tm,),
        in_specs=[
            pl.BlockSpec((tm, c), lambda i: (i, ab)),
            pl.BlockSpec((tm, c), lambda i: (i, gb)),
            pl.BlockSpec((CONV_HALO, c), lambda i: (jnp.maximum(i * per - 1, 0), ab)),
            pl.BlockSpec((CONV_HALO, c), lambda i: (jnp.maximum(i * per - 1, 0), gb)),
            pl.BlockSpec((CONV_HALO, c), lambda i: (0, 0)), vec, vec, vec,
        ],
        out_specs=[pl.BlockSpec((tm, c), lambda i: (i, 0)), pl.BlockSpec((tm, c), lambda i: (i, 0))],
        out_shape=[jax.ShapeDtypeStruct((t, c), BF16), jax.ShapeDtypeStruct((t, c), F32)],
        scratch_shapes=[pltpu.VMEM((CONV_HALO + tm, c), F32), pltpu.VMEM((SUBLANES - 1, span, c), F32)],
        compiler_params=_params("parallel"),
    )(proj, proj, proj, proj, conv_w, conv_b, ln_g, ln_b)


def _conv_bwd(proj, cpre, du2, conv_w, ln_g, ln_b, t, c, a_off, g_off):
    tm = _tile(t, CONV_TILE, CONV_HALO)
    per = tm // CONV_HALO
    nt = t // tm
    last_halo = t // CONV_HALO - 1
    ab, gb = a_off // c, g_off // c
    span = tm + CONV_HALO - SUBLANES

    def body(a_ref, g_ref, ap_ref, gp_ref, c_ref, cn_ref, du_ref, dun_ref, w_ref, lg_ref, lb_ref,
             da_ref, dg_ref, gw_ref, gcb_ref, glg_ref, glb_ref, upad, dpad, ushift, dshift):
        i = pl.program_id(0)

        @pl.when(i == 0)
        def _():
            gw_ref[...] = jnp.zeros(gw_ref.shape, F32)
            gcb_ref[...] = jnp.zeros(gcb_ref.shape, F32)
            glg_ref[...] = jnp.zeros(glg_ref.shape, F32)
            glb_ref[...] = jnp.zeros(glb_ref.shape, F32)

        lg = lg_ref[...]
        lb = lb_ref[...]

        def ln_bwd(cv, duv):
            xhat, rstd, z = _layernorm_rows(cv, lg, lb)
            sz = _sig(z)
            dz = duv * (sz * (1.0 + z * (1.0 - sz)))
            dxh = dz * lg
            dc = rstd * (dxh - jnp.mean(dxh, axis=1, keepdims=True)
                         - xhat * jnp.mean(dxh * xhat, axis=1, keepdims=True))
            return dc, dz, xhat

        dc, dz, xhat = ln_bwd(c_ref[...], du_ref[...].astype(F32))
        dc_next, _, _ = ln_bwd(cn_ref[...], dun_ref[...].astype(F32))
        glg_ref[...] += jnp.sum(dz * xhat, axis=0, keepdims=True)
        glb_ref[...] += jnp.sum(dz, axis=0, keepdims=True)
        gcb_ref[...] += jnp.sum(dc, axis=0, keepdims=True)
        dpad[pl.ds(0, tm), :] = dc
        dpad[pl.ds(tm, CONV_HALO), :] = jnp.where(i < nt - 1, dc_next, 0.0)

        av = a_ref[...].astype(F32)
        sg = _sig(g_ref[...].astype(F32))
        u_prev = ap_ref[...].astype(F32) * _sig(gp_ref[...].astype(F32))
        upad[pl.ds(0, CONV_HALO), :] = jnp.where(i > 0, u_prev, 0.0)
        upad[pl.ds(CONV_HALO, tm), :] = av * sg
        _shift_copies(upad, ushift, span)
        _shift_copies(dpad, dshift, span)

        du = jnp.zeros((tm, c), F32)
        for k in range(CONV_K):
            du = du + w_ref[pl.ds(k, 1), :] * _tap(dpad, dshift, CONV_K - 1 - k, tm)
            gw_ref[pl.ds(k, 1), :] += jnp.sum(
                dc * _tap(upad, ushift, CONV_HALO - (CONV_K - 1) + k, tm), axis=0, keepdims=True)
        da_ref[...] = (du * sg).astype(BF16)
        dg_ref[...] = (du * av * sg * (1.0 - sg)).astype(BF16)

    vec = pl.BlockSpec((1, c), lambda i: (0, 0))
    cur = pl.BlockSpec((tm, c), lambda i: (i, 0))
    nxt = pl.BlockSpec((CONV_HALO, c), lambda i: (jnp.minimum((i + 1) * per, last_halo), 0))
    wsp = pl.BlockSpec((CONV_HALO, c), lambda i: (0, 0))
    return pl.pallas_call(
        body,
        name="conv_bwd",
        grid=(nt,),
        in_specs=[
            pl.BlockSpec((tm, c), lambda i: (i, ab)),
            pl.BlockSpec((tm, c), lambda i: (i, gb)),
            pl.BlockSpec((CONV_HALO, c), lambda i: (jnp.maximum(i * per - 1, 0), ab)),
            pl.BlockSpec((CONV_HALO, c), lambda i: (jnp.maximum(i * per - 1, 0), gb)),
            cur, nxt, cur, nxt, wsp, vec, vec,
        ],
        out_specs=[cur, cur, wsp, vec, vec, vec],
        out_shape=[jax.ShapeDtypeStruct((t, c), BF16), jax.ShapeDtypeStruct((t, c), BF16),
                   jax.ShapeDtypeStruct((CONV_HALO, c), F32), jax.ShapeDtypeStruct((1, c), F32),
                   jax.ShapeDtypeStruct((1, c), F32), jax.ShapeDtypeStruct((1, c), F32)],
        scratch_shapes=[pltpu.VMEM((CONV_HALO + tm, c), F32), pltpu.VMEM((tm + CONV_HALO, c), F32),
                        pltpu.VMEM((SUBLANES - 1, span, c), F32), pltpu.VMEM((SUBLANES - 1, span, c), F32)],
        compiler_params=_params("arbitrary"),
    )(proj, proj, proj, proj, cpre, cpre, du2, du2, conv_w, ln_g, ln_b)


def _split3(x):
    hi = x.astype(BF16).astype(F32)
    r1 = x - hi
    mid = r1.astype(BF16).astype(F32)
    lo = (r1 - mid).astype(BF16).astype(F32)
    return hi, mid, lo


def _split3_dot(x, tri):
    dot = functools.partial(jnp.dot, preferred_element_type=F32)
    hi, mid, lo = _split3(x)
    return dot(hi.astype(BF16), tri) + dot(mid.astype(BF16), tri) + dot(lo.astype(BF16), tri)


def _to_blocks(a, nb, blk):
    return a.reshape(a.shape[0], nb, blk).transpose(1, 0, 2)


def _from_blocks(a):
    return a.transpose(1, 0, 2).reshape(a.shape[1], -1)


def _forget_fwd(f_t, b_col, nh, t):
    blk = _tile(t, CUM_BLOCK)
    nb = t // blk

    def body(f_ref, b_ref, c_ref):
        ri = lax.broadcasted_iota(jnp.int32, (blk, blk), 0)
        ci = lax.broadcasted_iota(jnp.int32, (blk, blk), 1)
        tri = (ri <= ci).astype(BF16)

        def step(bi, carry):
            xv = f_ref[bi] + b_ref[:, :1]
            lf = jnp.minimum(xv, 0.0) - jnp.log(1.0 + jnp.exp(-jnp.abs(xv)))
            cs = _split3_dot(lf, tri) + carry
            c_ref[bi] = cs
            return cs[:, blk - 1:blk]

        lax.fori_loop(0, nb, step, jnp.zeros((nh, 1), F32))

    out = pl.pallas_call(
        body,
        name="forget_fwd",
        out_shape=jax.ShapeDtypeStruct((nb, nh, blk), F32),
        compiler_params=pltpu.CompilerParams(vmem_limit_bytes=VMEM_LIMIT),
    )(_to_blocks(f_t, nb, blk), b_col)
    return _from_blocks(out)


def _forget_bwd(dc, f_t, b_col, nh, t):
    blk = _tile(t, CUM_BLOCK)
    nb = t // blk

    def body(dc_ref, f_ref, b_ref, df_ref, db_ref):
        ri = lax.broadcasted_iota(jnp.int32, (blk, blk), 0)
        ci = lax.broadcasted_iota(jnp.int32, (blk, blk), 1)
        tri = (ri >= ci).astype(BF16)

        def step(n, carry):
            tail, db = carry
            bi = nb - 1 - n
            rc = _split3_dot(dc_ref[bi], tri) + tail
            df = rc * _sig(-(f_ref[bi] + b_ref[:, :1]))
            df_ref[bi] = df
            return rc[:, 0:1], db + jnp.sum(df, axis=1, keepdims=True)

        _, db = lax.fori_loop(0, nb, step, (jnp.zeros((nh, 1), F32), jnp.zeros((nh, 1), F32)))
        db_ref[...] = jnp.broadcast_to(db, db_ref.shape)

    df, db = pl.pallas_call(
        body,
        name="forget_bwd",
        out_shape=[jax.ShapeDtypeStruct((nb, nh, blk), F32), jax.ShapeDtypeStruct((nh, LANES), F32)],
        compiler_params=pltpu.CompilerParams(vmem_limit_bytes=VMEM_LIMIT),
    )(_to_blocks(dc, nb, blk), _to_blocks(f_t, nb, blk), b_col)
    return _from_blocks(df), db


def _lanes(parts, rows):
    lane = lax.broadcasted_iota(jnp.int32, (rows, LANES), 1)
    out = jnp.zeros((rows, LANES), F32)
    for n, part in enumerate(parts):
        out = jnp.where(lane == n, part, out)
    return out


def _attn_prep_fwd(proj, cs, t, nh, hd, q_off, k_off, v_off):
    tr = _tile(t, ATTN_TILE, 16)
    qb, kb, vb = q_off // hd, k_off // hd, v_off // hd

    def body(q_ref, k_ref, v_ref, cs_ref, qa_ref, ka_ref, va_ref):
        hi, mid, lo = _split3(cs_ref[0][:, :1])
        qa_ref[0, :, :hd] = q_ref[...]
        qa_ref[0, :, hd:] = _lanes([1.0, 1.0, 1.0, hi, mid, lo], tr).astype(BF16)
        ka_ref[0, :, :hd] = k_ref[...]
        ka_ref[0, :, hd:] = _lanes([-hi, -mid, -lo] + [1.0] * 6, tr).astype(BF16)
        va_ref[0, :, :hd] = v_ref[...]
        va_ref[0, :, hd:] = _lanes([-1.0, -1.0, -1.0], tr).astype(BF16)

    wide = pl.BlockSpec((1, tr, 2 * hd), lambda h, i: (h, i, 0))
    return pl.pallas_call(
        body,
        name="attn_prep_fwd",
        grid=(nh, t // tr),
        in_specs=[pl.BlockSpec((tr, hd), lambda h, i: (i, qb + h)), pl.BlockSpec((tr, hd), lambda h, i: (i, kb + h)),
                  pl.BlockSpec((tr, hd), lambda h, i: (i, vb + h)), pl.BlockSpec((1, tr, LANES), lambda h, i: (h, i, 0))],
        out_specs=[wide, wide, wide],
        out_shape=[jax.ShapeDtypeStruct((nh, t, 2 * hd), BF16)] * 3,
        compiler_params=_params("parallel", "parallel"),
    )(proj, proj, proj, cs)


def _attn_prep_bwd(qa, lse, o, do, t, nh, hd):
    tr = _tile(t, ATTN_TILE, 16)
    inv_scale = math.sqrt(hd)

    def body(qa_ref, lse_ref, o_ref, do_ref, qb_ref, da_ref):
        l_hi, l_mid, l_lo = _split3(lse_ref[0][:, :1] * (-inv_scale))
        lane = lax.broadcasted_iota(jnp.int32, (tr, LANES), 1)
        extra = qa_ref[0, :, hd:].astype(F32)
        extra = jnp.where(lane == 6, l_hi, jnp.where(lane == 7, l_mid, jnp.where(lane == 8, l_lo, extra)))
        qb_ref[0, :, :hd] = qa_ref[0, :, :hd]
        qb_ref[0, :, hd:] = extra.astype(BF16)
        dov = do_ref[...]
        delta = jnp.sum(dov.astype(F32) * o_ref[...].astype(F32), axis=1, keepdims=True)
        da_ref[0, :, :hd] = dov
        da_ref[0, :, hd:] = _lanes(list(_split3(delta)), tr).astype(BF16)

    wide = pl.BlockSpec((1, tr, 2 * hd), lambda h, i: (h, i, 0))
    head = pl.BlockSpec((tr, hd), lambda h, i: (i, h))
    return pl.pallas_call(
        body,
        name="attn_prep_bwd",
        grid=(nh, t // tr),
        in_specs=[wide, pl.BlockSpec((1, tr, LANES), lambda h, i: (h, i, 0)), head, head],
        out_specs=[wide, wide],
        out_shape=[jax.ShapeDtypeStruct((nh, t, 2 * hd), BF16)] * 2,
        compiler_params=_params("parallel", "parallel"),
    )(qa, lse, o, do)


def _causal(s, row0, rows, cols):
    row = lax.broadcasted_iota(jnp.int32, (rows, cols), 0) + row0
    col = lax.broadcasted_iota(jnp.int32, (rows, cols), 1)
    return jnp.where(col <= row, s, NEG_INF)


def _side_refs(side):
    if side is None:
        return [], [], []
    return list(side.ins), list(side.out_shapes), [pltpu.SemaphoreType.DMA((cnt,)) for cnt in side.sem_counts]


def _attn_fwd(qa, ka, proj, t, nh, hd, v_off, side=None):
    tq = _tile(t, ATTN_TILE)
    nq = t // tq
    sub = _tile(tq, ATTN_SUB)
    ns = tq // sub
    scale = 1.0 / math.sqrt(hd)
    vb = v_off // hd
    s_ins, s_outs, s_sems = _side_refs(side)

    def body(qa_ref, ka_ref, v_ref, *rest):
        si_refs, (o_ref, lse_ref) = rest[:len(s_ins)], rest[len(s_ins):len(s_ins) + 2]
        so_refs, sem_refs = rest[len(s_ins) + 2:len(s_ins) + 2 + len(s_outs)], rest[len(s_ins) + 2 + len(s_outs):]
        h, i = pl.program_id(0), pl.program_id(1)
        if side:
            @pl.when((h == 0) & (i == 0))
            def _():
                side.start(si_refs, so_refs, sem_refs)

        def tile(j, carry, masked):
            rows = pl.ds(pl.multiple_of(j * tq, tq), tq)
            kj = ka_ref[0, rows, :]
            vj = v_ref[rows, :]
            new = []
            for r in range(ns):
                m, l, acc = carry[r]
                s = lax.dot_general(qa_ref[0, pl.ds(r * sub, sub), :], kj, _DIMS["nt"], preferred_element_type=F32)
                if masked:
                    s = _causal(s, r * sub, sub, tq)
                m_new = jnp.maximum(m, jnp.max(s, axis=1, keepdims=True))
                p = jnp.exp2((s - m_new) * (scale * LOG2E))
                alpha = jnp.exp2((m - m_new) * (scale * LOG2E))
                l = alpha * l + jnp.sum(p, axis=1, keepdims=True)
                acc = alpha * acc + jnp.dot(p.astype(BF16), vj, preferred_element_type=F32)
                new.append((m_new, l, acc))
            return tuple(new)

        init = tuple((jnp.full((sub, 1), NEG_INF, F32), jnp.zeros((sub, 1), F32), jnp.zeros((sub, hd), F32))
                     for _ in range(ns))
        carry = lax.fori_loop(0, i, lambda j, cr: tile(j, cr, False), init)
        carry = tile(i, carry, True)
        for r in range(ns):
            m, l, acc = carry[r]
            o_ref[pl.ds(r * sub, sub), :] = (acc / l).astype(BF16)
            lse_ref[0, pl.ds(r * sub, sub), :] = jnp.broadcast_to(m * scale + jnp.log(l), (sub, LANES))
        if side:
            @pl.when((h == nh - 1) & (i == nq - 1))
            def _():
                side.finish(si_refs, so_refs, sem_refs)

    return pl.pallas_call(
        body,
        name="attn_fwd",
        grid=(nh, nq),
        in_specs=[pl.BlockSpec((1, tq, 2 * hd), lambda h, i: (h, i, 0)),
                  pl.BlockSpec((1, t, 2 * hd), lambda h, i: (h, 0, 0)),
                  pl.BlockSpec((t, hd), lambda h, i: (0, vb + h))] + [ANY] * len(s_ins),
        out_specs=[pl.BlockSpec((tq, hd), lambda h, i: (i, h)),
                   pl.BlockSpec((1, tq, LANES), lambda h, i: (h, i, 0))] + [ANY] * len(s_outs),
        out_shape=[jax.ShapeDtypeStruct((t, nh * hd), BF16), jax.ShapeDtypeStruct((nh, t, LANES), F32)] + s_outs,
        scratch_shapes=s_sems,
        compiler_params=_params(*(("arbitrary",) * 2 if side else ("parallel",) * 2)),
    )(qa, ka, proj, *s_ins)


def _attn_bwd(qb, ka, va, da, t, nh, hd, side=None):
    tq = _tile(t, ATTN_TILE)
    nq = t // tq
    sub = _tile(tq, ATTN_SUB)
    ns = tq // sub
    scale = 1.0 / math.sqrt(hd)
    s_ins, s_outs, s_sems = _side_refs(side)

    def body(qb_ref, ka_ref, va_ref, da_ref, *rest):
        si_refs, rest = rest[:len(s_ins)], rest[len(s_ins):]
        (dq_ref, dk_ref, dv_ref, dck_ref, dcq_ref), rest = rest[:5], rest[5:]
        so_refs, (dq_all, *sem_refs) = rest[:len(s_outs)], rest[len(s_outs):]
        h, j = pl.program_id(0), pl.program_id(1)
        if side:
            @pl.when((h == 0) & (j == 0))
            def _():
                side.start(si_refs, so_refs, sem_refs)

        @pl.when(j == 0)
        def _():
            dq_all[...] = jnp.zeros(dq_all.shape, F32)
            dcq_ref[...] = jnp.zeros(dcq_ref.shape, F32)

        kaj = ka_ref[0]
        vaj = va_ref[0]
        kj = kaj[:, :hd]

        def tile(i, carry, masked):
            dk, dv, dck = carry
            for r in range(ns):
                rows = pl.ds(pl.multiple_of(i * tq + r * sub, sub), sub)
                qr = qb_ref[0, rows, :]
                dr = da_ref[0, rows, :]
                s = lax.dot_general(qr, kaj, _DIMS["nt"], preferred_element_type=F32)
                if masked:
                    s = _causal(s, r * sub, sub, tq)
                p = jnp.exp2(s * (scale * LOG2E))
                ds = p * lax.dot_general(dr, vaj, _DIMS["nt"], preferred_element_type=F32)
                dsb = ds.astype(BF16)
                dv = dv + lax.dot_general(p.astype(BF16), dr[:, :hd], _DIMS["tn"], preferred_element_type=F32)
                dk = dk + lax.dot_general(dsb, qr[:, :hd], _DIMS["tn"], preferred_element_type=F32)
                dck = dck - jnp.sum(ds, axis=0, keepdims=True)
                dq_all[rows, :] += jnp.dot(dsb, kj, preferred_element_type=F32)
                dcq_ref[0, rows, :] += jnp.sum(ds, axis=1, keepdims=True)
            return dk, dv, dck

        carry = (jnp.zeros((tq, hd), F32), jnp.zeros((tq, hd), F32), jnp.zeros((1, tq), F32))
        carry = tile(j, carry, True)
        dk, dv, dck = lax.fori_loop(j + 1, nq, lambda i, cr: tile(i, cr, False), carry)
        dk_ref[...] = (dk * scale).astype(BF16)
        dv_ref[...] = dv.astype(BF16)
        dck_ref[0] = dck

        @pl.when(j == nq - 1)
        def _():
            dq_ref[...] = (dq_all[...] * scale).astype(BF16)

        if side:
            @pl.when((h == nh - 1) & (j == nq - 1))
            def _():
                side.finish(si_refs, so_refs, sem_refs)

    whole = pl.BlockSpec((1, t, 2 * hd), lambda h, j: (h, 0, 0))
    block = pl.BlockSpec((1, tq, 2 * hd), lambda h, j: (h, j, 0))
    return pl.pallas_call(
        body,
        name="attn_bwd",
        grid=(nh, nq),
        in_specs=[whole, block, block, whole] + [ANY] * len(s_ins),
        out_specs=[
            pl.BlockSpec((t, hd), lambda h, j: (0, h)),
            pl.BlockSpec((tq, hd), lambda h, j: (j, h)),
            pl.BlockSpec((tq, hd), lambda h, j: (j, h)),
            pl.BlockSpec((1, 1, tq), lambda h, j: (h, 0, j)),
            pl.BlockSpec((1, t, LANES), lambda h, j: (h, 0, 0)),
        ] + [ANY] * len(s_outs),
        out_shape=[jax.ShapeDtypeStruct((t, nh * hd), BF16), jax.ShapeDtypeStruct((t, nh * hd), BF16),
                   jax.ShapeDtypeStruct((t, nh * hd), BF16), jax.ShapeDtypeStruct((nh, 1, t), F32),
                   jax.ShapeDtypeStruct((nh, t, LANES), F32)] + s_outs,
        scratch_shapes=[pltpu.VMEM((t, hd), F32)] + s_sems,
        compiler_params=_params("arbitrary", "arbitrary"),
    )(qb, ka, va, da, *s_ins)


class _Dims:
    def __init__(self, t, d, c, nh, aw, f, pd):
        self.t, self.d, self.c, self.nh, self.aw, self.f, self.pd = t, d, c, nh, aw, f, pd
        self.hd = aw // nh
        self.a_off, self.g_off = 0, c
        self.q_off, self.k_off, self.v_off = 2 * c, 2 * c + aw, 2 * c + 2 * aw
        self.gc_off = 2 * c + 3 * aw
        self.ga_off = self.gc_off + d
        self.n_main = self.ga_off + d


def _ffn_tn(f):
    return _tile(f, 1536)


class _Riders:
    def __init__(self, plan=None):
        self.plan = plan or {}

    def host(self, name, n_out, call, ctx=None):
        if name not in self.plan:
            return call()
        make_side, take = self.plan[name]
        outs = call(side=make_side(ctx))
        take(outs[n_out:])
        return outs[:n_out]


class _Weights:
    def __init__(self, fixed, gathered, views):
        self.fixed, self.gathered, self.views = fixed, gathered, views

    def __getitem__(self, key):
        if key in self.fixed:
            return self.fixed[key]
        name, off = self.views[key]
        return self.gathered[name], off


def _layer_fwd(dm, x, p, w, riders):
    t, d, c, f = dm.t, dm.d, dm.c, dm.f
    s = {"x": x}
    h = _rms_fwd("rms_mix", x, w["n_mix"], t, d)
    proj, = riders.host("mm_in", 1, functools.partial(
        _matmul, "mm_in", "nn", [h], [w["main"]], [(0, 0, 0)], 1, t, dm.n_main, d, [BF16]))
    fl = _mm("mm_forget", "nn", h, w["f"], t, LANES, d, F32)
    u2, cpre = _conv_fwd(proj, w["conv_w"], w["conv_b"], w["ln_g"], w["ln_b"], t, c, dm.a_off, dm.g_off)
    f_t = fl[:, :dm.nh].T
    cum = _forget_fwd(f_t, w["b_f"], dm.nh, t)
    cs = jnp.broadcast_to((cum * math.sqrt(dm.hd))[:, :, None], (dm.nh, t, LANES))
    qa, ka, va = _attn_prep_fwd(proj, cs, t, dm.nh, dm.hd, dm.q_off, dm.k_off, dm.v_off)
    o, lse = riders.host("attn_fwd", 2, functools.partial(_attn_fwd, qa, ka, proj, t, dm.nh, dm.hd, dm.v_off))

    def epi_conv(accs, ex):
        return accs[0], _sig(ex[0].astype(F32)) * accs[0]

    yc, m1 = _matmul("mm_conv_out", "nn", [u2], [w["co"]], [(0, 0, 0)], 1, t, d, c, [BF16, BF16],
                     epi=epi_conv, extras=[("mn", (proj, dm.gc_off))], tm=512)

    def epi_attn(accs, ex):
        return accs[0], ex[1].astype(F32) + _sig(ex[0].astype(F32)) * accs[0]

    ya, merged = _matmul("mm_attn_out", "nn", [o], [w["ao"]], [(0, 0, 0)], 1, t, d, dm.aw, [BF16, BF16],
                         epi=epi_attn, extras=[("mn", (proj, dm.ga_off)), ("mn", m1)], tm=512)
    x1 = _matmul("mm_out", "nn", [merged], [w["o"]], [(0, 0, 0)], 1, t, d, d, [F32],
                 epi=lambda accs, ex: [ex[0] + accs[0]], extras=[("mn", x)], tm=512, cols_outer=True)[0]

    hf = _rms_fwd("rms_ffn", x1, w["n_ffn"], t, d)

    def epi_glu(accs, ex):
        gate, up = accs
        return gate, up, gate * _sig(gate) * up

    gate, up, act = riders.host("mm_gate_up", 3, functools.partial(
        _matmul, "mm_gate_up", "nn", [hf], [w["g"], w["u"]], [(0, 0, 0), (0, 1, 1)], 2, t, f, d, [BF16, BF16, BF16],
        epi=epi_glu, tm=512, tn=_ffn_tn(f), cols_outer=True))
    x2, = riders.host("mm_down", 1, functools.partial(
        _matmul, "mm_down", "nn", [act], [w["d"]], [(0, 0, 0)], 1, t, d, f, [F32],
        epi=lambda accs, ex: [ex[0] + accs[0]], extras=[("mn", x1)], tm=512, tk=f, cols_outer=True))

    hp = _rms_fwd("rms_ple", x2, w["n_ple"], t, d)
    pp = _mm("mm_ple_proj", "nn", p, w["pp"], t, d, dm.pd, BF16, tm=512)

    def epi_ple(accs, ex):
        sg = _sig(accs[0])
        return sg, ex[1] + sg * ex[0].astype(F32)

    sg, x3 = _matmul("mm_ple_gate", "nn", [hp], [w["pg"]], [(0, 0, 0)], 1, t, d, d, [BF16, F32],
                     epi=epi_ple, extras=[("mn", pp), ("mn", x2)], tm=512, cols_outer=True)
    s.update(h=h, proj=proj, f_t=f_t, qa=qa, ka=ka, va=va, u2=u2, cpre=cpre, o=o, lse=lse, yc=yc, ya=ya,
             merged=merged, x1=x1, hf=hf, gate=gate, up=up, act=act, x2=x2, hp=hp, pp=pp, sg=sg, p=p)
    return x3, s


def _layer_bwd(dm, dx3, dx3b, s, w, riders):
    t, d, c, f = dm.t, dm.d, dm.c, dm.f
    g = {}

    def ple_ew(dxv, sgv, ppv):
        sgf, ppf = sgv.astype(F32), ppv.astype(F32)
        return dxv * sgf, dxv * ppf * sgf * (1.0 - sgf)

    d_pp, d_z = _ew("ple_bwd", ple_ew, [dx3, s["sg"], s["pp"]], t, d, [BF16, BF16], _tile(t, ROW_TILE, 8))
    g["pp"] = _mm("gw_ple_proj", "tn", s["p"], d_pp, dm.pd, d, t, F32)
    g["pg"] = _mm("gw_ple_gate", "tn", s["hp"], d_z, d, d, t, F32)
    d_hp = _mm("dx_ple_gate", "nt", d_z, w["pg"], t, d, d, BF16)
    dx2, dx2b, g["n_ple"] = _rms_bwd("rms_ple_bwd", d_hp, s["x2"], w["n_ple"], dx3, t, d)

    def epi_dglu(accs, ex):
        gate, up = ex[0].astype(F32), ex[1].astype(F32)
        sg = _sig(gate)
        return accs[0] * up * (sg * (1.0 + gate * (1.0 - sg))), accs[0] * gate * sg

    d_gate, d_up = riders.host("dx_down", 2, functools.partial(
        _matmul, "dx_down", "nt", [dx2b], [w["d"]], [(0, 0, 0)], 1, t, f, d, [BF16, BF16], epi=epi_dglu,
        extras=[("mn", s["gate"]), ("mn", s["up"])], tm=512, tn=_ffn_tn(f), cols_outer=True))
    g["d"] = _mm("gw_down", "tn", s["act"], dx2b, f, d, t, F32, tm=_ffn_tn(f))
    g["g"] = _mm("gw_gate", "tn", s["hf"], d_gate, d, f, t, F32, tn=_ffn_tn(f))
    g["u"] = _mm("gw_up", "tn", s["hf"], d_up, d, f, t, F32, tn=_ffn_tn(f))
    d_hf, = riders.host("dx_gate_up", 1, functools.partial(
        _matmul, "dx_gate_up", "nt", [d_gate, d_up], [w["g"], w["u"]], [(0, 0, 0), (1, 1, 0)], 1, t, d, f, [BF16]))
    dx1, dx1b, g["n_ffn"] = _rms_bwd("rms_ffn_bwd", d_hf, s["x1"], w["n_ffn"], dx2, t, d)

    g["o"] = _mm("gw_out", "tn", s["merged"], dx1b, d, d, t, F32)

    def epi_dmerge(accs, ex):
        dmv = accs[0]
        sgc, sga = _sig(ex[0].astype(F32)), _sig(ex[1].astype(F32))
        ycv, yav = ex[2].astype(F32), ex[3].astype(F32)
        return dmv * sgc, dmv * sga, dmv * ycv * sgc * (1.0 - sgc), dmv * yav * sga * (1.0 - sga)

    d_yc, d_ya, d_gc, d_ga = _matmul(
        "dx_out", "nt", [dx1b], [w["o"]], [(0, 0, 0)], 1, t, d, d, [BF16] * 4, epi=epi_dmerge,
        extras=[("mn", (s["proj"], dm.gc_off)), ("mn", (s["proj"], dm.ga_off)), ("mn", s["yc"]), ("mn", s["ya"])],
        tm=512, tn=_tile(d, 512))
    g["co"] = _mm("gw_conv_out", "tn", s["u2"], d_yc, c, d, t, F32)
    d_u2 = _mm("dx_conv_out", "nt", d_yc, w["co"], t, c, d, BF16)
    g["ao"] = _mm("gw_attn_out", "tn", s["o"], d_ya, dm.aw, d, t, F32)
    d_o = _mm("dx_attn_out", "nt", d_ya, w["ao"], t, dm.aw, d, BF16)

    qb, da = _attn_prep_bwd(s["qa"], s["lse"], s["o"], d_o, t, dm.nh, dm.hd)
    dq, dk, dv, dck, dcq = riders.host(
        "attn_bwd", 5, functools.partial(_attn_bwd, qb, s["ka"], s["va"], da, t, dm.nh, dm.hd), ctx=g)
    d_ft, g_bf = _forget_bwd(dck.reshape(dm.nh, t) + dcq[:, :, 0], s["f_t"], w["b_f"], dm.nh, t)
    g["b_f"] = g_bf[:, 0]
    d_f = jnp.pad(d_ft.T, ((0, 0), (0, LANES - dm.nh))).astype(BF16)

    d_a, d_gg, g["conv_w"], g["conv_b"], g["ln_g"], g["ln_b"] = _conv_bwd(
        s["proj"], s["cpre"], d_u2, w["conv_w"], w["ln_g"], w["ln_b"], t, c, dm.a_off, dm.g_off)

    d_proj = jnp.concatenate([d_a, d_gg, dq, dk, dv, d_gc, d_ga], axis=1)
    g["main"], = riders.host("gw_in", 1, functools.partial(
        _matmul, "gw_in", "tn", [s["h"]], [d_proj], [(0, 0, 0)], 1, d, dm.n_main, t, [F32], tk=_tile(t, 4096)))
    g["f"] = _mm("gw_forget", "tn", s["h"], d_f, d, LANES, t, F32)
    d_h_f = _mm("dx_forget", "nt", d_f, w["f"], t, d, LANES, BF16)
    d_h, = riders.host("dx_in", 1, functools.partial(
        _matmul, "dx_in", "nt", [d_proj], [w["main"]], [(0, 0, 0)], 1, t, d, dm.n_main, [BF16],
        epi=lambda accs, ex: [accs[0] + ex[0].astype(F32)], extras=[("mn", d_h_f)], tk=_tile(dm.n_main, 3072)))
    dx, dxb, g["n_mix"] = _rms_bwd("rms_mix_bwd", d_h, s["x"], w["n_mix"], dx1, t, d)
    return dx, dxb, g


def _adamw_tiles(wv, gv, mv, vv):
    m_new = ADAM_B1 * mv + (1.0 - ADAM_B1) * gv
    v_new = ADAM_B2 * vv + (1.0 - ADAM_B2) * (gv * gv)
    m_hat = m_new / (1.0 - ADAM_B1 ** ADAM_STEP)
    v_hat = v_new / (1.0 - ADAM_B2 ** ADAM_STEP)
    delta = -ADAM_LR * (m_hat / (jnp.sqrt(v_hat) + ADAM_EPS) + ADAM_WD * wv)
    return delta, m_new, v_new, gv


def _adamw(name, wv, gv, mv, vv):
    shape = wv.shape
    cols = shape[-1]
    rows = wv.size // cols
    tm = _tile(rows, max(8, (1 << 18) // cols), 8)
    flat = [a.reshape(rows, cols) for a in (wv, gv, mv, vv)]
    outs = _ew(name, _adamw_tiles, flat, rows, cols, [F32, F32, F32, F32], tm)
    return [o.reshape(shape) for o in outs]


def _place():
    return lax.axis_index("x"), lax.axis_index("y"), lax.axis_index("c")


def _other_chips(x, y):
    return [(1 - x, y), (x, 1 - y), (1 - x, 1 - y)]


def _window(ref, kind, chip, size):
    if kind == "chip":
        return ref.at[chip]
    if kind == "row":
        return ref.at[pl.ds(chip * size, size), :]
    return ref.at[:, pl.ds(pl.multiple_of(chip * size, LANES), size)]


class _Side:
    def __init__(self, ins, out_shapes, sem_counts, start, finish):
        self.ins, self.out_shapes, self.sem_counts, self.start, self.finish = ins, out_shapes, sem_counts, start, finish

    def join(self, other):
        ni, no, ns = len(self.ins), len(self.out_shapes), len(self.sem_counts)

        def both(first, second):
            def run(ins, outs, sems):
                first(ins[:ni], outs[:no], sems[:ns])
                second(ins[ni:], outs[no:], sems[ns:])
            return run

        return _Side(self.ins + other.ins, self.out_shapes + other.out_shapes, self.sem_counts + other.sem_counts,
                     both(self.start, other.start), both(self.finish, other.finish))


def _run_side(name, side):
    n_in, n_out = len(side.ins), len(side.out_shapes)

    def body(*refs):
        ins, outs, sems = refs[:n_in], refs[n_in:n_in + n_out], refs[n_in + n_out:]
        side.start(ins, outs, sems)
        side.finish(ins, outs, sems)

    return pl.pallas_call(
        body,
        name=name,
        in_specs=[ANY] * n_in,
        out_specs=[ANY] * n_out,
        out_shape=list(side.out_shapes),
        scratch_shapes=[pltpu.SemaphoreType.DMA((cnt,)) for cnt in side.sem_counts],
    )(*side.ins)


def _full_shape(shard, kind):
    _, a, b = shard.shape
    if kind == "chip":
        return (N_CHIPS, a, b)
    return (N_CHIPS * a, b) if kind == "row" else (a, N_CHIPS * b)


def _gather_side(shards, kinds, layer):
    n = len(shards)
    sizes = [s.shape[1] if k == "row" else s.shape[2] for s, k in zip(shards, kinds)]

    def copies(ins, outs, sems, forwards):
        ici_send, ici_recv, d2d_send, d2d_recv, own_send, own_recv = sems
        x, y, c = _place()
        me = 2 * x + y
        sibling = (x, y, 1 - c)
        own, ici, landed, fwd = [], [], [], []
        for w in range(n):
            mine = _window(outs[w], kinds[w], me, sizes[w])
            own.append(pltpu.make_async_remote_copy(
                src_ref=ins[w].at[layer], dst_ref=mine, send_sem=own_send.at[w], recv_sem=own_recv.at[w],
                device_id=sibling, device_id_type=MESH))
            for r, (px, py) in enumerate(_other_chips(x, y)):
                ici.append(pltpu.make_async_remote_copy(
                    src_ref=ins[w].at[layer], dst_ref=mine, send_sem=ici_send.at[3 * w + r],
                    recv_sem=ici_recv.at[3 * w + r], device_id=(px, py, layer), device_id_type=MESH))
                if forwards:
                    slab = _window(outs[w], kinds[w], 2 * px + py, sizes[w])
                    landed.append(pltpu.make_async_remote_copy(
                        src_ref=slab, dst_ref=slab, send_sem=ici_send.at[3 * w + r], recv_sem=ici_recv.at[3 * w + r],
                        device_id=(px, py, layer), device_id_type=MESH))
                    fwd.append(pltpu.make_async_remote_copy(
                        src_ref=slab, dst_ref=slab, send_sem=d2d_send.at[3 * w + r], recv_sem=d2d_recv.at[3 * w + r],
                        device_id=sibling, device_id_type=MESH))
        return c, own, ici, landed, fwd

    def start(ins, outs, sems):
        c, own, ici, _, _ = copies(ins, outs, sems, False)
        for cp in own:
            cp.start()

        @pl.when(c == layer)
        def _():
            for cp in ici:
                cp.start()

    def finish(ins, outs, sems):
        c, own, ici, landed, fwd = copies(ins, outs, sems, True)

        @pl.when(c == layer)
        def _():
            for got, cp in zip(landed, fwd):
                got.wait_recv()
                cp.start()
            for cp in ici + fwd:
                cp.wait_send()

        @pl.when(c != layer)
        def _():
            for cp in fwd:
                cp.wait_recv()

        for cp in own:
            cp.wait()

    out_shapes = [jax.ShapeDtypeStruct(_full_shape(s, k), s.dtype) for s, k in zip(shards, kinds)]
    return _Side(list(shards), out_shapes, [3 * n] * 4 + [n] * 2, start, finish)


def _swap_side(grads, layer):
    n = len(grads)

    def copies(ins, outs, sems):
        x, y, c = _place()
        return c, [pltpu.make_async_remote_copy(src_ref=ins[w], dst_ref=outs[w], send_sem=sems[0].at[w],
                                                recv_sem=sems[1].at[w], device_id=(x, y, layer), device_id_type=MESH)
                   for w in range(n)]

    def start(ins, outs, sems):
        c, cps = copies(ins, outs, sems)

        @pl.when(c != layer)
        def _():
            for cp in cps:
                cp.start()

    def finish(ins, outs, sems):
        c, cps = copies(ins, outs, sems)

        @pl.when(c != layer)
        def _():
            for cp in cps:
                cp.wait_send()

        @pl.when(c == layer)
        def _():
            for cp in cps:
                cp.wait_recv()

    return _Side(list(grads), [jax.ShapeDtypeStruct(a.shape, a.dtype) for a in grads], [n, n], start, finish)


def _add_core_partials(name, mine, got):
    shape = got.shape
    cols = shape[-1]
    rows = got.size // cols
    tm, tn = _tile(rows, 512, 8), _tile(cols, 2048)
    return _ew(name, lambda a, b: [a + b], [mine.reshape(rows, cols), got.reshape(rows, cols)], rows, cols,
               [BF16], tm, tn)[0].reshape(shape)


def _exchange_side(parts, kinds, sizes, layer):
    n = len(parts)

    def copies(ins, outs, sems):
        x, y, c = _place()
        cps = []
        for w in range(n):
            for r, (px, py) in enumerate(_other_chips(x, y)):
                cps.append(pltpu.make_async_remote_copy(
                    src_ref=_window(ins[w], kinds[w], 2 * px + py, sizes[w]), dst_ref=outs[w].at[r],
                    send_sem=sems[0].at[3 * w + r], recv_sem=sems[1].at[3 * w + r],
                    device_id=(px, py, layer), device_id_type=MESH))
        return c, cps

    def start(ins, outs, sems):
        c, cps = copies(ins, outs, sems)

        @pl.when(c == layer)
        def _():
            for cp in cps:
                cp.start()

    def finish(ins, outs, sems):
        c, cps = copies(ins, outs, sems)

        @pl.when(c == layer)
        def _():
            for cp in cps:
                cp.wait()

    out_shapes = [jax.ShapeDtypeStruct((3,) + tuple(_shard_shape(p, k, s)), p.dtype)
                  for p, k, s in zip(parts, kinds, sizes)]
    return _Side(list(parts), out_shapes, [3 * n, 3 * n], start, finish)


def _shard_shape(whole, kind, size):
    if kind == "chip":
        return whole.shape[1:]
    return (size, whole.shape[1]) if kind == "row" else (whole.shape[0], size)


def _sum_chip_partials(name, part, got, kind, size, chip, layer, both=None):
    rows, cols = _shard_shape(part, kind, size)
    tm = _tile(rows, max(8, (1 << 19) // cols), 16)

    def body(chip_ref, part_ref, g0_ref, g1_ref, g2_ref, *rest):
        total = part_ref[...].astype(F32)
        for ref in (g0_ref, g1_ref, g2_ref):
            total = total + ref[...].astype(F32)
        rest[-1][...] = total

    if kind == "chip":
        mine = pl.BlockSpec((None, tm, cols), lambda i, chip_ref: (chip_ref[0], i, 0))
    elif kind == "row":
        mine = pl.BlockSpec((tm, cols), lambda i, chip_ref: (chip_ref[0] * (rows // tm) + i, 0))
    else:
        mine = pl.BlockSpec((tm, cols), lambda i, chip_ref: (i, chip_ref[0]))
    theirs = [pl.BlockSpec((None, tm, cols), functools.partial(lambda r, i, chip_ref: (r, i, 0), r))
              for r in range(3)]
    kept = [] if both is None else [both]
    return pl.pallas_call(
        body,
        name=name,
        grid_spec=pltpu.PrefetchScalarGridSpec(
            num_scalar_prefetch=1, grid=(rows // tm,), in_specs=[mine] + theirs + [ANY] * len(kept),
            out_specs=pl.BlockSpec((None, tm, cols), lambda i, chip_ref: (layer, i, 0))),
        out_shape=jax.ShapeDtypeStruct((2, rows, cols), F32),
        input_output_aliases={5: 0} if kept else {},
        compiler_params=_params("parallel"),
    )(chip.reshape(1), part, got, got, got, *kept)


def _share_reduced(both):
    n = len(both)

    def body(*refs):
        ins, outs = refs[:n], refs[n:2 * n]
        send_sems, recv_sems = refs[2 * n:]
        x, y, c = _place()
        sends = []
        for w in range(n):
            cp = pltpu.make_async_remote_copy(src_ref=ins[w].at[c], dst_ref=outs[w].at[c], send_sem=send_sems.at[w],
                                              recv_sem=recv_sems.at[w], device_id=(x, y, 1 - c), device_id_type=MESH)
            cp.start()
            sends.append(cp)
        for w in range(n):
            got = outs[w].at[1 - c]
            pltpu.make_async_remote_copy(src_ref=got, dst_ref=got, send_sem=send_sems.at[w], recv_sem=recv_sems.at[w],
                                         device_id=(x, y, 1 - c), device_id_type=MESH).wait_recv()
        for cp in sends:
            cp.wait_send()

    return pl.pallas_call(
        body,
        name="share_reduced",
        in_specs=[ANY] * n,
        out_specs=[ANY] * n,
        out_shape=[jax.ShapeDtypeStruct(a.shape, a.dtype) for a in both],
        input_output_aliases={w: w for w in range(n)},
        scratch_shapes=[pltpu.SemaphoreType.DMA((n,)), pltpu.SemaphoreType.DMA((n,))],
    )(*both)


def _allreduce_small(v):
    rows, width = v.shape

    def body(v_ref, out_ref, slots, send_sems, recv_sems):
        x, y, c = _place()
        me = 4 * x + 2 * y + c
        slots[me] = v_ref[...]
        peers = [(x, y, 1 - c)]
        for px, py in _other_chips(x, y):
            peers += [(px, py, c), (px, py, 1 - c)]
        sends = []
        for r, peer in enumerate(peers):
            cp = pltpu.make_async_remote_copy(
                src_ref=v_ref, dst_ref=slots.at[me], send_sem=send_sems.at[r], recv_sem=recv_sems.at[r],
                device_id=peer, device_id_type=MESH)
            cp.start()
            sends.append(cp)
        for r, (px, py, pc) in enumerate(peers):
            got = slots.at[4 * px + 2 * py + pc]
            pltpu.make_async_remote_copy(
                src_ref=got, dst_ref=got, send_sem=send_sems.at[r], recv_sem=recv_sems.at[r],
                device_id=(px, py, pc), device_id_type=MESH).wait_recv()
        for cp in sends:
            cp.wait_send()
        total = slots[0]
        for n in range(1, 8):
            total = total + slots[n]
        out_ref[...] = total

    vm = pl.BlockSpec(memory_space=pltpu.VMEM)
    return pl.pallas_call(
        body,
        name="allreduce_small",
        in_specs=[vm],
        out_specs=vm,
        out_shape=jax.ShapeDtypeStruct((rows, width), F32),
        scratch_shapes=[pltpu.VMEM((8, rows, width), F32), pltpu.SemaphoreType.DMA((7,)),
                        pltpu.SemaphoreType.DMA((7,))],
    )(v)


def _pad_rows(flat, row_align):
    n = flat.shape[0]
    rows = -(-n // PACK_W)
    rows = -(-rows // row_align) * row_align
    return jnp.pad(flat, (0, rows * PACK_W - n)).reshape(rows, PACK_W)


def _unpack(buf, shapes):
    flat = buf.reshape(-1)
    out, off = [], 0
    for shp in shapes:
        n = math.prod(shp)
        out.append(flat[off:off + n].reshape(shp))
        off += n
    return out


BIG = (("w_in", "chip"), ("w_conv_out", "col"), ("w_attn_out", "col"), ("w_out", "row"), ("w_gate_up", "col"),
       ("w_down", "row"), ("w_ple_gate", "row"), ("w_ple_proj", "col"))


def kernel(x, p, norm_mix_g, w_in, b_forget, conv_w, conv_b, conv_ln_g, conv_ln_b, w_conv_out, w_attn_out, w_out, norm_ffn_g, w_gate_up, w_down, norm_ple_g, w_ple_gate, w_ple_proj, final_g, loss_target, m_norm_mix_g, m_w_in, m_b_forget, m_conv_w, m_conv_b, m_conv_ln_g, m_conv_ln_b, m_w_conv_out, m_w_attn_out, m_w_out, m_norm_ffn_g, m_w_gate_up, m_w_down, m_norm_ple_g, m_w_ple_gate, m_w_ple_proj, m_final_g, v_norm_mix_g, v_w_in, v_b_forget, v_conv_w, v_conv_b, v_conv_ln_g, v_conv_ln_b, v_w_conv_out, v_w_attn_out, v_w_out, v_norm_ffn_g, v_w_gate_up, v_w_down, v_norm_ple_g, v_w_ple_gate, v_w_ple_proj, v_final_g):
    wts = dict(norm_mix_g=norm_mix_g, w_in=w_in, b_forget=b_forget, conv_w=conv_w, conv_b=conv_b,
               conv_ln_g=conv_ln_g, conv_ln_b=conv_ln_b, w_conv_out=w_conv_out, w_attn_out=w_attn_out,
               w_out=w_out, norm_ffn_g=norm_ffn_g, w_gate_up=w_gate_up, w_down=w_down, norm_ple_g=norm_ple_g,
               w_ple_gate=w_ple_gate, w_ple_proj=w_ple_proj, final_g=final_g)
    mom1 = dict(norm_mix_g=m_norm_mix_g, w_in=m_w_in, b_forget=m_b_forget, conv_w=m_conv_w, conv_b=m_conv_b,
                conv_ln_g=m_conv_ln_g, conv_ln_b=m_conv_ln_b, w_conv_out=m_w_conv_out, w_attn_out=m_w_attn_out,
                w_out=m_w_out, norm_ffn_g=m_norm_ffn_g, w_gate_up=m_w_gate_up, w_down=m_w_down,
                norm_ple_g=m_norm_ple_g, w_ple_gate=m_w_ple_gate, w_ple_proj=m_w_ple_proj, final_g=m_final_g)
    mom2 = dict(norm_mix_g=v_norm_mix_g, w_in=v_w_in, b_forget=v_b_forget, conv_w=v_conv_w, conv_b=v_conv_b,
                conv_ln_g=v_conv_ln_g, conv_ln_b=v_conv_ln_b, w_conv_out=v_w_conv_out, w_attn_out=v_w_attn_out,
                w_out=v_w_out, norm_ffn_g=v_norm_ffn_g, w_gate_up=v_w_gate_up, w_down=v_w_down,
                norm_ple_g=v_norm_ple_g, w_ple_gate=v_w_ple_gate, w_ple_proj=v_w_ple_proj, final_g=v_final_g)
    order = list(wts)
    depth = w_in.shape[0]
    assert depth == 2, "the exchanges give one layer to each of a chip's two cores"
    t, d = x.shape[1], x.shape[2]
    c = conv_ln_g.shape[1]
    nh = b_forget.shape[1]
    aw = w_attn_out.shape[1]
    f = N_CHIPS * w_down.shape[1]
    pd = w_ple_proj.shape[1]
    dm = _Dims(t, d, c, nh, aw, f, pd)
    n_split = 2 * c + 3 * aw
    cw = conv_w.shape[2]
    chip = 2 * lax.axis_index("x") + lax.axis_index("y")
    big_names = [name for name, _ in BIG]
    big_kinds = [kind for _, kind in BIG]

    shards = {name: wts[name].astype(BF16) for name in big_names}
    shards["conv_w"] = conv_w
    kind_of = dict(BIG, conv_w="chip")
    full = [{}, {}]

    def gather_rider(names, layer):
        def make_side(ctx=None):
            return _gather_side([shards[n] for n in names], [kind_of[n] for n in names], layer)

        def take(results):
            full[layer].update(zip(names, results))

        return make_side, take

    make_side, take = gather_rider(["w_in", "conv_w"], 0)
    take(_run_side("gather_layer0_in", make_side()))
    fwd_riders = [_Riders({"attn_fwd": gather_rider(big_names[1:], 0),
                           "mm_in": gather_rider(["w_in", "w_conv_out", "w_out", "conv_w"], 1),
                           "mm_gate_up": gather_rider(["w_gate_up", "w_ple_gate", "w_attn_out"], 1),
                           "mm_down": gather_rider(["w_down", "w_ple_proj"], 1)}),
                  _Riders()]
    views = {"co": ("w_conv_out", 0), "ao": ("w_attn_out", 0), "o": ("w_out", 0), "g": ("w_gate_up", 0),
             "u": ("w_gate_up", f), "d": ("w_down", 0), "pg": ("w_ple_gate", 0), "pp": ("w_ple_proj", 0)}

    def layer_weights(l):
        fw = full[l]
        wi = jnp.concatenate([fw["w_in"][k] for k in range(N_CHIPS)], axis=1)
        cwl = jnp.concatenate([fw["conv_w"][k] for k in range(N_CHIPS)], axis=1)
        return _Weights({
            "main": jnp.concatenate([wi[:, :n_split], wi[:, n_split + nh:]], axis=1),
            "f": jnp.pad(wi[:, n_split:n_split + nh], ((0, 0), (0, LANES - nh))),
            "conv_w": jnp.pad(cwl, ((0, CONV_HALO - CONV_K), (0, 0))),
            "conv_b": conv_b[l][None], "ln_g": conv_ln_g[l][None], "ln_b": conv_ln_b[l][None],
            "b_f": jnp.broadcast_to(b_forget[l][:, None], (nh, LANES)),
            "n_mix": norm_mix_g[l][None], "n_ffn": norm_ffn_g[l][None], "n_ple": norm_ple_g[l][None],
        }, fw, views)

    xl = x[0]
    saved, lw = [], []
    for l in range(depth):
        lw.append(layer_weights(l))
        xl, s = _layer_fwd(dm, xl, p[l, 0], lw[l], fwd_riders[l])
        saved.append(s)
    loss_row, dx, dxb, g_final = _loss_head(xl, final_g[None], loss_target[0], t, d)
    loss = lax.psum(loss_row[0, 0], ("x", "y", "c"))

    sizes = [wts[name].shape[1] if kind == "row" else wts[name].shape[2] for name, kind in BIG]
    everything = list(range(len(BIG)))
    plain = {1: "co", 2: "ao", 3: "o", 5: "d", 6: "pg", 7: "pp"}

    def grad_of(g, w):
        if w == 0:
            w_in_g = jnp.concatenate([g["main"][:, :n_split], g["f"][:, :nh], g["main"][:, n_split:]], axis=1)
            return w_in_g.reshape(d, N_CHIPS, -1).transpose(1, 0, 2)
        return jnp.concatenate([g["g"], g["u"]], axis=1) if w == 4 else g[plain[w]]

    def add_partials(layer, group, mine, got):
        return {w: _add_core_partials(f"add_core_partials_l{layer}_{big_names[w]}", a, b)
                for w, a, b in zip(group, mine, got)}

    def exchange(parts, group, layer):
        return _exchange_side([parts[w] for w in group], [big_kinds[w] for w in group], [sizes[w] for w in group], layer)

    def sum_partials(layer, parts, got, both):
        return [_sum_chip_partials(f"sum_chip_partials_l{layer}_{big_names[w]}", parts[w], got[w], big_kinds[w],
                                   sizes[w], chip, layer, both=None if both is None else both[w])
                for w in everything]

    lg = [None] * depth
    dx, dxb, lg[1] = _layer_bwd(dm, dx, dxb, saved[1], lw[1], _Riders())
    g1 = [grad_of(lg[1], w) for w in everything]
    parts = [{}, {}]
    got = [{}, {}]
    early = everything[1:]

    def exchange_rider(group, layer):
        return (lambda ctx: exchange(parts[layer], group, layer)), (lambda res: got[layer].update(zip(group, res)))

    early_grads = []

    def attn_bwd_side(g):
        early_grads.extend(grad_of(g, w) for w in early)
        return exchange(parts[1], [0, 1, 2, 3], 1).join(_swap_side(early_grads, 0))

    def attn_bwd_take(res):
        got[1].update(zip([0, 1, 2, 3], res[:4]))
        parts[0].update(add_partials(0, early, early_grads, res[4:]))

    bwd_riders = _Riders({
        "dx_down": (lambda ctx: _swap_side(g1, 1), lambda res: parts[1].update(add_partials(1, everything, g1, res))),
        "dx_gate_up": exchange_rider([4, 5, 6, 7], 1),
        "attn_bwd": (attn_bwd_side, attn_bwd_take),
        "gw_in": exchange_rider([4], 0),
        "dx_in": exchange_rider([1, 2, 3, 5, 6, 7], 0)})
    dx, dxb, lg[0] = _layer_bwd(dm, dx, dxb, saved[0], lw[0], bwd_riders)
    both = sum_partials(1, parts[1], got[1], None)

    w_in_g = [grad_of(lg[0], 0)]
    parts[0].update(add_partials(0, [0], w_in_g, _run_side("swap_layer0_in_grads", _swap_side(w_in_g, 0))))
    got[0].update(zip([0], _run_side("exchange_layer0_in_grads", exchange(parts[0], [0], 0))))
    both = sum_partials(0, parts[0], got[0], both)
    grads = dict(zip(big_names, _share_reduced(both)))

    def stacked(key):
        return jnp.stack([lg[l][key] for l in range(depth)])

    small = ["norm_mix_g", "b_forget", "conv_b", "conv_ln_g", "conv_ln_b", "norm_ffn_g", "norm_ple_g", "final_g"]
    small_g = {
        "norm_mix_g": jnp.concatenate([lg[l]["n_mix"] for l in range(depth)]),
        "b_forget": stacked("b_f"),
        "conv_b": jnp.concatenate([lg[l]["conv_b"] for l in range(depth)]),
        "conv_ln_g": jnp.concatenate([lg[l]["ln_g"] for l in range(depth)]),
        "conv_ln_b": jnp.concatenate([lg[l]["ln_b"] for l in range(depth)]),
        "norm_ffn_g": jnp.concatenate([lg[l]["n_ffn"] for l in range(depth)]),
        "norm_ple_g": jnp.concatenate([lg[l]["n_ple"] for l in range(depth)]),
        "final_g": g_final[0],
    }
    conv_w_g = jnp.stack([lg[l]["conv_w"][:CONV_K] for l in range(depth)])
    small_shapes = [wts[n].shape for n in small] + [conv_w_g.shape]
    small_pack = _pad_rows(jnp.concatenate([small_g[n].reshape(-1) for n in small] + [conv_w_g.reshape(-1)]), 8)
    small_sum = _unpack(_allreduce_small(small_pack), small_shapes)
    for n, name in enumerate(small):
        grads[name] = small_sum[n]
    grads["conv_w"] = lax.dynamic_slice_in_dim(small_sum[len(small)], chip * cw, cw, axis=2)

    def pack_small(src):
        return _pad_rows(jnp.concatenate([src[n].reshape(-1) for n in small]), 8)

    small_upd = _adamw("adamw_small", pack_small(wts), pack_small(grads), pack_small(mom1), pack_small(mom2))
    small_upd = [_unpack(u, [wts[n].shape for n in small]) for u in small_upd]
    delta, new_m, new_v = {}, {}, {}
    for n, name in enumerate(small):
        delta[name], new_m[name], new_v[name] = small_upd[0][n], small_upd[1][n], small_upd[2][n]
    for name in big_names + ["conv_w"]:
        delta[name], new_m[name], new_v[name], grads[name] = _adamw(
            "adamw_" + name, wts[name], grads[name], mom1[name], mom2[name])

    return (loss, dx[None], *[grads[n] for n in order], *[delta[n] for n in order],
            *[new_m[n] for n in order], *[new_v[n] for n in order])
```

```python
import functools
import math

import jax
import jax.numpy as jnp
from jax import lax
from jax.experimental import pallas as pl
from jax.experimental.pallas import tpu as pltpu

F32 = jnp.float32
BF16 = jnp.bfloat16

EPS = 1e-6
CONV_K = 31
NEG_INF = -1e30
LOG2E = 1.4426950408889634
ADAM_LR = 0.001
ADAM_B1 = 0.9
ADAM_B2 = 0.999
ADAM_EPS = 1e-08
ADAM_WD = 0.01
ADAM_STEP = 10

LANES = 128
VMEM_LIMIT = 60 * 1024 * 1024
PACK_W = 1024
N_CHIPS = 4
CONV_HALO = 32
CUM_BLOCK = 256

MM_TM = 1024
MM_TN = 1024
MM_TK = 2048
ROW_TILE = 256
CONV_TILE = 256
ATTN_TILE = 1024
ATTN_SUB = 1024

MESH = pl.DeviceIdType.MESH
ANY = pl.BlockSpec(memory_space=pl.ANY)


def _params(*sem):
    return pltpu.CompilerParams(dimension_semantics=sem, vmem_limit_bytes=VMEM_LIMIT)


def _tile(dim, pref, align=LANES):
    if dim <= pref:
        return dim
    t = (pref // align) * align
    while t >= align:
        if dim % t == 0:
            return t
        t -= align
    return dim


def _sig(x):
    return 1.0 / (1.0 + jnp.exp(-x))


def _op(a):
    if not isinstance(a, tuple):
        return a, 0, ()
    return a if len(a) == 3 else (a[0], a[1], ())


def _spec(block, index, lead):
    if not lead:
        return pl.BlockSpec(block, index)
    return pl.BlockSpec((None,) * len(lead) + block, lambda *g: tuple(lead) + index(*g))


_DIMS = {
    "nn": (((1,), (0,)), ((), ())),
    "nt": (((1,), (1,)), ((), ())),
    "tn": (((0,), (0,)), ((), ())),
}


def _matmul(name, mode, a_ops, b_ops, terms, n_acc, m, n, k, out_dtypes, epi=None, extras=(),
            tm=None, tn=None, tk=None, side=None, cols_outer=False):
    tm = tm or _tile(m, MM_TM)
    tn = tn or _tile(n, MM_TN)
    tk = tk or _tile(k, MM_TK)
    ni, nj, nk = m // tm, n // tn, k // tk
    a_ops = [_op(a) for a in a_ops]
    b_ops = [_op(b) for b in b_ops]
    extras = [(kind, _op(e)) for kind, e in extras]
    na, nb, ne, no = len(a_ops), len(b_ops), len(extras), len(out_dtypes)
    n_acc_refs = n_acc if nk > 1 else 0
    s_ins = list(side.ins) if side else []
    s_outs = list(side.out_shapes) if side else []
    s_sems = [pltpu.SemaphoreType.DMA((cnt,)) for cnt in side.sem_counts] if side else []

    def at(index):
        return (lambda jj, ii, kk: index(ii, jj, kk)) if cols_outer else index

    def a_spec(off, lead):
        if mode == "tn":
            assert off % tm == 0
            return _spec((tk, tm), at(lambda i, j, kk: (kk, i + off // tm)), lead)
        assert off % tk == 0
        return _spec((tm, tk), at(lambda i, j, kk: (i, kk + off // tk)), lead)

    def b_spec(off, lead):
        if mode == "nt":
            assert off % tk == 0
            return _spec((tn, tk), at(lambda i, j, kk: (j, kk + off // tk)), lead)
        assert off % tn == 0
        return _spec((tk, tn), at(lambda i, j, kk: (kk, j + off // tn)), lead)

    def e_spec(kind, off, lead):
        assert off % tn == 0
        if kind == "n":
            return _spec((1, tn), at(lambda i, j, kk: (0, j + off // tn)), lead)
        return _spec((tm, tn), at(lambda i, j, kk: (i, j + off // tn)), lead)

    def body(*refs):
        refs = list(refs)
        a_refs, b_refs, e_refs, si_refs, o_refs, so_refs, acc_refs, sem_refs = (
            [refs.pop(0) for _ in range(cnt)]
            for cnt in (na, nb, ne, len(s_ins), no, len(s_outs), n_acc_refs, len(s_sems)))
        outer, inner, kk = pl.program_id(0), pl.program_id(1), pl.program_id(2)
        if side:
            @pl.when((outer == 0) & (inner == 0) & (kk == 0))
            def _():
                side.start(si_refs, so_refs, sem_refs)

        av = [r[...].astype(BF16) for r in a_refs]
        bv = [r[...].astype(BF16) for r in b_refs]
        sums = [None] * n_acc
        for ai, bi, ci in terms:
            part = lax.dot_general(av[ai], bv[bi], _DIMS[mode], preferred_element_type=F32)
            sums[ci] = part if sums[ci] is None else sums[ci] + part

        def finish(accs):
            outs = epi(accs, [e[...] for e in e_refs]) if epi is not None else accs
            for o, val in zip(o_refs, outs):
                o[...] = val.astype(o.dtype)

        if nk == 1:
            finish(sums)
        else:
            @pl.when(kk == 0)
            def _():
                for acc, part in zip(acc_refs, sums):
                    acc[...] = part

            @pl.when(kk > 0)
            def _():
                for acc, part in zip(acc_refs, sums):
                    acc[...] += part

            @pl.when(kk == nk - 1)
            def _():
                finish([acc[...] for acc in acc_refs])

        if side:
            @pl.when((outer == grid[0] - 1) & (inner == grid[1] - 1) & (kk == nk - 1))
            def _():
                side.finish(si_refs, so_refs, sem_refs)

    grid = (nj, ni, nk) if cols_outer else (ni, nj, nk)
    order = ("arbitrary",) * 3 if side else ("parallel", "parallel", "arbitrary")
    outs = pl.pallas_call(
        body,
        name=name,
        grid=grid,
        in_specs=[a_spec(off, lead) for _, off, lead in a_ops] + [b_spec(off, lead) for _, off, lead in b_ops]
        + [e_spec(kind, off, lead) for kind, (_, off, lead) in extras] + [ANY] * len(s_ins),
        out_specs=[pl.BlockSpec((tm, tn), at(lambda i, j, kk: (i, j))) for _ in out_dtypes] + [ANY] * len(s_outs),
        out_shape=[jax.ShapeDtypeStruct((m, n), dt) for dt in out_dtypes] + s_outs,
        scratch_shapes=[pltpu.VMEM((tm, tn), F32) for _ in range(n_acc_refs)] + s_sems,
        compiler_params=_params(*order),
    )(*[a[0] for a in a_ops], *[b[0] for b in b_ops], *[e[0] for _, e in extras], *s_ins)
    return outs


def _mm(name, mode, a, b, m, n, k, out_dtype, **kw):
    return _matmul(name, mode, [a], [b], [(0, 0, 0)], 1, m, n, k, [out_dtype], **kw)[0]


def _ew(name, fn, ins, m, n, out_dtypes, tm, tn=None):
    tn = tn or n
    ins = [_op(a) for a in ins]
    ni = len(ins)

    def spec(off, lead):
        assert off % tn == 0
        return _spec((tm, tn), lambda i, j: (i, j + off // tn), lead)

    def body(*refs):
        outs = fn(*[r[...] for r in refs[:ni]])
        for o, val in zip(refs[ni:], outs):
            o[...] = val.astype(o.dtype)

    return pl.pallas_call(
        body,
        name=name,
        grid=(m // tm, n // tn),
        in_specs=[spec(off, lead) for _, off, lead in ins],
        out_specs=[pl.BlockSpec((tm, tn), lambda i, j: (i, j)) for _ in out_dtypes],
        out_shape=[jax.ShapeDtypeStruct((m, n), dt) for dt in out_dtypes],
        compiler_params=_params("parallel", "parallel"),
    )(*[a[0] for a in ins])


def _rms_fwd(name, x, g, t, d):
    tr = _tile(t, ROW_TILE, 8)

    def body(x_ref, g_ref, h_ref):
        xv = x_ref[...]
        r = lax.rsqrt(jnp.mean(xv * xv, axis=1, keepdims=True) + EPS)
        h_ref[...] = (xv * r * g_ref[...]).astype(BF16)

    return pl.pallas_call(
        body,
        name=name,
        grid=(t // tr,),
        in_specs=[pl.BlockSpec((tr, d), lambda i: (i, 0)), pl.BlockSpec((1, d), lambda i: (0, 0))],
        out_specs=pl.BlockSpec((tr, d), lambda i: (i, 0)),
        out_shape=jax.ShapeDtypeStruct((t, d), BF16),
        compiler_params=_params("parallel"),
    )(x, g)


def _rms_bwd_rows(dh, xv, g):
    r = lax.rsqrt(jnp.mean(xv * xv, axis=1, keepdims=True) + EPS)
    xhat = xv * r
    dxh = dh * g
    dx = r * (dxh - xhat * jnp.mean(dxh * xhat, axis=1, keepdims=True))
    return dx, dh * xhat


def _rms_bwd(name, dh, x, g, dres, t, d):
    tr = _tile(t, ROW_TILE, 8)

    def body(dh_ref, x_ref, g_ref, dres_ref, dx_ref, dxb_ref, dg_ref):
        @pl.when(pl.program_id(0) == 0)
        def _():
            dg_ref[...] = jnp.zeros(dg_ref.shape, F32)

        dx, dg_rows = _rms_bwd_rows(dh_ref[...].astype(F32), x_ref[...], g_ref[...])
        dx = dx + dres_ref[...]
        dx_ref[...] = dx
        dxb_ref[...] = dx.astype(BF16)
        dg_ref[...] += jnp.sum(dg_rows, axis=0, keepdims=True)

    row = pl.BlockSpec((tr, d), lambda i: (i, 0))
    vec = pl.BlockSpec((1, d), lambda i: (0, 0))
    return pl.pallas_call(
        body,
        name=name,
        grid=(t // tr,),
        in_specs=[row, row, vec, row],
        out_specs=[row, row, vec],
        out_shape=[jax.ShapeDtypeStruct((t, d), F32), jax.ShapeDtypeStruct((t, d), BF16),
                   jax.ShapeDtypeStruct((1, d), F32)],
        compiler_params=_params("arbitrary"),
    )(dh, x, g, dres)


def _loss_head(x, g, target, t, d):
    tr = _tile(t, ROW_TILE, 8)

    def body(x_ref, g_ref, tgt_ref, loss_ref, dx_ref, dxb_ref, dg_ref):
        @pl.when(pl.program_id(0) == 0)
        def _():
            dg_ref[...] = jnp.zeros(dg_ref.shape, F32)
            loss_ref[...] = jnp.zeros(loss_ref.shape, F32)

        xv = x_ref[...]
        gv = g_ref[...]
        r = lax.rsqrt(jnp.mean(xv * xv, axis=1, keepdims=True) + EPS)
        err = xv * r * gv - tgt_ref[...]
        loss_ref[...] += (0.5 / d) * jnp.sum(err * err)
        dx, dg_rows = _rms_bwd_rows(err * (1.0 / d), xv, gv)
        dx_ref[...] = dx
        dxb_ref[...] = dx.astype(BF16)
        dg_ref[...] += jnp.sum(dg_rows, axis=0, keepdims=True)

    row = pl.BlockSpec((tr, d), lambda i: (i, 0))
    vec = pl.BlockSpec((1, d), lambda i: (0, 0))
    one = pl.BlockSpec((1, LANES), lambda i: (0, 0))
    return pl.pallas_call(
        body,
        name="loss_head",
        grid=(t // tr,),
        in_specs=[row, vec, row],
        out_specs=[one, row, row, vec],
        out_shape=[jax.ShapeDtypeStruct((1, LANES), F32), jax.ShapeDtypeStruct((t, d), F32),
                   jax.ShapeDtypeStruct((t, d), BF16), jax.ShapeDtypeStruct((1, d), F32)],
        compiler_params=_params("arbitrary"),
    )(x, g, target)


def _layernorm_rows(cv, g, b):
    mu = jnp.mean(cv, axis=1, keepdims=True)
    xc = cv - mu
    rstd = lax.rsqrt(jnp.mean(xc * xc, axis=1, keepdims=True) + EPS)
    xhat = xc * rstd
    return xhat, rstd, xhat * g + b


SUBLANES = 8


def _shift_copies(pad, shifted, span):
    for b in range(1, SUBLANES):
        shifted[b - 1, pl.ds(0, span), :] = pad[pl.ds(b, span), :]


def _tap(pad, shifted, offset, rows):
    b = offset % SUBLANES
    if b == 0:
        return pad[pl.ds(offset, rows), :]
    return shifted[b - 1, pl.ds(offset - b, rows), :]


def _conv_fwd(proj, conv_w, conv_b, ln_g, ln_b, t, c, a_off, g_off):
    tm = _tile(t, CONV_TILE, CONV_HALO)
    per = tm // CONV_HALO
    ab, gb = a_off // c, g_off // c
    span = tm + CONV_HALO - SUBLANES

    def body(a_ref, g_ref, ap_ref, gp_ref, w_ref, cb_ref, lg_ref, lb_ref, u2_ref, c_ref, upad, ushift):
        i = pl.program_id(0)
        u_prev = ap_ref[...].astype(F32) * _sig(gp_ref[...].astype(F32))
        upad[pl.ds(0, CONV_HALO), :] = jnp.where(i > 0, u_prev, 0.0)
        upad[pl.ds(CONV_HALO, tm), :] = a_ref[...].astype(F32) * _sig(g_ref[...].astype(F32))
        _shift_copies(upad, ushift, span)
        acc = jnp.zeros((tm, c), F32) + cb_ref[...]
        for k in range(CONV_K):
            acc = acc + w_ref[pl.ds(k, 1), :] * _tap(upad, ushift, CONV_HALO - (CONV_K - 1) + k, tm)
        c_ref[...] = acc
        _, _, z = _layernorm_rows(acc, lg_ref[...], lb_ref[...])
        u2_ref[...] = (z * _sig(z)).astype(BF16)

    vec = pl.BlockSpec((1, c), lambda i: (0, 0))
    return pl.pallas_call(
        body,
        name="conv_fwd",
        grid=(t // tm,),
        in_specs=[
            pl.BlockSpec((tm, c), lambda i: (i, ab)),
            pl.BlockSpec((tm, c), lambda i: (i, gb)),
            pl.BlockSpec((CONV_HALO, c), lambda i: (jnp.maximum(i * per - 1, 0), ab)),
            pl.BlockSpec((CONV_HALO, c), lambda i: (jnp.maximum(i * per - 1, 0), gb)),
            pl.BlockSpec((CONV_HALO, c), lambda i: (0, 0)), vec, vec, vec,
        ],
        out_specs=[pl.BlockSpec((tm, c), lambda i: (i, 0)), pl.BlockSpec((tm, c), lambda i: (i, 0))],
        out_shape=[jax.ShapeDtypeStruct((t, c), BF16), jax.ShapeDtypeStruct((t, c), F32)],
        scratch_shapes=[pltpu.VMEM((CONV_HALO + tm, c), F32), pltpu.VMEM((SUBLANES - 1, span, c), F32)],
        compiler_params=_params("parallel"),
    )(proj, proj, proj, proj, conv_w, conv_b, ln_g, ln_b)


def _conv_bwd(proj, cpre, du2, conv_w, ln_g, ln_b, t, c, a_off, g_off):
    tm = _tile(t, CONV_TILE, CONV_HALO)
    per = tm // CONV_HALO
    nt = t // tm
    last_halo = t // CONV_HALO - 1
    ab, gb = a_off // c, g_off // c
    span = tm + CONV_HALO - SUBLANES

    def body(a_ref, g_ref, ap_ref, gp_ref, c_ref, cn_ref, du_ref, dun_ref, w_ref, lg_ref, lb_ref,
             da_ref, dg_ref, gw_ref, gcb_ref, glg_ref, glb_ref, upad, dpad, ushift, dshift):
        i = pl.program_id(0)

        @pl.when(i == 0)
        def _():
            gw_ref[...] = jnp.zeros(gw_ref.shape, F32)
            gcb_ref[...] = jnp.zeros(gcb_ref.shape, F32)
            glg_ref[...] = jnp.zeros(glg_ref.shape, F32)
            glb_ref[...] = jnp.zeros(glb_ref.shape, F32)

        lg = lg_ref[...]
        lb = lb_ref[...]

        def ln_bwd(cv, duv):
            xhat, rstd, z = _layernorm_rows(cv, lg, lb)
            sz = _sig(z)
            dz = duv * (sz * (1.0 + z * (1.0 - sz)))
            dxh = dz * lg
            dc = rstd * (dxh - jnp.mean(dxh, axis=1, keepdims=True)
                         - xhat * jnp.mean(dxh * xhat, axis=1, keepdims=True))
            return dc, dz, xhat

        dc, dz, xhat = ln_bwd(c_ref[...], du_ref[...].astype(F32))
        dc_next, _, _ = ln_bwd(cn_ref[...], dun_ref[...].astype(F32))
        glg_ref[...] += jnp.sum(dz * xhat, axis=0, keepdims=True)
        glb_ref[...] += jnp.sum(dz, axis=0, keepdims=True)
        gcb_ref[...] += jnp.sum(dc, axis=0, keepdims=True)
        dpad[pl.ds(0, tm), :] = dc
        dpad[pl.ds(tm, CONV_HALO), :] = jnp.where(i < nt - 1, dc_next, 0.0)

        av = a_ref[...].astype(F32)
        sg = _sig(g_ref[...].astype(F32))
        u_prev = ap_ref[...].astype(F32) * _sig(gp_ref[...].astype(F32))
        upad[pl.ds(0, CONV_HALO), :] = jnp.where(i > 0, u_prev, 0.0)
        upad[pl.ds(CONV_HALO, tm), :] = av * sg
        _shift_copies(upad, ushift, span)
        _shift_copies(dpad, dshift, span)

        du = jnp.zeros((tm, c), F32)
        for k in range(CONV_K):
            du = du + w_ref[pl.ds(k, 1), :] * _tap(dpad, dshift, CONV_K - 1 - k, tm)
            gw_ref[pl.ds(k, 1), :] += jnp.sum(
                dc * _tap(upad, ushift, CONV_HALO - (CONV_K - 1) + k, tm), axis=0, keepdims=True)
        da_ref[...] = (du * sg).astype(BF16)
        dg_ref[...] = (du * av * sg * (1.0 - sg)).astype(BF16)

    vec = pl.BlockSpec((1, c), lambda i: (0, 0))
    cur = pl.BlockSpec((tm, c), lambda i: (i, 0))
    nxt = pl.BlockSpec((CONV_HALO, c), lambda i: (jnp.minimum((i + 1) * per, last_halo), 0))
    wsp = pl.BlockSpec((CONV_HALO, c), lambda i: (0, 0))
    return pl.pallas_call(
        body,
        name="conv_bwd",
        grid=(nt,),
        in_specs=[
            pl.BlockSpec((tm, c), lambda i: (i, ab)),
            pl.BlockSpec((tm, c), lambda i: (i, gb)),
            pl.BlockSpec((CONV_HALO, c), lambda i: (jnp.maximum(i * per - 1, 0), ab)),
            pl.BlockSpec((CONV_HALO, c), lambda i: (jnp.maximum(i * per - 1, 0), gb)),
            cur, nxt, cur, nxt, wsp, vec, vec,
        ],
        out_specs=[cur, cur, wsp, vec, vec, vec],
        out_shape=[jax.ShapeDtypeStruct((t, c), BF16), jax.ShapeDtypeStruct((t, c), BF16),
                   jax.ShapeDtypeStruct((CONV_HALO, c), F32), jax.ShapeDtypeStruct((1, c), F32),
                   jax.ShapeDtypeStruct((1, c), F32), jax.ShapeDtypeStruct((1, c), F32)],
        scratch_shapes=[pltpu.VMEM((CONV_HALO + tm, c), F32), pltpu.VMEM((tm + CONV_HALO, c), F32),
                        pltpu.VMEM((SUBLANES - 1, span, c), F32), pltpu.VMEM((SUBLANES - 1, span, c), F32)],
        compiler_params=_params("arbitrary"),
    )(proj, proj, proj, proj, cpre, cpre, du2, du2, conv_w, ln_g, ln_b)


def _split3(x):
    hi = x.astype(BF16).astype(F32)
    r1 = x - hi
    mid = r1.astype(BF16).astype(F32)
    lo = (r1 - mid).astype(BF16).astype(F32)
    return hi, mid, lo


def _split3_dot(x, tri):
    dot = functools.partial(jnp.dot, preferred_element_type=F32)
    hi, mid, lo = _split3(x)
    return dot(hi.astype(BF16), tri) + dot(mid.astype(BF16), tri) + dot(lo.astype(BF16), tri)


def _to_blocks(a, nb, blk):
    return a.reshape(a.shape[0], nb, blk).transpose(1, 0, 2)


def _from_blocks(a):
    return a.transpose(1, 0, 2).reshape(a.shape[1], -1)


def _forget_fwd(f_t, b_col, nh, t):
    blk = _tile(t, CUM_BLOCK)
    nb = t // blk

    def body(f_ref, b_ref, c_ref):
        ri = lax.broadcasted_iota(jnp.int32, (blk, blk), 0)
        ci = lax.broadcasted_iota(jnp.int32, (blk, blk), 1)
        tri = (ri <= ci).astype(BF16)

        def step(bi, carry):
            xv = f_ref[bi] + b_ref[:, :1]
            lf = jnp.minimum(xv, 0.0) - jnp.log(1.0 + jnp.exp(-jnp.abs(xv)))
            cs = _split3_dot(lf, tri) + carry
            c_ref[bi] = cs
            return cs[:, blk - 1:blk]

        lax.fori_loop(0, nb, step, jnp.zeros((nh, 1), F32))

    out = pl.pallas_call(
        body,
        name="forget_fwd",
        out_shape=jax.ShapeDtypeStruct((nb, nh, blk), F32),
        compiler_params=pltpu.CompilerParams(vmem_limit_bytes=VMEM_LIMIT),
    )(_to_blocks(f_t, nb, blk), b_col)
    return _from_blocks(out)


def _forget_bwd(dc, f_t, b_col, nh, t):
    blk = _tile(t, CUM_BLOCK)
    nb = t // blk

    def body(dc_ref, f_ref, b_ref, df_ref, db_ref):
        ri = lax.broadcasted_iota(jnp.int32, (blk, blk), 0)
        ci = lax.broadcasted_iota(jnp.int32, (blk, blk), 1)
        tri = (ri >= ci).astype(BF16)

        def step(n, carry):
            tail, db = carry
            bi = nb - 1 - n
            rc = _split3_dot(dc_ref[bi], tri) + tail
            df = rc * _sig(-(f_ref[bi] + b_ref[:, :1]))
            df_ref[bi] = df
            return rc[:, 0:1], db + jnp.sum(df, axis=1, keepdims=True)

        _, db = lax.fori_loop(0, nb, step, (jnp.zeros((nh, 1), F32), jnp.zeros((nh, 1), F32)))
        db_ref[...] = jnp.broadcast_to(db, db_ref.shape)

    df, db = pl.pallas_call(
        body,
        name="forget_bwd",
        out_shape=[jax.ShapeDtypeStruct((nb, nh, blk), F32), jax.ShapeDtypeStruct((nh, LANES), F32)],
        compiler_params=pltpu.CompilerParams(vmem_limit_bytes=VMEM_LIMIT),
    )(_to_blocks(dc, nb, blk), _to_blocks(f_t, nb, blk), b_col)
    return _from_blocks(df), db


def _lanes(parts, rows):
    lane = lax.broadcasted_iota(jnp.int32, (rows, LANES), 1)
    out = jnp.zeros((rows, LANES), F32)
    for n, part in enumerate(parts):
        out = jnp.where(lane == n, part, out)
    return out


def _attn_prep_fwd(proj, cs, t, nh, hd, q_off, k_off, v_off):
    tr = _tile(t, ATTN_TILE, 16)
    qb, kb, vb = q_off // hd, k_off // hd, v_off // hd

    def body(q_ref, k_ref, v_ref, cs_ref, qa_ref, ka_ref, va_ref):
        hi, mid, lo = _split3(cs_ref[0][:, :1])
        qa_ref[0, :, :hd] = q_ref[...]
        qa_ref[0, :, hd:] = _lanes([1.0, 1.0, 1.0, hi, mid, lo], tr).astype(BF16)
        ka_ref[0, :, :hd] = k_ref[...]
        ka_ref[0, :, hd:] = _lanes([-hi, -mid, -lo] + [1.0] * 6, tr).astype(BF16)
        va_ref[0, :, :hd] = v_ref[...]
        va_ref[0, :, hd:] = _lanes([-1.0, -1.0, -1.0], tr).astype(BF16)

    wide = pl.BlockSpec((1, tr, 2 * hd), lambda h, i: (h, i, 0))
    return pl.pallas_call(
        body,
        name="attn_prep_fwd",
        grid=(nh, t // tr),
        in_specs=[pl.BlockSpec((tr, hd), lambda h, i: (i, qb + h)), pl.BlockSpec((tr, hd), lambda h, i: (i, kb + h)),
                  pl.BlockSpec((tr, hd), lambda h, i: (i, vb + h)), pl.BlockSpec((1, tr, LANES), lambda h, i: (h, i, 0))],
        out_specs=[wide, wide, wide],
        out_shape=[jax.ShapeDtypeStruct((nh, t, 2 * hd), BF16)] * 3,
        compiler_params=_params("parallel", "parallel"),
    )(proj, proj, proj, cs)


def _attn_prep_bwd(qa, lse, o, do, t, nh, hd):
    tr = _tile(t, ATTN_TILE, 16)
    inv_scale = math.sqrt(hd)

    def body(qa_ref, lse_ref, o_ref, do_ref, qb_ref, da_ref):
        l_hi, l_mid, l_lo = _split3(lse_ref[0][:, :1] * (-inv_scale))
        lane = lax.broadcasted_iota(jnp.int32, (tr, LANES), 1)
        extra = qa_ref[0, :, hd:].astype(F32)
        extra = jnp.where(lane == 6, l_hi, jnp.where(lane == 7, l_mid, jnp.where(lane == 8, l_lo, extra)))
        qb_ref[0, :, :hd] = qa_ref[0, :, :hd]
        qb_ref[0, :, hd:] = extra.astype(BF16)
        dov = do_ref[...]
        delta = jnp.sum(dov.astype(F32) * o_ref[...].astype(F32), axis=1, keepdims=True)
        da_ref[0, :, :hd] = dov
        da_ref[0, :, hd:] = _lanes(list(_split3(delta)), tr).astype(BF16)

    wide = pl.BlockSpec((1, tr, 2 * hd), lambda h, i: (h, i, 0))
    head = pl.BlockSpec((tr, hd), lambda h, i: (i, h))
    return pl.pallas_call(
        body,
        name="attn_prep_bwd",
        grid=(nh, t // tr),
        in_specs=[wide, pl.BlockSpec((1, tr, LANES), lambda h, i: (h, i, 0)), head, head],
        out_specs=[wide, wide],
        out_shape=[jax.ShapeDtypeStruct((nh, t, 2 * hd), BF16)] * 2,
        compiler_params=_params("parallel", "parallel"),
    )(qa, lse, o, do)


def _causal(s, row0, rows, cols):
    row = lax.broadcasted_iota(jnp.int32, (rows, cols), 0) + row0
    col = lax.broadcasted_iota(jnp.int32, (rows, cols), 1)
    return jnp.where(col <= row, s, NEG_INF)


def _side_refs(side):
    if side is None:
        return [], [], []
    return list(side.ins), list(side.out_shapes), [pltpu.SemaphoreType.DMA((cnt,)) for cnt in side.sem_counts]


def _attn_fwd(qa, ka, proj, t, nh, hd, v_off, side=None):
    tq = _tile(t, ATTN_TILE)
    nq = t // tq
    sub = _tile(tq, ATTN_SUB)
    ns = tq // sub
    scale = 1.0 / math.sqrt(hd)
    vb = v_off // hd
    s_ins, s_outs, s_sems = _side_refs(side)

    def body(qa_ref, ka_ref, v_ref, *rest):
        si_refs, (o_ref, lse_ref) = rest[:len(s_ins)], rest[len(s_ins):len(s_ins) + 2]
        so_refs, sem_refs = rest[len(s_ins) + 2:len(s_ins) + 2 + len(s_outs)], rest[len(s_ins) + 2 + len(s_outs):]
        h, i = pl.program_id(0), pl.program_id(1)
        if side:
            @pl.when((h == 0) & (i == 0))
            def _():
                side.start(si_refs, so_refs, sem_refs)

        def tile(j, carry, masked):
            rows = pl.ds(pl.multiple_of(j * tq, tq), tq)
            kj = ka_ref[0, rows, :]
            vj = v_ref[rows, :]
            new = []
            for r in range(ns):
                m, l, acc = carry[r]
                s = lax.dot_general(qa_ref[0, pl.ds(r * sub, sub), :], kj, _DIMS["nt"], preferred_element_type=F32)
                if masked:
                    s = _causal(s, r * sub, sub, tq)
                m_new = jnp.maximum(m, jnp.max(s, axis=1, keepdims=True))
                p = jnp.exp2((s - m_new) * (scale * LOG2E))
                alpha = jnp.exp2((m - m_new) * (scale * LOG2E))
                l = alpha * l + jnp.sum(p, axis=1, keepdims=True)
                acc = alpha * acc + jnp.dot(p.astype(BF16), vj, preferred_element_type=F32)
                new.append((m_new, l, acc))
            return tuple(new)

        init = tuple((jnp.full((sub, 1), NEG_INF, F32), jnp.zeros((sub, 1), F32), jnp.zeros((sub, hd), F32))
                     for _ in range(ns))
        carry = lax.fori_loop(0, i, lambda j, cr: tile(j, cr, False), init)
        carry = tile(i, carry, True)
        for r in range(ns):
            m, l, acc = carry[r]
            o_ref[pl.ds(r * sub, sub), :] = (acc / l).astype(BF16)
            lse_ref[0, pl.ds(r * sub, sub), :] = jnp.broadcast_to(m * scale + jnp.log(l), (sub, LANES))
        if side:
            @pl.when((h == nh - 1) & (i == nq - 1))
            def _():
                side.finish(si_refs, so_refs, sem_refs)

    return pl.pallas_call(
        body,
        name="attn_fwd",
        grid=(nh, nq),
        in_specs=[pl.BlockSpec((1, tq, 2 * hd), lambda h, i: (h, i, 0)),
                  pl.BlockSpec((1, t, 2 * hd), lambda h, i: (h, 0, 0)),
                  pl.BlockSpec((t, hd), lambda h, i: (0, vb + h))] + [ANY] * len(s_ins),
        out_specs=[pl.BlockSpec((tq, hd), lambda h, i: (i, h)),
                   pl.BlockSpec((1, tq, LANES), lambda h, i: (h, i, 0))] + [ANY] * len(s_outs),
        out_shape=[jax.ShapeDtypeStruct((t, nh * hd), BF16), jax.ShapeDtypeStruct((nh, t, LANES), F32)] + s_outs,
        scratch_shapes=s_sems,
        compiler_params=_params(*(("arbitrary",) * 2 if side else ("parallel",) * 2)),
    )(qa, ka, proj, *s_ins)


def _attn_bwd(qb, ka, va, da, t, nh, hd, side=None):
    tq = _tile(t, ATTN_TILE)
    nq = t // tq
    sub = _tile(tq, ATTN_SUB)
    ns = tq // sub
    scale = 1.0 / math.sqrt(hd)
    s_ins, s_outs, s_sems = _side_refs(side)

    def body(qb_ref, ka_ref, va_ref, da_ref, *rest):
        si_refs, rest = rest[:len(s_ins)], rest[len(s_ins):]
        (dq_ref, dk_ref, dv_ref, dck_ref, dcq_ref), rest = rest[:5], rest[5:]
        so_refs, (dq_all, *sem_refs) = rest[:len(s_outs)], rest[len(s_outs):]
        h, j = pl.program_id(0), pl.program_id(1)
        if side:
            @pl.when((h == 0) & (j == 0))
            def _():
                side.start(si_refs, so_refs, sem_refs)

        @pl.when(j == 0)
        def _():
            dq_all[...] = jnp.zeros(dq_all.shape, F32)
            dcq_ref[...] = jnp.zeros(dcq_ref.shape, F32)

        kaj = ka_ref[0]
        vaj = va_ref[0]
        kj = kaj[:, :hd]

        def tile(i, carry, masked):
            dk, dv, dck = carry
            for r in range(ns):
                rows = pl.ds(pl.multiple_of(i * tq + r * sub, sub), sub)
                qr = qb_ref[0, rows, :]
                dr = da_ref[0, rows, :]
                s = lax.dot_general(qr, kaj, _DIMS["nt"], preferred_element_type=F32)
                if masked:
                    s = _causal(s, r * sub, sub, tq)
                p = jnp.exp2(s * (scale * LOG2E))
                ds = p * lax.dot_general(dr, vaj, _DIMS["nt"], preferred_element_type=F32)
                dsb = ds.astype(BF16)
                dv = dv + lax.dot_general(p.astype(BF16), dr[:, :hd], _DIMS["tn"], preferred_element_type=F32)
                dk = dk + lax.dot_general(dsb, qr[:, :hd], _DIMS["tn"], preferred_element_type=F32)
                dck = dck - jnp.sum(ds, axis=0, keepdims=True)
                dq_all[rows, :] += jnp.dot(dsb, kj, preferred_element_type=F32)
                dcq_ref[0, rows, :] += jnp.sum(ds, axis=1, keepdims=True)
            return dk, dv, dck

        carry = (jnp.zeros((tq, hd), F32), jnp.zeros((tq, hd), F32), jnp.zeros((1, tq), F32))
        carry = tile(j, carry, True)
        dk, dv, dck = lax.fori_loop(j + 1, nq, lambda i, cr: tile(i, cr, False), carry)
        dk_ref[...] = (dk * scale).astype(BF16)
        dv_ref[...] = dv.astype(BF16)
        dck_ref[0] = dck

        @pl.when(j == nq - 1)
        def _():
            dq_ref[...] = (dq_all[...] * scale).astype(BF16)

        if side:
            @pl.when((h == nh - 1) & (j == nq - 1))
            def _():
                side.finish(si_refs, so_refs, sem_refs)

    whole = pl.BlockSpec((1, t, 2 * hd), lambda h, j: (h, 0, 0))
    block = pl.BlockSpec((1, tq, 2 * hd), lambda h, j: (h, j, 0))
    return pl.pallas_call(
        body,
        name="attn_bwd",
        grid=(nh, nq),
        in_specs=[whole, block, block, whole] + [ANY] * len(s_ins),
        out_specs=[
            pl.BlockSpec((t, hd), lambda h, j: (0, h)),
            pl.BlockSpec((tq, hd), lambda h, j: (j, h)),
            pl.BlockSpec((tq, hd), lambda h, j: (j, h)),
            pl.BlockSpec((1, 1, tq), lambda h, j: (h, 0, j)),
            pl.BlockSpec((1, t, LANES), lambda h, j: (h, 0, 0)),
        ] + [ANY] * len(s_outs),
        out_shape=[jax.ShapeDtypeStruct((t, nh * hd), BF16), jax.ShapeDtypeStruct((t, nh * hd), BF16),
                   jax.ShapeDtypeStruct((t, nh * hd), BF16), jax.ShapeDtypeStruct((nh, 1, t), F32),
                   jax.ShapeDtypeStruct((nh, t, LANES), F32)] + s_outs,
        scratch_shapes=[pltpu.VMEM((t, hd), F32)] + s_sems,
        compiler_params=_params("arbitrary", "arbitrary"),
    )(qb, ka, va, da, *s_ins)


class _Dims:
    def __init__(self, t, d, c, nh, aw, f, pd):
        self.t, self.d, self.c, self.nh, self.aw, self.f, self.pd = t, d, c, nh, aw, f, pd
        self.hd = aw // nh
        self.a_off, self.g_off = 0, c
        self.q_off, self.k_off, self.v_off = 2 * c, 2 * c + aw, 2 * c + 2 * aw
        self.gc_off = 2 * c + 3 * aw
        self.ga_off = self.gc_off + d
        self.n_main = self.ga_off + d


def _ffn_tn(f):
    return _tile(f, 1536)


class _Riders:
    def __init__(self, plan=None):
        self.plan = plan or {}

    def host(self, name, n_out, call, ctx=None):
        if name not in self.plan:
            return call()
        make_side, take = self.plan[name]
        outs = call(side=make_side(ctx))
        take(outs[n_out:])
        return outs[:n_out]


class _Weights:
    def __init__(self, fixed, gathered, views):
        self.fixed, self.gathered, self.views = fixed, gathered, views

    def __getitem__(self, key):
        if key in self.fixed:
            return self.fixed[key]
        name, off = self.views[key]
        return self.gathered[name], off


def _layer_fwd(dm, x, p, w, riders):
    t, d, c, f = dm.t, dm.d, dm.c, dm.f
    s = {"x": x}
    h = _rms_fwd("rms_mix", x, w["n_mix"], t, d)
    proj, = riders.host("mm_in", 1, functools.partial(
        _matmul, "mm_in", "nn", [h], [w["main"]], [(0, 0, 0)], 1, t, dm.n_main, d, [BF16]))
    fl = _mm("mm_forget", "nn", h, w["f"], t, LANES, d, F32)
    u2, cpre = _conv_fwd(proj, w["conv_w"], w["conv_b"], w["ln_g"], w["ln_b"], t, c, dm.a_off, dm.g_off)
    f_t = fl[:, :dm.nh].T
    cum = _forget_fwd(f_t, w["b_f"], dm.nh, t)
    cs = jnp.broadcast_to((cum * math.sqrt(dm.hd))[:, :, None], (dm.nh, t, LANES))
    qa, ka, va = _attn_prep_fwd(proj, cs, t, dm.nh, dm.hd, dm.q_off, dm.k_off, dm.v_off)
    o, lse = riders.host("attn_fwd", 2, functools.partial(_attn_fwd, qa, ka, proj, t, dm.nh, dm.hd, dm.v_off))

    def epi_conv(accs, ex):
        return accs[0], _sig(ex[0].astype(F32)) * accs[0]

    yc, m1 = _matmul("mm_conv_out", "nn", [u2], [w["co"]], [(0, 0, 0)], 1, t, d, c, [BF16, BF16],
                     epi=epi_conv, extras=[("mn", (proj, dm.gc_off))], tm=512)

    def epi_attn(accs, ex):
        return accs[0], ex[1].astype(F32) + _sig(ex[0].astype(F32)) * accs[0]

    ya, merged = _matmul("mm_attn_out", "nn", [o], [w["ao"]], [(0, 0, 0)], 1, t, d, dm.aw, [BF16, BF16],
                         epi=epi_attn, extras=[("mn", (proj, dm.ga_off)), ("mn", m1)], tm=512)
    x1 = _matmul("mm_out", "nn", [merged], [w["o"]], [(0, 0, 0)], 1, t, d, d, [F32],
                 epi=lambda accs, ex: [ex[0] + accs[0]], extras=[("mn", x)], tm=512, cols_outer=True)[0]

    hf = _rms_fwd("rms_ffn", x1, w["n_ffn"], t, d)

    def epi_glu(accs, ex):
        gate, up = accs
        return gate, up, gate * _sig(gate) * up

    gate, up, act = riders.host("mm_gate_up", 3, functools.partial(
        _matmul, "mm_gate_up", "nn", [hf], [w["g"], w["u"]], [(0, 0, 0), (0, 1, 1)], 2, t, f, d, [BF16, BF16, BF16],
        epi=epi_glu, tm=512, tn=_ffn_tn(f), cols_outer=True))
    x2, = riders.host("mm_down", 1, functools.partial(
        _matmul, "mm_down", "nn", [act], [w["d"]], [(0, 0, 0)], 1, t, d, f, [F32],
        epi=lambda accs, ex: [ex[0] + accs[0]], extras=[("mn", x1)], tm=512, tk=f, cols_outer=True))

    hp = _rms_fwd("rms_ple", x2, w["n_ple"], t, d)
    pp = _mm("mm_ple_proj", "nn", p, w["pp"], t, d, dm.pd, BF16, tm=512)

    def epi_ple(accs, ex):
        sg = _sig(accs[0])
        return sg, ex[1] + sg * ex[0].astype(F32)

    sg, x3 = _matmul("mm_ple_gate", "nn", [hp], [w["pg"]], [(0, 0, 0)], 1, t, d, d, [BF16, F32],
                     epi=epi_ple, extras=[("mn", pp), ("mn", x2)], tm=512, cols_outer=True)
    s.update(h=h, proj=proj, f_t=f_t, qa=qa, ka=ka, va=va, u2=u2, cpre=cpre, o=o, lse=lse, yc=yc, ya=ya,
             merged=merged, x1=x1, hf=hf, gate=gate, up=up, act=act, x2=x2, hp=hp, pp=pp, sg=sg, p=p)
    return x3, s


def _layer_bwd(dm, dx3, dx3b, s, w, riders):
    t, d, c, f = dm.t, dm.d, dm.c, dm.f
    g = {}

    def ple_ew(dxv, sgv, ppv):
        sgf, ppf = sgv.astype(F32), ppv.astype(F32)
        return dxv * sgf, dxv * ppf * sgf * (1.0 - sgf)

    d_pp, d_z = _ew("ple_bwd", ple_ew, [dx3, s["sg"], s["pp"]], t, d, [BF16, BF16], _tile(t, ROW_TILE, 8))
    g["pp"] = _mm("gw_ple_proj", "tn", s["p"], d_pp, dm.pd, d, t, F32)
    g["pg"] = _mm("gw_ple_gate", "tn", s["hp"], d_z, d, d, t, F32)
    d_hp = _mm("dx_ple_gate", "nt", d_z, w["pg"], t, d, d, BF16)
    dx2, dx2b, g["n_ple"] = _rms_bwd("rms_ple_bwd", d_hp, s["x2"], w["n_ple"], dx3, t, d)

    def epi_dglu(accs, ex):
        gate, up = ex[0].astype(F32), ex[1].astype(F32)
        sg = _sig(gate)
        return accs[0] * up * (sg * (1.0 + gate * (1.0 - sg))), accs[0] * gate * sg

    d_gate, d_up = riders.host("dx_down", 2, functools.partial(
        _matmul, "dx_down", "nt", [dx2b], [w["d"]], [(0, 0, 0)], 1, t, f, d, [BF16, BF16], epi=epi_dglu,
        extras=[("mn", s["gate"]), ("mn", s["up"])], tm=512, tn=_ffn_tn(f), cols_outer=True))
    g["d"] = _mm("gw_down", "tn", s["act"], dx2b, f, d, t, F32, tm=_ffn_tn(f))
    g["g"] = _mm("gw_gate", "tn", s["hf"], d_gate, d, f, t, F32, tn=_ffn_tn(f))
    g["u"] = _mm("gw_up", "tn", s["hf"], d_up, d, f, t, F32, tn=_ffn_tn(f))
    d_hf, = riders.host("dx_gate_up", 1, functools.partial(
        _matmul, "dx_gate_up", "nt", [d_gate, d_up], [w["g"], w["u"]], [(0, 0, 0), (1, 1, 0)], 1, t, d, f, [BF16]))
    dx1, dx1b, g["n_ffn"] = _rms_bwd("rms_ffn_bwd", d_hf, s["x1"], w["n_ffn"], dx2, t, d)

    g["o"] = _mm("gw_out", "tn", s["merged"], dx1b, d, d, t, F32)

    def epi_dmerge(accs, ex):
        dmv = accs[0]
        sgc, sga = _sig(ex[0].astype(F32)), _sig(ex[1].astype(F32))
        ycv, yav = ex[2].astype(F32), ex[3].astype(F32)
        return dmv * sgc, dmv * sga, dmv * ycv * sgc * (1.0 - sgc), dmv * yav * sga * (1.0 - sga)

    d_yc, d_ya, d_gc, d_ga = _matmul(
        "dx_out", "nt", [dx1b], [w["o"]], [(0, 0, 0)], 1, t, d, d, [BF16] * 4, epi=epi_dmerge,
        extras=[("mn", (s["proj"], dm.gc_off)), ("mn", (s["proj"], dm.ga_off)), ("mn", s["yc"]), ("mn", s["ya"])],
        tm=512, tn=_tile(d, 512))
    g["co"] = _mm("gw_conv_out", "tn", s["u2"], d_yc, c, d, t, F32)
    d_u2 = _mm("dx_conv_out", "nt", d_yc, w["co"], t, c, d, BF16)
    g["ao"] = _mm("gw_attn_out", "tn", s["o"], d_ya, dm.aw, d, t, F32)
    d_o = _mm("dx_attn_out", "nt", d_ya, w["ao"], t, dm.aw, d, BF16)

    qb, da = _attn_prep_bwd(s["qa"], s["lse"], s["o"], d_o, t, dm.nh, dm.hd)
    dq, dk, dv, dck, dcq = riders.host(
        "attn_bwd", 5, functools.partial(_attn_bwd, qb, s["ka"], s["va"], da, t, dm.nh, dm.hd), ctx=g)
    d_ft, g_bf = _forget_bwd(dck.reshape(dm.nh, t) + dcq[:, :, 0], s["f_t"], w["b_f"], dm.nh, t)
    g["b_f"] = g_bf[:, 0]
    d_f = jnp.pad(d_ft.T, ((0, 0), (0, LANES - dm.nh))).astype(BF16)

    d_a, d_gg, g["conv_w"], g["conv_b"], g["ln_g"], g["ln_b"] = _conv_bwd(
        s["proj"], s["cpre"], d_u2, w["conv_w"], w["ln_g"], w["ln_b"], t, c, dm.a_off, dm.g_off)

    d_proj = jnp.concatenate([d_a, d_gg, dq, dk, dv, d_gc, d_ga], axis=1)
    g["main"], = riders.host("gw_in", 1, functools.partial(
        _matmul, "gw_in", "tn", [s["h"]], [d_proj], [(0, 0, 0)], 1, d, dm.n_main, t, [F32], tk=_tile(t, 4096)))
    g["f"] = _mm("gw_forget", "tn", s["h"], d_f, d, LANES, t, F32)
    d_h_f = _mm("dx_forget", "nt", d_f, w["f"], t, d, LANES, BF16)
    d_h, = riders.host("dx_in", 1, functools.partial(
        _matmul, "dx_in", "nt", [d_proj], [w["main"]], [(0, 0, 0)], 1, t, d, dm.n_main, [BF16],
        epi=lambda accs, ex: [accs[0] + ex[0].astype(F32)], extras=[("mn", d_h_f)], tk=_tile(dm.n_main, 3072)))
    dx, dxb, g["n_mix"] = _rms_bwd("rms_mix_bwd", d_h, s["x"], w["n_mix"], dx1, t, d)
    return dx, dxb, g


def _adamw_tiles(wv, gv, mv, vv):
    m_new = ADAM_B1 * mv + (1.0 - ADAM_B1) * gv
    v_new = ADAM_B2 * vv + (1.0 - ADAM_B2) * (gv * gv)
    m_hat = m_new / (1.0 - ADAM_B1 ** ADAM_STEP)
    v_hat = v_new / (1.0 - ADAM_B2 ** ADAM_STEP)
    delta = -ADAM_LR * (m_hat / (jnp.sqrt(v_hat) + ADAM_EPS) + ADAM_WD * wv)
    return delta, m_new, v_new, gv


def _adamw(name, wv, gv, mv, vv):
    shape = wv.shape
    cols = shape[-1]
    rows = wv.size // cols
    tm = _tile(rows, max(8, (1 << 19) // cols), 8)
    flat = [a.reshape(rows, cols) for a in (wv, gv, mv, vv)]
    outs = _ew(name, _adamw_tiles, flat, rows, cols, [F32, F32, F32, F32], tm)
    return [o.reshape(shape) for o in outs]


def _place():
    return lax.axis_index("x"), lax.axis_index("y"), lax.axis_index("c")


def _other_chips(x, y):
    return [(1 - x, y), (x, 1 - y), (1 - x, 1 - y)]


def _window(ref, kind, chip, size):
    if kind == "chip":
        return ref.at[chip]
    if kind == "row":
        return ref.at[pl.ds(chip * size, size), :]
    return ref.at[:, pl.ds(pl.multiple_of(chip * size, LANES), size)]


class _Side:
    def __init__(self, ins, out_shapes, sem_counts, start, finish):
        self.ins, self.out_shapes, self.sem_counts, self.start, self.finish = ins, out_shapes, sem_counts, start, finish

    def join(self, other):
        ni, no, ns = len(self.ins), len(self.out_shapes), len(self.sem_counts)

        def both(first, second):
            def run(ins, outs, sems):
                first(ins[:ni], outs[:no], sems[:ns])
                second(ins[ni:], outs[no:], sems[ns:])
            return run

        return _Side(self.ins + other.ins, self.out_shapes + other.out_shapes, self.sem_counts + other.sem_counts,
                     both(self.start, other.start), both(self.finish, other.finish))


def _run_side(name, side):
    n_in, n_out = len(side.ins), len(side.out_shapes)

    def body(*refs):
        ins, outs, sems = refs[:n_in], refs[n_in:n_in + n_out], refs[n_in + n_out:]
        side.start(ins, outs, sems)
        side.finish(ins, outs, sems)

    return pl.pallas_call(
        body,
        name=name,
        in_specs=[ANY] * n_in,
        out_specs=[ANY] * n_out,
        out_shape=list(side.out_shapes),
        scratch_shapes=[pltpu.SemaphoreType.DMA((cnt,)) for cnt in side.sem_counts],
    )(*side.ins)


def _full_shape(shard, kind):
    _, a, b = shard.shape
    if kind == "chip":
        return (N_CHIPS, a, b)
    return (N_CHIPS * a, b) if kind == "row" else (a, N_CHIPS * b)


def _gather_side(shards, kinds, layer):
    n = len(shards)
    sizes = [s.shape[1] if k == "row" else s.shape[2] for s, k in zip(shards, kinds)]

    def copies(ins, outs, sems, forwards):
        ici_send, ici_recv, d2d_send, d2d_recv, own_send, own_recv = sems
        x, y, c = _place()
        me = 2 * x + y
        sibling = (x, y, 1 - c)
        own, ici, landed, fwd = [], [], [], []
        for w in range(n):
            mine = _window(outs[w], kinds[w], me, sizes[w])
            own.append(pltpu.make_async_remote_copy(
                src_ref=ins[w].at[layer], dst_ref=mine, send_sem=own_send.at[w], recv_sem=own_recv.at[w],
                device_id=sibling, device_id_type=MESH))
            for r, (px, py) in enumerate(_other_chips(x, y)):
                ici.append(pltpu.make_async_remote_copy(
                    src_ref=ins[w].at[layer], dst_ref=mine, send_sem=ici_send.at[3 * w + r],
                    recv_sem=ici_recv.at[3 * w + r], device_id=(px, py, layer), device_id_type=MESH))
                if forwards:
                    slab = _window(outs[w], kinds[w], 2 * px + py, sizes[w])
                    landed.append(pltpu.make_async_remote_copy(
                        src_ref=slab, dst_ref=slab, send_sem=ici_send.at[3 * w + r], recv_sem=ici_recv.at[3 * w + r],
                        device_id=(px, py, layer), device_id_type=MESH))
                    fwd.append(pltpu.make_async_remote_copy(
                        src_ref=slab, dst_ref=slab, send_sem=d2d_send.at[3 * w + r], recv_sem=d2d_recv.at[3 * w + r],
                        device_id=sibling, device_id_type=MESH))
        return c, own, ici, landed, fwd

    def start(ins, outs, sems):
        c, own, ici, _, _ = copies(ins, outs, sems, False)
        for cp in own:
            cp.start()

        @pl.when(c == layer)
        def _():
            for cp in ici:
                cp.start()

    def finish(ins, outs, sems):
        c, own, ici, landed, fwd = copies(ins, outs, sems, True)

        @pl.when(c == layer)
        def _():
            for got, cp in zip(landed, fwd):
                got.wait_recv()
                cp.start()
            for cp in ici + fwd:
                cp.wait_send()

        @pl.when(c != layer)
        def _():
            for cp in fwd:
                cp.wait_recv()

        for cp in own:
            cp.wait()

    out_shapes = [jax.ShapeDtypeStruct(_full_shape(s, k), s.dtype) for s, k in zip(shards, kinds)]
    return _Side(list(shards), out_shapes, [3 * n] * 4 + [n] * 2, start, finish)


def _swap_side(grads, layer):
    n = len(grads)

    def copies(ins, outs, sems):
        x, y, c = _place()
        return c, [pltpu.make_async_remote_copy(src_ref=ins[w], dst_ref=outs[w], send_sem=sems[0].at[w],
                                                recv_sem=sems[1].at[w], device_id=(x, y, layer), device_id_type=MESH)
                   for w in range(n)]

    def start(ins, outs, sems):
        c, cps = copies(ins, outs, sems)

        @pl.when(c != layer)
        def _():
            for cp in cps:
                cp.start()

    def finish(ins, outs, sems):
        c, cps = copies(ins, outs, sems)

        @pl.when(c != layer)
        def _():
            for cp in cps:
                cp.wait_send()

        @pl.when(c == layer)
        def _():
            for cp in cps:
                cp.wait_recv()

    return _Side(list(grads), [jax.ShapeDtypeStruct(a.shape, a.dtype) for a in grads], [n, n], start, finish)


def _add_core_partials(name, mine, got):
    shape = got.shape
    cols = shape[-1]
    rows = got.size // cols
    tm, tn = _tile(rows, 512, 8), _tile(cols, 2048)
    return _ew(name, lambda a, b: [a + b], [mine.reshape(rows, cols), got.reshape(rows, cols)], rows, cols,
               [BF16], tm, tn)[0].reshape(shape)


def _exchange_side(parts, kinds, sizes, layer):
    n = len(parts)

    def copies(ins, outs, sems):
        x, y, c = _place()
        cps = []
        for w in range(n):
            for r, (px, py) in enumerate(_other_chips(x, y)):
                cps.append(pltpu.make_async_remote_copy(
                    src_ref=_window(ins[w], kinds[w], 2 * px + py, sizes[w]), dst_ref=outs[w].at[r],
                    send_sem=sems[0].at[3 * w + r], recv_sem=sems[1].at[3 * w + r],
                    device_id=(px, py, layer), device_id_type=MESH))
        return c, cps

    def start(ins, outs, sems):
        c, cps = copies(ins, outs, sems)

        @pl.when(c == layer)
        def _():
            for cp in cps:
                cp.start()

    def finish(ins, outs, sems):
        c, cps = copies(ins, outs, sems)

        @pl.when(c == layer)
        def _():
            for cp in cps:
                cp.wait()

    out_shapes = [jax.ShapeDtypeStruct((3,) + tuple(_shard_shape(p, k, s)), p.dtype)
                  for p, k, s in zip(parts, kinds, sizes)]
    return _Side(list(parts), out_shapes, [3 * n, 3 * n], start, finish)


def _shard_shape(whole, kind, size):
    if kind == "chip":
        return whole.shape[1:]
    return (size, whole.shape[1]) if kind == "row" else (whole.shape[0], size)


def _sum_chip_partials(name, part, got, kind, size, chip, layer, both=None):
    rows, cols = _shard_shape(part, kind, size)
    tm = _tile(rows, max(8, (1 << 19) // cols), 16)

    def body(chip_ref, part_ref, g0_ref, g1_ref, g2_ref, *rest):
        total = part_ref[...].astype(F32)
        for ref in (g0_ref, g1_ref, g2_ref):
            total = total + ref[...].astype(F32)
        rest[-1][...] = total

    if kind == "chip":
        mine = pl.BlockSpec((None, tm, cols), lambda i, chip_ref: (chip_ref[0], i, 0))
    elif kind == "row":
        mine = pl.BlockSpec((tm, cols), lambda i, chip_ref: (chip_ref[0] * (rows // tm) + i, 0))
    else:
        mine = pl.BlockSpec((tm, cols), lambda i, chip_ref: (i, chip_ref[0]))
    theirs = [pl.BlockSpec((None, tm, cols), functools.partial(lambda r, i, chip_ref: (r, i, 0), r))
              for r in range(3)]
    kept = [] if both is None else [both]
    return pl.pallas_call(
        body,
        name=name,
        grid_spec=pltpu.PrefetchScalarGridSpec(
            num_scalar_prefetch=1, grid=(rows // tm,), in_specs=[mine] + theirs + [ANY] * len(kept),
            out_specs=pl.BlockSpec((None, tm, cols), lambda i, chip_ref: (layer, i, 0))),
        out_shape=jax.ShapeDtypeStruct((2, rows, cols), F32),
        input_output_aliases={5: 0} if kept else {},
        compiler_params=_params("parallel"),
    )(chip.reshape(1), part, got, got, got, *kept)


def _share_reduced(both):
    n = len(both)

    def body(*refs):
        ins, outs = refs[:n], refs[n:2 * n]
        send_sems, recv_sems = refs[2 * n:]
        x, y, c = _place()
        sends = []
        for w in range(n):
            cp = pltpu.make_async_remote_copy(src_ref=ins[w].at[c], dst_ref=outs[w].at[c], send_sem=send_sems.at[w],
                                              recv_sem=recv_sems.at[w], device_id=(x, y, 1 - c), device_id_type=MESH)
            cp.start()
            sends.append(cp)
        for w in range(n):
            got = outs[w].at[1 - c]
            pltpu.make_async_remote_copy(src_ref=got, dst_ref=got, send_sem=send_sems.at[w], recv_sem=recv_sems.at[w],
                                         device_id=(x, y, 1 - c), device_id_type=MESH).wait_recv()
        for cp in sends:
            cp.wait_send()

    return pl.pallas_call(
        body,
        name="share_reduced",
        in_specs=[ANY] * n,
        out_specs=[ANY] * n,
        out_shape=[jax.ShapeDtypeStruct(a.shape, a.dtype) for a in both],
        input_output_aliases={w: w for w in range(n)},
        scratch_shapes=[pltpu.SemaphoreType.DMA((n,)), pltpu.SemaphoreType.DMA((n,))],
    )(*both)


def _allreduce_small(v):
    rows, width = v.shape

    def body(v_ref, out_ref, slots, send_sems, recv_sems):
        x, y, c = _place()
        me = 4 * x + 2 * y + c
        slots[me] = v_ref[...]
        peers = [(x, y, 1 - c)]
        for px, py in _other_chips(x, y):
            peers += [(px, py, c), (px, py, 1 - c)]
        sends = []
        for r, peer in enumerate(peers):
            cp = pltpu.make_async_remote_copy(
                src_ref=v_ref, dst_ref=slots.at[me], send_sem=send_sems.at[r], recv_sem=recv_sems.at[r],
                device_id=peer, device_id_type=MESH)
            cp.start()
            sends.append(cp)
        for r, (px, py, pc) in enumerate(peers):
            got = slots.at[4 * px + 2 * py + pc]
            pltpu.make_async_remote_copy(
                src_ref=got, dst_ref=got, send_sem=send_sems.at[r], recv_sem=recv_sems.at[r],
                device_id=(px, py, pc), device_id_type=MESH).wait_recv()
        for cp in sends:
            cp.wait_send()
        total = slots[0]
        for n in range(1, 8):
            total = total + slots[n]
        out_ref[...] = total

    vm = pl.BlockSpec(memory_space=pltpu.VMEM)
    return pl.pallas_call(
        body,
        name="allreduce_small",
        in_specs=[vm],
        out_specs=vm,
        out_shape=jax.ShapeDtypeStruct((rows, width), F32),
        scratch_shapes=[pltpu.VMEM((8, rows, width), F32), pltpu.SemaphoreType.DMA((7,)),
                        pltpu.SemaphoreType.DMA((7,))],
    )(v)


def _pad_rows(flat, row_align):
    n = flat.shape[0]
    rows = -(-n // PACK_W)
    rows = -(-rows // row_align) * row_align
    return jnp.pad(flat, (0, rows * PACK_W - n)).reshape(rows, PACK_W)


def _unpack(buf, shapes):
    flat = buf.reshape(-1)
    out, off = [], 0
    for shp in shapes:
        n = math.prod(shp)
        out.append(flat[off:off + n].reshape(shp))
        off += n
    return out


BIG = (("w_in", "chip"), ("w_conv_out", "col"), ("w_attn_out", "col"), ("w_out", "row"), ("w_gate_up", "col"),
       ("w_down", "row"), ("w_ple_gate", "row"), ("w_ple_proj", "col"))


def kernel(x, p, norm_mix_g, w_in, b_forget, conv_w, conv_b, conv_ln_g, conv_ln_b, w_conv_out, w_attn_out, w_out, norm_ffn_g, w_gate_up, w_down, norm_ple_g, w_ple_gate, w_ple_proj, final_g, loss_target, m_norm_mix_g, m_w_in, m_b_forget, m_conv_w, m_conv_b, m_conv_ln_g, m_conv_ln_b, m_w_conv_out, m_w_attn_out, m_w_out, m_norm_ffn_g, m_w_gate_up, m_w_down, m_norm_ple_g, m_w_ple_gate, m_w_ple_proj, m_final_g, v_norm_mix_g, v_w_in, v_b_forget, v_conv_w, v_conv_b, v_conv_ln_g, v_conv_ln_b, v_w_conv_out, v_w_attn_out, v_w_out, v_norm_ffn_g, v_w_gate_up, v_w_down, v_norm_ple_g, v_w_ple_gate, v_w_ple_proj, v_final_g):
    wts = dict(norm_mix_g=norm_mix_g, w_in=w_in, b_forget=b_forget, conv_w=conv_w, conv_b=conv_b,
               conv_ln_g=conv_ln_g, conv_ln_b=conv_ln_b, w_conv_out=w_conv_out, w_attn_out=w_attn_out,
               w_out=w_out, norm_ffn_g=norm_ffn_g, w_gate_up=w_gate_up, w_down=w_down, norm_ple_g=norm_ple_g,
               w_ple_gate=w_ple_gate, w_ple_proj=w_ple_proj, final_g=final_g)
    mom1 = dict(norm_mix_g=m_norm_mix_g, w_in=m_w_in, b_forget=m_b_forget, conv_w=m_conv_w, conv_b=m_conv_b,
                conv_ln_g=m_conv_ln_g, conv_ln_b=m_conv_ln_b, w_conv_out=m_w_conv_out, w_attn_out=m_w_attn_out,
                w_out=m_w_out, norm_ffn_g=m_norm_ffn_g, w_gate_up=m_w_gate_up, w_down=m_w_down,
                norm_ple_g=m_norm_ple_g, w_ple_gate=m_w_ple_gate, w_ple_proj=m_w_ple_proj, final_g=m_final_g)
    mom2 = dict(norm_mix_g=v_norm_mix_g, w_in=v_w_in, b_forget=v_b_forget, conv_w=v_conv_w, conv_b=v_conv_b,
                conv_ln_g=v_conv_ln_g, conv_ln_b=v_conv_ln_b, w_conv_out=v_w_conv_out, w_attn_out=v_w_attn_out,
                w_out=v_w_out, norm_ffn_g=v_norm_ffn_g, w_gate_up=v_w_gate_up, w_down=v_w_down,
                norm_ple_g=v_norm_ple_g, w_ple_gate=v_w_ple_gate, w_ple_proj=v_w_ple_proj, final_g=v_final_g)
    order = list(wts)
    depth = w_in.shape[0]
    assert depth == 2, "the exchanges give one layer to each of a chip's two cores"
    t, d = x.shape[1], x.shape[2]
    c = conv_ln_g.shape[1]
    nh = b_forget.shape[1]
    aw = w_attn_out.shape[1]
    f = N_CHIPS * w_down.shape[1]
    pd = w_ple_proj.shape[1]
    dm = _Dims(t, d, c, nh, aw, f, pd)
    n_split = 2 * c + 3 * aw
    cw = conv_w.shape[2]
    chip = 2 * lax.axis_index("x") + lax.axis_index("y")
    big_names = [name for name, _ in BIG]
    big_kinds = [kind for _, kind in BIG]

    shards = {name: wts[name].astype(BF16) for name in big_names}
    shards["conv_w"] = conv_w
    kind_of = dict(BIG, conv_w="chip")
    full = [{}, {}]

    def gather_rider(names, layer):
        def make_side(ctx=None):
            return _gather_side([shards[n] for n in names], [kind_of[n] for n in names], layer)

        def take(results):
            full[layer].update(zip(names, results))

        return make_side, take

    make_side, take = gather_rider(["w_in", "conv_w"], 0)
    take(_run_side("gather_layer0_in", make_side()))
    fwd_riders = [_Riders({"attn_fwd": gather_rider(big_names[1:], 0),
                           "mm_in": gather_rider(["w_in", "w_conv_out", "w_out", "conv_w"], 1),
                           "mm_gate_up": gather_rider(["w_gate_up", "w_ple_gate", "w_attn_out"], 1),
                           "mm_down": gather_rider(["w_down", "w_ple_proj"], 1)}),
                  _Riders()]
    views = {"co": ("w_conv_out", 0), "ao": ("w_attn_out", 0), "o": ("w_out", 0), "g": ("w_gate_up", 0),
             "u": ("w_gate_up", f), "d": ("w_down", 0), "pg": ("w_ple_gate", 0), "pp": ("w_ple_proj", 0)}

    def layer_weights(l):
        fw = full[l]
        wi = jnp.concatenate([fw["w_in"][k] for k in range(N_CHIPS)], axis=1)
        cwl = jnp.concatenate([fw["conv_w"][k] for k in range(N_CHIPS)], axis=1)
        return _Weights({
            "main": jnp.concatenate([wi[:, :n_split], wi[:, n_split + nh:]], axis=1),
            "f": jnp.pad(wi[:, n_split:n_split + nh], ((0, 0), (0, LANES - nh))),
            "conv_w": jnp.pad(cwl, ((0, CONV_HALO - CONV_K), (0, 0))),
            "conv_b": conv_b[l][None], "ln_g": conv_ln_g[l][None], "ln_b": conv_ln_b[l][None],
            "b_f": jnp.broadcast_to(b_forget[l][:, None], (nh, LANES)),
            "n_mix": norm_mix_g[l][None], "n_ffn": norm_ffn_g[l][None], "n_ple": norm_ple_g[l][None],
        }, fw, views)

    xl = x[0]
    saved, lw = [], []
    for l in range(depth):
        lw.append(layer_weights(l))
        xl, s = _layer_fwd(dm, xl, p[l, 0], lw[l], fwd_riders[l])
        saved.append(s)
    loss_row, dx, dxb, g_final = _loss_head(xl, final_g[None], loss_target[0], t, d)
    loss = lax.psum(loss_row[0, 0], ("x", "y", "c"))

    sizes = [wts[name].shape[1] if kind == "row" else wts[name].shape[2] for name, kind in BIG]
    everything = list(range(len(BIG)))
    plain = {1: "co", 2: "ao", 3: "o", 5: "d", 6: "pg", 7: "pp"}

    def grad_of(g, w):
        if w == 0:
            w_in_g = jnp.concatenate([g["main"][:, :n_split], g["f"][:, :nh], g["main"][:, n_split:]], axis=1)
            return w_in_g.reshape(d, N_CHIPS, -1).transpose(1, 0, 2)
        return jnp.concatenate([g["g"], g["u"]], axis=1) if w == 4 else g[plain[w]]

    def add_partials(layer, group, mine, got):
        return {w: _add_core_partials(f"add_core_partials_l{layer}_{big_names[w]}", a, b)
                for w, a, b in zip(group, mine, got)}

    def exchange(parts, group, layer):
        return _exchange_side([parts[w] for w in group], [big_kinds[w] for w in group], [sizes[w] for w in group], layer)

    def sum_partials(layer, parts, got, both):
        return [_sum_chip_partials(f"sum_chip_partials_l{layer}_{big_names[w]}", parts[w], got[w], big_kinds[w],
                                   sizes[w], chip, layer, both=None if both is None else both[w])
                for w in everything]

    lg = [None] * depth
    dx, dxb, lg[1] = _layer_bwd(dm, dx, dxb, saved[1], lw[1], _Riders())
    g1 = [grad_of(lg[1], w) for w in everything]
    parts = [{}, {}]
    got = [{}, {}]
    early = everything[1:]

    def exchange_rider(group, layer):
        return (lambda ctx: exchange(parts[layer], group, layer)), (lambda res: got[layer].update(zip(group, res)))

    early_grads = []

    def attn_bwd_side(g):
        early_grads.extend(grad_of(g, w) for w in early)
        return exchange(parts[1], [0, 1, 2, 3], 1).join(_swap_side(early_grads, 0))

    def attn_bwd_take(res):
        got[1].update(zip([0, 1, 2, 3], res[:4]))
        parts[0].update(add_partials(0, early, early_grads, res[4:]))

    bwd_riders = _Riders({
        "dx_down": (lambda ctx: _swap_side(g1, 1), lambda res: parts[1].update(add_partials(1, everything, g1, res))),
        "dx_gate_up": exchange_rider([4, 5, 6, 7], 1),
        "attn_bwd": (attn_bwd_side, attn_bwd_take),
        "gw_in": exchange_rider([4], 0),
        "dx_in": exchange_rider([1, 2, 3, 5, 6, 7], 0)})
    dx, dxb, lg[0] = _layer_bwd(dm, dx, dxb, saved[0], lw[0], bwd_riders)
    both = sum_partials(1, parts[1], got[1], None)

    w_in_g = [grad_of(lg[0], 0)]
    parts[0].update(add_partials(0, [0], w_in_g, _run_side("swap_layer0_in_grads", _swap_side(w_in_g, 0))))
    got[0].update(zip([0], _run_side("exchange_layer0_in_grads", exchange(parts[0], [0], 0))))
    both = sum_partials(0, parts[0], got[0], both)
    grads = dict(zip(big_names, _share_reduced(both)))

    def stacked(key):
        return jnp.stack([lg[l][key] for l in range(depth)])

    small = ["norm_mix_g", "b_forget", "conv_b", "conv_ln_g", "conv_ln_b", "norm_ffn_g", "norm_ple_g", "final_g"]
    small_g = {
        "norm_mix_g": jnp.concatenate([lg[l]["n_mix"] for l in range(depth)]),
        "b_forget": stacked("b_f"),
        "conv_b": jnp.concatenate([lg[l]["conv_b"] for l in range(depth)]),
        "conv_ln_g": jnp.concatenate([lg[l]["ln_g"] for l in range(depth)]),
        "conv_ln_b": jnp.concatenate([lg[l]["ln_b"] for l in range(depth)]),
        "norm_ffn_g": jnp.concatenate([lg[l]["n_ffn"] for l in range(depth)]),
        "norm_ple_g": jnp.concatenate([lg[l]["n_ple"] for l in range(depth)]),
        "final_g": g_final[0],
    }
    conv_w_g = jnp.stack([lg[l]["conv_w"][:CONV_K] for l in range(depth)])
    small_shapes = [wts[n].shape for n in small] + [conv_w_g.shape]
    small_pack = _pad_rows(jnp.concatenate([small_g[n].reshape(-1) for n in small] + [conv_w_g.reshape(-1)]), 8)
    small_sum = _unpack(_allreduce_small(small_pack), small_shapes)
    for n, name in enumerate(small):
        grads[name] = small_sum[n]
    grads["conv_w"] = lax.dynamic_slice_in_dim(small_sum[len(small)], chip * cw, cw, axis=2)

    def pack_small(src):
        return _pad_rows(jnp.concatenate([src[n].reshape(-1) for n in small]), 8)

    small_upd = _adamw("adamw_small", pack_small(wts), pack_small(grads), pack_small(mom1), pack_small(mom2))
    small_upd = [_unpack(u, [wts[n].shape for n in small]) for u in small_upd]
    delta, new_m, new_v = {}, {}, {}
    for n, name in enumerate(small):
        delta[name], new_m[name], new_v[name] = small_upd[0][n], small_upd[1][n], small_upd[2][n]
    for name in big_names + ["conv_w"]:
        delta[name], new_m[name], new_v[name], grads[name] = _adamw(
            "adamw_" + name, wts[name], grads[name], mom1[name], mom2[name])

    return (loss, dx[None], *[grads[n] for n in order], *[delta[n] for n in order],
            *[new_m[n] for n in order], *[new_v[n] for n in order])
```
